```python
import math
import jax, jax.numpy as jnp
from jax import lax
import numpy as np

D_MODEL = 2048
BATCH = 8
SEQ = 4096
DEPTH = 4

N_MIXERS = 2
SSD_EXPAND = 2
D_INNER = SSD_EXPAND * D_MODEL
HEAD_DIM = 64
N_HEADS = D_INNER // HEAD_DIM
N_GROUPS = 8
HEADS_PER_GROUP = N_HEADS // N_GROUPS
D_STATE = 128
CONV_WIDTH = 5
CHUNK = 128
D_BC = N_GROUPS * D_STATE
D_XBC = D_INNER + 2 * D_BC
D_IN_PROJ = D_INNER + D_XBC + 2 * N_HEADS
DT_MIN = 1e-3
DT_MAX = 1e-1
A_INIT_MIN = 1.0
A_INIT_MAX = 16.0
POOL_WINDOWS = (2, 4, 8, 16)
N_POOL_GROUPS = len(POOL_WINDOWS)
POOL_GROUP_DIM = D_MODEL // N_POOL_GROUPS
D_FF = 4 * D_MODEL
DEEPNORM_ALPHA = (2.0 * DEPTH) ** 0.25
DEEPNORM_BETA = (8.0 * DEPTH) ** -0.25
LN_EPS = 1e-5
RMS_EPS = 1e-5
N_SSD_LAYERS = (DEPTH + 1) // 2
N_POOL_LAYERS = DEPTH // 2

kernel_name = "bidir_ssd_pool_hybrid_deepnorm"


def layer_norm(x, g, b):
    xf = x.astype(jnp.float32)
    mu = jnp.mean(xf, axis=-1, keepdims=True)
    var = jnp.mean(jnp.square(xf - mu), axis=-1, keepdims=True)
    y = (xf - mu) * lax.rsqrt(var + LN_EPS) * g.astype(jnp.float32) + b.astype(jnp.float32)
    return y.astype(x.dtype)


def centred_depthwise_conv(x, w, bias):
    c = x.shape[-1]
    pad = CONV_WIDTH // 2
    y = lax.conv_general_dilated(
        x, w, window_strides=(1,), padding=[(pad, pad)],
        dimension_numbers=("NWC", "WIO", "NWC"), feature_group_count=c)
    return y + bias


def ssd_chunked(x, dt, A, Bm, Cm):
    b, T, g, r, p = x.shape
    n = Bm.shape[-1]
    c = T // CHUNK
    f32 = jnp.float32
    xdt = (x.astype(f32) * dt[..., None]).reshape(b, c, CHUNK, g, r, p)
    a_cs = jnp.cumsum((dt * A).reshape(b, c, CHUNK, g, r), axis=2)
    Bc = Bm.astype(f32).reshape(b, c, CHUNK, g, n)
    Cc = Cm.astype(f32).reshape(b, c, CHUNK, g, n)
    mask = jnp.tril(jnp.ones((CHUNK, CHUNK), dtype=bool))[None, None, :, :, None, None]
    seg = a_cs[:, :, :, None] - a_cs[:, :, None, :]
    decay = jnp.exp(jnp.where(mask, seg, -jnp.inf))
    scores = jnp.einsum("bclgn,bcsgn->bclsg", Cc, Bc)
    y_diag = jnp.einsum("bclsg,bclsgr,bcsgrp->bclgrp", scores, decay, xdt)
    decay_to_end = jnp.exp(a_cs[:, :, -1:] - a_cs)
    states = jnp.einsum("bclgn,bclgr,bclgrp->bcgrpn", Bc, decay_to_end, xdt)
    chunk_decay = jnp.exp(a_cs[:, :, -1])

    def step(h, inp):
        s_c, d_c = inp
        return h * d_c[..., None, None] + s_c, h

    h0 = jnp.zeros((b, g, r, p, n), f32)
    _, prev = lax.scan(step, h0, (jnp.moveaxis(states, 1, 0), jnp.moveaxis(chunk_decay, 1, 0)))
    prev = jnp.moveaxis(prev, 0, 1)
    y_off = jnp.einsum("bclgn,bcgrpn,bclgr->bclgrp", Cc, prev, jnp.exp(a_cs))
    return (y_diag + y_off).reshape(b, T, g, r, p)


def ssd_mixer(u, in_proj, conv_w, conv_b, dt_bias, a_log, d_skip, norm_w, out_proj):
    b, T, _ = u.shape
    f32 = jnp.float32
    zxbcdt = u @ in_proj
    z = zxbcdt[..., :D_INNER]
    xbc = zxbcdt[..., D_INNER:D_INNER + D_XBC]
    dt_raw = zxbcdt[..., D_INNER + D_XBC:]
    xbc = jax.nn.silu(centred_depthwise_conv(xbc, conv_w, conv_b))
    xs = xbc[..., :D_INNER].reshape(b, T, N_GROUPS, HEADS_PER_GROUP, HEAD_DIM)
    Bm = xbc[..., D_INNER:D_INNER + D_BC].reshape(b, T, N_GROUPS, D_STATE)
    Cm = xbc[..., D_INNER + D_BC:].reshape(b, T, N_GROUPS, D_STATE)
    dt = jax.nn.softplus(dt_raw.astype(f32).reshape(b, T, 2, N_GROUPS, HEADS_PER_GROUP)
                         + dt_bias.astype(f32).reshape(2, N_GROUPS, HEADS_PER_GROUP))
    A = -jnp.exp(a_log.astype(f32)).reshape(2, N_GROUPS, HEADS_PER_GROUP)
    flip = lambda t: jnp.flip(t, axis=1)
    y_fwd = ssd_chunked(xs, dt[:, :, 0], A[0], Bm, Cm)
    y_bwd = flip(ssd_chunked(flip(xs), flip(dt[:, :, 1]), A[1], flip(Bm), flip(Cm)))
    y = y_fwd + y_bwd + xs.astype(f32) * d_skip.astype(f32).reshape(N_GROUPS, HEADS_PER_GROUP)[..., None]
    gy = (y.reshape(b, T, D_INNER) * jax.nn.silu(z.astype(f32))).reshape(b, T, N_GROUPS, D_INNER // N_GROUPS)
    gy = gy * lax.rsqrt(jnp.mean(jnp.square(gy), axis=-1, keepdims=True) + RMS_EPS)
    gy = gy.reshape(b, T, D_INNER) * norm_w.astype(f32)
    return gy.astype(u.dtype) @ out_proj


def pool_mixer(u, w, bias, scale):
    b, T, _ = u.shape
    uf = u.astype(jnp.float32)
    cs = jnp.concatenate([jnp.zeros((b, 1, D_MODEL), jnp.float32), jnp.cumsum(uf, axis=1)], axis=1)
    t = jnp.arange(T)
    groups = []
    for gi, win in enumerate(POOL_WINDOWS):
        lo_c, hi_c = gi * POOL_GROUP_DIM, (gi + 1) * POOL_GROUP_DIM
        start = t - win // 2
        lo = jnp.clip(start, 0, T)
        hi = jnp.clip(start + win, 0, T)
        csg = cs[..., lo_c:hi_c]
        wsum = jnp.take(csg, hi, axis=1) - jnp.take(csg, lo, axis=1)
        cnt = (hi - lo).astype(jnp.float32)[None, :, None]
        groups.append(wsum / cnt - uf[..., lo_c:hi_c])
    m = jnp.stack(groups, axis=2).astype(u.dtype)
    y = jnp.einsum("btgc,gcd->btgd", m, w) + bias
    return y.reshape(b, T, D_MODEL) * scale


def sq_relu_mlp(u, w1, w2):
    return jnp.square(jax.nn.relu(u @ w1)) @ w2


def _fwd_setup_inputs(seed: int = 0) -> dict:
    key = jax.random.key(seed)
    ks = jax.random.split(key, 20)
    f32 = jnp.float32
    nrm = lambda k, shape: jax.random.normal(k, shape, f32)
    x = nrm(ks[0], (BATCH, SEQ, D_MODEL))
    ssd_in_proj = nrm(ks[1], (N_SSD_LAYERS, D_MODEL, D_IN_PROJ)) * D_MODEL ** -0.5
    ssd_conv_w = nrm(ks[2], (N_SSD_LAYERS, CONV_WIDTH, 1, D_XBC)) * CONV_WIDTH ** -0.5
    ssd_conv_b = 0.01 * nrm(ks[3], (N_SSD_LAYERS, D_XBC))
    u_dt = jax.random.uniform(ks[4], (N_SSD_LAYERS, 2, N_HEADS), f32)
    dt0 = jnp.exp(u_dt * (math.log(DT_MAX) - math.log(DT_MIN)) + math.log(DT_MIN))
    dt0 = jnp.maximum(dt0, 1e-4)
    ssd_dt_bias = dt0 + jnp.log(-jnp.expm1(-dt0))
    ssd_A_log = jnp.log(jax.random.uniform(ks[5], (N_SSD_LAYERS, 2, N_HEADS), f32, A_INIT_MIN, A_INIT_MAX))
    ssd_D = 1.0 + 0.1 * nrm(ks[6], (N_SSD_LAYERS, N_HEADS))
    ssd_norm_w = 1.0 + 0.1 * nrm(ks[7], (N_SSD_LAYERS, D_INNER))
    ssd_out_proj = nrm(ks[8], (N_SSD_LAYERS, D_INNER, D_MODEL)) * (D_INNER ** -0.5 * DEEPNORM_BETA)
    pool_w = nrm(ks[9], (N_POOL_LAYERS, N_POOL_GROUPS, POOL_GROUP_DIM, POOL_GROUP_DIM)) * (POOL_GROUP_DIM ** -0.5 * DEEPNORM_BETA)
    pool_b = 0.01 * nrm(ks[10], (N_POOL_LAYERS, N_POOL_GROUPS, POOL_GROUP_DIM))
    pool_scale = 1.0 + 0.1 * nrm(ks[11], (N_POOL_LAYERS, D_MODEL))
    mlp_w1 = nrm(ks[12], (DEPTH, D_MODEL, D_FF)) * D_MODEL ** -0.5
    mlp_w2 = nrm(ks[13], (DEPTH, D_FF, D_MODEL)) * (D_FF ** -0.5 * DEEPNORM_BETA)
    ln_mix_g = 1.0 + 0.1 * nrm(ks[14], (DEPTH, D_MODEL))
    ln_mix_b = 0.01 * nrm(ks[15], (DEPTH, D_MODEL))
    ln_ffn_g = 1.0 + 0.1 * nrm(ks[16], (DEPTH, D_MODEL))
    ln_ffn_b = 0.01 * nrm(ks[17], (DEPTH, D_MODEL))
    return {"x": x, "ssd_in_proj": ssd_in_proj, "ssd_conv_w": ssd_conv_w, "ssd_conv_b": ssd_conv_b,
            "ssd_dt_bias": ssd_dt_bias, "ssd_A_log": ssd_A_log, "ssd_D": ssd_D, "ssd_norm_w": ssd_norm_w,
            "ssd_out_proj": ssd_out_proj, "pool_w": pool_w, "pool_b": pool_b, "pool_scale": pool_scale,
            "mlp_w1": mlp_w1, "mlp_w2": mlp_w2, "ln_mix_g": ln_mix_g, "ln_mix_b": ln_mix_b,
            "ln_ffn_g": ln_ffn_g, "ln_ffn_b": ln_ffn_b}


def _fwd_reference(x, ssd_in_proj, ssd_conv_w, ssd_conv_b, ssd_dt_bias, ssd_A_log, ssd_D, ssd_norm_w,
              ssd_out_proj, pool_w, pool_b, pool_scale, mlp_w1, mlp_w2, ln_mix_g, ln_mix_b,
              ln_ffn_g, ln_ffn_b):
    for i in range(DEPTH):
        j = i // N_MIXERS
        if i % N_MIXERS == 0:
            mix = ssd_mixer(x, ssd_in_proj[j], ssd_conv_w[j], ssd_conv_b[j], ssd_dt_bias[j],
                            ssd_A_log[j], ssd_D[j], ssd_norm_w[j], ssd_out_proj[j])
        else:
            mix = pool_mixer(x, pool_w[j], pool_b[j], pool_scale[j])
        x = layer_norm(DEEPNORM_ALPHA * x + mix, ln_mix_g[i], ln_mix_b[i])
        x = layer_norm(DEEPNORM_ALPHA * x + sq_relu_mlp(x, mlp_w1[i], mlp_w2[i]), ln_ffn_g[i], ln_ffn_b[i])
    return x


import jax as _jax
import jax.numpy as _jnp

TWIN_FORMAT = 'train_step'
FWD_PARAMS = ['x', 'ssd_in_proj', 'ssd_conv_w', 'ssd_conv_b', 'ssd_dt_bias', 'ssd_A_log', 'ssd_D', 'ssd_norm_w', 'ssd_out_proj', 'pool_w', 'pool_b', 'pool_scale', 'mlp_w1', 'mlp_w2', 'ln_mix_g', 'ln_mix_b', 'ln_ffn_g', 'ln_ffn_b']
TWIN_WEIGHTS = ['ssd_in_proj', 'ssd_conv_w', 'ssd_conv_b', 'ssd_dt_bias', 'ssd_A_log', 'ssd_D', 'ssd_norm_w', 'ssd_out_proj', 'pool_w', 'pool_b', 'pool_scale', 'mlp_w1', 'mlp_w2', 'ln_mix_g', 'ln_mix_b', 'ln_ffn_g', 'ln_ffn_b']
TWIN_DIFF_INPUT = 'x'
TWIN_INPUTS = ['x', 'ssd_in_proj', 'ssd_conv_w', 'ssd_conv_b', 'ssd_dt_bias', 'ssd_A_log', 'ssd_D', 'ssd_norm_w', 'ssd_out_proj', 'pool_w', 'pool_b', 'pool_scale', 'mlp_w1', 'mlp_w2', 'ln_mix_g', 'ln_mix_b', 'ln_ffn_g', 'ln_ffn_b', 'loss_target', 'm_ssd_in_proj', 'm_ssd_conv_w', 'm_ssd_conv_b', 'm_ssd_dt_bias', 'm_ssd_A_log', 'm_ssd_D', 'm_ssd_norm_w', 'm_ssd_out_proj', 'm_pool_w', 'm_pool_b', 'm_pool_scale', 'm_mlp_w1', 'm_mlp_w2', 'm_ln_mix_g', 'm_ln_mix_b', 'm_ln_ffn_g', 'm_ln_ffn_b', 'v_ssd_in_proj', 'v_ssd_conv_w', 'v_ssd_conv_b', 'v_ssd_dt_bias', 'v_ssd_A_log', 'v_ssd_D', 'v_ssd_norm_w', 'v_ssd_out_proj', 'v_pool_w', 'v_pool_b', 'v_pool_scale', 'v_mlp_w1', 'v_mlp_w2', 'v_ln_mix_g', 'v_ln_mix_b', 'v_ln_ffn_g', 'v_ln_ffn_b']
TWIN_OUTPUTS = ['loss', 'grad_x', 'grad_ssd_in_proj', 'grad_ssd_conv_w', 'grad_ssd_conv_b', 'grad_ssd_dt_bias', 'grad_ssd_A_log', 'grad_ssd_D', 'grad_ssd_norm_w', 'grad_ssd_out_proj', 'grad_pool_w', 'grad_pool_b', 'grad_pool_scale', 'grad_mlp_w1', 'grad_mlp_w2', 'grad_ln_mix_g', 'grad_ln_mix_b', 'grad_ln_ffn_g', 'grad_ln_ffn_b', 'delta_ssd_in_proj', 'delta_ssd_conv_w', 'delta_ssd_conv_b', 'delta_ssd_dt_bias', 'delta_ssd_A_log', 'delta_ssd_D', 'delta_ssd_norm_w', 'delta_ssd_out_proj', 'delta_pool_w', 'delta_pool_b', 'delta_pool_scale', 'delta_mlp_w1', 'delta_mlp_w2', 'delta_ln_mix_g', 'delta_ln_mix_b', 'delta_ln_ffn_g', 'delta_ln_ffn_b', 'new_m_ssd_in_proj', 'new_m_ssd_conv_w', 'new_m_ssd_conv_b', 'new_m_ssd_dt_bias', 'new_m_ssd_A_log', 'new_m_ssd_D', 'new_m_ssd_norm_w', 'new_m_ssd_out_proj', 'new_m_pool_w', 'new_m_pool_b', 'new_m_pool_scale', 'new_m_mlp_w1', 'new_m_mlp_w2', 'new_m_ln_mix_g', 'new_m_ln_mix_b', 'new_m_ln_ffn_g', 'new_m_ln_ffn_b', 'new_v_ssd_in_proj', 'new_v_ssd_conv_w', 'new_v_ssd_conv_b', 'new_v_ssd_dt_bias', 'new_v_ssd_A_log', 'new_v_ssd_D', 'new_v_ssd_norm_w', 'new_v_ssd_out_proj', 'new_v_pool_w', 'new_v_pool_b', 'new_v_pool_scale', 'new_v_mlp_w1', 'new_v_mlp_w2', 'new_v_ln_mix_g', 'new_v_ln_mix_b', 'new_v_ln_ffn_g', 'new_v_ln_ffn_b']
TWIN_LEAF_KINDS = {'loss': 'loss', 'grad_x': 'grad_x', 'grad_ssd_in_proj': 'grad_w', 'grad_ssd_conv_w': 'grad_w', 'grad_ssd_conv_b': 'grad_w', 'grad_ssd_dt_bias': 'grad_w', 'grad_ssd_A_log': 'grad_w', 'grad_ssd_D': 'grad_w', 'grad_ssd_norm_w': 'grad_w', 'grad_ssd_out_proj': 'grad_w', 'grad_pool_w': 'grad_w', 'grad_pool_b': 'grad_w', 'grad_pool_scale': 'grad_w', 'grad_mlp_w1': 'grad_w', 'grad_mlp_w2': 'grad_w', 'grad_ln_mix_g': 'grad_w', 'grad_ln_mix_b': 'grad_w', 'grad_ln_ffn_g': 'grad_w', 'grad_ln_ffn_b': 'grad_w', 'delta_ssd_in_proj': 'delta_w', 'delta_ssd_conv_w': 'delta_w', 'delta_ssd_conv_b': 'delta_w', 'delta_ssd_dt_bias': 'delta_w', 'delta_ssd_A_log': 'delta_w', 'delta_ssd_D': 'delta_w', 'delta_ssd_norm_w': 'delta_w', 'delta_ssd_out_proj': 'delta_w', 'delta_pool_w': 'delta_w', 'delta_pool_b': 'delta_w', 'delta_pool_scale': 'delta_w', 'delta_mlp_w1': 'delta_w', 'delta_mlp_w2': 'delta_w', 'delta_ln_mix_g': 'delta_w', 'delta_ln_mix_b': 'delta_w', 'delta_ln_ffn_g': 'delta_w', 'delta_ln_ffn_b': 'delta_w', 'new_m_ssd_in_proj': 'new_m', 'new_m_ssd_conv_w': 'new_m', 'new_m_ssd_conv_b': 'new_m', 'new_m_ssd_dt_bias': 'new_m', 'new_m_ssd_A_log': 'new_m', 'new_m_ssd_D': 'new_m', 'new_m_ssd_norm_w': 'new_m', 'new_m_ssd_out_proj': 'new_m', 'new_m_pool_w': 'new_m', 'new_m_pool_b': 'new_m', 'new_m_pool_scale': 'new_m', 'new_m_mlp_w1': 'new_m', 'new_m_mlp_w2': 'new_m', 'new_m_ln_mix_g': 'new_m', 'new_m_ln_mix_b': 'new_m', 'new_m_ln_ffn_g': 'new_m', 'new_m_ln_ffn_b': 'new_m', 'new_v_ssd_in_proj': 'new_v', 'new_v_ssd_conv_w': 'new_v', 'new_v_ssd_conv_b': 'new_v', 'new_v_ssd_dt_bias': 'new_v', 'new_v_ssd_A_log': 'new_v', 'new_v_ssd_D': 'new_v', 'new_v_ssd_norm_w': 'new_v', 'new_v_ssd_out_proj': 'new_v', 'new_v_pool_w': 'new_v', 'new_v_pool_b': 'new_v', 'new_v_pool_scale': 'new_v', 'new_v_mlp_w1': 'new_v', 'new_v_mlp_w2': 'new_v', 'new_v_ln_mix_g': 'new_v', 'new_v_ln_mix_b': 'new_v', 'new_v_ln_ffn_g': 'new_v', 'new_v_ln_ffn_b': 'new_v'}


def _forward(args):
    return _fwd_reference(*[args[k] for k in FWD_PARAMS])


def _output_shape():
    def fwd():
        inp = _fwd_setup_inputs(0)
        return _fwd_reference(*[inp[k] for k in FWD_PARAMS])
    out = _jax.eval_shape(fwd)
    return out.shape, out.dtype

N_MICROBATCH = 1
ADAM_LR = 0.001
ADAM_B1 = 0.9
ADAM_B2 = 0.999
ADAM_EPS = 1e-08
ADAM_WD = 0.01
ADAM_STEP = 10
PER_EXAMPLE_BATCH_AXIS = {'x': 0, 'loss_target': 0}
SHARED_INPUTS = []
_WEIGHT_DTYPES = {'ssd_in_proj': _jnp.float32, 'ssd_conv_w': _jnp.float32, 'ssd_conv_b': _jnp.float32, 'ssd_dt_bias': _jnp.float32, 'ssd_A_log': _jnp.float32, 'ssd_D': _jnp.float32, 'ssd_norm_w': _jnp.float32, 'ssd_out_proj': _jnp.float32, 'pool_w': _jnp.float32, 'pool_b': _jnp.float32, 'pool_scale': _jnp.float32, 'mlp_w1': _jnp.float32, 'mlp_w2': _jnp.float32, 'ln_mix_g': _jnp.float32, 'ln_mix_b': _jnp.float32, 'ln_ffn_g': _jnp.float32, 'ln_ffn_b': _jnp.float32}
MOMENT_SCALE = {'ssd_in_proj': 1.621056e-02, 'ssd_conv_w': 1.773368e-02, 'ssd_conv_b': 4.696991e-02, 'ssd_dt_bias': 2.733077e-02, 'ssd_A_log': 7.607931e-02, 'ssd_D': 8.344057e-02, 'ssd_norm_w': 2.905463e-02, 'ssd_out_proj': 9.429389e-02, 'pool_w': 6.547015e-02, 'pool_b': 4.088914e-01, 'pool_scale': 1.198942e-01, 'mlp_w1': 1.879688e-02, 'mlp_w2': 1.961283e-01, 'ln_mix_g': 2.698685e+00, 'ln_mix_b': 6.872708e-01, 'ln_ffn_g': 9.079445e+00, 'ln_ffn_b': 1.999578e+00}


def _to_microbatches(a, axis):
    t = _jnp.moveaxis(a, axis, 0)
    t = t.reshape((N_MICROBATCH, t.shape[0] // N_MICROBATCH) + t.shape[1:])
    return _jnp.moveaxis(t, 1, axis + 1)


def setup_inputs(seed: int = 0) -> dict:
    inp = _fwd_setup_inputs(seed)
    key = _jax.random.fold_in(_jax.random.key(seed), 7919)
    shape, _ = _output_shape()
    out = dict(inp)
    out["loss_target"] = _jax.random.normal(_jax.random.fold_in(key, 0), shape, _jnp.float32)
    for i, name in enumerate(TWIN_WEIGHTS):
        w = inp[name].astype(_jnp.float32)
        if MOMENT_SCALE is None:
            s = _jnp.sqrt(_jnp.mean(_jnp.square(w)) + 1e-30)
        else:
            s = MOMENT_SCALE[name]
        km, kv = _jax.random.split(_jax.random.fold_in(key, i + 1))
        out[name] = w
        out["m_" + name] = s * _jax.random.normal(km, w.shape, _jnp.float32)
        out["v_" + name] = (s * s) * _jax.random.uniform(kv, w.shape, _jnp.float32, 0.5, 1.5)
    if N_MICROBATCH > 1:
        for name, axis in PER_EXAMPLE_BATCH_AXIS.items():
            out[name] = _to_microbatches(out[name], axis)
    return {'x': out['x'], 'ssd_in_proj': out['ssd_in_proj'], 'ssd_conv_w': out['ssd_conv_w'], 'ssd_conv_b': out['ssd_conv_b'], 'ssd_dt_bias': out['ssd_dt_bias'], 'ssd_A_log': out['ssd_A_log'], 'ssd_D': out['ssd_D'], 'ssd_norm_w': out['ssd_norm_w'], 'ssd_out_proj': out['ssd_out_proj'], 'pool_w': out['pool_w'], 'pool_b': out['pool_b'], 'pool_scale': out['pool_scale'], 'mlp_w1': out['mlp_w1'], 'mlp_w2': out['mlp_w2'], 'ln_mix_g': out['ln_mix_g'], 'ln_mix_b': out['ln_mix_b'], 'ln_ffn_g': out['ln_ffn_g'], 'ln_ffn_b': out['ln_ffn_b'], 'loss_target': out['loss_target'], 'm_ssd_in_proj': out['m_ssd_in_proj'], 'm_ssd_conv_w': out['m_ssd_conv_w'], 'm_ssd_conv_b': out['m_ssd_conv_b'], 'm_ssd_dt_bias': out['m_ssd_dt_bias'], 'm_ssd_A_log': out['m_ssd_A_log'], 'm_ssd_D': out['m_ssd_D'], 'm_ssd_norm_w': out['m_ssd_norm_w'], 'm_ssd_out_proj': out['m_ssd_out_proj'], 'm_pool_w': out['m_pool_w'], 'm_pool_b': out['m_pool_b'], 'm_pool_scale': out['m_pool_scale'], 'm_mlp_w1': out['m_mlp_w1'], 'm_mlp_w2': out['m_mlp_w2'], 'm_ln_mix_g': out['m_ln_mix_g'], 'm_ln_mix_b': out['m_ln_mix_b'], 'm_ln_ffn_g': out['m_ln_ffn_g'], 'm_ln_ffn_b': out['m_ln_ffn_b'], 'v_ssd_in_proj': out['v_ssd_in_proj'], 'v_ssd_conv_w': out['v_ssd_conv_w'], 'v_ssd_conv_b': out['v_ssd_conv_b'], 'v_ssd_dt_bias': out['v_ssd_dt_bias'], 'v_ssd_A_log': out['v_ssd_A_log'], 'v_ssd_D': out['v_ssd_D'], 'v_ssd_norm_w': out['v_ssd_norm_w'], 'v_ssd_out_proj': out['v_ssd_out_proj'], 'v_pool_w': out['v_pool_w'], 'v_pool_b': out['v_pool_b'], 'v_pool_scale': out['v_pool_scale'], 'v_mlp_w1': out['v_mlp_w1'], 'v_mlp_w2': out['v_mlp_w2'], 'v_ln_mix_g': out['v_ln_mix_g'], 'v_ln_mix_b': out['v_ln_mix_b'], 'v_ln_ffn_g': out['v_ln_ffn_g'], 'v_ln_ffn_b': out['v_ln_ffn_b']}


def _loss(weights, diff, rest, loss_target):
    with _jax.named_scope("forward"):
        args = {**rest, TWIN_DIFF_INPUT: diff, **{k: w.astype(_WEIGHT_DTYPES[k]) for k, w in weights.items()}}
        y = _forward(args)
    with _jax.named_scope("loss_head"):
        err = _jnp.square(y.astype(_jnp.float32) - loss_target)
        return 0.5 * _jnp.sum(_jnp.mean(err, axis=-1)) if err.ndim else 0.5 * err


def _adamw(w, g, m, v):
    m = ADAM_B1 * m + (1.0 - ADAM_B1) * g
    v = ADAM_B2 * v + (1.0 - ADAM_B2) * _jnp.square(g)
    m_hat = m / (1.0 - ADAM_B1 ** ADAM_STEP)
    v_hat = v / (1.0 - ADAM_B2 ** ADAM_STEP)
    delta = -ADAM_LR * (m_hat / (_jnp.sqrt(v_hat) + ADAM_EPS) + ADAM_WD * w)
    return delta, m, v


def reference(x, ssd_in_proj, ssd_conv_w, ssd_conv_b, ssd_dt_bias, ssd_A_log, ssd_D, ssd_norm_w, ssd_out_proj, pool_w, pool_b, pool_scale, mlp_w1, mlp_w2, ln_mix_g, ln_mix_b, ln_ffn_g, ln_ffn_b, loss_target, m_ssd_in_proj, m_ssd_conv_w, m_ssd_conv_b, m_ssd_dt_bias, m_ssd_A_log, m_ssd_D, m_ssd_norm_w, m_ssd_out_proj, m_pool_w, m_pool_b, m_pool_scale, m_mlp_w1, m_mlp_w2, m_ln_mix_g, m_ln_mix_b, m_ln_ffn_g, m_ln_ffn_b, v_ssd_in_proj, v_ssd_conv_w, v_ssd_conv_b, v_ssd_dt_bias, v_ssd_A_log, v_ssd_D, v_ssd_norm_w, v_ssd_out_proj, v_pool_w, v_pool_b, v_pool_scale, v_mlp_w1, v_mlp_w2, v_ln_mix_g, v_ln_mix_b, v_ln_ffn_g, v_ln_ffn_b):
    given = dict(x=x, ssd_in_proj=ssd_in_proj, ssd_conv_w=ssd_conv_w, ssd_conv_b=ssd_conv_b, ssd_dt_bias=ssd_dt_bias, ssd_A_log=ssd_A_log, ssd_D=ssd_D, ssd_norm_w=ssd_norm_w, ssd_out_proj=ssd_out_proj, pool_w=pool_w, pool_b=pool_b, pool_scale=pool_scale, mlp_w1=mlp_w1, mlp_w2=mlp_w2, ln_mix_g=ln_mix_g, ln_mix_b=ln_mix_b, ln_ffn_g=ln_ffn_g, ln_ffn_b=ln_ffn_b, loss_target=loss_target, m_ssd_in_proj=m_ssd_in_proj, m_ssd_conv_w=m_ssd_conv_w, m_ssd_conv_b=m_ssd_conv_b, m_ssd_dt_bias=m_ssd_dt_bias, m_ssd_A_log=m_ssd_A_log, m_ssd_D=m_ssd_D, m_ssd_norm_w=m_ssd_norm_w, m_ssd_out_proj=m_ssd_out_proj, m_pool_w=m_pool_w, m_pool_b=m_pool_b, m_pool_scale=m_pool_scale, m_mlp_w1=m_mlp_w1, m_mlp_w2=m_mlp_w2, m_ln_mix_g=m_ln_mix_g, m_ln_mix_b=m_ln_mix_b, m_ln_ffn_g=m_ln_ffn_g, m_ln_ffn_b=m_ln_ffn_b, v_ssd_in_proj=v_ssd_in_proj, v_ssd_conv_w=v_ssd_conv_w, v_ssd_conv_b=v_ssd_conv_b, v_ssd_dt_bias=v_ssd_dt_bias, v_ssd_A_log=v_ssd_A_log, v_ssd_D=v_ssd_D, v_ssd_norm_w=v_ssd_norm_w, v_ssd_out_proj=v_ssd_out_proj, v_pool_w=v_pool_w, v_pool_b=v_pool_b, v_pool_scale=v_pool_scale, v_mlp_w1=v_mlp_w1, v_mlp_w2=v_mlp_w2, v_ln_mix_g=v_ln_mix_g, v_ln_mix_b=v_ln_mix_b, v_ln_ffn_g=v_ln_ffn_g, v_ln_ffn_b=v_ln_ffn_b)
    weights = {n: given[n] for n in TWIN_WEIGHTS}
    shared = {n: given[n] for n in SHARED_INPUTS}
    per_example = {n: given[n] for n in ['x']}
    grad_fn = _jax.value_and_grad(_loss, argnums=(0, 1))

    def one_microbatch(ex, loss_target):
        ex = dict(ex)
        diff = ex.pop(TWIN_DIFF_INPUT)
        return grad_fn(weights, diff, {**shared, **ex}, loss_target)

    if N_MICROBATCH == 1:
        loss, (grad_w, grad_x) = one_microbatch(per_example, given["loss_target"])
    else:
        def body(carry, xs):
            loss_sum, grad_sum = carry
            l_k, (gw_k, gx_k) = one_microbatch(xs[0], xs[1])
            with _jax.named_scope("update"):
                return (loss_sum + l_k, _jax.tree.map(_jnp.add, grad_sum, gw_k)), gx_k

        init = (_jnp.zeros((), _jnp.float32), _jax.tree.map(_jnp.zeros_like, weights))
        (loss, grad_w), grad_x = _jax.lax.scan(body, init, (per_example, given["loss_target"]))
    with _jax.named_scope("update"):
        delta_w, new_m, new_v = {}, {}, {}
        for n in TWIN_WEIGHTS:
            delta_w[n], new_m[n], new_v[n] = _adamw(weights[n], grad_w[n], given["m_" + n], given["v_" + n])
    return (loss, grad_x, *[grad_w[n] for n in TWIN_WEIGHTS], *[delta_w[n] for n in TWIN_WEIGHTS],
            *[new_m[n] for n in TWIN_WEIGHTS], *[new_v[n] for n in TWIN_WEIGHTS])
```

```python
import functools
import math

import jax
import jax.numpy as jnp
from jax import lax
from jax.experimental import pallas as pl
from jax.experimental.pallas import tpu as pltpu

F32 = jnp.float32
BF16 = jnp.bfloat16

N_DEV = 8
HEAD_DIM = 64
N_GROUPS = 8
D_STATE = 128
CHUNK = 128
CONV_WIDTH = 5
POOL_WINDOWS = (2, 4, 8, 16)
HALO = 8
LN_EPS = 1e-5
RMS_EPS = 1e-5
ADAM_LR = 0.001
ADAM_B1 = 0.9
ADAM_B2 = 0.999
ADAM_EPS = 1e-08
ADAM_WD = 0.01
ADAM_STEP = 10
LANES = 128
VMEM_LIMIT_BYTES = 56 * 1024 * 1024
HIGHEST = lax.Precision.HIGHEST


def _call(body, **kw):
    return pl.pallas_call(body, **kw)


def _params(*sem):
    return pltpu.CompilerParams(dimension_semantics=sem, vmem_limit_bytes=VMEM_LIMIT_BYTES)


def _tile(dim, target, align=LANES):
    if dim <= target:
        return dim
    t = (target // align) * align
    while t >= align:
        if dim % t == 0:
            return t
        t -= align
    return dim


def _dot(a, b, dims, precision=None):
    return lax.dot_general(a, b, (dims, ((), ())), precision=precision, preferred_element_type=F32)


NN = ((1,), (0,))
NT = ((1,), (1,))
TN = ((0,), (0,))


def _sigmoid(x):
    return 1.0 / (1.0 + jnp.exp(-x))


def _matmul(a, b, mode, *, name, outs, epilogue=None, extras=(), tm=1024, tn=1024, tk=512):
    if mode == "nn":
        (M, K), (K2, N) = a.shape, b.shape
    elif mode == "nt":
        (M, K), (N, K2) = a.shape, b.shape
    else:
        (K, M), (K2, N) = a.shape, b.shape
    assert K == K2, (a.shape, b.shape, mode)
    tm, tn, tk = _tile(M, tm), _tile(N, tn), _tile(K, tk)
    nk = K // tk
    dims = {"nn": NN, "nt": NT, "tn": TN}[mode]
    a_spec = (pl.BlockSpec((tk, tm), lambda i, j, k: (k, i)) if mode == "tn"
              else pl.BlockSpec((tm, tk), lambda i, j, k: (i, k)))
    b_spec = (pl.BlockSpec((tn, tk), lambda i, j, k: (j, k)) if mode == "nt"
              else pl.BlockSpec((tk, tn), lambda i, j, k: (k, j)))
    mn_spec = pl.BlockSpec((tm, tn), lambda i, j, k: (i, j))
    n_extra, n_out = len(extras), len(outs)

    def body(*refs):
        a_ref, b_ref = refs[0], refs[1]
        extra_refs = refs[2:2 + n_extra]
        out_refs = refs[2 + n_extra:2 + n_extra + n_out]
        acc_ref = refs[-1]
        k = pl.program_id(2)

        @pl.when(k == 0)
        def _():
            acc_ref[...] = jnp.zeros_like(acc_ref)

        acc_ref[...] += _dot(a_ref[...].astype(BF16), b_ref[...].astype(BF16), dims)

        @pl.when(k == nk - 1)
        def _():
            acc = acc_ref[...]
            res = (acc,) if epilogue is None else epilogue(acc, *[r[...] for r in extra_refs])
            for o_ref, r in zip(out_refs, res):
                o_ref[...] = r.astype(o_ref.dtype)

    res = _call(
        body, name=name, grid=(M // tm, N // tn, nk),
        in_specs=[a_spec, b_spec] + [mn_spec] * n_extra,
        out_specs=[mn_spec] * n_out,
        out_shape=[jax.ShapeDtypeStruct((M, N), dt) for dt in outs],
        scratch_shapes=[pltpu.VMEM((tm, tn), F32)],
        compiler_params=_params("parallel", "parallel", "arbitrary"),
    )(a, b, *extras)
    return res


def _ln_fwd(x, f, g, b, alpha, name):
    T, D = x.shape
    tm = _tile(T, 256, 8)

    def body(x_ref, f_ref, g_ref, b_ref, y_ref):
        s = alpha * x_ref[...] + f_ref[...]
        mu = jnp.mean(s, axis=-1, keepdims=True)
        d = s - mu
        var = jnp.mean(d * d, axis=-1, keepdims=True)
        y_ref[...] = d * lax.rsqrt(var + LN_EPS) * g_ref[...] + b_ref[...]

    row = pl.BlockSpec((tm, D), lambda i: (i, 0))
    vec = pl.BlockSpec((1, D), lambda i: (0, 0))
    return _call(body, name=name, grid=(T // tm,), in_specs=[row, row, vec, vec], out_specs=row,
                 out_shape=jax.ShapeDtypeStruct((T, D), F32), compiler_params=_params("parallel"))(
                     x, f, g.reshape(1, D), b.reshape(1, D))


def _ln_bwd(x, f, g, dy, alpha, name):
    T, D = x.shape
    tm = _tile(T, 256, 8)

    def body(x_ref, f_ref, g_ref, dy_ref, ds_ref, dg_ref, db_ref):
        i = pl.program_id(0)

        @pl.when(i == 0)
        def _():
            dg_ref[...] = jnp.zeros_like(dg_ref)
            db_ref[...] = jnp.zeros_like(db_ref)

        s = alpha * x_ref[...] + f_ref[...]
        mu = jnp.mean(s, axis=-1, keepdims=True)
        d = s - mu
        var = jnp.mean(d * d, axis=-1, keepdims=True)
        rstd = lax.rsqrt(var + LN_EPS)
        xhat = d * rstd
        dy_ = dy_ref[...]
        dg_ref[...] += jnp.sum(dy_ * xhat, axis=0, keepdims=True)
        db_ref[...] += jnp.sum(dy_, axis=0, keepdims=True)
        dxh = dy_ * g_ref[...]
        m1 = jnp.mean(dxh, axis=-1, keepdims=True)
        m2 = jnp.mean(dxh * xhat, axis=-1, keepdims=True)
        ds_ref[...] = rstd * (dxh - m1 - xhat * m2)

    row = pl.BlockSpec((tm, D), lambda i: (i, 0))
    vec = pl.BlockSpec((1, D), lambda i: (0, 0))
    return _call(body, name=name, grid=(T // tm,), in_specs=[row, row, vec, row], out_specs=[row, vec, vec],
                 out_shape=[jax.ShapeDtypeStruct((T, D), F32), jax.ShapeDtypeStruct((1, D), F32),
                            jax.ShapeDtypeStruct((1, D), F32)],
                 compiler_params=_params("arbitrary"))(x, f, g.reshape(1, D), dy)


def _loss_head(y, target):
    T, D = y.shape
    tm = _tile(T, 256, 8)

    def body(y_ref, t_ref, loss_ref, dy_ref):
        i = pl.program_id(0)

        @pl.when(i == 0)
        def _():
            loss_ref[...] = jnp.zeros_like(loss_ref)

        err = y_ref[...] - t_ref[...]
        dy_ref[...] = err * (1.0 / D)
        per_tok = jnp.mean(err * err, axis=-1, keepdims=True)
        loss_ref[...] += 0.5 * jnp.sum(per_tok)

    row = pl.BlockSpec((tm, D), lambda i: (i, 0))
    return _call(body, name="loss_head", grid=(T // tm,), in_specs=[row, row],
                 out_specs=[pl.BlockSpec((1, LANES), lambda i: (0, 0)), row],
                 out_shape=[jax.ShapeDtypeStruct((1, LANES), F32), jax.ShapeDtypeStruct((T, D), F32)],
                 compiler_params=_params("arbitrary"))(y, target)


def _halo_specs(tt, cw, col_of, n_tiles, grid_rank_tokens_axis):
    per = tt // HALO
    ax = grid_rank_tokens_axis

    def cur(*g):
        return (g[ax], col_of(*g))

    def prev(*g):
        return (jnp.maximum(g[ax] * per - 1, 0), col_of(*g))

    def nxt(*g):
        return (jnp.minimum((g[ax] + 1) * per, n_tiles * per - 1), col_of(*g))

    return [pl.BlockSpec((tt, cw), cur), pl.BlockSpec((HALO, cw), prev), pl.BlockSpec((HALO, cw), nxt)]


def _fill_ext(ext_ref, cur_ref, prev_ref, next_ref, i, n_tiles, tt):
    ext_ref[pl.ds(0, HALO), :] = jnp.where(i > 0, prev_ref[...], 0.0)
    ext_ref[pl.ds(HALO, tt), :] = cur_ref[...]
    ext_ref[pl.ds(HALO + tt, HALO), :] = jnp.where(i < n_tiles - 1, next_ref[...], 0.0)


def _conv_pre(ext_ref, w, bias, tt, lo=0, n=None):
    n = tt if n is None else n
    pad = CONV_WIDTH // 2
    acc = None
    for k in range(CONV_WIDTH):
        term = ext_ref[pl.ds(HALO + lo + k - pad, n), :] * w[k:k + 1, :]
        acc = term if acc is None else acc + term
    return acc + bias


def _conv_fwd(zx, conv_w, conv_b, d_inner, name):
    T = zx.shape[0]
    d_xbc = conv_w.shape[1]
    cw = _tile(d_xbc, 512)
    assert d_inner % cw == 0
    off = d_inner // cw
    tt = _tile(T, 512, 8)
    nt = T // tt

    def body(cur_ref, prev_ref, next_ref, w_ref, b_ref, o_ref, ext_ref):
        i = pl.program_id(1)
        _fill_ext(ext_ref, cur_ref, prev_ref, next_ref, i, nt, tt)
        pre = _conv_pre(ext_ref, w_ref[...], b_ref[...], tt)
        o_ref[...] = pre * _sigmoid(pre)

    specs = _halo_specs(tt, cw, lambda j, i: off + j, nt, 1)
    return _call(body, name=name, grid=(d_xbc // cw, nt),
                 in_specs=specs + [pl.BlockSpec((CONV_WIDTH, cw), lambda j, i: (0, j)),
                                   pl.BlockSpec((1, cw), lambda j, i: (0, j))],
                 out_specs=pl.BlockSpec((tt, cw), lambda j, i: (i, j)),
                 out_shape=jax.ShapeDtypeStruct((T, d_xbc), F32),
                 scratch_shapes=[pltpu.VMEM((tt + 2 * HALO, cw), F32)],
                 compiler_params=_params("parallel", "parallel"))(zx, zx, zx, conv_w, conv_b.reshape(1, d_xbc))


def _conv_dpre(zx, conv_w, conv_b, d_inner, col_lo, dirs, extra, name):
    T = zx.shape[0]
    ncols = dirs.shape[2]
    cw = _tile(ncols, 512)
    assert d_inner % cw == 0 and col_lo % cw == 0
    off_zx = (d_inner + col_lo) // cw
    off_w = col_lo // cw
    tt = _tile(T, 512, 8)
    nt = T // tt
    has_extra = extra is not None

    def body(cur_ref, prev_ref, next_ref, w_ref, b_ref, dirs_ref, *rest):
        o_ref, ext_ref = rest[-2], rest[-1]
        i = pl.program_id(1)
        _fill_ext(ext_ref, cur_ref, prev_ref, next_ref, i, nt, tt)
        pre = _conv_pre(ext_ref, w_ref[...], b_ref[...], tt)
        sig = _sigmoid(pre)
        dact = dirs_ref[0] + dirs_ref[1]
        if has_extra:
            dact = dact + rest[0][...]
        o_ref[...] = dact * (sig * (1.0 + pre * (1.0 - sig)))

    specs = _halo_specs(tt, cw, lambda j, i: off_zx + j, nt, 1)
    in_specs = specs + [pl.BlockSpec((CONV_WIDTH, cw), lambda j, i: (0, off_w + j)),
                        pl.BlockSpec((1, cw), lambda j, i: (0, off_w + j)),
                        pl.BlockSpec((2, tt, cw), lambda j, i: (0, i, j))]
    args = [zx, zx, zx, conv_w, conv_b.reshape(1, -1), dirs]
    if has_extra:
        in_specs.append(pl.BlockSpec((tt, cw), lambda j, i: (i, j)))
        args.append(extra)
    return _call(body, name=name, grid=(ncols // cw, nt), in_specs=in_specs,
                 out_specs=pl.BlockSpec((tt, cw), lambda j, i: (i, j)),
                 out_shape=jax.ShapeDtypeStruct((T, ncols), F32),
                 scratch_shapes=[pltpu.VMEM((tt + 2 * HALO, cw), F32)],
                 compiler_params=_params("parallel", "parallel"))(*args)


def _conv_bwd(zx, dpre, conv_w, d_inner, name):
    T = zx.shape[0]
    d_xbc = conv_w.shape[1]
    cw = _tile(d_xbc, 512)
    off = d_inner // cw
    tt = _tile(T, 512, 8)
    nt = T // tt
    pad = CONV_WIDTH // 2

    def body(zc, zp, zn, dc, dp, dn, w_ref, din_ref, dw_ref, db_ref, zext, dext):
        i = pl.program_id(1)

        @pl.when(i == 0)
        def _():
            dw_ref[...] = jnp.zeros_like(dw_ref)
            db_ref[...] = jnp.zeros_like(db_ref)

        _fill_ext(zext, zc, zp, zn, i, nt, tt)
        _fill_ext(dext, dc, dp, dn, i, nt, tt)
        w = w_ref[...]
        d = dc[...]
        acc = None
        for k in range(CONV_WIDTH):
            term = dext[pl.ds(HALO + pad - k, tt), :] * w[k:k + 1, :]
            acc = term if acc is None else acc + term
            dw_ref[k:k + 1, :] += jnp.sum(d * zext[pl.ds(HALO + k - pad, tt), :], axis=0, keepdims=True)
        din_ref[...] = acc
        db_ref[...] += jnp.sum(d, axis=0, keepdims=True)

    zspecs = _halo_specs(tt, cw, lambda j, i: off + j, nt, 1)
    dspecs = _halo_specs(tt, cw, lambda j, i: j, nt, 1)
    return _call(body, name=name, grid=(d_xbc // cw, nt),
                 in_specs=zspecs + dspecs + [pl.BlockSpec((CONV_WIDTH, cw), lambda j, i: (0, j))],
                 out_specs=[pl.BlockSpec((tt, cw), lambda j, i: (i, j)),
                            pl.BlockSpec((CONV_WIDTH, cw), lambda j, i: (0, j)),
                            pl.BlockSpec((1, cw), lambda j, i: (0, j))],
                 out_shape=[jax.ShapeDtypeStruct((T, d_xbc), F32), jax.ShapeDtypeStruct((CONV_WIDTH, d_xbc), F32),
                            jax.ShapeDtypeStruct((1, d_xbc), F32)],
                 scratch_shapes=[pltpu.VMEM((tt + 2 * HALO, cw), F32), pltpu.VMEM((tt + 2 * HALO, cw), F32)],
                 compiler_params=_params("parallel", "arbitrary"))(zx, zx, zx, dpre, dpre, dpre, conv_w)


def _tri(n):
    r = lax.broadcasted_iota(jnp.int32, (n, n), 0)
    c = lax.broadcasted_iota(jnp.int32, (n, n), 1)
    return (r >= c).astype(F32), (r <= c).astype(F32)


def _dt_fwd(raw, bias, a_log, name):
    T, H2 = raw.shape
    half = H2 // 2

    def body(raw_ref, bias_ref, alog_ref, dt_ref, cs_ref):
        x = raw_ref[...] + bias_ref[...]
        dt = jnp.maximum(x, 0.0) + jnp.log(1.0 + jnp.exp(-jnp.abs(x)))
        a = dt * (-jnp.exp(alog_ref[...]))
        lower, upper = _tri(CHUNK)
        cs_f = _dot(lower, a, NN, HIGHEST)
        cs_b = _dot(upper, a, NN, HIGHEST)
        lane = lax.broadcasted_iota(jnp.int32, (CHUNK, H2), 1)
        dt_ref[...] = dt
        cs_ref[...] = jnp.where(lane < half, cs_f, cs_b)

    row = pl.BlockSpec((CHUNK, H2), lambda c: (c, 0))
    vec = pl.BlockSpec((1, H2), lambda c: (0, 0))
    return _call(body, name=name, grid=(T // CHUNK,), in_specs=[row, vec, vec], out_specs=[row, row],
                 out_shape=[jax.ShapeDtypeStruct((T, H2), F32)] * 2,
                 compiler_params=_params("parallel"))(raw, bias.reshape(1, H2), a_log.reshape(1, H2))


def _dt_bwd(raw, bias, a_log, dcs, dtot, dxdtx, name):
    T, H2 = raw.shape
    half = H2 // 2

    def body(raw_ref, bias_ref, alog_ref, dcs_ref, dtot_ref, dx_ref, draw_ref, dbias_ref, dalog_ref):
        c = pl.program_id(0)

        @pl.when(c == 0)
        def _():
            dbias_ref[...] = jnp.zeros_like(dbias_ref)
            dalog_ref[...] = jnp.zeros_like(dalog_ref)

        x = raw_ref[...] + bias_ref[...]
        dt = jnp.maximum(x, 0.0) + jnp.log(1.0 + jnp.exp(-jnp.abs(x)))
        A = -jnp.exp(alog_ref[...])
        lower, upper = _tri(CHUNK)
        g = dcs_ref[...]
        lane = lax.broadcasted_iota(jnp.int32, (CHUNK, H2), 1)
        da = jnp.where(lane < half, _dot(upper, g, NN, HIGHEST), _dot(lower, g, NN, HIGHEST)) + dtot_ref[...]
        ddt = da * A + dx_ref[...]
        draw = ddt * _sigmoid(x)
        draw_ref[...] = draw
        dbias_ref[...] += jnp.sum(draw, axis=0, keepdims=True)
        dalog_ref[...] += jnp.sum(da * dt, axis=0, keepdims=True) * A

    row = pl.BlockSpec((CHUNK, H2), lambda c: (c, 0))
    vec = pl.BlockSpec((1, H2), lambda c: (0, 0))
    return _call(body, name=name, grid=(T // CHUNK,), in_specs=[row, vec, vec, row, row, row],
                 out_specs=[row, vec, vec],
                 out_shape=[jax.ShapeDtypeStruct((T, H2), F32), jax.ShapeDtypeStruct((1, H2), F32),
                            jax.ShapeDtypeStruct((1, H2), F32)],
                 compiler_params=_params("arbitrary"))(raw, bias.reshape(1, H2), a_log.reshape(1, H2), dcs, dtot, dxdtx)


def _cols(a, hg):
    T = a.shape[0]
    return a.reshape(T, 2, N_GROUPS, hg).transpose(1, 2, 0, 3)


def _rows(a, hg):
    T = a.shape[0]
    return a.reshape(T, 2, N_GROUPS, hg).transpose(1, 2, 3, 0)


def _uncols(a):
    T = a.shape[2]
    return a.transpose(2, 0, 1, 3).reshape(T, -1)


def _ssd_masks(d):
    r = lax.broadcasted_iota(jnp.int32, (CHUNK, CHUNK), 0)
    c = lax.broadcasted_iota(jnp.int32, (CHUNK, CHUNK), 1)
    return ((r >= c) & (d == 0)) | ((r <= c) & (d == 1))


def _ssd_fwd(xbc, dtc, csc, csr, d_inner, name):
    T = xbc.shape[0]
    nc = T // CHUNK
    gw = d_inner // N_GROUPS
    hg = gw // HEAD_DIM
    P, N = HEAD_DIM, D_STATE
    b_off = d_inner // N
    c_off = b_off + N_GROUPS

    def cidx(d, c):
        return c + d * (nc - 1 - 2 * c)

    def body(xs_ref, b_ref, c_ref, dtc_ref, csc_ref, csr_ref, y_ref, st_ref, h_ref):
        d = pl.program_id(0)
        c = pl.program_id(2)

        @pl.when(c == 0)
        def _():
            h_ref[...] = jnp.zeros_like(h_ref)

        Bb = b_ref[...].astype(BF16)
        Cb = c_ref[...].astype(BF16)
        S = _dot(Cb, Bb, NT)
        mask = _ssd_masks(d)
        st_ref[...] = h_ref[...]
        for j in range(hg):
            cs_c = csc_ref[:, j:j + 1]
            cs_r = csr_ref[j:j + 1, :]
            dt_c = dtc_ref[:, j:j + 1]
            decay = jnp.exp(jnp.where(mask, cs_c - cs_r, -jnp.inf))
            M = (S * decay).astype(BF16)
            xdt = xs_ref[:, j * P:(j + 1) * P] * dt_c
            tot = jnp.where(d == 0, cs_r[:, CHUNK - 1:CHUNK], cs_r[:, 0:1])
            Hj = h_ref[j]
            y = _dot(M, xdt.astype(BF16), NN) + jnp.exp(cs_c) * _dot(Cb, Hj.astype(BF16), NN)
            y_ref[:, j * P:(j + 1) * P] = y
            dte = jnp.exp(tot - cs_c)
            h_ref[j] = jnp.exp(tot) * Hj + _dot(Bb, (dte * xdt).astype(BF16), TN)

    col = lambda d, g, c: (d, g, cidx(d, c), 0)
    return _call(
        body, name=name, grid=(2, N_GROUPS, nc),
        in_specs=[pl.BlockSpec((CHUNK, gw), lambda d, g, c: (cidx(d, c), g)),
                  pl.BlockSpec((CHUNK, N), lambda d, g, c: (cidx(d, c), b_off + g)),
                  pl.BlockSpec((CHUNK, N), lambda d, g, c: (cidx(d, c), c_off + g)),
                  pl.BlockSpec((None, None, CHUNK, hg), col),
                  pl.BlockSpec((None, None, CHUNK, hg), col),
                  pl.BlockSpec((None, None, hg, CHUNK), lambda d, g, c: (d, g, 0, cidx(d, c)))],
        out_specs=[pl.BlockSpec((None, CHUNK, gw), lambda d, g, c: (d, cidx(d, c), g)),
                   pl.BlockSpec((None, None, None, hg, N, P), lambda d, g, c: (d, cidx(d, c), g, 0, 0, 0))],
        out_shape=[jax.ShapeDtypeStruct((2, T, d_inner), F32),
                   jax.ShapeDtypeStruct((2, nc, N_GROUPS, hg, N, P), F32)],
        scratch_shapes=[pltpu.VMEM((hg, N, P), F32)],
        compiler_params=_params("parallel", "parallel", "arbitrary"),
    )(xbc, xbc, xbc, dtc, csc, csr)


def _ssd_bwd(xbc, dtc, csc, csr, states, dy, d_inner, name):
    T = xbc.shape[0]
    nc = T // CHUNK
    gw = d_inner // N_GROUPS
    hg = gw // HEAD_DIM
    P, N = HEAD_DIM, D_STATE
    b_off = d_inner // N
    c_off = b_off + N_GROUPS

    def cidx(d, c):
        return (nc - 1 - c) + d * (2 * c - nc + 1)

    def body(xs_ref, b_ref, c_ref, dtc_ref, csc_ref, csr_ref, st_ref, dy_ref,
             dxs_ref, db_ref, dc_ref, dcs_ref, dtot_ref, dxdtx_ref, dh_ref):
        d = pl.program_id(0)
        c = pl.program_id(2)

        @pl.when(c == 0)
        def _():
            dh_ref[...] = jnp.zeros_like(dh_ref)

        Bf = b_ref[...]
        Cf = c_ref[...]
        Bb = Bf.astype(BF16)
        Cb = Cf.astype(BF16)
        S = _dot(Cb, Bb, NT)
        mask = _ssd_masks(d)
        dS = jnp.zeros((CHUNK, CHUNK), F32)
        dB = jnp.zeros((CHUNK, N), F32)
        dC = jnp.zeros((CHUNK, N), F32)
        for j in range(hg):
            cs_c = csc_ref[:, j:j + 1]
            cs_r = csr_ref[j:j + 1, :]
            dt_c = dtc_ref[:, j:j + 1]
            decay = jnp.exp(jnp.where(mask, cs_c - cs_r, -jnp.inf))
            Mf = S * decay
            X = xs_ref[:, j * P:(j + 1) * P]
            xdt = X * dt_c
            xdtb = xdt.astype(BF16)
            tot = jnp.where(d == 0, cs_r[:, CHUNK - 1:CHUNK], cs_r[:, 0:1])
            e_tot = jnp.exp(tot)
            Hp = st_ref[j]
            Hpb = Hp.astype(BF16)
            dH = dh_ref[j]
            dHb = dH.astype(BF16)
            dY = dy_ref[:, j * P:(j + 1) * P]
            dYb = dY.astype(BF16)
            dM = jnp.where(mask, _dot(dYb, xdtb, NT), 0.0)
            G = dM * Mf
            dcs = jnp.sum(G, axis=1, keepdims=True) - jnp.sum(G.T, axis=1, keepdims=True)
            dxdt = _dot(Mf.astype(BF16), dYb, TN)
            dS = dS + dM * decay
            E = jnp.exp(cs_c)
            y_off = E * _dot(Cb, Hpb, NN)
            dcs = dcs + jnp.sum(dY * y_off, axis=1, keepdims=True)
            dCH = (E * dY).astype(BF16)
            dC = dC + _dot(dCH, Hpb, NT)
            dHp = _dot(Cb, dCH, TN)
            dte = jnp.exp(tot - cs_c)
            Q = _dot(Bb, dHb, NN)
            dB = dB + _dot((dte * xdt).astype(BF16), dHb, NT)
            dxdt = dxdt + dte * Q
            ddte = jnp.sum(Q * xdt, axis=1, keepdims=True) * dte
            dcs = dcs - ddte
            dtot = jnp.sum(ddte) + e_tot * jnp.sum(dH * Hp)
            dh_ref[j] = e_tot * dH + dHp
            dxs_ref[:, j * P:(j + 1) * P] = dxdt * dt_c
            dcs_ref[:, j:j + 1] = dcs
            dtot_ref[:, j:j + 1] = jnp.zeros((CHUNK, 1), F32) + dtot
            dxdtx_ref[:, j:j + 1] = jnp.sum(dxdt * X, axis=1, keepdims=True)
        dSb = dS.astype(BF16)
        dc_ref[...] = dC + _dot(dSb, Bb, NN)
        db_ref[...] = dB + _dot(dSb, Cb, TN)

    col = lambda d, g, c: (d, g, cidx(d, c), 0)
    colspec = pl.BlockSpec((None, None, CHUNK, hg), col)
    return _call(
        body, name=name, grid=(2, N_GROUPS, nc),
        in_specs=[pl.BlockSpec((CHUNK, gw), lambda d, g, c: (cidx(d, c), g)),
                  pl.BlockSpec((CHUNK, N), lambda d, g, c: (cidx(d, c), b_off + g)),
                  pl.BlockSpec((CHUNK, N), lambda d, g, c: (cidx(d, c), c_off + g)),
                  colspec, colspec,
                  pl.BlockSpec((None, None, hg, CHUNK), lambda d, g, c: (d, g, 0, cidx(d, c))),
                  pl.BlockSpec((None, None, None, hg, N, P), lambda d, g, c: (d, cidx(d, c), g, 0, 0, 0)),
                  pl.BlockSpec((CHUNK, gw), lambda d, g, c: (cidx(d, c), g))],
        out_specs=[pl.BlockSpec((None, CHUNK, gw), lambda d, g, c: (d, cidx(d, c), g)),
                   pl.BlockSpec((None, CHUNK, N), lambda d, g, c: (d, cidx(d, c), g)),
                   pl.BlockSpec((None, CHUNK, N), lambda d, g, c: (d, cidx(d, c), g)),
                   colspec, colspec, colspec],
        out_shape=[jax.ShapeDtypeStruct((2, T, d_inner), F32),
                   jax.ShapeDtypeStruct((2, T, N_GROUPS * N), F32),
                   jax.ShapeDtypeStruct((2, T, N_GROUPS * N), F32)]
        + [jax.ShapeDtypeStruct((2, N_GROUPS, T, hg), F32)] * 3,
        scratch_shapes=[pltpu.VMEM((hg, N, P), F32)],
        compiler_params=_params("parallel", "parallel", "arbitrary"),
    )(xbc, xbc, xbc, dtc, csc, csr, states, dy)


def _gnorm_fwd(y2, xbc, zx, dvec, nw, d_inner, name):
    T = xbc.shape[0]
    gw = d_inner // N_GROUPS
    tm = _tile(T, 128, 8)

    def body(y_ref, xs_ref, z_ref, d_ref, w_ref, o_ref):
        for g in range(N_GROUPS):
            sl = slice(g * gw, (g + 1) * gw)
            y = y_ref[0, :, sl] + y_ref[1, :, sl] + xs_ref[:, sl] * d_ref[:, sl]
            z = z_ref[:, sl]
            gy = y * (z * _sigmoid(z))
            rs = lax.rsqrt(jnp.mean(gy * gy, axis=-1, keepdims=True) + RMS_EPS)
            o_ref[:, sl] = (gy * rs * w_ref[:, sl]).astype(o_ref.dtype)

    row = pl.BlockSpec((tm, d_inner), lambda i: (i, 0))
    vec = pl.BlockSpec((1, d_inner), lambda i: (0, 0))
    return _call(body, name=name, grid=(T // tm,),
                 in_specs=[pl.BlockSpec((2, tm, d_inner), lambda i: (0, i, 0)), row, row, vec, vec],
                 out_specs=row, out_shape=jax.ShapeDtypeStruct((T, d_inner), BF16),
                 compiler_params=_params("parallel"))(y2, xbc, zx, dvec, nw)


def _gnorm_bwd(y2, xbc, zx, dvec, nw, dgn, sel, d_inner, name):
    T = xbc.shape[0]
    gw = d_inner // N_GROUPS
    n_heads = d_inner // HEAD_DIM
    tm = _tile(T, 128, 8)
    n_tiles = T // tm

    def body(y_ref, xs_ref, z_ref, d_ref, w_ref, dg_ref, sel_ref, dy_ref, dxs_ref, dz_ref, dw_ref, dd_ref, dch_ref):
        i = pl.program_id(0)

        @pl.when(i == 0)
        def _():
            dw_ref[...] = jnp.zeros_like(dw_ref)
            dch_ref[...] = jnp.zeros_like(dch_ref)

        for g in range(N_GROUPS):
            sl = slice(g * gw, (g + 1) * gw)
            xs = xs_ref[:, sl]
            y = y_ref[0, :, sl] + y_ref[1, :, sl] + xs * d_ref[:, sl]
            z = z_ref[:, sl]
            sig = _sigmoid(z)
            sz = z * sig
            gy = y * sz
            rs = lax.rsqrt(jnp.mean(gy * gy, axis=-1, keepdims=True) + RMS_EPS)
            n = gy * rs
            dout = dg_ref[:, sl]
            dw_ref[:, sl] += jnp.sum(dout * n, axis=0, keepdims=True)
            dn = dout * w_ref[:, sl]
            dgy = rs * (dn - n * jnp.mean(dn * n, axis=-1, keepdims=True))
            dy = dgy * sz
            dy_ref[:, sl] = dy
            dz_ref[:, sl] = dgy * y * (sig * (1.0 + z * (1.0 - sig)))
            dxs_ref[:, sl] = dy * d_ref[:, sl]
            dch_ref[:, sl] += jnp.sum(dy * xs, axis=0, keepdims=True)

        @pl.when(i == n_tiles - 1)
        def _():
            dd_ref[...] = _dot(dch_ref[...], sel_ref[...], NN, HIGHEST)

    row = pl.BlockSpec((tm, d_inner), lambda i: (i, 0))
    vec = pl.BlockSpec((1, d_inner), lambda i: (0, 0))
    hvec = pl.BlockSpec((1, n_heads), lambda i: (0, 0))
    return _call(body, name=name, grid=(n_tiles,),
                 in_specs=[pl.BlockSpec((2, tm, d_inner), lambda i: (0, i, 0)), row, row, vec, vec, row,
                           pl.BlockSpec((d_inner, n_heads), lambda i: (0, 0))],
                 out_specs=[row, row, row, vec, hvec],
                 out_shape=[jax.ShapeDtypeStruct((T, d_inner), F32)] * 3
                 + [jax.ShapeDtypeStruct((1, d_inner), F32), jax.ShapeDtypeStruct((1, n_heads), F32)],
                 scratch_shapes=[pltpu.VMEM((1, d_inner), F32)],
                 compiler_params=_params("arbitrary"))(y2, xbc, zx, dvec, nw, dgn, sel)


def _pool_counts(i, tt, T, win, rows, row0):
    t = i * tt + row0 + lax.broadcasted_iota(jnp.int32, (rows, 1), 0)
    start = t - win // 2
    lo = jnp.clip(start, 0, T)
    hi = jnp.clip(start + win, 0, T)
    return jnp.maximum(hi - lo, 1).astype(F32)


def _pool_features(ext_ref, i, tt, T, gi, gd):
    win = POOL_WINDOWS[gi]
    sl = slice(gi * gd, (gi + 1) * gd)
    acc = None
    for o in range(-(win // 2), win - win // 2):
        term = ext_ref[pl.ds(HALO + o, tt), sl]
        acc = term if acc is None else acc + term
    return acc / _pool_counts(i, tt, T, win, tt, 0) - ext_ref[pl.ds(HALO, tt), sl]


def _pool_fwd(u, w, bias, scale, name):
    T, D = u.shape
    ng = len(POOL_WINDOWS)
    gd = D // ng
    tt = _tile(T, 512, 8)
    nt = T // tt

    def body(cur, prev, nxt, w_ref, b_ref, s_ref, o_ref, ext):
        i = pl.program_id(0)
        _fill_ext(ext, cur, prev, nxt, i, nt, tt)
        for gi in range(ng):
            sl = slice(gi * gd, (gi + 1) * gd)
            m = _pool_features(ext, i, tt, T, gi, gd)
            pre = _dot(m.astype(BF16), w_ref[gi], NN) + b_ref[:, sl]
            o_ref[:, sl] = pre * s_ref[:, sl]

    vec = pl.BlockSpec((1, D), lambda i: (0, 0))
    return _call(body, name=name, grid=(nt,),
                 in_specs=_halo_specs(tt, D, lambda i: 0, nt, 0)
                 + [pl.BlockSpec((ng, gd, gd), lambda i: (0, 0, 0)), vec, vec],
                 out_specs=pl.BlockSpec((tt, D), lambda i: (i, 0)),
                 out_shape=jax.ShapeDtypeStruct((T, D), F32),
                 scratch_shapes=[pltpu.VMEM((tt + 2 * HALO, D), F32)],
                 compiler_params=_params("parallel"))(u, u, u, w, bias, scale)


def _pool_bwd_a(u, w, bias, scale, dy, name):
    T, D = u.shape
    ng = len(POOL_WINDOWS)
    gd = D // ng
    tt = _tile(T, 512, 8)
    nt = T // tt

    def body(cur, prev, nxt, w_ref, b_ref, s_ref, dy_ref, dm_ref, dw_ref, db_ref, ds_ref, ext):
        i = pl.program_id(0)

        @pl.when(i == 0)
        def _():
            dw_ref[...] = jnp.zeros_like(dw_ref)
            db_ref[...] = jnp.zeros_like(db_ref)
            ds_ref[...] = jnp.zeros_like(ds_ref)

        _fill_ext(ext, cur, prev, nxt, i, nt, tt)
        for gi in range(ng):
            sl = slice(gi * gd, (gi + 1) * gd)
            mb = _pool_features(ext, i, tt, T, gi, gd).astype(BF16)
            wg = w_ref[gi]
            pre = _dot(mb, wg, NN) + b_ref[:, sl]
            dy_ = dy_ref[:, sl]
            ds_ref[:, sl] += jnp.sum(dy_ * pre, axis=0, keepdims=True)
            dpre = dy_ * s_ref[:, sl]
            db_ref[:, sl] += jnp.sum(dpre, axis=0, keepdims=True)
            dpb = dpre.astype(BF16)
            dw_ref[gi] += _dot(mb, dpb, TN)
            dm_ref[:, sl] = _dot(dpb, wg, NT)

    vec = pl.BlockSpec((1, D), lambda i: (0, 0))
    row = pl.BlockSpec((tt, D), lambda i: (i, 0))
    wspec = pl.BlockSpec((ng, gd, gd), lambda i: (0, 0, 0))
    return _call(body, name=name, grid=(nt,),
                 in_specs=_halo_specs(tt, D, lambda i: 0, nt, 0) + [wspec, vec, vec, row],
                 out_specs=[row, wspec, vec, vec],
                 out_shape=[jax.ShapeDtypeStruct((T, D), F32), jax.ShapeDtypeStruct((ng, gd, gd), F32),
                            jax.ShapeDtypeStruct((1, D), F32), jax.ShapeDtypeStruct((1, D), F32)],
                 scratch_shapes=[pltpu.VMEM((tt + 2 * HALO, D), F32)],
                 compiler_params=_params("arbitrary"))(u, u, u, w, bias, scale, dy)


def _pool_bwd_b(dm, dy, alpha, name):
    T, D = dm.shape
    ng = len(POOL_WINDOWS)
    gd = D // ng
    tt = _tile(T, 512, 8)
    nt = T // tt

    def body(cur, prev, nxt, dy_ref, o_ref, ext):
        i = pl.program_id(0)
        _fill_ext(ext, cur, prev, nxt, i, nt, tt)
        for gi, win in enumerate(POOL_WINDOWS):
            sl = slice(gi * gd, (gi + 1) * gd)
            rows = tt + 2 * HALO
            ext[:, sl] = ext[:, sl] / _pool_counts(i, tt, T, win, rows, -HALO)
            acc = None
            for o in range(-(win // 2) + 1, win // 2 + 1):
                term = ext[pl.ds(HALO + o, tt), sl]
                acc = term if acc is None else acc + term
            o_ref[:, sl] = alpha * dy_ref[:, sl] + acc - cur[:, sl]

    row = pl.BlockSpec((tt, D), lambda i: (i, 0))
    return _call(body, name=name, grid=(nt,),
                 in_specs=_halo_specs(tt, D, lambda i: 0, nt, 0) + [row], out_specs=row,
                 out_shape=jax.ShapeDtypeStruct((T, D), F32),
                 scratch_shapes=[pltpu.VMEM((tt + 2 * HALO, D), F32)],
                 compiler_params=_params("parallel"))(dm, dm, dm, dy)


def _mesh_pos():
    x, y, c = lax.axis_index("x"), lax.axis_index("y"), lax.axis_index("c")
    return x, y, c


def _peer(x, y, c, k):
    dx, dy, dc = (k >> 2) & 1, (k >> 1) & 1, k & 1
    px = (1 - x) if dx else x
    py = (1 - y) if dy else y
    pc = (1 - c) if dc else c
    return px, py, pc


def _exchange(arrays, gather, name):
    n = len(arrays)

    def body(*refs):
        ins, outs = refs[:n], refs[n:2 * n]
        send_sems, recv_sems, local_sems = refs[2 * n:]
        x, y, c = _mesh_pos()
        me = 4 * x + 2 * y + c
        copies = []
        for a in range(n):
            src_mine = ins[a] if gather else ins[a].at[me]
            local = pltpu.make_async_copy(src_mine, outs[a].at[me], local_sems.at[a])
            local.start()
            copies.append(local)
            for k in range(1, N_DEV):
                px, py, pc = _peer(x, y, c, k)
                peer = 4 * px + 2 * py + pc
                rc = pltpu.make_async_remote_copy(
                    src_ref=ins[a] if gather else ins[a].at[peer],
                    dst_ref=outs[a].at[me],
                    send_sem=send_sems.at[a, k - 1], recv_sem=recv_sems.at[a, k - 1],
                    device_id=(px, py, pc), device_id_type=pl.DeviceIdType.MESH)
                rc.start()
                copies.append(rc)
        for cp in copies:
            cp.wait()

    any_spec = pl.BlockSpec(memory_space=pl.ANY)
    out_shape = [jax.ShapeDtypeStruct(((N_DEV,) + a.shape) if gather else a.shape, a.dtype) for a in arrays]
    return _call(body, name=name, in_specs=[any_spec] * n, out_specs=[any_spec] * n, out_shape=out_shape,
                 scratch_shapes=[pltpu.SemaphoreType.DMA((n, N_DEV - 1)), pltpu.SemaphoreType.DMA((n, N_DEV - 1)),
                                 pltpu.SemaphoreType.DMA((n,))],
                 compiler_params=pltpu.CompilerParams(has_side_effects=True))(*arrays)


def _adamw(parts, w, m, v, name):
    R, C = w.shape
    tr = _tile(R, max(8, (1 << 18) // C // 8 * 8), 8)

    def body(p_ref, w_ref, m_ref, v_ref, g_ref, d_ref, nm_ref, nv_ref):
        g = p_ref[0].astype(F32)
        for i in range(1, N_DEV):
            g = g + p_ref[i].astype(F32)
        mm = ADAM_B1 * m_ref[...] + (1.0 - ADAM_B1) * g
        vv = ADAM_B2 * v_ref[...] + (1.0 - ADAM_B2) * (g * g)
        m_hat = mm / (1.0 - ADAM_B1 ** ADAM_STEP)
        v_hat = vv / (1.0 - ADAM_B2 ** ADAM_STEP)
        g_ref[...] = g
        d_ref[...] = -ADAM_LR * (m_hat / (jnp.sqrt(v_hat) + ADAM_EPS) + ADAM_WD * w_ref[...])
        nm_ref[...] = mm
        nv_ref[...] = vv

    row = pl.BlockSpec((tr, C), lambda i: (i, 0))
    return _call(body, name=name, grid=(R // tr,),
                 in_specs=[pl.BlockSpec((N_DEV, tr, C), lambda i: (0, i, 0)), row, row, row],
                 out_specs=[row] * 4, out_shape=[jax.ShapeDtypeStruct((R, C), F32)] * 4,
                 compiler_params=_params("parallel"))(parts, w, m, v)


def _pack(arrays):
    flat, meta, off = [], [], 0
    for a in arrays:
        flat.append(a.reshape(-1).astype(F32))
        meta.append((off, a.shape))
        off += a.size
    total = -(-off // (8 * LANES)) * (8 * LANES)
    flat.append(jnp.zeros((total - off,), F32))
    return jnp.concatenate(flat).reshape(total // LANES, LANES), meta


def _unpack(packed, meta):
    flat = packed.reshape(-1)
    return [flat[off:off + math.prod(shape)].reshape(shape) for off, shape in meta]


def kernel(x, ssd_in_proj, ssd_conv_w, ssd_conv_b, ssd_dt_bias, ssd_A_log, ssd_D, ssd_norm_w, ssd_out_proj, pool_w, pool_b, pool_scale, mlp_w1, mlp_w2, ln_mix_g, ln_mix_b, ln_ffn_g, ln_ffn_b, loss_target, m_ssd_in_proj, m_ssd_conv_w, m_ssd_conv_b, m_ssd_dt_bias, m_ssd_A_log, m_ssd_D, m_ssd_norm_w, m_ssd_out_proj, m_pool_w, m_pool_b, m_pool_scale, m_mlp_w1, m_mlp_w2, m_ln_mix_g, m_ln_mix_b, m_ln_ffn_g, m_ln_ffn_b, v_ssd_in_proj, v_ssd_conv_w, v_ssd_conv_b, v_ssd_dt_bias, v_ssd_A_log, v_ssd_D, v_ssd_norm_w, v_ssd_out_proj, v_pool_w, v_pool_b, v_pool_scale, v_mlp_w1, v_mlp_w2, v_ln_mix_g, v_ln_mix_b, v_ln_ffn_g, v_ln_ffn_b):
    T, D = x.shape[1], x.shape[2]
    depth = mlp_w1.shape[0]
    n_ssd, n_pool = ssd_in_proj.shape[0], pool_w.shape[0]
    d_inner = ssd_out_proj.shape[1] * N_DEV
    n_heads = d_inner // HEAD_DIM
    hg = n_heads // N_GROUPS
    d_bc = N_GROUPS * D_STATE
    d_xbc = d_inner + 2 * d_bc
    d_in_proj = ssd_in_proj.shape[2] * N_DEV
    d_ff = mlp_w1.shape[2] * N_DEV
    ng = len(POOL_WINDOWS)
    gd = D // ng
    alpha = (2.0 * depth) ** 0.25
    x0 = x.reshape(T, D)
    target = loss_target.reshape(T, D)

    small_sharded = [ssd_conv_w, pool_b, pool_scale]
    small_pack, small_meta = _pack(small_sharded)
    pw_rows = pool_w.shape[1] * pool_w.shape[2]
    g_in, g_out, g_pw, g_w1, g_w2, g_small = _exchange(
        [ssd_in_proj.astype(BF16), ssd_out_proj.astype(BF16),
         pool_w.reshape(n_pool, pw_rows, gd).astype(BF16),
         mlp_w1.astype(BF16), mlp_w2.astype(BF16), small_pack], True, "gather_weights")
    w_in = g_in.transpose(1, 2, 0, 3).reshape(n_ssd, D, d_in_proj)
    w_out = g_out.transpose(1, 0, 2, 3).reshape(n_ssd, d_inner, D)
    w_pool = g_pw.reshape(N_DEV, n_pool, ng, gd // N_DEV, gd).transpose(1, 2, 0, 3, 4).reshape(n_pool, ng, gd, gd)
    w_1 = g_w1.transpose(1, 2, 0, 3).reshape(depth, D, d_ff)
    w_2 = g_w2.transpose(1, 0, 2, 3).reshape(depth, d_ff, D)
    smalls = [_unpack(g_small[j], small_meta) for j in range(N_DEV)]
    conv_w = jnp.concatenate([s[0] for s in smalls], axis=-1).reshape(n_ssd, CONV_WIDTH, d_xbc)
    pool_bias = jnp.concatenate([s[1] for s in smalls], axis=-1).reshape(n_pool, 1, D)
    pool_sc = jnp.concatenate([s[2] for s in smalls], axis=-1).reshape(n_pool, 1, D)

    sel = (jnp.arange(d_inner)[:, None] // HEAD_DIM == jnp.arange(n_heads)[None, :]).astype(F32)

    saved = []
    h = x0
    for i in range(depth):
        j = i // 2
        s = {}
        s["x0"] = h
        if i % 2 == 0:
            (zx,) = _matmul(h, w_in[j], "nn", name=f"in_proj_{i}", outs=[F32], tn=1152)
            xbc = _conv_fwd(zx, conv_w[j], ssd_conv_b[j], d_inner, f"conv_fwd_{i}")
            raw = zx[:, d_inner + d_xbc:]
            dt, cs = _dt_fwd(raw, ssd_dt_bias[j], ssd_A_log[j], f"dt_fwd_{i}")
            dtc, csc, csr = _cols(dt, hg), _cols(cs, hg), _rows(cs, hg)
            y2, states = _ssd_fwd(xbc, dtc, csc, csr, d_inner, f"ssd_fwd_{i}")
            dvec = jnp.repeat(ssd_D[j], HEAD_DIM).reshape(1, d_inner)
            nw = ssd_norm_w[j].reshape(1, d_inner)
            gn = _gnorm_fwd(y2, xbc, zx, dvec, nw, d_inner, f"gnorm_fwd_{i}")
            (mix,) = _matmul(gn, w_out[j], "nn", name=f"out_proj_{i}", outs=[F32])
            s.update(zx=zx, xbc=xbc, raw=raw, dtc=dtc, csc=csc, csr=csr, y2=y2, states=states, dvec=dvec, nw=nw, gn=gn)
        else:
            mix = _pool_fwd(h, w_pool[j], pool_bias[j], pool_sc[j], f"pool_fwd_{i}")
        s["mix"] = mix
        x1 = _ln_fwd(h, mix, ln_mix_g[i], ln_mix_b[i], alpha, f"ln_mix_fwd_{i}")
        u, hh = _matmul(x1, w_1[i], "nn", name=f"mlp_up_{i}", outs=[F32, BF16],
                        epilogue=lambda acc: (acc, jnp.square(jnp.maximum(acc, 0.0))))
        (m2,) = _matmul(hh, w_2[i], "nn", name=f"mlp_down_{i}", outs=[F32])
        x2 = _ln_fwd(x1, m2, ln_ffn_g[i], ln_ffn_b[i], alpha, f"ln_ffn_fwd_{i}")
        s.update(x1=x1, u=u, hh=hh, m2=m2)
        saved.append(s)
        h = x2

    loss_row, dh = _loss_head(h, target)
    loss = lax.psum(loss_row[0, 0], ("x", "y", "c"))

    gr = {k: [None] * depth for k in ("w1", "w2", "ln_mix_g", "ln_mix_b", "ln_ffn_g", "ln_ffn_b")}
    gs = {k: [None] * n_ssd for k in ("in", "out", "conv_w", "conv_b", "dt_bias", "A_log", "D", "norm_w")}
    gp = {k: [None] * n_pool for k in ("w", "b", "scale")}
    for i in reversed(range(depth)):
        j = i // 2
        s = saved[i]
        ds2, gr["ln_ffn_g"][i], gr["ln_ffn_b"][i] = _ln_bwd(s["x1"], s["m2"], ln_ffn_g[i], dh, alpha, f"ln_ffn_bwd_{i}")
        (du,) = _matmul(ds2, w_2[i], "nt", name=f"mlp_down_dx_{i}", outs=[BF16], extras=[s["u"]],
                        epilogue=lambda acc, u_: (acc * (2.0 * jnp.maximum(u_, 0.0)),))
        (gr["w2"][i],) = _matmul(s["hh"], ds2, "tn", name=f"mlp_down_dw_{i}", outs=[BF16])
        (dx1,) = _matmul(du, w_1[i], "nt", name=f"mlp_up_dx_{i}", outs=[F32], extras=[ds2],
                         epilogue=lambda acc, e: (acc + alpha * e,))
        (gr["w1"][i],) = _matmul(s["x1"], du, "tn", name=f"mlp_up_dw_{i}", outs=[BF16])
        ds1, gr["ln_mix_g"][i], gr["ln_mix_b"][i] = _ln_bwd(s["x0"], s["mix"], ln_mix_g[i], dx1, alpha, f"ln_mix_bwd_{i}")
        if i % 2 == 0:
            (dgn,) = _matmul(ds1, w_out[j], "nt", name=f"out_proj_dx_{i}", outs=[F32])
            (gs["out"][j],) = _matmul(s["gn"], ds1, "tn", name=f"out_proj_dw_{i}", outs=[BF16])
            dy, dxs_d, dz, gs["norm_w"][j], gs["D"][j] = _gnorm_bwd(
                s["y2"], s["xbc"], s["zx"], s["dvec"], s["nw"], dgn, sel, d_inner, f"gnorm_bwd_{i}")
            dxs, dB, dC, dcs, dtot, dxdtx = _ssd_bwd(s["xbc"], s["dtc"], s["csc"], s["csr"], s["states"], dy,
                                                     d_inner, f"ssd_bwd_{i}")
            draw, gs["dt_bias"][j], gs["A_log"][j] = _dt_bwd(
                s["raw"], ssd_dt_bias[j], ssd_A_log[j], _uncols(dcs), _uncols(dtot), _uncols(dxdtx), f"dt_bwd_{i}")
            dpre = jnp.concatenate([
                _conv_dpre(s["zx"], conv_w[j], ssd_conv_b[j], d_inner, 0, dxs, dxs_d, f"conv_dpre_x_{i}"),
                _conv_dpre(s["zx"], conv_w[j], ssd_conv_b[j], d_inner, d_inner, dB, None, f"conv_dpre_b_{i}"),
                _conv_dpre(s["zx"], conv_w[j], ssd_conv_b[j], d_inner, d_inner + d_bc, dC, None, f"conv_dpre_c_{i}"),
            ], axis=1)
            din, gs["conv_w"][j], gs["conv_b"][j] = _conv_bwd(s["zx"], dpre, conv_w[j], d_inner, f"conv_bwd_{i}")
            dzx = jnp.concatenate([dz, din, draw], axis=1)
            (gs["in"][j],) = _matmul(s["x0"], dzx, "tn", name=f"in_proj_dw_{i}", outs=[BF16], tn=1152)
            (dh,) = _matmul(dzx, w_in[j], "nt", name=f"in_proj_dx_{i}", outs=[F32], extras=[ds1], tk=1152,
                            epilogue=lambda acc, e: (acc + alpha * e,))
        else:
            dm, gp["w"][j], gp["b"][j], gp["scale"][j] = _pool_bwd_a(
                s["x0"], w_pool[j], pool_bias[j], pool_sc[j], ds1, f"pool_bwd_a_{i}")
            dh = _pool_bwd_b(dm, ds1, alpha, f"pool_bwd_b_{i}")
    grad_x = dh.reshape(x.shape)

    def dev_major_cols(stack):
        L, R, NC = stack.shape
        return stack.reshape(L, R, N_DEV, NC // N_DEV).transpose(2, 0, 1, 3)

    def dev_major_rows(stack):
        L, NR, C = stack.shape
        return stack.reshape(L, N_DEV, NR // N_DEV, C).transpose(1, 0, 2, 3)

    s_in = dev_major_cols(jnp.stack(gs["in"]))
    s_out = dev_major_rows(jnp.stack(gs["out"]))
    s_pw = (jnp.stack(gp["w"]).astype(BF16).reshape(n_pool, ng, N_DEV, gd // N_DEV, gd)
            .transpose(2, 0, 1, 3, 4).reshape(N_DEV, n_pool, pw_rows, gd))
    s_w1 = dev_major_cols(jnp.stack(gr["w1"]))
    s_w2 = dev_major_rows(jnp.stack(gr["w2"]))
    g_conv_w = jnp.stack(gs["conv_w"]).reshape(n_ssd, CONV_WIDTH, 1, N_DEV, d_xbc // N_DEV)
    g_pool_b = jnp.stack(gp["b"]).reshape(n_pool, ng, N_DEV, gd // N_DEV)
    g_pool_s = jnp.stack(gp["scale"]).reshape(n_pool, N_DEV, D // N_DEV)
    s_small = jnp.stack([_pack([g_conv_w[:, :, :, k], g_pool_b[:, :, k], g_pool_s[:, k]])[0] for k in range(N_DEV)])
    repl_grads = [jnp.stack(gs["conv_b"]).reshape(ssd_conv_b.shape), jnp.stack(gs["dt_bias"]).reshape(ssd_dt_bias.shape),
                  jnp.stack(gs["A_log"]).reshape(ssd_A_log.shape), jnp.stack(gs["D"]).reshape(ssd_D.shape),
                  jnp.stack(gs["norm_w"]).reshape(ssd_norm_w.shape),
                  jnp.stack(gr["ln_mix_g"]).reshape(ln_mix_g.shape), jnp.stack(gr["ln_mix_b"]).reshape(ln_mix_b.shape),
                  jnp.stack(gr["ln_ffn_g"]).reshape(ln_ffn_g.shape), jnp.stack(gr["ln_ffn_b"]).reshape(ln_ffn_b.shape)]
    repl_pack, repl_meta = _pack(repl_grads)
    s_repl = jnp.broadcast_to(repl_pack[None], (N_DEV,) + repl_pack.shape)
    r_in, r_out, r_pw, r_w1, r_w2, r_small, r_repl = _exchange(
        [s_in, s_out, s_pw, s_w1, s_w2, s_small, s_repl], False, "exchange_grads")

    def update(parts, w, m, v, name):
        shape = w.shape
        C = shape[-1]
        res = _adamw(parts.reshape(N_DEV, -1, C), w.reshape(-1, C), m.reshape(-1, C), v.reshape(-1, C), name)
        return [r.reshape(shape) for r in res]

    upd = {}
    upd["ssd_in_proj"] = update(r_in, ssd_in_proj, m_ssd_in_proj, v_ssd_in_proj, "adamw_in_proj")
    upd["ssd_out_proj"] = update(r_out, ssd_out_proj, m_ssd_out_proj, v_ssd_out_proj, "adamw_out_proj")
    upd["pool_w"] = update(r_pw, pool_w, m_pool_w, v_pool_w, "adamw_pool_w")
    upd["mlp_w1"] = update(r_w1, mlp_w1, m_mlp_w1, v_mlp_w1, "adamw_w1")
    upd["mlp_w2"] = update(r_w2, mlp_w2, m_mlp_w2, v_mlp_w2, "adamw_w2")
    sm = update(r_small, small_pack, _pack([m_ssd_conv_w, m_pool_b, m_pool_scale])[0],
                _pack([v_ssd_conv_w, v_pool_b, v_pool_scale])[0], "adamw_small_sharded")
    for idx, nm in enumerate(["ssd_conv_w", "pool_b", "pool_scale"]):
        upd[nm] = [_unpack(r, small_meta)[idx] for r in sm]
    repl_names = ["ssd_conv_b", "ssd_dt_bias", "ssd_A_log", "ssd_D", "ssd_norm_w",
                  "ln_mix_g", "ln_mix_b", "ln_ffn_g", "ln_ffn_b"]
    repl_w = [ssd_conv_b, ssd_dt_bias, ssd_A_log, ssd_D, ssd_norm_w, ln_mix_g, ln_mix_b, ln_ffn_g, ln_ffn_b]
    repl_m = [m_ssd_conv_b, m_ssd_dt_bias, m_ssd_A_log, m_ssd_D, m_ssd_norm_w, m_ln_mix_g, m_ln_mix_b, m_ln_ffn_g, m_ln_ffn_b]
    repl_v = [v_ssd_conv_b, v_ssd_dt_bias, v_ssd_A_log, v_ssd_D, v_ssd_norm_w, v_ln_mix_g, v_ln_mix_b, v_ln_ffn_g, v_ln_ffn_b]
    rp = update(r_repl, _pack(repl_w)[0], _pack(repl_m)[0], _pack(repl_v)[0], "adamw_replicated")
    for idx, nm in enumerate(repl_names):
        upd[nm] = [_unpack(r, repl_meta)[idx] for r in rp]

    order = ["ssd_in_proj", "ssd_conv_w", "ssd_conv_b", "ssd_dt_bias", "ssd_A_log", "ssd_D", "ssd_norm_w",
             "ssd_out_proj", "pool_w", "pool_b", "pool_scale", "mlp_w1", "mlp_w2",
             "ln_mix_g", "ln_mix_b", "ln_ffn_g", "ln_ffn_b"]
    return (loss, grad_x, *[upd[n][0] for n in order], *[upd[n][1] for n in order],
            *[upd[n][2] for n in order], *[upd[n][3] for n in order])
```

```python
import functools
import math

import jax
import jax.numpy as jnp
from jax import lax
from jax.experimental import pallas as pl
from jax.experimental.pallas import tpu as pltpu

F32 = jnp.float32
BF16 = jnp.bfloat16

N_DEV = 8
HEAD_DIM = 64
N_GROUPS = 8
D_STATE = 128
CHUNK = 128
CONV_WIDTH = 5
POOL_WINDOWS = (2, 4, 8, 16)
HALO = 8
LN_EPS = 1e-5
RMS_EPS = 1e-5
ADAM_LR = 0.001
ADAM_B1 = 0.9
ADAM_B2 = 0.999
ADAM_EPS = 1e-08
ADAM_WD = 0.01
ADAM_STEP = 10
LANES = 128
VMEM_LIMIT_BYTES = 56 * 1024 * 1024
HIGHEST = lax.Precision.HIGHEST


def _call(body, **kw):
    return pl.pallas_call(body, **kw)


def _params(*sem):
    return pltpu.CompilerParams(dimension_semantics=sem, vmem_limit_bytes=VMEM_LIMIT_BYTES)


def _tile(dim, target, align=LANES):
    if dim <= target:
        return dim
    t = (target // align) * align
    while t >= align:
        if dim % t == 0:
            return t
        t -= align
    return dim


def _dot(a, b, dims, precision=None):
    return lax.dot_general(a, b, (dims, ((), ())), precision=precision, preferred_element_type=F32)


NN = ((1,), (0,))
NT = ((1,), (1,))
TN = ((0,), (0,))


def _sigmoid(x):
    return 1.0 / (1.0 + jnp.exp(-x))


def _matmul(a, b, mode, *, name, outs, epilogue=None, extras=(), tm=1024, tn=1024, tk=2048):
    if mode == "nn":
        (M, K), (K2, N) = a.shape, b.shape
    elif mode == "nt":
        (M, K), (N, K2) = a.shape, b.shape
    else:
        (K, M), (K2, N) = a.shape, b.shape
    assert K == K2, (a.shape, b.shape, mode)
    tm, tn, tk = _tile(M, tm), _tile(N, tn), _tile(K, tk)
    nk = K // tk
    dims = {"nn": NN, "nt": NT, "tn": TN}[mode]
    a_spec = (pl.BlockSpec((tk, tm), lambda i, j, k: (k, i)) if mode == "tn"
              else pl.BlockSpec((tm, tk), lambda i, j, k: (i, k)))
    b_spec = (pl.BlockSpec((tn, tk), lambda i, j, k: (j, k)) if mode == "nt"
              else pl.BlockSpec((tk, tn), lambda i, j, k: (k, j)))
    mn_spec = pl.BlockSpec((tm, tn), lambda i, j, k: (i, j))
    n_extra, n_out = len(extras), len(outs)

    def finish(acc, extra_refs, out_refs):
        res = (acc,) if epilogue is None else epilogue(acc, *[r[...] for r in extra_refs])
        for o_ref, r in zip(out_refs, res):
            o_ref[...] = r.astype(o_ref.dtype)

    def body(*refs):
        a_ref, b_ref = refs[0], refs[1]
        extra_refs = refs[2:2 + n_extra]
        out_refs = refs[2 + n_extra:2 + n_extra + n_out]
        part = _dot(a_ref[...].astype(BF16), b_ref[...].astype(BF16), dims)
        if nk == 1:
            finish(part, extra_refs, out_refs)
            return
        acc_ref = refs[-1]
        k = pl.program_id(2)

        @pl.when(k == 0)
        def _():
            acc_ref[...] = part

        @pl.when((k > 0) & (k < nk - 1))
        def _():
            acc_ref[...] += part

        @pl.when(k == nk - 1)
        def _():
            finish(acc_ref[...] + part, extra_refs, out_refs)

    res = _call(
        body, name=name, grid=(M // tm, N // tn, nk),
        in_specs=[a_spec, b_spec] + [mn_spec] * n_extra,
        out_specs=[mn_spec] * n_out,
        out_shape=[jax.ShapeDtypeStruct((M, N), dt) for dt in outs],
        scratch_shapes=[pltpu.VMEM((tm, tn), F32)] if nk > 1 else [],
        compiler_params=_params("parallel", "parallel", "arbitrary"),
    )(a, b, *extras)
    return res


def _ln_fwd(x, f, g, b, alpha, name):
    T, D = x.shape
    tm = _tile(T, 256, 8)

    def body(x_ref, f_ref, g_ref, b_ref, y_ref, yb_ref):
        s = alpha * x_ref[...] + f_ref[...]
        mu = jnp.mean(s, axis=-1, keepdims=True)
        d = s - mu
        var = jnp.mean(d * d, axis=-1, keepdims=True)
        y = d * lax.rsqrt(var + LN_EPS) * g_ref[...] + b_ref[...]
        y_ref[...] = y
        yb_ref[...] = y.astype(BF16)

    row = pl.BlockSpec((tm, D), lambda i: (i, 0))
    vec = pl.BlockSpec((1, D), lambda i: (0, 0))
    return _call(body, name=name, grid=(T // tm,), in_specs=[row, row, vec, vec], out_specs=[row, row],
                 out_shape=[jax.ShapeDtypeStruct((T, D), F32), jax.ShapeDtypeStruct((T, D), BF16)],
                 compiler_params=_params("parallel"))(x, f, g.reshape(1, D), b.reshape(1, D))


def _ln_bwd(x, f, g, dy, alpha, name):
    T, D = x.shape
    tm = _tile(T, 256, 8)

    def body(x_ref, f_ref, g_ref, dy_ref, ds_ref, dsb_ref, dg_ref, db_ref):
        i = pl.program_id(0)

        @pl.when(i == 0)
        def _():
            dg_ref[...] = jnp.zeros_like(dg_ref)
            db_ref[...] = jnp.zeros_like(db_ref)

        s = alpha * x_ref[...] + f_ref[...]
        mu = jnp.mean(s, axis=-1, keepdims=True)
        d = s - mu
        var = jnp.mean(d * d, axis=-1, keepdims=True)
        rstd = lax.rsqrt(var + LN_EPS)
        xhat = d * rstd
        dy_ = dy_ref[...]
        dg_ref[...] += jnp.sum(dy_ * xhat, axis=0, keepdims=True)
        db_ref[...] += jnp.sum(dy_, axis=0, keepdims=True)
        dxh = dy_ * g_ref[...]
        m1 = jnp.mean(dxh, axis=-1, keepdims=True)
        m2 = jnp.mean(dxh * xhat, axis=-1, keepdims=True)
        ds = rstd * (dxh - m1 - xhat * m2)
        ds_ref[...] = ds
        dsb_ref[...] = ds.astype(BF16)

    row = pl.BlockSpec((tm, D), lambda i: (i, 0))
    vec = pl.BlockSpec((1, D), lambda i: (0, 0))
    return _call(body, name=name, grid=(T // tm,), in_specs=[row, row, vec, row], out_specs=[row, row, vec, vec],
                 out_shape=[jax.ShapeDtypeStruct((T, D), F32), jax.ShapeDtypeStruct((T, D), BF16),
                            jax.ShapeDtypeStruct((1, D), F32), jax.ShapeDtypeStruct((1, D), F32)],
                 compiler_params=_params("arbitrary"))(x, f, g.reshape(1, D), dy)


def _loss_head(y, target):
    T, D = y.shape
    tm = _tile(T, 256, 8)

    def body(y_ref, t_ref, loss_ref, dy_ref):
        i = pl.program_id(0)

        @pl.when(i == 0)
        def _():
            loss_ref[...] = jnp.zeros_like(loss_ref)

        err = y_ref[...] - t_ref[...]
        dy_ref[...] = err * (1.0 / D)
        per_tok = jnp.mean(err * err, axis=-1, keepdims=True)
        loss_ref[...] += 0.5 * jnp.sum(per_tok)

    row = pl.BlockSpec((tm, D), lambda i: (i, 0))
    return _call(body, name="loss_head", grid=(T // tm,), in_specs=[row, row],
                 out_specs=[pl.BlockSpec((1, LANES), lambda i: (0, 0)), row],
                 out_shape=[jax.ShapeDtypeStruct((1, LANES), F32), jax.ShapeDtypeStruct((T, D), F32)],
                 compiler_params=_params("arbitrary"))(y, target)


def _halo_specs(tt, cw, col_of, n_tiles, grid_rank_tokens_axis):
    per = tt // HALO
    ax = grid_rank_tokens_axis

    def cur(*g):
        return (g[ax], col_of(*g))

    def prev(*g):
        return (jnp.maximum(g[ax] * per - 1, 0), col_of(*g))

    def nxt(*g):
        return (jnp.minimum((g[ax] + 1) * per, n_tiles * per - 1), col_of(*g))

    return [pl.BlockSpec((tt, cw), cur), pl.BlockSpec((HALO, cw), prev), pl.BlockSpec((HALO, cw), nxt)]


def _fill_ext(ext_ref, cur_ref, prev_ref, next_ref, i, n_tiles, tt):
    ext_ref[pl.ds(0, HALO), :] = jnp.where(i > 0, prev_ref[...], 0.0)
    ext_ref[pl.ds(HALO, tt), :] = cur_ref[...]
    ext_ref[pl.ds(HALO + tt, HALO), :] = jnp.where(i < n_tiles - 1, next_ref[...], 0.0)


def _conv_pre(ext_ref, w, bias, tt, lo=0, n=None):
    n = tt if n is None else n
    pad = CONV_WIDTH // 2
    acc = None
    for k in range(CONV_WIDTH):
        term = ext_ref[pl.ds(HALO + lo + k - pad, n), :] * w[k:k + 1, :]
        acc = term if acc is None else acc + term
    return acc + bias


def _conv_fwd(zx, conv_w, conv_b, d_inner, name):
    T = zx.shape[0]
    d_xbc = conv_w.shape[1]
    cw = _tile(d_xbc, 512)
    assert d_inner % cw == 0
    off = d_inner // cw
    tt = _tile(T, 512, 8)
    nt = T // tt

    def body(cur_ref, prev_ref, next_ref, w_ref, b_ref, o_ref, ext_ref):
        i = pl.program_id(1)
        _fill_ext(ext_ref, cur_ref, prev_ref, next_ref, i, nt, tt)
        pre = _conv_pre(ext_ref, w_ref[...], b_ref[...], tt)
        o_ref[...] = pre * _sigmoid(pre)

    specs = _halo_specs(tt, cw, lambda j, i: off + j, nt, 1)
    return _call(body, name=name, grid=(d_xbc // cw, nt),
                 in_specs=specs + [pl.BlockSpec((CONV_WIDTH, cw), lambda j, i: (0, j)),
                                   pl.BlockSpec((1, cw), lambda j, i: (0, j))],
                 out_specs=pl.BlockSpec((tt, cw), lambda j, i: (i, j)),
                 out_shape=jax.ShapeDtypeStruct((T, d_xbc), F32),
                 scratch_shapes=[pltpu.VMEM((tt + 2 * HALO, cw), F32)],
                 compiler_params=_params("parallel", "parallel"))(zx, zx, zx, conv_w, conv_b.reshape(1, d_xbc))


def _conv_dpre(zx, conv_w, conv_b, d_inner, col_lo, dirs, extra, name):
    T = zx.shape[0]
    ncols = dirs.shape[2]
    cw = _tile(ncols, 512)
    assert d_inner % cw == 0 and col_lo % cw == 0
    off_zx = (d_inner + col_lo) // cw
    off_w = col_lo // cw
    tt = _tile(T, 512, 8)
    nt = T // tt
    has_extra = extra is not None

    def body(cur_ref, prev_ref, next_ref, w_ref, b_ref, dirs_ref, *rest):
        o_ref, ext_ref = rest[-2], rest[-1]
        i = pl.program_id(1)
        _fill_ext(ext_ref, cur_ref, prev_ref, next_ref, i, nt, tt)
        pre = _conv_pre(ext_ref, w_ref[...], b_ref[...], tt)
        sig = _sigmoid(pre)
        dact = dirs_ref[0] + dirs_ref[1]
        if has_extra:
            dact = dact + rest[0][...]
        o_ref[...] = dact * (sig * (1.0 + pre * (1.0 - sig)))

    specs = _halo_specs(tt, cw, lambda j, i: off_zx + j, nt, 1)
    in_specs = specs + [pl.BlockSpec((CONV_WIDTH, cw), lambda j, i: (0, off_w + j)),
                        pl.BlockSpec((1, cw), lambda j, i: (0, off_w + j)),
                        pl.BlockSpec((2, tt, cw), lambda j, i: (0, i, j))]
    args = [zx, zx, zx, conv_w, conv_b.reshape(1, -1), dirs]
    if has_extra:
        in_specs.append(pl.BlockSpec((tt, cw), lambda j, i: (i, j)))
        args.append(extra)
    return _call(body, name=name, grid=(ncols // cw, nt), in_specs=in_specs,
                 out_specs=pl.BlockSpec((tt, cw), lambda j, i: (i, j)),
                 out_shape=jax.ShapeDtypeStruct((T, ncols), F32),
                 scratch_shapes=[pltpu.VMEM((tt + 2 * HALO, cw), F32)],
                 compiler_params=_params("parallel", "parallel"))(*args)


def _conv_bwd(zx, dpre, conv_w, d_inner, name):
    T = zx.shape[0]
    d_xbc = conv_w.shape[1]
    cw = _tile(d_xbc, 512)
    off = d_inner // cw
    tt = _tile(T, 512, 8)
    nt = T // tt
    pad = CONV_WIDTH // 2

    def body(zc, zp, zn, dc, dp, dn, w_ref, din_ref, dw_ref, db_ref, zext, dext):
        i = pl.program_id(1)

        @pl.when(i == 0)
        def _():
            dw_ref[...] = jnp.zeros_like(dw_ref)
            db_ref[...] = jnp.zeros_like(db_ref)

        _fill_ext(zext, zc, zp, zn, i, nt, tt)
        _fill_ext(dext, dc, dp, dn, i, nt, tt)
        w = w_ref[...]
        d = dc[...]
        acc = None
        for k in range(CONV_WIDTH):
            term = dext[pl.ds(HALO + pad - k, tt), :] * w[k:k + 1, :]
            acc = term if acc is None else acc + term
            dw_ref[k:k + 1, :] += jnp.sum(d * zext[pl.ds(HALO + k - pad, tt), :], axis=0, keepdims=True)
        din_ref[...] = acc.astype(din_ref.dtype)
        db_ref[...] += jnp.sum(d, axis=0, keepdims=True)

    zspecs = _halo_specs(tt, cw, lambda j, i: off + j, nt, 1)
    dspecs = _halo_specs(tt, cw, lambda j, i: j, nt, 1)
    return _call(body, name=name, grid=(d_xbc // cw, nt),
                 in_specs=zspecs + dspecs + [pl.BlockSpec((CONV_WIDTH, cw), lambda j, i: (0, j))],
                 out_specs=[pl.BlockSpec((tt, cw), lambda j, i: (i, j)),
                            pl.BlockSpec((CONV_WIDTH, cw), lambda j, i: (0, j)),
                            pl.BlockSpec((1, cw), lambda j, i: (0, j))],
                 out_shape=[jax.ShapeDtypeStruct((T, d_xbc), BF16), jax.ShapeDtypeStruct((CONV_WIDTH, d_xbc), F32),
                            jax.ShapeDtypeStruct((1, d_xbc), F32)],
                 scratch_shapes=[pltpu.VMEM((tt + 2 * HALO, cw), F32), pltpu.VMEM((tt + 2 * HALO, cw), F32)],
                 compiler_params=_params("parallel", "arbitrary"))(zx, zx, zx, dpre, dpre, dpre, conv_w)


def _tri(n):
    r = lax.broadcasted_iota(jnp.int32, (n, n), 0)
    c = lax.broadcasted_iota(jnp.int32, (n, n), 1)
    return (r >= c).astype(F32), (r <= c).astype(F32)


def _dt_fwd(raw, bias, a_log, name):
    T, H2 = raw.shape
    half = H2 // 2

    def body(raw_ref, bias_ref, alog_ref, dt_ref, cs_ref):
        x = raw_ref[...] + bias_ref[...]
        dt = jnp.maximum(x, 0.0) + jnp.log(1.0 + jnp.exp(-jnp.abs(x)))
        a = dt * (-jnp.exp(alog_ref[...]))
        lower, upper = _tri(CHUNK)
        cs_f = _dot(lower, a, NN, HIGHEST)
        cs_b = _dot(upper, a, NN, HIGHEST)
        lane = lax.broadcasted_iota(jnp.int32, (CHUNK, H2), 1)
        dt_ref[...] = dt
        cs_ref[...] = jnp.where(lane < half, cs_f, cs_b)

    row = pl.BlockSpec((CHUNK, H2), lambda c: (c, 0))
    vec = pl.BlockSpec((1, H2), lambda c: (0, 0))
    return _call(body, name=name, grid=(T // CHUNK,), in_specs=[row, vec, vec], out_specs=[row, row],
                 out_shape=[jax.ShapeDtypeStruct((T, H2), F32)] * 2,
                 compiler_params=_params("parallel"))(raw, bias.reshape(1, H2), a_log.reshape(1, H2))


def _dt_bwd(raw, bias, a_log, dcs, dtot, dxdtx, name):
    T, H2 = raw.shape
    half = H2 // 2

    def body(raw_ref, bias_ref, alog_ref, dcs_ref, dtot_ref, dx_ref, draw_ref, dbias_ref, dalog_ref):
        c = pl.program_id(0)

        @pl.when(c == 0)
        def _():
            dbias_ref[...] = jnp.zeros_like(dbias_ref)
            dalog_ref[...] = jnp.zeros_like(dalog_ref)

        x = raw_ref[...] + bias_ref[...]
        dt = jnp.maximum(x, 0.0) + jnp.log(1.0 + jnp.exp(-jnp.abs(x)))
        A = -jnp.exp(alog_ref[...])
        lower, upper = _tri(CHUNK)
        g = dcs_ref[...]
        lane = lax.broadcasted_iota(jnp.int32, (CHUNK, H2), 1)
        da = jnp.where(lane < half, _dot(upper, g, NN, HIGHEST), _dot(lower, g, NN, HIGHEST)) + dtot_ref[...]
        ddt = da * A + dx_ref[...]
        draw = ddt * _sigmoid(x)
        draw_ref[...] = draw.astype(draw_ref.dtype)
        dbias_ref[...] += jnp.sum(draw, axis=0, keepdims=True)
        dalog_ref[...] += jnp.sum(da * dt, axis=0, keepdims=True) * A

    row = pl.BlockSpec((CHUNK, H2), lambda c: (c, 0))
    vec = pl.BlockSpec((1, H2), lambda c: (0, 0))
    return _call(body, name=name, grid=(T // CHUNK,), in_specs=[row, vec, vec, row, row, row],
                 out_specs=[row, vec, vec],
                 out_shape=[jax.ShapeDtypeStruct((T, H2), BF16), jax.ShapeDtypeStruct((1, H2), F32),
                            jax.ShapeDtypeStruct((1, H2), F32)],
                 compiler_params=_params("arbitrary"))(raw, bias.reshape(1, H2), a_log.reshape(1, H2), dcs, dtot, dxdtx)


def _cols(a, hg):
    T = a.shape[0]
    return a.reshape(T, 2, N_GROUPS, hg).transpose(1, 2, 0, 3)


def _rows(a, hg):
    T = a.shape[0]
    return a.reshape(T, 2, N_GROUPS, hg).transpose(1, 2, 3, 0)


def _uncols(a):
    T = a.shape[2]
    return a.transpose(2, 0, 1, 3).reshape(T, -1)


def _ssd_masks(d):
    r = lax.broadcasted_iota(jnp.int32, (CHUNK, CHUNK), 0)
    c = lax.broadcasted_iota(jnp.int32, (CHUNK, CHUNK), 1)
    return ((r >= c) & (d == 0)) | ((r <= c) & (d == 1))


def _head_expand(hg):
    r = lax.broadcasted_iota(jnp.int32, (hg, hg * HEAD_DIM), 0)
    c = lax.broadcasted_iota(jnp.int32, (hg, hg * HEAD_DIM), 1)
    return (c // HEAD_DIM == r).astype(F32)


def _head_select(hg):
    r = lax.broadcasted_iota(jnp.int32, (hg * HEAD_DIM, hg), 0)
    c = lax.broadcasted_iota(jnp.int32, (hg * HEAD_DIM, hg), 1)
    return (r // HEAD_DIM == c).astype(F32)


def _ssd_common(d, csc_ref, dtc_ref, hg):
    expand = _head_expand(hg)
    csx = _dot(csc_ref[...], expand, NN, HIGHEST)
    dtx = _dot(dtc_ref[...], expand, NN, HIGHEST)
    totx = jnp.where(d == 0, csx[CHUNK - 1:CHUNK, :], csx[0:1, :])
    return csx, dtx, totx


def _ssd_fwd(xbc, dtc, csc, csr, d_inner, name):
    T = xbc.shape[0]
    nc = T // CHUNK
    gw = d_inner // N_GROUPS
    hg = gw // HEAD_DIM
    P, N = HEAD_DIM, D_STATE
    b_off = d_inner // N
    c_off = b_off + N_GROUPS

    def cidx(d, c):
        return c + d * (nc - 1 - 2 * c)

    def body(xs_ref, b_ref, c_ref, dtc_ref, csc_ref, csr_ref, y_ref, st_ref, h_ref):
        d = pl.program_id(0)
        c = pl.program_id(2)

        @pl.when(c == 0)
        def _():
            h_ref[...] = jnp.zeros_like(h_ref)

        Bb = b_ref[...].astype(BF16)
        Cb = c_ref[...].astype(BF16)
        S = _dot(Cb, Bb, NT)
        mask = _ssd_masks(d)
        csx, dtx, totx = _ssd_common(d, csc_ref, dtc_ref, hg)
        H = h_ref[...]
        st_ref[...] = H
        xdt = xs_ref[...] * dtx
        xdtb = xdt.astype(BF16)
        y_off = jnp.exp(csx) * _dot(Cb, H.astype(BF16), NN)
        for j in range(hg):
            sl = slice(j * P, (j + 1) * P)
            decay = jnp.exp(jnp.where(mask, csc_ref[:, j:j + 1] - csr_ref[j:j + 1, :], -jnp.inf))
            y_ref[:, sl] = _dot((S * decay).astype(BF16), xdtb[:, sl], NN) + y_off[:, sl]
        h_ref[...] = jnp.exp(totx) * H + _dot(Bb, (jnp.exp(totx - csx) * xdt).astype(BF16), TN)

    col = lambda d, g, c: (d, g, cidx(d, c), 0)
    return _call(
        body, name=name, grid=(2, N_GROUPS, nc),
        in_specs=[pl.BlockSpec((CHUNK, gw), lambda d, g, c: (cidx(d, c), g)),
                  pl.BlockSpec((CHUNK, N), lambda d, g, c: (cidx(d, c), b_off + g)),
                  pl.BlockSpec((CHUNK, N), lambda d, g, c: (cidx(d, c), c_off + g)),
                  pl.BlockSpec((None, None, CHUNK, hg), col),
                  pl.BlockSpec((None, None, CHUNK, hg), col),
                  pl.BlockSpec((None, None, hg, CHUNK), lambda d, g, c: (d, g, 0, cidx(d, c)))],
        out_specs=[pl.BlockSpec((None, CHUNK, gw), lambda d, g, c: (d, cidx(d, c), g)),
                   pl.BlockSpec((None, None, None, N, gw), lambda d, g, c: (d, cidx(d, c), g, 0, 0))],
        out_shape=[jax.ShapeDtypeStruct((2, T, d_inner), F32),
                   jax.ShapeDtypeStruct((2, nc, N_GROUPS, N, gw), F32)],
        scratch_shapes=[pltpu.VMEM((N, gw), F32)],
        compiler_params=_params("parallel", "parallel", "arbitrary"),
    )(xbc, xbc, xbc, dtc, csc, csr)


def _ssd_bwd(xbc, dtc, csc, csr, states, y2, dy, d_inner, name):
    T = xbc.shape[0]
    nc = T // CHUNK
    gw = d_inner // N_GROUPS
    hg = gw // HEAD_DIM
    P, N = HEAD_DIM, D_STATE
    b_off = d_inner // N
    c_off = b_off + N_GROUPS

    def cidx(d, c):
        return (nc - 1 - c) + d * (2 * c - nc + 1)

    def body(xs_ref, b_ref, c_ref, dtc_ref, csc_ref, csr_ref, st_ref, y_ref, dy_ref,
             dxs_ref, db_ref, dc_ref, dcs_ref, dtot_ref, dxdtx_ref, dh_ref, dxdt_ref):
        d = pl.program_id(0)
        c = pl.program_id(2)

        @pl.when(c == 0)
        def _():
            dh_ref[...] = jnp.zeros_like(dh_ref)

        Bb = b_ref[...].astype(BF16)
        Cb = c_ref[...].astype(BF16)
        S = _dot(Cb, Bb, NT)
        mask = _ssd_masks(d)
        csx, dtx, totx = _ssd_common(d, csc_ref, dtc_ref, hg)
        select = _head_select(hg)
        X = xs_ref[...]
        xdt = X * dtx
        xdtb = xdt.astype(BF16)
        dY = dy_ref[...]
        dYb = dY.astype(BF16)
        Hp = st_ref[...]
        Hpb = Hp.astype(BF16)
        dH = dh_ref[...]
        dHb = dH.astype(BF16)
        e_tot = jnp.exp(totx)
        dCH = (jnp.exp(csx) * dY).astype(BF16)
        dC = _dot(dCH, Hpb, NT)
        dHp = _dot(Cb, dCH, TN)
        Q = _dot(Bb, dHb, NN)
        dte = jnp.exp(totx - csx)
        wx = dte * xdt
        dB = _dot(wx.astype(BF16), dHb, NT)
        ddte = Q * wx
        dS = jnp.zeros((CHUNK, CHUNK), F32)
        for j in range(hg):
            sl = slice(j * P, (j + 1) * P)
            decay = jnp.exp(jnp.where(mask, csc_ref[:, j:j + 1] - csr_ref[j:j + 1, :], -jnp.inf))
            dS = dS + _dot(dYb[:, sl], xdtb[:, sl], NT) * decay
            dxdt_ref[:, sl] = _dot((S * decay).astype(BF16), dYb[:, sl], TN)
        dxdt_diag = dxdt_ref[...]
        dxdt = dxdt_diag + dte * Q
        dcs_ref[...] = _dot(dYb.astype(F32) * y_ref[...] - xdtb.astype(F32) * dxdt_diag - ddte, select, NN, HIGHEST)
        dtot_row = (jnp.sum(ddte, axis=0, keepdims=True) + e_tot * jnp.sum(dH * Hp, axis=0, keepdims=True))
        dtot_ref[...] = jnp.zeros((CHUNK, hg), F32) + _dot(dtot_row, select, NN, HIGHEST)
        dxdtx_ref[...] = _dot(dxdt * X, select, NN, HIGHEST)
        dxs_ref[...] = dxdt * dtx
        dh_ref[...] = e_tot * dH + dHp
        dSb = dS.astype(BF16)
        dc_ref[...] = dC + _dot(dSb, Bb, NN)
        db_ref[...] = dB + _dot(dSb, Cb, TN)

    col = lambda d, g, c: (d, g, cidx(d, c), 0)
    colspec = pl.BlockSpec((None, None, CHUNK, hg), col)
    rowblk = pl.BlockSpec((None, CHUNK, gw), lambda d, g, c: (d, cidx(d, c), g))
    return _call(
        body, name=name, grid=(2, N_GROUPS, nc),
        in_specs=[pl.BlockSpec((CHUNK, gw), lambda d, g, c: (cidx(d, c), g)),
                  pl.BlockSpec((CHUNK, N), lambda d, g, c: (cidx(d, c), b_off + g)),
                  pl.BlockSpec((CHUNK, N), lambda d, g, c: (cidx(d, c), c_off + g)),
                  colspec, colspec,
                  pl.BlockSpec((None, None, hg, CHUNK), lambda d, g, c: (d, g, 0, cidx(d, c))),
                  pl.BlockSpec((None, None, None, N, gw), lambda d, g, c: (d, cidx(d, c), g, 0, 0)),
                  rowblk,
                  pl.BlockSpec((CHUNK, gw), lambda d, g, c: (cidx(d, c), g))],
        out_specs=[rowblk,
                   pl.BlockSpec((None, CHUNK, N), lambda d, g, c: (d, cidx(d, c), g)),
                   pl.BlockSpec((None, CHUNK, N), lambda d, g, c: (d, cidx(d, c), g)),
                   colspec, colspec, colspec],
        out_shape=[jax.ShapeDtypeStruct((2, T, d_inner), F32),
                   jax.ShapeDtypeStruct((2, T, N_GROUPS * N), F32),
                   jax.ShapeDtypeStruct((2, T, N_GROUPS * N), F32)]
        + [jax.ShapeDtypeStruct((2, N_GROUPS, T, hg), F32)] * 3,
        scratch_shapes=[pltpu.VMEM((N, gw), F32), pltpu.VMEM((CHUNK, gw), F32)],
        compiler_params=_params("parallel", "parallel", "arbitrary"),
    )(xbc, xbc, xbc, dtc, csc, csr, states, y2, dy)


def _gnorm_fwd(y2, xbc, zx, dvec, nw, d_inner, name):
    T = xbc.shape[0]
    gw = d_inner // N_GROUPS
    tm = _tile(T, 128, 8)

    def body(y_ref, xs_ref, z_ref, d_ref, w_ref, o_ref):
        for g in range(N_GROUPS):
            sl = slice(g * gw, (g + 1) * gw)
            y = y_ref[0, :, sl] + y_ref[1, :, sl] + xs_ref[:, sl] * d_ref[:, sl]
            z = z_ref[:, sl]
            gy = y * (z * _sigmoid(z))
            rs = lax.rsqrt(jnp.mean(gy * gy, axis=-1, keepdims=True) + RMS_EPS)
            o_ref[:, sl] = (gy * rs * w_ref[:, sl]).astype(o_ref.dtype)

    row = pl.BlockSpec((tm, d_inner), lambda i: (i, 0))
    vec = pl.BlockSpec((1, d_inner), lambda i: (0, 0))
    return _call(body, name=name, grid=(T // tm,),
                 in_specs=[pl.BlockSpec((2, tm, d_inner), lambda i: (0, i, 0)), row, row, vec, vec],
                 out_specs=row, out_shape=jax.ShapeDtypeStruct((T, d_inner), BF16),
                 compiler_params=_params("parallel"))(y2, xbc, zx, dvec, nw)


def _gnorm_bwd(y2, xbc, zx, dvec, nw, dgn, sel, d_inner, name):
    T = xbc.shape[0]
    gw = d_inner // N_GROUPS
    n_heads = d_inner // HEAD_DIM
    tm = _tile(T, 128, 8)
    n_tiles = T // tm

    def body(y_ref, xs_ref, z_ref, d_ref, w_ref, dg_ref, sel_ref, dy_ref, dxs_ref, dz_ref, dw_ref, dd_ref, dch_ref):
        i = pl.program_id(0)

        @pl.when(i == 0)
        def _():
            dw_ref[...] = jnp.zeros_like(dw_ref)
            dch_ref[...] = jnp.zeros_like(dch_ref)

        for g in range(N_GROUPS):
            sl = slice(g * gw, (g + 1) * gw)
            xs = xs_ref[:, sl]
            y = y_ref[0, :, sl] + y_ref[1, :, sl] + xs * d_ref[:, sl]
            z = z_ref[:, sl]
            sig = _sigmoid(z)
            sz = z * sig
            gy = y * sz
            rs = lax.rsqrt(jnp.mean(gy * gy, axis=-1, keepdims=True) + RMS_EPS)
            n = gy * rs
            dout = dg_ref[:, sl]
            dw_ref[:, sl] += jnp.sum(dout * n, axis=0, keepdims=True)
            dn = dout * w_ref[:, sl]
            dgy = rs * (dn - n * jnp.mean(dn * n, axis=-1, keepdims=True))
            dy = dgy * sz
            dy_ref[:, sl] = dy
            dz_ref[:, sl] = (dgy * y * (sig * (1.0 + z * (1.0 - sig)))).astype(dz_ref.dtype)
            dxs_ref[:, sl] = dy * d_ref[:, sl]
            dch_ref[:, sl] += jnp.sum(dy * xs, axis=0, keepdims=True)

        @pl.when(i == n_tiles - 1)
        def _():
            dd_ref[...] = _dot(dch_ref[...], sel_ref[...], NN, HIGHEST)

    row = pl.BlockSpec((tm, d_inner), lambda i: (i, 0))
    vec = pl.BlockSpec((1, d_inner), lambda i: (0, 0))
    hvec = pl.BlockSpec((1, n_heads), lambda i: (0, 0))
    return _call(body, name=name, grid=(n_tiles,),
                 in_specs=[pl.BlockSpec((2, tm, d_inner), lambda i: (0, i, 0)), row, row, vec, vec, row,
                           pl.BlockSpec((d_inner, n_heads), lambda i: (0, 0))],
                 out_specs=[row, row, row, vec, hvec],
                 out_shape=[jax.ShapeDtypeStruct((T, d_inner), F32), jax.ShapeDtypeStruct((T, d_inner), F32),
                            jax.ShapeDtypeStruct((T, d_inner), BF16),
                            jax.ShapeDtypeStruct((1, d_inner), F32), jax.ShapeDtypeStruct((1, n_heads), F32)],
                 scratch_shapes=[pltpu.VMEM((1, d_inner), F32)],
                 compiler_params=_params("arbitrary"))(y2, xbc, zx, dvec, nw, dgn, sel)


def _pool_counts(i, tt, T, win, rows, row0):
    t = i * tt + row0 + lax.broadcasted_iota(jnp.int32, (rows, 1), 0)
    start = t - win // 2
    lo = jnp.clip(start, 0, T)
    hi = jnp.clip(start + win, 0, T)
    return jnp.maximum(hi - lo, 1).astype(F32)


def _pool_features(ext_ref, i, tt, T, gi, gd):
    win = POOL_WINDOWS[gi]
    sl = slice(gi * gd, (gi + 1) * gd)
    acc = None
    for o in range(-(win // 2), win - win // 2):
        term = ext_ref[pl.ds(HALO + o, tt), sl]
        acc = term if acc is None else acc + term
    return acc / _pool_counts(i, tt, T, win, tt, 0) - ext_ref[pl.ds(HALO, tt), sl]


def _pool_fwd(u, w, bias, scale, name):
    T, D = u.shape
    ng = len(POOL_WINDOWS)
    gd = D // ng
    tt = _tile(T, 512, 8)
    nt = T // tt

    def body(cur, prev, nxt, w_ref, b_ref, s_ref, o_ref, ext):
        i = pl.program_id(0)
        _fill_ext(ext, cur, prev, nxt, i, nt, tt)
        for gi in range(ng):
            sl = slice(gi * gd, (gi + 1) * gd)
            m = _pool_features(ext, i, tt, T, gi, gd)
            pre = _dot(m.astype(BF16), w_ref[gi], NN) + b_ref[:, sl]
            o_ref[:, sl] = pre * s_ref[:, sl]

    vec = pl.BlockSpec((1, D), lambda i: (0, 0))
    return _call(body, name=name, grid=(nt,),
                 in_specs=_halo_specs(tt, D, lambda i: 0, nt, 0)
                 + [pl.BlockSpec((ng, gd, gd), lambda i: (0, 0, 0)), vec, vec],
                 out_specs=pl.BlockSpec((tt, D), lambda i: (i, 0)),
                 out_shape=jax.ShapeDtypeStruct((T, D), F32),
                 scratch_shapes=[pltpu.VMEM((tt + 2 * HALO, D), F32)],
                 compiler_params=_params("parallel"))(u, u, u, w, bias, scale)


def _pool_bwd_a(u, w, bias, scale, dy, name):
    T, D = u.shape
    ng = len(POOL_WINDOWS)
    gd = D // ng
    tt = _tile(T, 512, 8)
    nt = T // tt

    def body(cur, prev, nxt, w_ref, b_ref, s_ref, dy_ref, dm_ref, dw_ref, db_ref, ds_ref, ext):
        i = pl.program_id(0)

        @pl.when(i == 0)
        def _():
            dw_ref[...] = jnp.zeros_like(dw_ref)
            db_ref[...] = jnp.zeros_like(db_ref)
            ds_ref[...] = jnp.zeros_like(ds_ref)

        _fill_ext(ext, cur, prev, nxt, i, nt, tt)
        for gi in range(ng):
            sl = slice(gi * gd, (gi + 1) * gd)
            mb = _pool_features(ext, i, tt, T, gi, gd).astype(BF16)
            wg = w_ref[gi]
            pre = _dot(mb, wg, NN) + b_ref[:, sl]
            dy_ = dy_ref[:, sl]
            ds_ref[:, sl] += jnp.sum(dy_ * pre, axis=0, keepdims=True)
            dpre = dy_ * s_ref[:, sl]
            db_ref[:, sl] += jnp.sum(dpre, axis=0, keepdims=True)
            dpb = dpre.astype(BF16)
            dw_ref[gi] += _dot(mb, dpb, TN)
            dm_ref[:, sl] = _dot(dpb, wg, NT)

    vec = pl.BlockSpec((1, D), lambda i: (0, 0))
    row = pl.BlockSpec((tt, D), lambda i: (i, 0))
    wspec = pl.BlockSpec((ng, gd, gd), lambda i: (0, 0, 0))
    return _call(body, name=name, grid=(nt,),
                 in_specs=_halo_specs(tt, D, lambda i: 0, nt, 0) + [wspec, vec, vec, row],
                 out_specs=[row, wspec, vec, vec],
                 out_shape=[jax.ShapeDtypeStruct((T, D), F32), jax.ShapeDtypeStruct((ng, gd, gd), F32),
                            jax.ShapeDtypeStruct((1, D), F32), jax.ShapeDtypeStruct((1, D), F32)],
                 scratch_shapes=[pltpu.VMEM((tt + 2 * HALO, D), F32)],
                 compiler_params=_params("arbitrary"))(u, u, u, w, bias, scale, dy)


def _pool_bwd_b(dm, dy, alpha, name):
    T, D = dm.shape
    ng = len(POOL_WINDOWS)
    gd = D // ng
    tt = _tile(T, 512, 8)
    nt = T // tt

    def body(cur, prev, nxt, dy_ref, o_ref, ext):
        i = pl.program_id(0)
        _fill_ext(ext, cur, prev, nxt, i, nt, tt)
        for gi, win in enumerate(POOL_WINDOWS):
            sl = slice(gi * gd, (gi + 1) * gd)
            rows = tt + 2 * HALO
            ext[:, sl] = ext[:, sl] / _pool_counts(i, tt, T, win, rows, -HALO)
            acc = None
            for o in range(-(win // 2) + 1, win // 2 + 1):
                term = ext[pl.ds(HALO + o, tt), sl]
                acc = term if acc is None else acc + term
            o_ref[:, sl] = alpha * dy_ref[:, sl] + acc - cur[:, sl]

    row = pl.BlockSpec((tt, D), lambda i: (i, 0))
    return _call(body, name=name, grid=(nt,),
                 in_specs=_halo_specs(tt, D, lambda i: 0, nt, 0) + [row], out_specs=row,
                 out_shape=jax.ShapeDtypeStruct((T, D), F32),
                 scratch_shapes=[pltpu.VMEM((tt + 2 * HALO, D), F32)],
                 compiler_params=_params("parallel"))(dm, dm, dm, dy)


def _mesh_pos():
    x, y, c = lax.axis_index("x"), lax.axis_index("y"), lax.axis_index("c")
    return x, y, c


def _peer(x, y, c, k):
    dx, dy, dc = (k >> 2) & 1, (k >> 1) & 1, k & 1
    px = (1 - x) if dx else x
    py = (1 - y) if dy else y
    pc = (1 - c) if dc else c
    return px, py, pc


def _exchange(arrays, gather, name):
    n = len(arrays)

    def body(*refs):
        ins, outs = refs[:n], refs[n:2 * n]
        send_sems, recv_sems, local_sems = refs[2 * n:]
        x, y, c = _mesh_pos()
        me = 4 * x + 2 * y + c
        copies = []
        for a in range(n):
            src_mine = ins[a] if gather else ins[a].at[me]
            local = pltpu.make_async_copy(src_mine, outs[a].at[me], local_sems.at[a])
            local.start()
            copies.append(local)
            for k in range(1, N_DEV):
                px, py, pc = _peer(x, y, c, k)
                peer = 4 * px + 2 * py + pc
                rc = pltpu.make_async_remote_copy(
                    src_ref=ins[a] if gather else ins[a].at[peer],
                    dst_ref=outs[a].at[me],
                    send_sem=send_sems.at[a, k - 1], recv_sem=recv_sems.at[a, k - 1],
                    device_id=(px, py, pc), device_id_type=pl.DeviceIdType.MESH)
                rc.start()
                copies.append(rc)
        for cp in copies:
            cp.wait()

    any_spec = pl.BlockSpec(memory_space=pl.ANY)
    out_shape = [jax.ShapeDtypeStruct(((N_DEV,) + a.shape) if gather else a.shape, a.dtype) for a in arrays]
    return _call(body, name=name, in_specs=[any_spec] * n, out_specs=[any_spec] * n, out_shape=out_shape,
                 scratch_shapes=[pltpu.SemaphoreType.DMA((n, N_DEV - 1)), pltpu.SemaphoreType.DMA((n, N_DEV - 1)),
                                 pltpu.SemaphoreType.DMA((n,))],
                 compiler_params=pltpu.CompilerParams(has_side_effects=True))(*arrays)


def _adamw(parts, w, m, v, name):
    R, C = w.shape
    tr = _tile(R, max(8, (1 << 18) // C // 8 * 8), 8)

    def body(p_ref, w_ref, m_ref, v_ref, g_ref, d_ref, nm_ref, nv_ref):
        g = p_ref[0].astype(F32)
        for i in range(1, N_DEV):
            g = g + p_ref[i].astype(F32)
        mm = ADAM_B1 * m_ref[...] + (1.0 - ADAM_B1) * g
        vv = ADAM_B2 * v_ref[...] + (1.0 - ADAM_B2) * (g * g)
        m_hat = mm / (1.0 - ADAM_B1 ** ADAM_STEP)
        v_hat = vv / (1.0 - ADAM_B2 ** ADAM_STEP)
        g_ref[...] = g
        d_ref[...] = -ADAM_LR * (m_hat / (jnp.sqrt(v_hat) + ADAM_EPS) + ADAM_WD * w_ref[...])
        nm_ref[...] = mm
        nv_ref[...] = vv

    row = pl.BlockSpec((tr, C), lambda i: (i, 0))
    return _call(body, name=name, grid=(R // tr,),
                 in_specs=[pl.BlockSpec((N_DEV, tr, C), lambda i: (0, i, 0)), row, row, row],
                 out_specs=[row] * 4, out_shape=[jax.ShapeDtypeStruct((R, C), F32)] * 4,
                 compiler_params=_params("parallel"))(parts, w, m, v)


def _pack(arrays):
    flat, meta, off = [], [], 0
    for a in arrays:
        flat.append(a.reshape(-1).astype(F32))
        meta.append((off, a.shape))
        off += a.size
    total = -(-off // (8 * LANES)) * (8 * LANES)
    flat.append(jnp.zeros((total - off,), F32))
    return jnp.concatenate(flat).reshape(total // LANES, LANES), meta


def _unpack(packed, meta):
    flat = packed.reshape(-1)
    return [flat[off:off + math.prod(shape)].reshape(shape) for off, shape in meta]


def kernel(x, ssd_in_proj, ssd_conv_w, ssd_conv_b, ssd_dt_bias, ssd_A_log, ssd_D, ssd_norm_w, ssd_out_proj, pool_w, pool_b, pool_scale, mlp_w1, mlp_w2, ln_mix_g, ln_mix_b, ln_ffn_g, ln_ffn_b, loss_target, m_ssd_in_proj, m_ssd_conv_w, m_ssd_conv_b, m_ssd_dt_bias, m_ssd_A_log, m_ssd_D, m_ssd_norm_w, m_ssd_out_proj, m_pool_w, m_pool_b, m_pool_scale, m_mlp_w1, m_mlp_w2, m_ln_mix_g, m_ln_mix_b, m_ln_ffn_g, m_ln_ffn_b, v_ssd_in_proj, v_ssd_conv_w, v_ssd_conv_b, v_ssd_dt_bias, v_ssd_A_log, v_ssd_D, v_ssd_norm_w, v_ssd_out_proj, v_pool_w, v_pool_b, v_pool_scale, v_mlp_w1, v_mlp_w2, v_ln_mix_g, v_ln_mix_b, v_ln_ffn_g, v_ln_ffn_b):
    T, D = x.shape[1], x.shape[2]
    depth = mlp_w1.shape[0]
    n_ssd, n_pool = ssd_in_proj.shape[0], pool_w.shape[0]
    d_inner = ssd_out_proj.shape[1] * N_DEV
    n_heads = d_inner // HEAD_DIM
    hg = n_heads // N_GROUPS
    d_bc = N_GROUPS * D_STATE
    d_xbc = d_inner + 2 * d_bc
    d_in_proj = ssd_in_proj.shape[2] * N_DEV
    d_ff = mlp_w1.shape[2] * N_DEV
    ng = len(POOL_WINDOWS)
    gd = D // ng
    alpha = (2.0 * depth) ** 0.25
    x0 = x.reshape(T, D)
    target = loss_target.reshape(T, D)

    small_sharded = [ssd_conv_w, pool_b, pool_scale]
    small_pack, small_meta = _pack(small_sharded)
    pw_rows = pool_w.shape[1] * pool_w.shape[2]
    g_in, g_out, g_pw, g_w1, g_w2, g_small = _exchange(
        [ssd_in_proj.astype(BF16), ssd_out_proj.astype(BF16),
         pool_w.reshape(n_pool, pw_rows, gd).astype(BF16),
         mlp_w1.astype(BF16), mlp_w2.astype(BF16), small_pack], True, "gather_weights")
    w_in = g_in.transpose(1, 2, 0, 3).reshape(n_ssd, D, d_in_proj)
    w_out = g_out.transpose(1, 0, 2, 3).reshape(n_ssd, d_inner, D)
    w_pool = g_pw.reshape(N_DEV, n_pool, ng, gd // N_DEV, gd).transpose(1, 2, 0, 3, 4).reshape(n_pool, ng, gd, gd)
    w_1 = g_w1.transpose(1, 2, 0, 3).reshape(depth, D, d_ff)
    w_2 = g_w2.transpose(1, 0, 2, 3).reshape(depth, d_ff, D)
    smalls = [_unpack(g_small[j], small_meta) for j in range(N_DEV)]
    conv_w = jnp.concatenate([s[0] for s in smalls], axis=-1).reshape(n_ssd, CONV_WIDTH, d_xbc)
    pool_bias = jnp.concatenate([s[1] for s in smalls], axis=-1).reshape(n_pool, 1, D)
    pool_sc = jnp.concatenate([s[2] for s in smalls], axis=-1).reshape(n_pool, 1, D)

    sel = (jnp.arange(d_inner)[:, None] // HEAD_DIM == jnp.arange(n_heads)[None, :]).astype(F32)

    saved = []
    h, hb = x0, x0.astype(BF16)
    for i in range(depth):
        j = i // 2
        s = {}
        s["x0"], s["x0b"] = h, hb
        if i % 2 == 0:
            (zx,) = _matmul(hb, w_in[j], "nn", name=f"in_proj_{i}", outs=[F32], tn=1152)
            xbc = _conv_fwd(zx, conv_w[j], ssd_conv_b[j], d_inner, f"conv_fwd_{i}")
            raw = zx[:, d_inner + d_xbc:]
            dt, cs = _dt_fwd(raw, ssd_dt_bias[j], ssd_A_log[j], f"dt_fwd_{i}")
            dtc, csc, csr = _cols(dt, hg), _cols(cs, hg), _rows(cs, hg)
            y2, states = _ssd_fwd(xbc, dtc, csc, csr, d_inner, f"ssd_fwd_{i}")
            dvec = jnp.repeat(ssd_D[j], HEAD_DIM).reshape(1, d_inner)
            nw = ssd_norm_w[j].reshape(1, d_inner)
            gn = _gnorm_fwd(y2, xbc, zx, dvec, nw, d_inner, f"gnorm_fwd_{i}")
            (mix,) = _matmul(gn, w_out[j], "nn", name=f"out_proj_{i}", outs=[F32])
            s.update(zx=zx, xbc=xbc, raw=raw, dtc=dtc, csc=csc, csr=csr, y2=y2, states=states, dvec=dvec, nw=nw, gn=gn)
        else:
            mix = _pool_fwd(h, w_pool[j], pool_bias[j], pool_sc[j], f"pool_fwd_{i}")
        s["mix"] = mix
        x1, x1b = _ln_fwd(h, mix, ln_mix_g[i], ln_mix_b[i], alpha, f"ln_mix_fwd_{i}")
        u, hh = _matmul(x1b, w_1[i], "nn", name=f"mlp_up_{i}", outs=[F32, BF16],
                        epilogue=lambda acc: (acc, jnp.square(jnp.maximum(acc, 0.0))))
        (m2,) = _matmul(hh, w_2[i], "nn", name=f"mlp_down_{i}", outs=[F32])
        x2, x2b = _ln_fwd(x1, m2, ln_ffn_g[i], ln_ffn_b[i], alpha, f"ln_ffn_fwd_{i}")
        s.update(x1=x1, x1b=x1b, u=u, hh=hh, m2=m2)
        saved.append(s)
        h, hb = x2, x2b

    loss_row, dh = _loss_head(h, target)
    loss = lax.psum(loss_row[0, 0], ("x", "y", "c"))

    gr = {k: [None] * depth for k in ("w1", "w2", "ln_mix_g", "ln_mix_b", "ln_ffn_g", "ln_ffn_b")}
    gs = {k: [None] * n_ssd for k in ("in", "out", "conv_w", "conv_b", "dt_bias", "A_log", "D", "norm_w")}
    gp = {k: [None] * n_pool for k in ("w", "b", "scale")}
    for i in reversed(range(depth)):
        j = i // 2
        s = saved[i]
        ds2, ds2b, gr["ln_ffn_g"][i], gr["ln_ffn_b"][i] = _ln_bwd(s["x1"], s["m2"], ln_ffn_g[i], dh, alpha, f"ln_ffn_bwd_{i}")
        (du,) = _matmul(ds2b, w_2[i], "nt", name=f"mlp_down_dx_{i}", outs=[BF16], extras=[s["u"]],
                        epilogue=lambda acc, u_: (acc * (2.0 * jnp.maximum(u_, 0.0)),))
        (gr["w2"][i],) = _matmul(s["hh"], ds2b, "tn", name=f"mlp_down_dw_{i}", outs=[BF16])
        (dx1,) = _matmul(du, w_1[i], "nt", name=f"mlp_up_dx_{i}", outs=[F32], extras=[ds2],
                         epilogue=lambda acc, e: (acc + alpha * e,))
        (gr["w1"][i],) = _matmul(s["x1b"], du, "tn", name=f"mlp_up_dw_{i}", outs=[BF16])
        ds1, ds1b, gr["ln_mix_g"][i], gr["ln_mix_b"][i] = _ln_bwd(s["x0"], s["mix"], ln_mix_g[i], dx1, alpha, f"ln_mix_bwd_{i}")
        if i % 2 == 0:
            (dgn,) = _matmul(ds1b, w_out[j], "nt", name=f"out_proj_dx_{i}", outs=[F32])
            (gs["out"][j],) = _matmul(s["gn"], ds1b, "tn", name=f"out_proj_dw_{i}", outs=[BF16])
            dy, dxs_d, dz, gs["norm_w"][j], gs["D"][j] = _gnorm_bwd(
                s["y2"], s["xbc"], s["zx"], s["dvec"], s["nw"], dgn, sel, d_inner, f"gnorm_bwd_{i}")
            dxs, dB, dC, dcs, dtot, dxdtx = _ssd_bwd(s["xbc"], s["dtc"], s["csc"], s["csr"], s["states"], s["y2"], dy,
                                                     d_inner, f"ssd_bwd_{i}")
            draw, gs["dt_bias"][j], gs["A_log"][j] = _dt_bwd(
                s["raw"], ssd_dt_bias[j], ssd_A_log[j], _uncols(dcs), _uncols(dtot), _uncols(dxdtx), f"dt_bwd_{i}")
            dpre = jnp.concatenate([
                _conv_dpre(s["zx"], conv_w[j], ssd_conv_b[j], d_inner, 0, dxs, dxs_d, f"conv_dpre_x_{i}"),
                _conv_dpre(s["zx"], conv_w[j], ssd_conv_b[j], d_inner, d_inner, dB, None, f"conv_dpre_b_{i}"),
                _conv_dpre(s["zx"], conv_w[j], ssd_conv_b[j], d_inner, d_inner + d_bc, dC, None, f"conv_dpre_c_{i}"),
            ], axis=1)
            din, gs["conv_w"][j], gs["conv_b"][j] = _conv_bwd(s["zx"], dpre, conv_w[j], d_inner, f"conv_bwd_{i}")
            dzx = jnp.concatenate([dz, din, draw], axis=1)
            (gs["in"][j],) = _matmul(s["x0b"], dzx, "tn", name=f"in_proj_dw_{i}", outs=[BF16], tn=1152)
            (dh,) = _matmul(dzx, w_in[j], "nt", name=f"in_proj_dx_{i}", outs=[F32], extras=[ds1], tk=1152,
                            epilogue=lambda acc, e: (acc + alpha * e,))
        else:
            dm, gp["w"][j], gp["b"][j], gp["scale"][j] = _pool_bwd_a(
                s["x0"], w_pool[j], pool_bias[j], pool_sc[j], ds1, f"pool_bwd_a_{i}")
            dh = _pool_bwd_b(dm, ds1, alpha, f"pool_bwd_b_{i}")
    grad_x = dh.reshape(x.shape)

    def dev_major_cols(stack):
        L, R, NC = stack.shape
        return stack.reshape(L, R, N_DEV, NC // N_DEV).transpose(2, 0, 1, 3)

    def dev_major_rows(stack):
        L, NR, C = stack.shape
        return stack.reshape(L, N_DEV, NR // N_DEV, C).transpose(1, 0, 2, 3)

    s_in = dev_major_cols(jnp.stack(gs["in"]))
    s_out = dev_major_rows(jnp.stack(gs["out"]))
    s_pw = (jnp.stack(gp["w"]).astype(BF16).reshape(n_pool, ng, N_DEV, gd // N_DEV, gd)
            .transpose(2, 0, 1, 3, 4).reshape(N_DEV, n_pool, pw_rows, gd))
    s_w1 = dev_major_cols(jnp.stack(gr["w1"]))
    s_w2 = dev_major_rows(jnp.stack(gr["w2"]))
    g_conv_w = jnp.stack(gs["conv_w"]).reshape(n_ssd, CONV_WIDTH, 1, N_DEV, d_xbc // N_DEV)
    g_pool_b = jnp.stack(gp["b"]).reshape(n_pool, ng, N_DEV, gd // N_DEV)
    g_pool_s = jnp.stack(gp["scale"]).reshape(n_pool, N_DEV, D // N_DEV)
    s_small = jnp.stack([_pack([g_conv_w[:, :, :, k], g_pool_b[:, :, k], g_pool_s[:, k]])[0] for k in range(N_DEV)])
    repl_grads = [jnp.stack(gs["conv_b"]).reshape(ssd_conv_b.shape), jnp.stack(gs["dt_bias"]).reshape(ssd_dt_bias.shape),
                  jnp.stack(gs["A_log"]).reshape(ssd_A_log.shape), jnp.stack(gs["D"]).reshape(ssd_D.shape),
                  jnp.stack(gs["norm_w"]).reshape(ssd_norm_w.shape),
                  jnp.stack(gr["ln_mix_g"]).reshape(ln_mix_g.shape), jnp.stack(gr["ln_mix_b"]).reshape(ln_mix_b.shape),
                  jnp.stack(gr["ln_ffn_g"]).reshape(ln_ffn_g.shape), jnp.stack(gr["ln_ffn_b"]).reshape(ln_ffn_b.shape)]
    repl_pack, repl_meta = _pack(repl_grads)
    s_repl = jnp.broadcast_to(repl_pack[None], (N_DEV,) + repl_pack.shape)
    r_in, r_out, r_pw, r_w1, r_w2, r_small, r_repl = _exchange(
        [s_in, s_out, s_pw, s_w1, s_w2, s_small, s_repl], False, "exchange_grads")

    def update(parts, w, m, v, name):
        shape = w.shape
        C = shape[-1]
        res = _adamw(parts.reshape(N_DEV, -1, C), w.reshape(-1, C), m.reshape(-1, C), v.reshape(-1, C), name)
        return [r.reshape(shape) for r in res]

    upd = {}
    upd["ssd_in_proj"] = update(r_in, ssd_in_proj, m_ssd_in_proj, v_ssd_in_proj, "adamw_in_proj")
    upd["ssd_out_proj"] = update(r_out, ssd_out_proj, m_ssd_out_proj, v_ssd_out_proj, "adamw_out_proj")
    upd["pool_w"] = update(r_pw, pool_w, m_pool_w, v_pool_w, "adamw_pool_w")
    upd["mlp_w1"] = update(r_w1, mlp_w1, m_mlp_w1, v_mlp_w1, "adamw_w1")
    upd["mlp_w2"] = update(r_w2, mlp_w2, m_mlp_w2, v_mlp_w2, "adamw_w2")
    sm = update(r_small, small_pack, _pack([m_ssd_conv_w, m_pool_b, m_pool_scale])[0],
                _pack([v_ssd_conv_w, v_pool_b, v_pool_scale])[0], "adamw_small_sharded")
    for idx, nm in enumerate(["ssd_conv_w", "pool_b", "pool_scale"]):
        upd[nm] = [_unpack(r, small_meta)[idx] for r in sm]
    repl_names = ["ssd_conv_b", "ssd_dt_bias", "ssd_A_log", "ssd_D", "ssd_norm_w",
                  "ln_mix_g", "ln_mix_b", "ln_ffn_g", "ln_ffn_b"]
    repl_w = [ssd_conv_b, ssd_dt_bias, ssd_A_log, ssd_D, ssd_norm_w, ln_mix_g, ln_mix_b, ln_ffn_g, ln_ffn_b]
    repl_m = [m_ssd_conv_b, m_ssd_dt_bias, m_ssd_A_log, m_ssd_D, m_ssd_norm_w, m_ln_mix_g, m_ln_mix_b, m_ln_ffn_g, m_ln_ffn_b]
    repl_v = [v_ssd_conv_b, v_ssd_dt_bias, v_ssd_A_log, v_ssd_D, v_ssd_norm_w, v_ln_mix_g, v_ln_mix_b, v_ln_ffn_g, v_ln_ffn_b]
    rp = update(r_repl, _pack(repl_w)[0], _pack(repl_m)[0], _pack(repl_v)[0], "adamw_replicated")
    for idx, nm in enumerate(repl_names):
        upd[nm] = [_unpack(r, repl_meta)[idx] for r in rp]

    order = ["ssd_in_proj", "ssd_conv_w", "ssd_conv_b", "ssd_dt_bias", "ssd_A_log", "ssd_D", "ssd_norm_w",
             "ssd_out_proj", "pool_w", "pool_b", "pool_scale", "mlp_w1", "mlp_w2",
             "ln_mix_g", "ln_mix_b", "ln_ffn_g", "ln_ffn_b"]
    return (loss, grad_x, *[upd[n][0] for n in order], *[upd[n][1] for n in order],
            *[upd[n][2] for n in order], *[upd[n][3] for n in order])
```

```python
import functools
import math

import jax
import jax.numpy as jnp
from jax import lax
from jax.experimental import pallas as pl
from jax.experimental.pallas import tpu as pltpu

F32 = jnp.float32
BF16 = jnp.bfloat16

N_DEV = 8
HEAD_DIM = 64
N_GROUPS = 8
D_STATE = 128
CHUNK = 128
CONV_WIDTH = 5
POOL_WINDOWS = (2, 4, 8, 16)
HALO = 8
LN_EPS = 1e-5
RMS_EPS = 1e-5
ADAM_LR = 0.001
ADAM_B1 = 0.9
ADAM_B2 = 0.999
ADAM_EPS = 1e-08
ADAM_WD = 0.01
ADAM_STEP = 10
LANES = 128
VMEM_LIMIT_BYTES = 56 * 1024 * 1024
HIGHEST = lax.Precision.HIGHEST


def _pallas(body, **kw):
    return pl.pallas_call(body, **kw)


def _params(*sem):
    return pltpu.CompilerParams(dimension_semantics=sem, vmem_limit_bytes=VMEM_LIMIT_BYTES)


def _mesh_pos():
    return lax.axis_index("x"), lax.axis_index("y"), lax.axis_index("c")


def _peer(x, y, c, k):
    dx, dy, dc = (k >> 2) & 1, (k >> 1) & 1, k & 1
    px = (1 - x) if dx else x
    py = (1 - y) if dy else y
    pc = (1 - c) if dc else c
    return px, py, pc


def _comm_copies(ins, outs, gather, send_sems, recv_sems, local_sems):
    x, y, c = _mesh_pos()
    me = 4 * x + 2 * y + c
    copies = []
    for a in range(len(ins)):
        copies.append(pltpu.make_async_copy(ins[a] if gather else ins[a].at[me], outs[a].at[me], local_sems.at[a]))
        for k in range(1, N_DEV):
            px, py, pc = _peer(x, y, c, k)
            copies.append(pltpu.make_async_remote_copy(
                src_ref=ins[a] if gather else ins[a].at[4 * px + 2 * py + pc],
                dst_ref=outs[a].at[me],
                send_sem=send_sems.at[a, k - 1], recv_sem=recv_sems.at[a, k - 1],
                device_id=(px, py, pc), device_id_type=pl.DeviceIdType.MESH))
    return copies


def _comm_shapes(arrays, gather):
    out_shape = [jax.ShapeDtypeStruct(((N_DEV,) + a.shape) if gather else a.shape, a.dtype) for a in arrays]
    n = len(arrays)
    sems = [pltpu.SemaphoreType.DMA((n, N_DEV - 1)), pltpu.SemaphoreType.DMA((n, N_DEV - 1)),
            pltpu.SemaphoreType.DMA((n,))]
    return out_shape, sems


def _call(body, *, comm=None, **kw):
    if comm is None:
        return _pallas(body, **kw)
    arrays, gather = comm
    n = len(arrays)
    grid = tuple(kw["grid"])
    single = not isinstance(kw["out_shape"], (list, tuple))
    out_shape = [kw["out_shape"]] if single else list(kw["out_shape"])
    out_specs = [kw["out_specs"]] if single else list(kw["out_specs"])
    in_specs = list(kw["in_specs"])
    scratch = list(kw.get("scratch_shapes", ()))
    n_in, n_out, n_scr = len(in_specs), len(out_shape), len(scratch)
    c_shape, c_sems = _comm_shapes(arrays, gather)

    def wrapped(*refs):
        ins, refs = refs[:n_in], refs[n_in:]
        c_ins, refs = refs[:n], refs[n:]
        outs, refs = refs[:n_out], refs[n_out:]
        c_outs, refs = refs[:n], refs[n:]
        scr, sems = refs[:n_scr], refs[n_scr:]
        first = last = None
        for ax, size in enumerate(grid):
            i = pl.program_id(ax)
            first = (i == 0) if first is None else first & (i == 0)
            last = (i == size - 1) if last is None else last & (i == size - 1)

        @pl.when(first)
        def _():
            for cp in _comm_copies(c_ins, c_outs, gather, *sems):
                cp.start()

        body(*ins, *outs, *scr)

        @pl.when(last)
        def _():
            for cp in _comm_copies(c_ins, c_outs, gather, *sems):
                cp.wait()

    any_spec = pl.BlockSpec(memory_space=pl.ANY)
    call = _pallas(wrapped, name=kw["name"], grid=grid, in_specs=in_specs + [any_spec] * n,
                   out_specs=out_specs + [any_spec] * n, out_shape=out_shape + c_shape,
                   scratch_shapes=scratch + c_sems,
                   compiler_params=_params(*(("arbitrary",) * len(grid))))

    def run(*args):
        res = call(*args, *arrays)
        own = res[:n_out]
        return (own[0] if single else list(own)), list(res[n_out:])

    return run


def _tile(dim, target, align=LANES):
    if dim <= target:
        return dim
    t = (target // align) * align
    while t >= align:
        if dim % t == 0:
            return t
        t -= align
    return dim


def _dot(a, b, dims, precision=None):
    return lax.dot_general(a, b, (dims, ((), ())), precision=precision, preferred_element_type=F32)


NN = ((1,), (0,))
NT = ((1,), (1,))
TN = ((0,), (0,))


def _sigmoid(x):
    return 1.0 / (1.0 + jnp.exp(-x))


def _matmul(a, b, mode, *, name, outs, epilogue=None, extras=(), tm=1024, tn=1024, tk=2048, comm=None):
    if mode == "nn":
        (M, K), (K2, N) = a.shape, b.shape
    elif mode == "nt":
        (M, K), (N, K2) = a.shape, b.shape
    else:
        (K, M), (K2, N) = a.shape, b.shape
    assert K == K2, (a.shape, b.shape, mode)
    tm, tn, tk = _tile(M, tm), _tile(N, tn), _tile(K, tk)
    nk = K // tk
    dims = {"nn": NN, "nt": NT, "tn": TN}[mode]
    a_spec = (pl.BlockSpec((tk, tm), lambda i, j, k: (k, i)) if mode == "tn"
              else pl.BlockSpec((tm, tk), lambda i, j, k: (i, k)))
    b_spec = (pl.BlockSpec((tn, tk), lambda i, j, k: (j, k)) if mode == "nt"
              else pl.BlockSpec((tk, tn), lambda i, j, k: (k, j)))
    mn_spec = pl.BlockSpec((tm, tn), lambda i, j, k: (i, j))
    n_extra, n_out = len(extras), len(outs)

    def finish(acc, extra_refs, out_refs):
        res = (acc,) if epilogue is None else epilogue(acc, *[r[...] for r in extra_refs])
        for o_ref, r in zip(out_refs, res):
            o_ref[...] = r.astype(o_ref.dtype)

    def body(*refs):
        a_ref, b_ref = refs[0], refs[1]
        extra_refs = refs[2:2 + n_extra]
        out_refs = refs[2 + n_extra:2 + n_extra + n_out]
        part = _dot(a_ref[...].astype(BF16), b_ref[...].astype(BF16), dims)
        if nk == 1:
            finish(part, extra_refs, out_refs)
            return
        acc_ref = refs[-1]
        k = pl.program_id(2)

        @pl.when(k == 0)
        def _():
            acc_ref[...] = part

        @pl.when((k > 0) & (k < nk - 1))
        def _():
            acc_ref[...] += part

        @pl.when(k == nk - 1)
        def _():
            finish(acc_ref[...] + part, extra_refs, out_refs)

    res = _call(
        body, comm=comm, name=name, grid=(M // tm, N // tn, nk),
        in_specs=[a_spec, b_spec] + [mn_spec] * n_extra,
        out_specs=[mn_spec] * n_out,
        out_shape=[jax.ShapeDtypeStruct((M, N), dt) for dt in outs],
        scratch_shapes=[pltpu.VMEM((tm, tn), F32)] if nk > 1 else [],
        compiler_params=_params("parallel", "parallel", "arbitrary"),
    )(a, b, *extras)
    return res


def _ln_fwd(x, f, g, b, alpha, name):
    T, D = x.shape
    tm = _tile(T, 256, 8)

    def body(x_ref, f_ref, g_ref, b_ref, y_ref, yb_ref):
        s = alpha * x_ref[...] + f_ref[...]
        mu = jnp.mean(s, axis=-1, keepdims=True)
        d = s - mu
        var = jnp.mean(d * d, axis=-1, keepdims=True)
        y = d * lax.rsqrt(var + LN_EPS) * g_ref[...] + b_ref[...]
        y_ref[...] = y
        yb_ref[...] = y.astype(BF16)

    row = pl.BlockSpec((tm, D), lambda i: (i, 0))
    vec = pl.BlockSpec((1, D), lambda i: (0, 0))
    return _call(body, name=name, grid=(T // tm,), in_specs=[row, row, vec, vec], out_specs=[row, row],
                 out_shape=[jax.ShapeDtypeStruct((T, D), F32), jax.ShapeDtypeStruct((T, D), BF16)],
                 compiler_params=_params("parallel"))(x, f, g.reshape(1, D), b.reshape(1, D))


def _ln_bwd(x, f, g, dy, alpha, name):
    T, D = x.shape
    tm = _tile(T, 256, 8)

    def body(x_ref, f_ref, g_ref, dy_ref, ds_ref, dsb_ref, dg_ref, db_ref):
        i = pl.program_id(0)

        @pl.when(i == 0)
        def _():
            dg_ref[...] = jnp.zeros_like(dg_ref)
            db_ref[...] = jnp.zeros_like(db_ref)

        s = alpha * x_ref[...] + f_ref[...]
        mu = jnp.mean(s, axis=-1, keepdims=True)
        d = s - mu
        var = jnp.mean(d * d, axis=-1, keepdims=True)
        rstd = lax.rsqrt(var + LN_EPS)
        xhat = d * rstd
        dy_ = dy_ref[...]
        dg_ref[...] += jnp.sum(dy_ * xhat, axis=0, keepdims=True)
        db_ref[...] += jnp.sum(dy_, axis=0, keepdims=True)
        dxh = dy_ * g_ref[...]
        m1 = jnp.mean(dxh, axis=-1, keepdims=True)
        m2 = jnp.mean(dxh * xhat, axis=-1, keepdims=True)
        ds = rstd * (dxh - m1 - xhat * m2)
        ds_ref[...] = ds
        dsb_ref[...] = ds.astype(BF16)

    row = pl.BlockSpec((tm, D), lambda i: (i, 0))
    vec = pl.BlockSpec((1, D), lambda i: (0, 0))
    return _call(body, name=name, grid=(T // tm,), in_specs=[row, row, vec, row], out_specs=[row, row, vec, vec],
                 out_shape=[jax.ShapeDtypeStruct((T, D), F32), jax.ShapeDtypeStruct((T, D), BF16),
                            jax.ShapeDtypeStruct((1, D), F32), jax.ShapeDtypeStruct((1, D), F32)],
                 compiler_params=_params("arbitrary"))(x, f, g.reshape(1, D), dy)


def _loss_head(y, target):
    T, D = y.shape
    tm = _tile(T, 256, 8)

    def body(y_ref, t_ref, loss_ref, dy_ref):
        i = pl.program_id(0)

        @pl.when(i == 0)
        def _():
            loss_ref[...] = jnp.zeros_like(loss_ref)

        err = y_ref[...] - t_ref[...]
        dy_ref[...] = err * (1.0 / D)
        per_tok = jnp.mean(err * err, axis=-1, keepdims=True)
        loss_ref[...] += 0.5 * jnp.sum(per_tok)

    row = pl.BlockSpec((tm, D), lambda i: (i, 0))
    return _call(body, name="loss_head", grid=(T // tm,), in_specs=[row, row],
                 out_specs=[pl.BlockSpec((1, LANES), lambda i: (0, 0)), row],
                 out_shape=[jax.ShapeDtypeStruct((1, LANES), F32), jax.ShapeDtypeStruct((T, D), F32)],
                 compiler_params=_params("arbitrary"))(y, target)


def _halo_specs(tt, cw, col_of, n_tiles, grid_rank_tokens_axis):
    per = tt // HALO
    ax = grid_rank_tokens_axis

    def cur(*g):
        return (g[ax], col_of(*g))

    def prev(*g):
        return (jnp.maximum(g[ax] * per - 1, 0), col_of(*g))

    def nxt(*g):
        return (jnp.minimum((g[ax] + 1) * per, n_tiles * per - 1), col_of(*g))

    return [pl.BlockSpec((tt, cw), cur), pl.BlockSpec((HALO, cw), prev), pl.BlockSpec((HALO, cw), nxt)]


def _fill_ext(ext_ref, cur_ref, prev_ref, next_ref, i, n_tiles, tt):
    ext_ref[pl.ds(0, HALO), :] = jnp.where(i > 0, prev_ref[...], 0.0)
    ext_ref[pl.ds(HALO, tt), :] = cur_ref[...]
    ext_ref[pl.ds(HALO + tt, HALO), :] = jnp.where(i < n_tiles - 1, next_ref[...], 0.0)


def _conv_pre(ext_ref, w, bias, tt, lo=0, n=None):
    n = tt if n is None else n
    pad = CONV_WIDTH // 2
    acc = None
    for k in range(CONV_WIDTH):
        term = ext_ref[pl.ds(HALO + lo + k - pad, n), :] * w[k:k + 1, :]
        acc = term if acc is None else acc + term
    return acc + bias


def _conv_fwd(zx, conv_w, conv_b, d_inner, name, comm=None):
    T = zx.shape[0]
    d_xbc = conv_w.shape[1]
    cw = _tile(d_xbc, 512)
    assert d_inner % cw == 0
    off = d_inner // cw
    tt = _tile(T, 512, 8)
    nt = T // tt

    def body(cur_ref, prev_ref, next_ref, w_ref, b_ref, o_ref, ext_ref):
        i = pl.program_id(1)
        _fill_ext(ext_ref, cur_ref, prev_ref, next_ref, i, nt, tt)
        pre = _conv_pre(ext_ref, w_ref[...], b_ref[...], tt)
        o_ref[...] = pre * _sigmoid(pre)

    specs = _halo_specs(tt, cw, lambda j, i: off + j, nt, 1)
    return _call(body, comm=comm, name=name, grid=(d_xbc // cw, nt),
                 in_specs=specs + [pl.BlockSpec((CONV_WIDTH, cw), lambda j, i: (0, j)),
                                   pl.BlockSpec((1, cw), lambda j, i: (0, j))],
                 out_specs=pl.BlockSpec((tt, cw), lambda j, i: (i, j)),
                 out_shape=jax.ShapeDtypeStruct((T, d_xbc), F32),
                 scratch_shapes=[pltpu.VMEM((tt + 2 * HALO, cw), F32)],
                 compiler_params=_params("parallel", "parallel"))(zx, zx, zx, conv_w, conv_b.reshape(1, d_xbc))


def _conv_dpre(zx, conv_w, conv_b, d_inner, col_lo, dirs, extra, name):
    T = zx.shape[0]
    ncols = dirs.shape[2]
    cw = _tile(ncols, 512)
    assert d_inner % cw == 0 and col_lo % cw == 0
    off_zx = (d_inner + col_lo) // cw
    off_w = col_lo // cw
    tt = _tile(T, 512, 8)
    nt = T // tt
    has_extra = extra is not None

    def body(cur_ref, prev_ref, next_ref, w_ref, b_ref, dirs_ref, *rest):
        o_ref, ext_ref = rest[-2], rest[-1]
        i = pl.program_id(1)
        _fill_ext(ext_ref, cur_ref, prev_ref, next_ref, i, nt, tt)
        pre = _conv_pre(ext_ref, w_ref[...], b_ref[...], tt)
        sig = _sigmoid(pre)
        dact = dirs_ref[0] + dirs_ref[1]
        if has_extra:
            dact = dact + rest[0][...]
        o_ref[...] = dact * (sig * (1.0 + pre * (1.0 - sig)))

    specs = _halo_specs(tt, cw, lambda j, i: off_zx + j, nt, 1)
    in_specs = specs + [pl.BlockSpec((CONV_WIDTH, cw), lambda j, i: (0, off_w + j)),
                        pl.BlockSpec((1, cw), lambda j, i: (0, off_w + j)),
                        pl.BlockSpec((2, tt, cw), lambda j, i: (0, i, j))]
    args = [zx, zx, zx, conv_w, conv_b.reshape(1, -1), dirs]
    if has_extra:
        in_specs.append(pl.BlockSpec((tt, cw), lambda j, i: (i, j)))
        args.append(extra)
    return _call(body, name=name, grid=(ncols // cw, nt), in_specs=in_specs,
                 out_specs=pl.BlockSpec((tt, cw), lambda j, i: (i, j)),
                 out_shape=jax.ShapeDtypeStruct((T, ncols), F32),
                 scratch_shapes=[pltpu.VMEM((tt + 2 * HALO, cw), F32)],
                 compiler_params=_params("parallel", "parallel"))(*args)


def _conv_bwd(zx, dpre, conv_w, d_inner, name):
    T = zx.shape[0]
    d_xbc = conv_w.shape[1]
    cw = _tile(d_xbc, 512)
    off = d_inner // cw
    tt = _tile(T, 512, 8)
    nt = T // tt
    pad = CONV_WIDTH // 2

    def body(zc, zp, zn, dc, dp, dn, w_ref, din_ref, dw_ref, db_ref, zext, dext):
        i = pl.program_id(1)

        @pl.when(i == 0)
        def _():
            dw_ref[...] = jnp.zeros_like(dw_ref)
            db_ref[...] = jnp.zeros_like(db_ref)

        _fill_ext(zext, zc, zp, zn, i, nt, tt)
        _fill_ext(dext, dc, dp, dn, i, nt, tt)
        w = w_ref[...]
        d = dc[...]
        acc = None
        for k in range(CONV_WIDTH):
            term = dext[pl.ds(HALO + pad - k, tt), :] * w[k:k + 1, :]
            acc = term if acc is None else acc + term
            dw_ref[k:k + 1, :] += jnp.sum(d * zext[pl.ds(HALO + k - pad, tt), :], axis=0, keepdims=True)
        din_ref[...] = acc.astype(din_ref.dtype)
        db_ref[...] += jnp.sum(d, axis=0, keepdims=True)

    zspecs = _halo_specs(tt, cw, lambda j, i: off + j, nt, 1)
    dspecs = _halo_specs(tt, cw, lambda j, i: j, nt, 1)
    return _call(body, name=name, grid=(d_xbc // cw, nt),
                 in_specs=zspecs + dspecs + [pl.BlockSpec((CONV_WIDTH, cw), lambda j, i: (0, j))],
                 out_specs=[pl.BlockSpec((tt, cw), lambda j, i: (i, j)),
                            pl.BlockSpec((CONV_WIDTH, cw), lambda j, i: (0, j)),
                            pl.BlockSpec((1, cw), lambda j, i: (0, j))],
                 out_shape=[jax.ShapeDtypeStruct((T, d_xbc), BF16), jax.ShapeDtypeStruct((CONV_WIDTH, d_xbc), F32),
                            jax.ShapeDtypeStruct((1, d_xbc), F32)],
                 scratch_shapes=[pltpu.VMEM((tt + 2 * HALO, cw), F32), pltpu.VMEM((tt + 2 * HALO, cw), F32)],
                 compiler_params=_params("parallel", "arbitrary"))(zx, zx, zx, dpre, dpre, dpre, conv_w)


def _tri(n):
    r = lax.broadcasted_iota(jnp.int32, (n, n), 0)
    c = lax.broadcasted_iota(jnp.int32, (n, n), 1)
    return (r >= c).astype(F32), (r <= c).astype(F32)


def _dt_fwd(raw, bias, a_log, name):
    T, H2 = raw.shape
    half = H2 // 2

    def body(raw_ref, bias_ref, alog_ref, dt_ref, cs_ref):
        x = raw_ref[...] + bias_ref[...]
        dt = jnp.maximum(x, 0.0) + jnp.log(1.0 + jnp.exp(-jnp.abs(x)))
        a = dt * (-jnp.exp(alog_ref[...]))
        lower, upper = _tri(CHUNK)
        cs_f = _dot(lower, a, NN, HIGHEST)
        cs_b = _dot(upper, a, NN, HIGHEST)
        lane = lax.broadcasted_iota(jnp.int32, (CHUNK, H2), 1)
        dt_ref[...] = dt
        cs_ref[...] = jnp.where(lane < half, cs_f, cs_b)

    row = pl.BlockSpec((CHUNK, H2), lambda c: (c, 0))
    vec = pl.BlockSpec((1, H2), lambda c: (0, 0))
    return _call(body, name=name, grid=(T // CHUNK,), in_specs=[row, vec, vec], out_specs=[row, row],
                 out_shape=[jax.ShapeDtypeStruct((T, H2), F32)] * 2,
                 compiler_params=_params("parallel"))(raw, bias.reshape(1, H2), a_log.reshape(1, H2))


def _dt_bwd(raw, bias, a_log, dcs, dtot, dxdtx, name):
    T, H2 = raw.shape
    half = H2 // 2

    def body(raw_ref, bias_ref, alog_ref, dcs_ref, dtot_ref, dx_ref, draw_ref, dbias_ref, dalog_ref):
        c = pl.program_id(0)

        @pl.when(c == 0)
        def _():
            dbias_ref[...] = jnp.zeros_like(dbias_ref)
            dalog_ref[...] = jnp.zeros_like(dalog_ref)

        x = raw_ref[...] + bias_ref[...]
        dt = jnp.maximum(x, 0.0) + jnp.log(1.0 + jnp.exp(-jnp.abs(x)))
        A = -jnp.exp(alog_ref[...])
        lower, upper = _tri(CHUNK)
        g = dcs_ref[...]
        lane = lax.broadcasted_iota(jnp.int32, (CHUNK, H2), 1)
        da = jnp.where(lane < half, _dot(upper, g, NN, HIGHEST), _dot(lower, g, NN, HIGHEST)) + dtot_ref[...]
        ddt = da * A + dx_ref[...]
        draw = ddt * _sigmoid(x)
        draw_ref[...] = draw.astype(draw_ref.dtype)
        dbias_ref[...] += jnp.sum(draw, axis=0, keepdims=True)
        dalog_ref[...] += jnp.sum(da * dt, axis=0, keepdims=True) * A

    row = pl.BlockSpec((CHUNK, H2), lambda c: (c, 0))
    vec = pl.BlockSpec((1, H2), lambda c: (0, 0))
    return _call(body, name=name, grid=(T // CHUNK,), in_specs=[row, vec, vec, row, row, row],
                 out_specs=[row, vec, vec],
                 out_shape=[jax.ShapeDtypeStruct((T, H2), BF16), jax.ShapeDtypeStruct((1, H2), F32),
                            jax.ShapeDtypeStruct((1, H2), F32)],
                 compiler_params=_params("arbitrary"))(raw, bias.reshape(1, H2), a_log.reshape(1, H2), dcs, dtot, dxdtx)


def _cols(a, hg):
    T = a.shape[0]
    return a.reshape(T, 2, N_GROUPS, hg).transpose(1, 2, 0, 3)


def _rows(a, hg):
    T = a.shape[0]
    return a.reshape(T, 2, N_GROUPS, hg).transpose(1, 2, 3, 0)


def _uncols(a):
    T = a.shape[2]
    return a.transpose(2, 0, 1, 3).reshape(T, -1)


def _ssd_masks(d):
    r = lax.broadcasted_iota(jnp.int32, (CHUNK, CHUNK), 0)
    c = lax.broadcasted_iota(jnp.int32, (CHUNK, CHUNK), 1)
    return ((r >= c) & (d == 0)) | ((r <= c) & (d == 1))


def _head_expand(hg):
    r = lax.broadcasted_iota(jnp.int32, (hg, hg * HEAD_DIM), 0)
    c = lax.broadcasted_iota(jnp.int32, (hg, hg * HEAD_DIM), 1)
    return (c // HEAD_DIM == r).astype(F32)


def _head_select(hg):
    r = lax.broadcasted_iota(jnp.int32, (hg * HEAD_DIM, hg), 0)
    c = lax.broadcasted_iota(jnp.int32, (hg * HEAD_DIM, hg), 1)
    return (r // HEAD_DIM == c).astype(F32)


def _ssd_common(d, csc_ref, dtc_ref, hg):
    expand = _head_expand(hg)
    csx = _dot(csc_ref[...], expand, NN, HIGHEST)
    dtx = _dot(dtc_ref[...], expand, NN, HIGHEST)
    totx = jnp.where(d == 0, csx[CHUNK - 1:CHUNK, :], csx[0:1, :])
    return csx, dtx, totx


def _ssd_fwd(xbc, dtc, csc, csr, d_inner, name, comm=None):
    T = xbc.shape[0]
    nc = T // CHUNK
    gw = d_inner // N_GROUPS
    hg = gw // HEAD_DIM
    P, N = HEAD_DIM, D_STATE
    b_off = d_inner // N
    c_off = b_off + N_GROUPS

    def cidx(d, c):
        return c + d * (nc - 1 - 2 * c)

    def body(xs_ref, b_ref, c_ref, dtc_ref, csc_ref, csr_ref, y_ref, st_ref, h_ref):
        d = pl.program_id(0)
        c = pl.program_id(2)

        @pl.when(c == 0)
        def _():
            h_ref[...] = jnp.zeros_like(h_ref)

        Bb = b_ref[...].astype(BF16)
        Cb = c_ref[...].astype(BF16)
        S = _dot(Cb, Bb, NT)
        mask = _ssd_masks(d)
        csx, dtx, totx = _ssd_common(d, csc_ref, dtc_ref, hg)
        H = h_ref[...]
        st_ref[...] = H
        xdt = xs_ref[...] * dtx
        xdtb = xdt.astype(BF16)
        y_off = jnp.exp(csx) * _dot(Cb, H.astype(BF16), NN)
        for j in range(hg):
            sl = slice(j * P, (j + 1) * P)
            decay = jnp.exp(jnp.where(mask, csc_ref[:, j:j + 1] - csr_ref[j:j + 1, :], -jnp.inf))
            y_ref[:, sl] = _dot((S * decay).astype(BF16), xdtb[:, sl], NN) + y_off[:, sl]
        h_ref[...] = jnp.exp(totx) * H + _dot(Bb, (jnp.exp(totx - csx) * xdt).astype(BF16), TN)

    col = lambda d, g, c: (d, g, cidx(d, c), 0)
    return _call(
        body, comm=comm, name=name, grid=(2, N_GROUPS, nc),
        in_specs=[pl.BlockSpec((CHUNK, gw), lambda d, g, c: (cidx(d, c), g)),
                  pl.BlockSpec((CHUNK, N), lambda d, g, c: (cidx(d, c), b_off + g)),
                  pl.BlockSpec((CHUNK, N), lambda d, g, c: (cidx(d, c), c_off + g)),
                  pl.BlockSpec((None, None, CHUNK, hg), col),
                  pl.BlockSpec((None, None, CHUNK, hg), col),
                  pl.BlockSpec((None, None, hg, CHUNK), lambda d, g, c: (d, g, 0, cidx(d, c)))],
        out_specs=[pl.BlockSpec((None, CHUNK, gw), lambda d, g, c: (d, cidx(d, c), g)),
                   pl.BlockSpec((None, None, None, N, gw), lambda d, g, c: (d, cidx(d, c), g, 0, 0))],
        out_shape=[jax.ShapeDtypeStruct((2, T, d_inner), F32),
                   jax.ShapeDtypeStruct((2, nc, N_GROUPS, N, gw), F32)],
        scratch_shapes=[pltpu.VMEM((N, gw), F32)],
        compiler_params=_params("parallel", "parallel", "arbitrary"),
    )(xbc, xbc, xbc, dtc, csc, csr)


def _ssd_bwd(xbc, dtc, csc, csr, states, y2, dy, d_inner, name):
    T = xbc.shape[0]
    nc = T // CHUNK
    gw = d_inner // N_GROUPS
    hg = gw // HEAD_DIM
    P, N = HEAD_DIM, D_STATE
    b_off = d_inner // N
    c_off = b_off + N_GROUPS

    def cidx(d, c):
        return (nc - 1 - c) + d * (2 * c - nc + 1)

    def body(xs_ref, b_ref, c_ref, dtc_ref, csc_ref, csr_ref, st_ref, y_ref, dy_ref,
             dxs_ref, db_ref, dc_ref, dcs_ref, dtot_ref, dxdtx_ref, dh_ref, dxdt_ref):
        d = pl.program_id(0)
        c = pl.program_id(2)

        @pl.when(c == 0)
        def _():
            dh_ref[...] = jnp.zeros_like(dh_ref)

        Bb = b_ref[...].astype(BF16)
        Cb = c_ref[...].astype(BF16)
        S = _dot(Cb, Bb, NT)
        mask = _ssd_masks(d)
        csx, dtx, totx = _ssd_common(d, csc_ref, dtc_ref, hg)
        select = _head_select(hg)
        X = xs_ref[...]
        xdt = X * dtx
        xdtb = xdt.astype(BF16)
        dY = dy_ref[...]
        dYb = dY.astype(BF16)
        Hp = st_ref[...]
        Hpb = Hp.astype(BF16)
        dH = dh_ref[...]
        dHb = dH.astype(BF16)
        e_tot = jnp.exp(totx)
        dCH = (jnp.exp(csx) * dY).astype(BF16)
        dC = _dot(dCH, Hpb, NT)
        dHp = _dot(Cb, dCH, TN)
        Q = _dot(Bb, dHb, NN)
        dte = jnp.exp(totx - csx)
        wx = dte * xdt
        dB = _dot(wx.astype(BF16), dHb, NT)
        ddte = Q * wx
        dS = jnp.zeros((CHUNK, CHUNK), F32)
        for j in range(hg):
            sl = slice(j * P, (j + 1) * P)
            decay = jnp.exp(jnp.where(mask, csc_ref[:, j:j + 1] - csr_ref[j:j + 1, :], -jnp.inf))
            dS = dS + _dot(dYb[:, sl], xdtb[:, sl], NT) * decay
            dxdt_ref[:, sl] = _dot((S * decay).astype(BF16), dYb[:, sl], TN)
        dxdt_diag = dxdt_ref[...]
        dxdt = dxdt_diag + dte * Q
        dcs_ref[...] = _dot(dYb.astype(F32) * y_ref[...] - xdtb.astype(F32) * dxdt_diag - ddte, select, NN, HIGHEST)
        dtot_row = (jnp.sum(ddte, axis=0, keepdims=True) + e_tot * jnp.sum(dH * Hp, axis=0, keepdims=True))
        dtot_ref[...] = jnp.zeros((CHUNK, hg), F32) + _dot(dtot_row, select, NN, HIGHEST)
        dxdtx_ref[...] = _dot(dxdt * X, select, NN, HIGHEST)
        dxs_ref[...] = dxdt * dtx
        dh_ref[...] = e_tot * dH + dHp
        dSb = dS.astype(BF16)
        dc_ref[...] = dC + _dot(dSb, Bb, NN)
        db_ref[...] = dB + _dot(dSb, Cb, TN)

    col = lambda d, g, c: (d, g, cidx(d, c), 0)
    colspec = pl.BlockSpec((None, None, CHUNK, hg), col)
    rowblk = pl.BlockSpec((None, CHUNK, gw), lambda d, g, c: (d, cidx(d, c), g))
    return _call(
        body, name=name, grid=(2, N_GROUPS, nc),
        in_specs=[pl.BlockSpec((CHUNK, gw), lambda d, g, c: (cidx(d, c), g)),
                  pl.BlockSpec((CHUNK, N), lambda d, g, c: (cidx(d, c), b_off + g)),
                  pl.BlockSpec((CHUNK, N), lambda d, g, c: (cidx(d, c), c_off + g)),
                  colspec, colspec,
                  pl.BlockSpec((None, None, hg, CHUNK), lambda d, g, c: (d, g, 0, cidx(d, c))),
                  pl.BlockSpec((None, None, None, N, gw), lambda d, g, c: (d, cidx(d, c), g, 0, 0)),
                  rowblk,
                  pl.BlockSpec((CHUNK, gw), lambda d, g, c: (cidx(d, c), g))],
        out_specs=[rowblk,
                   pl.BlockSpec((None, CHUNK, N), lambda d, g, c: (d, cidx(d, c), g)),
                   pl.BlockSpec((None, CHUNK, N), lambda d, g, c: (d, cidx(d, c), g)),
                   colspec, colspec, colspec],
        out_shape=[jax.ShapeDtypeStruct((2, T, d_inner), F32),
                   jax.ShapeDtypeStruct((2, T, N_GROUPS * N), F32),
                   jax.ShapeDtypeStruct((2, T, N_GROUPS * N), F32)]
        + [jax.ShapeDtypeStruct((2, N_GROUPS, T, hg), F32)] * 3,
        scratch_shapes=[pltpu.VMEM((N, gw), F32), pltpu.VMEM((CHUNK, gw), F32)],
        compiler_params=_params("parallel", "parallel", "arbitrary"),
    )(xbc, xbc, xbc, dtc, csc, csr, states, y2, dy)


def _gnorm_fwd(y2, xbc, zx, dvec, nw, d_inner, name):
    T = xbc.shape[0]
    gw = d_inner // N_GROUPS
    tm = _tile(T, 128, 8)

    def body(y_ref, xs_ref, z_ref, d_ref, w_ref, o_ref):
        for g in range(N_GROUPS):
            sl = slice(g * gw, (g + 1) * gw)
            y = y_ref[0, :, sl] + y_ref[1, :, sl] + xs_ref[:, sl] * d_ref[:, sl]
            z = z_ref[:, sl]
            gy = y * (z * _sigmoid(z))
            rs = lax.rsqrt(jnp.mean(gy * gy, axis=-1, keepdims=True) + RMS_EPS)
            o_ref[:, sl] = (gy * rs * w_ref[:, sl]).astype(o_ref.dtype)

    row = pl.BlockSpec((tm, d_inner), lambda i: (i, 0))
    vec = pl.BlockSpec((1, d_inner), lambda i: (0, 0))
    return _call(body, name=name, grid=(T // tm,),
                 in_specs=[pl.BlockSpec((2, tm, d_inner), lambda i: (0, i, 0)), row, row, vec, vec],
                 out_specs=row, out_shape=jax.ShapeDtypeStruct((T, d_inner), BF16),
                 compiler_params=_params("parallel"))(y2, xbc, zx, dvec, nw)


def _gnorm_bwd(y2, xbc, zx, dvec, nw, dgn, sel, d_inner, name, comm=None):
    T = xbc.shape[0]
    gw = d_inner // N_GROUPS
    n_heads = d_inner // HEAD_DIM
    tm = _tile(T, 128, 8)
    n_tiles = T // tm

    def body(y_ref, xs_ref, z_ref, d_ref, w_ref, dg_ref, sel_ref, dy_ref, dxs_ref, dz_ref, dw_ref, dd_ref, dch_ref):
        i = pl.program_id(0)

        @pl.when(i == 0)
        def _():
            dw_ref[...] = jnp.zeros_like(dw_ref)
            dch_ref[...] = jnp.zeros_like(dch_ref)

        for g in range(N_GROUPS):
            sl = slice(g * gw, (g + 1) * gw)
            xs = xs_ref[:, sl]
            y = y_ref[0, :, sl] + y_ref[1, :, sl] + xs * d_ref[:, sl]
            z = z_ref[:, sl]
            sig = _sigmoid(z)
            sz = z * sig
            gy = y * sz
            rs = lax.rsqrt(jnp.mean(gy * gy, axis=-1, keepdims=True) + RMS_EPS)
            n = gy * rs
            dout = dg_ref[:, sl]
            dw_ref[:, sl] += jnp.sum(dout * n, axis=0, keepdims=True)
            dn = dout * w_ref[:, sl]
            dgy = rs * (dn - n * jnp.mean(dn * n, axis=-1, keepdims=True))
            dy = dgy * sz
            dy_ref[:, sl] = dy
            dz_ref[:, sl] = (dgy * y * (sig * (1.0 + z * (1.0 - sig)))).astype(dz_ref.dtype)
            dxs_ref[:, sl] = dy * d_ref[:, sl]
            dch_ref[:, sl] += jnp.sum(dy * xs, axis=0, keepdims=True)

        @pl.when(i == n_tiles - 1)
        def _():
            dd_ref[...] = _dot(dch_ref[...], sel_ref[...], NN, HIGHEST)

    row = pl.BlockSpec((tm, d_inner), lambda i: (i, 0))
    vec = pl.BlockSpec((1, d_inner), lambda i: (0, 0))
    hvec = pl.BlockSpec((1, n_heads), lambda i: (0, 0))
    return _call(body, comm=comm, name=name, grid=(n_tiles,),
                 in_specs=[pl.BlockSpec((2, tm, d_inner), lambda i: (0, i, 0)), row, row, vec, vec, row,
                           pl.BlockSpec((d_inner, n_heads), lambda i: (0, 0))],
                 out_specs=[row, row, row, vec, hvec],
                 out_shape=[jax.ShapeDtypeStruct((T, d_inner), F32), jax.ShapeDtypeStruct((T, d_inner), F32),
                            jax.ShapeDtypeStruct((T, d_inner), BF16),
                            jax.ShapeDtypeStruct((1, d_inner), F32), jax.ShapeDtypeStruct((1, n_heads), F32)],
                 scratch_shapes=[pltpu.VMEM((1, d_inner), F32)],
                 compiler_params=_params("arbitrary"))(y2, xbc, zx, dvec, nw, dgn, sel)


def _pool_counts(i, tt, T, win, rows, row0):
    t = i * tt + row0 + lax.broadcasted_iota(jnp.int32, (rows, 1), 0)
    start = t - win // 2
    lo = jnp.clip(start, 0, T)
    hi = jnp.clip(start + win, 0, T)
    return jnp.maximum(hi - lo, 1).astype(F32)


def _pool_features(ext_ref, i, tt, T, gi, gd):
    win = POOL_WINDOWS[gi]
    sl = slice(gi * gd, (gi + 1) * gd)
    acc = None
    for o in range(-(win // 2), win - win // 2):
        term = ext_ref[pl.ds(HALO + o, tt), sl]
        acc = term if acc is None else acc + term
    return acc / _pool_counts(i, tt, T, win, tt, 0) - ext_ref[pl.ds(HALO, tt), sl]


def _pool_fwd(u, w, bias, scale, name):
    T, D = u.shape
    ng = len(POOL_WINDOWS)
    gd = D // ng
    tt = _tile(T, 512, 8)
    nt = T // tt

    def body(cur, prev, nxt, w_ref, b_ref, s_ref, o_ref, ext):
        i = pl.program_id(0)
        _fill_ext(ext, cur, prev, nxt, i, nt, tt)
        for gi in range(ng):
            sl = slice(gi * gd, (gi + 1) * gd)
            m = _pool_features(ext, i, tt, T, gi, gd)
            pre = _dot(m.astype(BF16), w_ref[gi], NN) + b_ref[:, sl]
            o_ref[:, sl] = pre * s_ref[:, sl]

    vec = pl.BlockSpec((1, D), lambda i: (0, 0))
    return _call(body, name=name, grid=(nt,),
                 in_specs=_halo_specs(tt, D, lambda i: 0, nt, 0)
                 + [pl.BlockSpec((ng, gd, gd), lambda i: (0, 0, 0)), vec, vec],
                 out_specs=pl.BlockSpec((tt, D), lambda i: (i, 0)),
                 out_shape=jax.ShapeDtypeStruct((T, D), F32),
                 scratch_shapes=[pltpu.VMEM((tt + 2 * HALO, D), F32)],
                 compiler_params=_params("parallel"))(u, u, u, w, bias, scale)


def _pool_bwd_a(u, w, bias, scale, dy, name):
    T, D = u.shape
    ng = len(POOL_WINDOWS)
    gd = D // ng
    tt = _tile(T, 512, 8)
    nt = T // tt

    def body(cur, prev, nxt, w_ref, b_ref, s_ref, dy_ref, dm_ref, dw_ref, db_ref, ds_ref, ext):
        i = pl.program_id(0)

        @pl.when(i == 0)
        def _():
            dw_ref[...] = jnp.zeros_like(dw_ref)
            db_ref[...] = jnp.zeros_like(db_ref)
            ds_ref[...] = jnp.zeros_like(ds_ref)

        _fill_ext(ext, cur, prev, nxt, i, nt, tt)
        for gi in range(ng):
            sl = slice(gi * gd, (gi + 1) * gd)
            mb = _pool_features(ext, i, tt, T, gi, gd).astype(BF16)
            wg = w_ref[gi]
            pre = _dot(mb, wg, NN) + b_ref[:, sl]
            dy_ = dy_ref[:, sl]
            ds_ref[:, sl] += jnp.sum(dy_ * pre, axis=0, keepdims=True)
            dpre = dy_ * s_ref[:, sl]
            db_ref[:, sl] += jnp.sum(dpre, axis=0, keepdims=True)
            dpb = dpre.astype(BF16)
            dw_ref[gi] += _dot(mb, dpb, TN)
            dm_ref[:, sl] = _dot(dpb, wg, NT)

    vec = pl.BlockSpec((1, D), lambda i: (0, 0))
    row = pl.BlockSpec((tt, D), lambda i: (i, 0))
    wspec = pl.BlockSpec((ng, gd, gd), lambda i: (0, 0, 0))
    return _call(body, name=name, grid=(nt,),
                 in_specs=_halo_specs(tt, D, lambda i: 0, nt, 0) + [wspec, vec, vec, row],
                 out_specs=[row, wspec, vec, vec],
                 out_shape=[jax.ShapeDtypeStruct((T, D), F32), jax.ShapeDtypeStruct((ng, gd, gd), F32),
                            jax.ShapeDtypeStruct((1, D), F32), jax.ShapeDtypeStruct((1, D), F32)],
                 scratch_shapes=[pltpu.VMEM((tt + 2 * HALO, D), F32)],
                 compiler_params=_params("arbitrary"))(u, u, u, w, bias, scale, dy)


def _pool_bwd_b(dm, dy, alpha, name):
    T, D = dm.shape
    ng = len(POOL_WINDOWS)
    gd = D // ng
    tt = _tile(T, 512, 8)
    nt = T // tt

    def body(cur, prev, nxt, dy_ref, o_ref, ext):
        i = pl.program_id(0)
        _fill_ext(ext, cur, prev, nxt, i, nt, tt)
        for gi, win in enumerate(POOL_WINDOWS):
            sl = slice(gi * gd, (gi + 1) * gd)
            rows = tt + 2 * HALO
            ext[:, sl] = ext[:, sl] / _pool_counts(i, tt, T, win, rows, -HALO)
            acc = None
            for o in range(-(win // 2) + 1, win // 2 + 1):
                term = ext[pl.ds(HALO + o, tt), sl]
                acc = term if acc is None else acc + term
            o_ref[:, sl] = alpha * dy_ref[:, sl] + acc - cur[:, sl]

    row = pl.BlockSpec((tt, D), lambda i: (i, 0))
    return _call(body, name=name, grid=(nt,),
                 in_specs=_halo_specs(tt, D, lambda i: 0, nt, 0) + [row], out_specs=row,
                 out_shape=jax.ShapeDtypeStruct((T, D), F32),
                 scratch_shapes=[pltpu.VMEM((tt + 2 * HALO, D), F32)],
                 compiler_params=_params("parallel"))(dm, dm, dm, dy)


def _exchange(arrays, gather, name):
    n = len(arrays)

    def body(*refs):
        copies = _comm_copies(refs[:n], refs[n:2 * n], gather, *refs[2 * n:])
        for cp in copies:
            cp.start()
        for cp in copies:
            cp.wait()

    any_spec = pl.BlockSpec(memory_space=pl.ANY)
    out_shape, sems = _comm_shapes(arrays, gather)
    return _call(body, name=name, in_specs=[any_spec] * n, out_specs=[any_spec] * n, out_shape=out_shape,
                 scratch_shapes=sems)(*arrays)


def _adamw(parts, w, m, v, name):
    R, C = w.shape
    tr = _tile(R, max(8, (1 << 18) // C // 8 * 8), 8)

    def body(p_ref, w_ref, m_ref, v_ref, g_ref, d_ref, nm_ref, nv_ref):
        g = p_ref[0].astype(F32)
        for i in range(1, N_DEV):
            g = g + p_ref[i].astype(F32)
        mm = ADAM_B1 * m_ref[...] + (1.0 - ADAM_B1) * g
        vv = ADAM_B2 * v_ref[...] + (1.0 - ADAM_B2) * (g * g)
        m_hat = mm / (1.0 - ADAM_B1 ** ADAM_STEP)
        v_hat = vv / (1.0 - ADAM_B2 ** ADAM_STEP)
        g_ref[...] = g
        d_ref[...] = -ADAM_LR * (m_hat / (jnp.sqrt(v_hat) + ADAM_EPS) + ADAM_WD * w_ref[...])
        nm_ref[...] = mm
        nv_ref[...] = vv

    row = pl.BlockSpec((tr, C), lambda i: (i, 0))
    return _call(body, name=name, grid=(R // tr,),
                 in_specs=[pl.BlockSpec((N_DEV, tr, C), lambda i: (0, i, 0)), row, row, row],
                 out_specs=[row] * 4, out_shape=[jax.ShapeDtypeStruct((R, C), F32)] * 4,
                 compiler_params=_params("parallel"))(parts, w, m, v)


def _pack(arrays):
    flat, meta, off = [], [], 0
    for a in arrays:
        flat.append(a.reshape(-1).astype(F32))
        meta.append((off, a.shape))
        off += a.size
    total = -(-off // (8 * LANES)) * (8 * LANES)
    flat.append(jnp.zeros((total - off,), F32))
    return jnp.concatenate(flat).reshape(total // LANES, LANES), meta


def _unpack(packed, meta):
    flat = packed.reshape(-1)
    return [flat[off:off + math.prod(shape)].reshape(shape) for off, shape in meta]


def kernel(x, ssd_in_proj, ssd_conv_w, ssd_conv_b, ssd_dt_bias, ssd_A_log, ssd_D, ssd_norm_w, ssd_out_proj, pool_w, pool_b, pool_scale, mlp_w1, mlp_w2, ln_mix_g, ln_mix_b, ln_ffn_g, ln_ffn_b, loss_target, m_ssd_in_proj, m_ssd_conv_w, m_ssd_conv_b, m_ssd_dt_bias, m_ssd_A_log, m_ssd_D, m_ssd_norm_w, m_ssd_out_proj, m_pool_w, m_pool_b, m_pool_scale, m_mlp_w1, m_mlp_w2, m_ln_mix_g, m_ln_mix_b, m_ln_ffn_g, m_ln_ffn_b, v_ssd_in_proj, v_ssd_conv_w, v_ssd_conv_b, v_ssd_dt_bias, v_ssd_A_log, v_ssd_D, v_ssd_norm_w, v_ssd_out_proj, v_pool_w, v_pool_b, v_pool_scale, v_mlp_w1, v_mlp_w2, v_ln_mix_g, v_ln_mix_b, v_ln_ffn_g, v_ln_ffn_b):
    T, D = x.shape[1], x.shape[2]
    depth = mlp_w1.shape[0]
    n_ssd, n_pool = ssd_in_proj.shape[0], pool_w.shape[0]
    d_inner = ssd_out_proj.shape[1] * N_DEV
    n_heads = d_inner // HEAD_DIM
    hg = n_heads // N_GROUPS
    d_bc = N_GROUPS * D_STATE
    d_xbc = d_inner + 2 * d_bc
    d_in_proj = ssd_in_proj.shape[2] * N_DEV
    d_ff = mlp_w1.shape[2] * N_DEV
    ng = len(POOL_WINDOWS)
    gd = D // ng
    alpha = (2.0 * depth) ** 0.25
    x0 = x.reshape(T, D)
    target = loss_target.reshape(T, D)

    assert depth == 4 and n_ssd == 2 and n_pool == 2, "the exchange schedules below are written for this stack"

    small_pack, small_meta = _pack([ssd_conv_w, pool_b, pool_scale])
    pw_rows = pool_w.shape[1] * pool_w.shape[2]
    in_b, out_b = ssd_in_proj.astype(BF16), ssd_out_proj.astype(BF16)
    pw_b = pool_w.reshape(n_pool, pw_rows, gd).astype(BF16)
    w1_b, w2_b = mlp_w1.astype(BF16), mlp_w2.astype(BF16)
    shard = {("in", 0): in_b[0], ("in", 1): in_b[1], ("out", 0): out_b[0], ("out", 1): out_b[1],
             ("pool", 0): pw_b[0], ("pool", 1): pw_b[1]}
    for i in range(depth):
        shard["w1", i], shard["w2", i] = w1_b[i], w2_b[i]

    def full_cols(g):
        return g.transpose(1, 0, 2).reshape(g.shape[1], -1)

    def full_rows(g):
        return g.reshape(-1, g.shape[-1])

    def full_pool(g):
        return g.reshape(N_DEV, ng, gd // N_DEV, gd).transpose(1, 0, 2, 3).reshape(ng, gd, gd)

    def slab_cols(g):
        return g.reshape(g.shape[0], N_DEV, -1).transpose(1, 0, 2)

    def slab_rows(g):
        return g.reshape(N_DEV, -1, g.shape[-1])

    def slab_pool(g):
        return g.astype(BF16).reshape(ng, N_DEV, gd // N_DEV, gd).transpose(1, 0, 2, 3).reshape(N_DEV, pw_rows, gd)

    to_full = {"in": full_cols, "out": full_rows, "pool": full_pool, "w1": full_cols, "w2": full_rows}
    to_slab = {"in": slab_cols, "out": slab_rows, "pool": slab_pool, "w1": slab_cols, "w2": slab_rows}

    gather_on = {"in_proj_0": [("out", 0), ("pool", 0)], "conv_fwd_0": [("w1", 0)], "ssd_fwd_0": [("w2", 0), ("w1", 1)],
                 "out_proj_0": [("w2", 1)], "mlp_up_0": [("in", 1)], "mlp_down_0": [("out", 1)],
                 "mlp_up_1": [("w1", 2)], "mlp_down_1": [("w2", 2)],
                 "in_proj_2": [("pool", 1), ("w1", 3)], "conv_fwd_2": [("w2", 3)]}
    slabs_on = {"mlp_up_dx_3": [("w2", 3)], "mlp_down_dx_2": [("w1", 3), ("pool", 1)], "mlp_up_dx_2": [("w2", 2)],
                "out_proj_dx_2": [("w1", 2)], "gnorm_bwd_2": [("out", 1)], "in_proj_dx_2": [("in", 1)],
                "mlp_up_dx_1": [("w2", 1)], "mlp_down_dx_0": [("w1", 1), ("pool", 0)], "mlp_up_dx_0": [("w2", 0)],
                "out_proj_dx_0": [("w1", 0)], "gnorm_bwd_0": [("out", 0)], "in_proj_dx_0": [("in", 0)]}
    W, G, R = {}, {}, {}

    def run(fn, *args, name, **kw):
        if name in gather_on:
            keys = gather_on[name]
            res, got = fn(*args, name=name, comm=([shard[k] for k in keys], True), **kw)
            for k, g in zip(keys, got):
                W[k] = to_full[k[0]](g)
            return res
        if name in slabs_on:
            keys = slabs_on[name]
            res, got = fn(*args, name=name, comm=([G[k] for k in keys], False), **kw)
            for k, g in zip(keys, got):
                R[k] = g
            return res
        return fn(*args, name=name, **kw)

    g_in0, g_small = _exchange([shard["in", 0], small_pack], True, "gather_first")
    W["in", 0] = full_cols(g_in0)
    smalls = [_unpack(g_small[k], small_meta) for k in range(N_DEV)]
    conv_w = jnp.concatenate([s[0] for s in smalls], axis=-1).reshape(n_ssd, CONV_WIDTH, d_xbc)
    pool_bias = jnp.concatenate([s[1] for s in smalls], axis=-1).reshape(n_pool, 1, D)
    pool_sc = jnp.concatenate([s[2] for s in smalls], axis=-1).reshape(n_pool, 1, D)

    sel = (jnp.arange(d_inner)[:, None] // HEAD_DIM == jnp.arange(n_heads)[None, :]).astype(F32)

    saved = []
    h, hb = x0, x0.astype(BF16)
    for i in range(depth):
        j = i // 2
        s = {}
        s["x0"], s["x0b"] = h, hb
        if i % 2 == 0:
            (zx,) = run(_matmul, hb, W["in", j], "nn", name=f"in_proj_{i}", outs=[F32], tn=1152)
            xbc = run(_conv_fwd, zx, conv_w[j], ssd_conv_b[j], d_inner, name=f"conv_fwd_{i}")
            raw = zx[:, d_inner + d_xbc:]
            dt, cs = _dt_fwd(raw, ssd_dt_bias[j], ssd_A_log[j], f"dt_fwd_{i}")
            dtc, csc, csr = _cols(dt, hg), _cols(cs, hg), _rows(cs, hg)
            y2, states = run(_ssd_fwd, xbc, dtc, csc, csr, d_inner, name=f"ssd_fwd_{i}")
            dvec = jnp.repeat(ssd_D[j], HEAD_DIM).reshape(1, d_inner)
            nw = ssd_norm_w[j].reshape(1, d_inner)
            gn = _gnorm_fwd(y2, xbc, zx, dvec, nw, d_inner, f"gnorm_fwd_{i}")
            (mix,) = run(_matmul, gn, W["out", j], "nn", name=f"out_proj_{i}", outs=[F32])
            s.update(zx=zx, xbc=xbc, raw=raw, dtc=dtc, csc=csc, csr=csr, y2=y2, states=states, dvec=dvec, nw=nw, gn=gn)
        else:
            mix = _pool_fwd(h, W["pool", j], pool_bias[j], pool_sc[j], f"pool_fwd_{i}")
        s["mix"] = mix
        x1, x1b = _ln_fwd(h, mix, ln_mix_g[i], ln_mix_b[i], alpha, f"ln_mix_fwd_{i}")
        u, hh = run(_matmul, x1b, W["w1", i], "nn", name=f"mlp_up_{i}", outs=[F32, BF16],
                    epilogue=lambda acc: (acc, jnp.square(jnp.maximum(acc, 0.0))))
        (m2,) = run(_matmul, hh, W["w2", i], "nn", name=f"mlp_down_{i}", outs=[F32])
        x2, x2b = _ln_fwd(x1, m2, ln_ffn_g[i], ln_ffn_b[i], alpha, f"ln_ffn_fwd_{i}")
        s.update(x1=x1, x1b=x1b, u=u, hh=hh, m2=m2)
        saved.append(s)
        h, hb = x2, x2b

    loss_row, dh = _loss_head(h, target)
    loss = lax.psum(loss_row[0, 0], ("x", "y", "c"))

    big = {"in": (ssd_in_proj, m_ssd_in_proj, v_ssd_in_proj), "out": (ssd_out_proj, m_ssd_out_proj, v_ssd_out_proj),
           "pool": (pool_w, m_pool_w, v_pool_w), "w1": (mlp_w1, m_mlp_w1, v_mlp_w1), "w2": (mlp_w2, m_mlp_w2, v_mlp_w2)}
    done = {}

    def update(parts, w, m, v, name):
        shape = w.shape
        C = shape[-1]
        res = _adamw(parts.reshape(N_DEV, -1, C), w.reshape(-1, C), m.reshape(-1, C), v.reshape(-1, C), name)
        return [r.reshape(shape) for r in res]

    def settle():
        for key in list(R):
            if key not in done:
                w, m, v = big[key[0]]
                done[key] = update(R[key], w[key[1]], m[key[1]], v[key[1]], f"adamw_{key[0]}_{key[1]}")

    gr = {k: [None] * depth for k in ("ln_mix_g", "ln_mix_b", "ln_ffn_g", "ln_ffn_b")}
    gs = {k: [None] * n_ssd for k in ("conv_w", "conv_b", "dt_bias", "A_log", "D", "norm_w")}
    gp = {k: [None] * n_pool for k in ("b", "scale")}
    for i in reversed(range(depth)):
        j = i // 2
        s = saved[i]
        ds2, ds2b, gr["ln_ffn_g"][i], gr["ln_ffn_b"][i] = _ln_bwd(s["x1"], s["m2"], ln_ffn_g[i], dh, alpha, f"ln_ffn_bwd_{i}")
        (du,) = run(_matmul, ds2b, W["w2", i], "nt", name=f"mlp_down_dx_{i}", outs=[BF16], extras=[s["u"]],
                    epilogue=lambda acc, u_: (acc * (2.0 * jnp.maximum(u_, 0.0)),))
        (g_w2,) = _matmul(s["hh"], ds2b, "tn", name=f"mlp_down_dw_{i}", outs=[BF16])
        G["w2", i] = slab_rows(g_w2)
        (dx1,) = run(_matmul, du, W["w1", i], "nt", name=f"mlp_up_dx_{i}", outs=[F32], extras=[ds2],
                     epilogue=lambda acc, e: (acc + alpha * e,))
        (g_w1,) = _matmul(s["x1b"], du, "tn", name=f"mlp_up_dw_{i}", outs=[BF16])
        G["w1", i] = slab_cols(g_w1)
        ds1, ds1b, gr["ln_mix_g"][i], gr["ln_mix_b"][i] = _ln_bwd(s["x0"], s["mix"], ln_mix_g[i], dx1, alpha, f"ln_mix_bwd_{i}")
        if i % 2 == 0:
            (dgn,) = run(_matmul, ds1b, W["out", j], "nt", name=f"out_proj_dx_{i}", outs=[F32])
            (g_out,) = _matmul(s["gn"], ds1b, "tn", name=f"out_proj_dw_{i}", outs=[BF16])
            G["out", j] = slab_rows(g_out)
            dy, dxs_d, dz, gs["norm_w"][j], gs["D"][j] = run(
                _gnorm_bwd, s["y2"], s["xbc"], s["zx"], s["dvec"], s["nw"], dgn, sel, d_inner, name=f"gnorm_bwd_{i}")
            dxs, dB, dC, dcs, dtot, dxdtx = _ssd_bwd(s["xbc"], s["dtc"], s["csc"], s["csr"], s["states"], s["y2"], dy,
                                                     d_inner, f"ssd_bwd_{i}")
            draw, gs["dt_bias"][j], gs["A_log"][j] = _dt_bwd(
                s["raw"], ssd_dt_bias[j], ssd_A_log[j], _uncols(dcs), _uncols(dtot), _uncols(dxdtx), f"dt_bwd_{i}")
            dpre = jnp.concatenate([
                _conv_dpre(s["zx"], conv_w[j], ssd_conv_b[j], d_inner, 0, dxs, dxs_d, f"conv_dpre_x_{i}"),
                _conv_dpre(s["zx"], conv_w[j], ssd_conv_b[j], d_inner, d_inner, dB, None, f"conv_dpre_b_{i}"),
                _conv_dpre(s["zx"], conv_w[j], ssd_conv_b[j], d_inner, d_inner + d_bc, dC, None, f"conv_dpre_c_{i}"),
            ], axis=1)
            din, gs["conv_w"][j], gs["conv_b"][j] = _conv_bwd(s["zx"], dpre, conv_w[j], d_inner, f"conv_bwd_{i}")
            dzx = jnp.concatenate([dz, din, draw], axis=1)
            (g_in,) = _matmul(s["x0b"], dzx, "tn", name=f"in_proj_dw_{i}", outs=[BF16], tn=1152)
            G["in", j] = slab_cols(g_in)
            (dh,) = run(_matmul, dzx, W["in", j], "nt", name=f"in_proj_dx_{i}", outs=[F32], extras=[ds1], tk=1152,
                        epilogue=lambda acc, e: (acc + alpha * e,))
        else:
            dm, g_pw, gp["b"][j], gp["scale"][j] = _pool_bwd_a(
                s["x0"], W["pool", j], pool_bias[j], pool_sc[j], ds1, f"pool_bwd_a_{i}")
            G["pool", j] = slab_pool(g_pw)
            dh = _pool_bwd_b(dm, ds1, alpha, f"pool_bwd_b_{i}")
        settle()
    grad_x = dh.reshape(x.shape)

    g_conv_w = jnp.stack(gs["conv_w"]).reshape(n_ssd, CONV_WIDTH, 1, N_DEV, d_xbc // N_DEV)
    g_pool_b = jnp.stack(gp["b"]).reshape(n_pool, ng, N_DEV, gd // N_DEV)
    g_pool_s = jnp.stack(gp["scale"]).reshape(n_pool, N_DEV, D // N_DEV)
    s_small = jnp.stack([_pack([g_conv_w[:, :, :, k], g_pool_b[:, :, k], g_pool_s[:, k]])[0] for k in range(N_DEV)])
    repl_grads = [jnp.stack(gs["conv_b"]).reshape(ssd_conv_b.shape), jnp.stack(gs["dt_bias"]).reshape(ssd_dt_bias.shape),
                  jnp.stack(gs["A_log"]).reshape(ssd_A_log.shape), jnp.stack(gs["D"]).reshape(ssd_D.shape),
                  jnp.stack(gs["norm_w"]).reshape(ssd_norm_w.shape),
                  jnp.stack(gr["ln_mix_g"]).reshape(ln_mix_g.shape), jnp.stack(gr["ln_mix_b"]).reshape(ln_mix_b.shape),
                  jnp.stack(gr["ln_ffn_g"]).reshape(ln_ffn_g.shape), jnp.stack(gr["ln_ffn_b"]).reshape(ln_ffn_b.shape)]
    repl_pack, repl_meta = _pack(repl_grads)
    s_repl = jnp.broadcast_to(repl_pack[None], (N_DEV,) + repl_pack.shape)
    r_small, r_repl = _exchange([s_small, s_repl], False, "exchange_small")

    upd = {}
    for nm, kind, count in (("ssd_in_proj", "in", n_ssd), ("ssd_out_proj", "out", n_ssd), ("pool_w", "pool", n_pool),
                            ("mlp_w1", "w1", depth), ("mlp_w2", "w2", depth)):
        upd[nm] = [jnp.stack([done[kind, l][q] for l in range(count)]) for q in range(4)]
    sm = update(r_small, small_pack, _pack([m_ssd_conv_w, m_pool_b, m_pool_scale])[0],
                _pack([v_ssd_conv_w, v_pool_b, v_pool_scale])[0], "adamw_small_sharded")
    for idx, nm in enumerate(["ssd_conv_w", "pool_b", "pool_scale"]):
        upd[nm] = [_unpack(r, small_meta)[idx] for r in sm]
    repl_names = ["ssd_conv_b", "ssd_dt_bias", "ssd_A_log", "ssd_D", "ssd_norm_w",
                  "ln_mix_g", "ln_mix_b", "ln_ffn_g", "ln_ffn_b"]
    repl_w = [ssd_conv_b, ssd_dt_bias, ssd_A_log, ssd_D, ssd_norm_w, ln_mix_g, ln_mix_b, ln_ffn_g, ln_ffn_b]
    repl_m = [m_ssd_conv_b, m_ssd_dt_bias, m_ssd_A_log, m_ssd_D, m_ssd_norm_w, m_ln_mix_g, m_ln_mix_b, m_ln_ffn_g, m_ln_ffn_b]
    repl_v = [v_ssd_conv_b, v_ssd_dt_bias, v_ssd_A_log, v_ssd_D, v_ssd_norm_w, v_ln_mix_g, v_ln_mix_b, v_ln_ffn_g, v_ln_ffn_b]
    rp = update(r_repl, _pack(repl_w)[0], _pack(repl_m)[0], _pack(repl_v)[0], "adamw_replicated")
    for idx, nm in enumerate(repl_names):
        upd[nm] = [_unpack(r, repl_meta)[idx] for r in rp]

    order = ["ssd_in_proj", "ssd_conv_w", "ssd_conv_b", "ssd_dt_bias", "ssd_A_log", "ssd_D", "ssd_norm_w",
             "ssd_out_proj", "pool_w", "pool_b", "pool_scale", "mlp_w1", "mlp_w2",
             "ln_mix_g", "ln_mix_b", "ln_ffn_g", "ln_ffn_b"]
    return (loss, grad_x, *[upd[n][0] for n in order], *[upd[n][1] for n in order],
            *[upd[n][2] for n in order], *[upd[n][3] for n in order])
```

```python
import functools
import math

import jax
import jax.numpy as jnp
from jax import lax
from jax.experimental import pallas as pl
from jax.experimental.pallas import tpu as pltpu

F32 = jnp.float32
BF16 = jnp.bfloat16

N_DEV = 8
HEAD_DIM = 64
N_GROUPS = 8
D_STATE = 128
CHUNK = 128
CONV_WIDTH = 5
POOL_WINDOWS = (2, 4, 8, 16)
HALO = 8
LN_EPS = 1e-5
RMS_EPS = 1e-5
ADAM_LR = 0.001
ADAM_B1 = 0.9
ADAM_B2 = 0.999
ADAM_EPS = 1e-08
ADAM_WD = 0.01
ADAM_STEP = 10
LANES = 128
VMEM_LIMIT_BYTES = 56 * 1024 * 1024
HIGHEST = lax.Precision.HIGHEST


def _pallas(body, **kw):
    return pl.pallas_call(body, **kw)


def _params(*sem):
    return pltpu.CompilerParams(dimension_semantics=sem, vmem_limit_bytes=VMEM_LIMIT_BYTES)


def _mesh_pos():
    return lax.axis_index("x"), lax.axis_index("y"), lax.axis_index("c")


def _peer(x, y, c, k):
    dx, dy, dc = (k >> 2) & 1, (k >> 1) & 1, k & 1
    px = (1 - x) if dx else x
    py = (1 - y) if dy else y
    pc = (1 - c) if dc else c
    return px, py, pc


def _comm_copies(ops, ins, outs, send_sems, recv_sems, local_sems):
    x, y, c = _mesh_pos()
    me = 4 * x + 2 * y + c
    copies = []
    for a, op in enumerate(ops):
        src, dst = ins[a], outs[a]

        def remote(k, s, d, to):
            return pltpu.make_async_remote_copy(src_ref=s, dst_ref=d, send_sem=send_sems.at[a, k - 1],
                                                recv_sem=recv_sems.at[a, k - 1], device_id=to,
                                                device_id_type=pl.DeviceIdType.MESH)

        if op[0] in ("gather", "gather1"):
            copies.append(pltpu.make_async_copy(src, dst.at[me], local_sems.at[a]))
            for k in (range(1, N_DEV) if op[0] == "gather" else (1, 2, 4, 6)):
                copies.append(remote(k, src, dst.at[me], _peer(x, y, c, k)))
        elif op[0] == "gather2":
            for k in (2, 4, 6):
                qx, qy, qc = _peer(x, y, c, k)
                slot = 4 * qx + 2 * qy + qc
                copies.append(remote(k, src.at[slot], dst.at[slot], _peer(x, y, c, 1)))
        else:
            rows = pl.ds(op[1], op[2])
            copies.append(pltpu.make_async_copy(src.at[me, rows], dst.at[me], local_sems.at[a]))
            for k in range(1, N_DEV):
                px, py, pc = _peer(x, y, c, k)
                copies.append(remote(k, src.at[4 * px + 2 * py + pc, rows], dst.at[me], (px, py, pc)))
    return copies


def _comm_plan(comm):
    ops = [(c[0],) + tuple(c[2:]) for c in comm]
    arrays = [c[1] for c in comm]
    shapes = []
    for op, a in zip(ops, arrays):
        if op[0] in ("gather", "gather1"):
            shapes.append(jax.ShapeDtypeStruct((N_DEV,) + a.shape, a.dtype))
        elif op[0] == "gather2":
            shapes.append(jax.ShapeDtypeStruct(a.shape, a.dtype))
        else:
            shapes.append(jax.ShapeDtypeStruct((N_DEV, op[2]) + a.shape[2:], a.dtype))
    n = len(ops)
    sems = [pltpu.SemaphoreType.DMA((n, N_DEV - 1)), pltpu.SemaphoreType.DMA((n, N_DEV - 1)),
            pltpu.SemaphoreType.DMA((n,))]
    in_place = [a for a, op in enumerate(ops) if op[0] == "gather2"]
    return ops, arrays, shapes, sems, in_place


def _call(body, *, comm=None, **kw):
    if not comm:
        return _pallas(body, **kw)
    ops, arrays, c_shape, c_sems, in_place = _comm_plan(comm)
    n = len(ops)
    grid = tuple(kw["grid"])
    single = not isinstance(kw["out_shape"], (list, tuple))
    out_shape = [kw["out_shape"]] if single else list(kw["out_shape"])
    out_specs = [kw["out_specs"]] if single else list(kw["out_specs"])
    in_specs = list(kw["in_specs"])
    scratch = list(kw.get("scratch_shapes", ()))
    n_in, n_out, n_scr = len(in_specs), len(out_shape), len(scratch)

    def wrapped(*refs):
        ins, refs = refs[:n_in], refs[n_in:]
        c_ins, refs = refs[:n], refs[n:]
        outs, refs = refs[:n_out], refs[n_out:]
        c_outs, refs = refs[:n], refs[n:]
        scr, sems = refs[:n_scr], refs[n_scr:]
        first = last = None
        for ax, size in enumerate(grid):
            i = pl.program_id(ax)
            first = (i == 0) if first is None else first & (i == 0)
            last = (i == size - 1) if last is None else last & (i == size - 1)

        @pl.when(first)
        def _():
            for cp in _comm_copies(ops, c_ins, c_outs, *sems):
                cp.start()

        body(*ins, *outs, *scr)

        @pl.when(last)
        def _():
            for cp in _comm_copies(ops, c_ins, c_outs, *sems):
                cp.wait()

    any_spec = pl.BlockSpec(memory_space=pl.ANY)
    call = _pallas(wrapped, name=kw["name"], grid=grid, in_specs=in_specs + [any_spec] * n,
                   out_specs=out_specs + [any_spec] * n, out_shape=out_shape + c_shape,
                   scratch_shapes=scratch + c_sems,
                   input_output_aliases={n_in + a: n_out + a for a in in_place},
                   compiler_params=_params(*(("arbitrary",) * len(grid))))

    def run(*args):
        res = call(*args, *arrays)
        own = res[:n_out]
        return (own[0] if single else list(own)), list(res[n_out:])

    return run


def _tile(dim, target, align=LANES):
    if dim <= target:
        return dim
    t = (target // align) * align
    while t >= align:
        if dim % t == 0:
            return t
        t -= align
    return dim


def _dot(a, b, dims, precision=None):
    return lax.dot_general(a, b, (dims, ((), ())), precision=precision, preferred_element_type=F32)


NN = ((1,), (0,))
NT = ((1,), (1,))
TN = ((0,), (0,))


def _sigmoid(x):
    return 1.0 / (1.0 + jnp.exp(-x))


def _matmul(a, b, mode, *, name, outs, epilogue=None, extras=(), tm=1024, tn=1024, tk=2048, comm=None):
    if mode == "nn":
        (M, K), (K2, N) = a.shape, b.shape
    elif mode == "nt":
        (M, K), (N, K2) = a.shape, b.shape
    else:
        (K, M), (K2, N) = a.shape, b.shape
    assert K == K2, (a.shape, b.shape, mode)
    tm, tn, tk = _tile(M, tm), _tile(N, tn), _tile(K, tk)
    nk = K // tk
    dims = {"nn": NN, "nt": NT, "tn": TN}[mode]
    a_spec = (pl.BlockSpec((tk, tm), lambda i, j, k: (k, i)) if mode == "tn"
              else pl.BlockSpec((tm, tk), lambda i, j, k: (i, k)))
    b_spec = (pl.BlockSpec((tn, tk), lambda i, j, k: (j, k)) if mode == "nt"
              else pl.BlockSpec((tk, tn), lambda i, j, k: (k, j)))
    mn_spec = pl.BlockSpec((tm, tn), lambda i, j, k: (i, j))
    n_extra, n_out = len(extras), len(outs)

    def finish(acc, extra_refs, out_refs):
        res = (acc,) if epilogue is None else epilogue(acc, *[r[...] for r in extra_refs])
        for o_ref, r in zip(out_refs, res):
            o_ref[...] = r.astype(o_ref.dtype)

    def body(*refs):
        a_ref, b_ref = refs[0], refs[1]
        extra_refs = refs[2:2 + n_extra]
        out_refs = refs[2 + n_extra:2 + n_extra + n_out]
        part = _dot(a_ref[...].astype(BF16), b_ref[...].astype(BF16), dims)
        if nk == 1:
            finish(part, extra_refs, out_refs)
            return
        acc_ref = refs[-1]
        k = pl.program_id(2)

        @pl.when(k == 0)
        def _():
            acc_ref[...] = part

        @pl.when((k > 0) & (k < nk - 1))
        def _():
            acc_ref[...] += part

        @pl.when(k == nk - 1)
        def _():
            finish(acc_ref[...] + part, extra_refs, out_refs)

    res = _call(
        body, comm=comm, name=name, grid=(M // tm, N // tn, nk),
        in_specs=[a_spec, b_spec] + [mn_spec] * n_extra,
        out_specs=[mn_spec] * n_out,
        out_shape=[jax.ShapeDtypeStruct((M, N), dt) for dt in outs],
        scratch_shapes=[pltpu.VMEM((tm, tn), F32)] if nk > 1 else [],
        compiler_params=_params("parallel", "parallel", "arbitrary"),
    )(a, b, *extras)
    return res


def _ln_fwd(x, f, g, b, alpha, name):
    T, D = x.shape
    tm = _tile(T, 256, 8)

    def body(x_ref, f_ref, g_ref, b_ref, y_ref, yb_ref):
        s = alpha * x_ref[...] + f_ref[...]
        mu = jnp.mean(s, axis=-1, keepdims=True)
        d = s - mu
        var = jnp.mean(d * d, axis=-1, keepdims=True)
        y = d * lax.rsqrt(var + LN_EPS) * g_ref[...] + b_ref[...]
        y_ref[...] = y
        yb_ref[...] = y.astype(BF16)

    row = pl.BlockSpec((tm, D), lambda i: (i, 0))
    vec = pl.BlockSpec((1, D), lambda i: (0, 0))
    return _call(body, name=name, grid=(T // tm,), in_specs=[row, row, vec, vec], out_specs=[row, row],
                 out_shape=[jax.ShapeDtypeStruct((T, D), F32), jax.ShapeDtypeStruct((T, D), BF16)],
                 compiler_params=_params("parallel"))(x, f, g.reshape(1, D), b.reshape(1, D))


def _ln_bwd(x, f, g, dy, alpha, name):
    T, D = x.shape
    tm = _tile(T, 256, 8)

    def body(x_ref, f_ref, g_ref, dy_ref, ds_ref, dsb_ref, dg_ref, db_ref):
        i = pl.program_id(0)

        @pl.when(i == 0)
        def _():
            dg_ref[...] = jnp.zeros_like(dg_ref)
            db_ref[...] = jnp.zeros_like(db_ref)

        s = alpha * x_ref[...] + f_ref[...]
        mu = jnp.mean(s, axis=-1, keepdims=True)
        d = s - mu
        var = jnp.mean(d * d, axis=-1, keepdims=True)
        rstd = lax.rsqrt(var + LN_EPS)
        xhat = d * rstd
        dy_ = dy_ref[...]
        dg_ref[...] += jnp.sum(dy_ * xhat, axis=0, keepdims=True)
        db_ref[...] += jnp.sum(dy_, axis=0, keepdims=True)
        dxh = dy_ * g_ref[...]
        m1 = jnp.mean(dxh, axis=-1, keepdims=True)
        m2 = jnp.mean(dxh * xhat, axis=-1, keepdims=True)
        ds = rstd * (dxh - m1 - xhat * m2)
        ds_ref[...] = ds
        dsb_ref[...] = ds.astype(BF16)

    row = pl.BlockSpec((tm, D), lambda i: (i, 0))
    vec = pl.BlockSpec((1, D), lambda i: (0, 0))
    return _call(body, name=name, grid=(T // tm,), in_specs=[row, row, vec, row], out_specs=[row, row, vec, vec],
                 out_shape=[jax.ShapeDtypeStruct((T, D), F32), jax.ShapeDtypeStruct((T, D), BF16),
                            jax.ShapeDtypeStruct((1, D), F32), jax.ShapeDtypeStruct((1, D), F32)],
                 compiler_params=_params("arbitrary"))(x, f, g.reshape(1, D), dy)


def _loss_head(y, target):
    T, D = y.shape
    tm = _tile(T, 256, 8)

    def body(y_ref, t_ref, loss_ref, dy_ref):
        i = pl.program_id(0)

        @pl.when(i == 0)
        def _():
            loss_ref[...] = jnp.zeros_like(loss_ref)

        err = y_ref[...] - t_ref[...]
        dy_ref[...] = err * (1.0 / D)
        per_tok = jnp.mean(err * err, axis=-1, keepdims=True)
        loss_ref[...] += 0.5 * jnp.sum(per_tok)

    row = pl.BlockSpec((tm, D), lambda i: (i, 0))
    return _call(body, name="loss_head", grid=(T // tm,), in_specs=[row, row],
                 out_specs=[pl.BlockSpec((1, LANES), lambda i: (0, 0)), row],
                 out_shape=[jax.ShapeDtypeStruct((1, LANES), F32), jax.ShapeDtypeStruct((T, D), F32)],
                 compiler_params=_params("arbitrary"))(y, target)


def _halo_specs(tt, cw, col_of, n_tiles, grid_rank_tokens_axis):
    per = tt // HALO
    ax = grid_rank_tokens_axis

    def cur(*g):
        return (g[ax], col_of(*g))

    def prev(*g):
        return (jnp.maximum(g[ax] * per - 1, 0), col_of(*g))

    def nxt(*g):
        return (jnp.minimum((g[ax] + 1) * per, n_tiles * per - 1), col_of(*g))

    return [pl.BlockSpec((tt, cw), cur), pl.BlockSpec((HALO, cw), prev), pl.BlockSpec((HALO, cw), nxt)]


def _fill_ext(ext_ref, cur_ref, prev_ref, next_ref, i, n_tiles, tt):
    ext_ref[pl.ds(0, HALO), :] = jnp.where(i > 0, prev_ref[...], 0.0)
    ext_ref[pl.ds(HALO, tt), :] = cur_ref[...]
    ext_ref[pl.ds(HALO + tt, HALO), :] = jnp.where(i < n_tiles - 1, next_ref[...], 0.0)


def _conv_pre(ext_ref, w, bias, tt, lo=0, n=None):
    n = tt if n is None else n
    pad = CONV_WIDTH // 2
    acc = None
    for k in range(CONV_WIDTH):
        term = ext_ref[pl.ds(HALO + lo + k - pad, n), :] * w[k:k + 1, :]
        acc = term if acc is None else acc + term
    return acc + bias


def _conv_fwd(zx, conv_w, conv_b, d_inner, name, comm=None):
    T = zx.shape[0]
    d_xbc = conv_w.shape[1]
    cw = _tile(d_xbc, 512)
    assert d_inner % cw == 0
    off = d_inner // cw
    tt = _tile(T, 512, 8)
    nt = T // tt

    def body(cur_ref, prev_ref, next_ref, w_ref, b_ref, o_ref, ext_ref):
        i = pl.program_id(1)
        _fill_ext(ext_ref, cur_ref, prev_ref, next_ref, i, nt, tt)
        pre = _conv_pre(ext_ref, w_ref[...], b_ref[...], tt)
        o_ref[...] = pre * _sigmoid(pre)

    specs = _halo_specs(tt, cw, lambda j, i: off + j, nt, 1)
    return _call(body, comm=comm, name=name, grid=(d_xbc // cw, nt),
                 in_specs=specs + [pl.BlockSpec((CONV_WIDTH, cw), lambda j, i: (0, j)),
                                   pl.BlockSpec((1, cw), lambda j, i: (0, j))],
                 out_specs=pl.BlockSpec((tt, cw), lambda j, i: (i, j)),
                 out_shape=jax.ShapeDtypeStruct((T, d_xbc), F32),
                 scratch_shapes=[pltpu.VMEM((tt + 2 * HALO, cw), F32)],
                 compiler_params=_params("parallel", "parallel"))(zx, zx, zx, conv_w, conv_b.reshape(1, d_xbc))


def _conv_dpre(zx, conv_w, conv_b, d_inner, col_lo, dirs, extra, name):
    T = zx.shape[0]
    ncols = dirs.shape[2]
    cw = _tile(ncols, 512)
    assert d_inner % cw == 0 and col_lo % cw == 0
    off_zx = (d_inner + col_lo) // cw
    off_w = col_lo // cw
    tt = _tile(T, 512, 8)
    nt = T // tt
    has_extra = extra is not None

    def body(cur_ref, prev_ref, next_ref, w_ref, b_ref, dirs_ref, *rest):
        o_ref, ext_ref = rest[-2], rest[-1]
        i = pl.program_id(1)
        _fill_ext(ext_ref, cur_ref, prev_ref, next_ref, i, nt, tt)
        pre = _conv_pre(ext_ref, w_ref[...], b_ref[...], tt)
        sig = _sigmoid(pre)
        dact = dirs_ref[0] + dirs_ref[1]
        if has_extra:
            dact = dact + rest[0][...]
        o_ref[...] = dact * (sig * (1.0 + pre * (1.0 - sig)))

    specs = _halo_specs(tt, cw, lambda j, i: off_zx + j, nt, 1)
    in_specs = specs + [pl.BlockSpec((CONV_WIDTH, cw), lambda j, i: (0, off_w + j)),
                        pl.BlockSpec((1, cw), lambda j, i: (0, off_w + j)),
                        pl.BlockSpec((2, tt, cw), lambda j, i: (0, i, j))]
    args = [zx, zx, zx, conv_w, conv_b.reshape(1, -1), dirs]
    if has_extra:
        in_specs.append(pl.BlockSpec((tt, cw), lambda j, i: (i, j)))
        args.append(extra)
    return _call(body, name=name, grid=(ncols // cw, nt), in_specs=in_specs,
                 out_specs=pl.BlockSpec((tt, cw), lambda j, i: (i, j)),
                 out_shape=jax.ShapeDtypeStruct((T, ncols), F32),
                 scratch_shapes=[pltpu.VMEM((tt + 2 * HALO, cw), F32)],
                 compiler_params=_params("parallel", "parallel"))(*args)


def _conv_bwd(zx, dpre, conv_w, d_inner, name, comm=None):
    T = zx.shape[0]
    d_xbc = conv_w.shape[1]
    cw = _tile(d_xbc, 512)
    off = d_inner // cw
    tt = _tile(T, 512, 8)
    nt = T // tt
    pad = CONV_WIDTH // 2

    def body(zc, zp, zn, dc, dp, dn, w_ref, din_ref, dw_ref, db_ref, zext, dext):
        i = pl.program_id(1)

        @pl.when(i == 0)
        def _():
            dw_ref[...] = jnp.zeros_like(dw_ref)
            db_ref[...] = jnp.zeros_like(db_ref)

        _fill_ext(zext, zc, zp, zn, i, nt, tt)
        _fill_ext(dext, dc, dp, dn, i, nt, tt)
        w = w_ref[...]
        d = dc[...]
        acc = None
        for k in range(CONV_WIDTH):
            term = dext[pl.ds(HALO + pad - k, tt), :] * w[k:k + 1, :]
            acc = term if acc is None else acc + term
            dw_ref[k:k + 1, :] += jnp.sum(d * zext[pl.ds(HALO + k - pad, tt), :], axis=0, keepdims=True)
        din_ref[...] = acc.astype(din_ref.dtype)
        db_ref[...] += jnp.sum(d, axis=0, keepdims=True)

    zspecs = _halo_specs(tt, cw, lambda j, i: off + j, nt, 1)
    dspecs = _halo_specs(tt, cw, lambda j, i: j, nt, 1)
    return _call(body, comm=comm, name=name, grid=(d_xbc // cw, nt),
                 in_specs=zspecs + dspecs + [pl.BlockSpec((CONV_WIDTH, cw), lambda j, i: (0, j))],
                 out_specs=[pl.BlockSpec((tt, cw), lambda j, i: (i, j)),
                            pl.BlockSpec((CONV_WIDTH, cw), lambda j, i: (0, j)),
                            pl.BlockSpec((1, cw), lambda j, i: (0, j))],
                 out_shape=[jax.ShapeDtypeStruct((T, d_xbc), BF16), jax.ShapeDtypeStruct((CONV_WIDTH, d_xbc), F32),
                            jax.ShapeDtypeStruct((1, d_xbc), F32)],
                 scratch_shapes=[pltpu.VMEM((tt + 2 * HALO, cw), F32), pltpu.VMEM((tt + 2 * HALO, cw), F32)],
                 compiler_params=_params("parallel", "arbitrary"))(zx, zx, zx, dpre, dpre, dpre, conv_w)


def _tri(n):
    r = lax.broadcasted_iota(jnp.int32, (n, n), 0)
    c = lax.broadcasted_iota(jnp.int32, (n, n), 1)
    return (r >= c).astype(F32), (r <= c).astype(F32)


def _dt_fwd(raw, bias, a_log, name):
    T, H2 = raw.shape
    half = H2 // 2

    def body(raw_ref, bias_ref, alog_ref, dt_ref, cs_ref):
        x = raw_ref[...] + bias_ref[...]
        dt = jnp.maximum(x, 0.0) + jnp.log(1.0 + jnp.exp(-jnp.abs(x)))
        a = dt * (-jnp.exp(alog_ref[...]))
        lower, upper = _tri(CHUNK)
        cs_f = _dot(lower, a, NN, HIGHEST)
        cs_b = _dot(upper, a, NN, HIGHEST)
        lane = lax.broadcasted_iota(jnp.int32, (CHUNK, H2), 1)
        dt_ref[...] = dt
        cs_ref[...] = jnp.where(lane < half, cs_f, cs_b)

    row = pl.BlockSpec((CHUNK, H2), lambda c: (c, 0))
    vec = pl.BlockSpec((1, H2), lambda c: (0, 0))
    return _call(body, name=name, grid=(T // CHUNK,), in_specs=[row, vec, vec], out_specs=[row, row],
                 out_shape=[jax.ShapeDtypeStruct((T, H2), F32)] * 2,
                 compiler_params=_params("parallel"))(raw, bias.reshape(1, H2), a_log.reshape(1, H2))


def _dt_bwd(raw, bias, a_log, dcs, dtot, dxdtx, name):
    T, H2 = raw.shape
    half = H2 // 2

    def body(raw_ref, bias_ref, alog_ref, dcs_ref, dtot_ref, dx_ref, draw_ref, dbias_ref, dalog_ref):
        c = pl.program_id(0)

        @pl.when(c == 0)
        def _():
            dbias_ref[...] = jnp.zeros_like(dbias_ref)
            dalog_ref[...] = jnp.zeros_like(dalog_ref)

        x = raw_ref[...] + bias_ref[...]
        dt = jnp.maximum(x, 0.0) + jnp.log(1.0 + jnp.exp(-jnp.abs(x)))
        A = -jnp.exp(alog_ref[...])
        lower, upper = _tri(CHUNK)
        g = dcs_ref[...]
        lane = lax.broadcasted_iota(jnp.int32, (CHUNK, H2), 1)
        da = jnp.where(lane < half, _dot(upper, g, NN, HIGHEST), _dot(lower, g, NN, HIGHEST)) + dtot_ref[...]
        ddt = da * A + dx_ref[...]
        draw = ddt * _sigmoid(x)
        draw_ref[...] = draw.astype(draw_ref.dtype)
        dbias_ref[...] += jnp.sum(draw, axis=0, keepdims=True)
        dalog_ref[...] += jnp.sum(da * dt, axis=0, keepdims=True) * A

    row = pl.BlockSpec((CHUNK, H2), lambda c: (c, 0))
    vec = pl.BlockSpec((1, H2), lambda c: (0, 0))
    return _call(body, name=name, grid=(T // CHUNK,), in_specs=[row, vec, vec, row, row, row],
                 out_specs=[row, vec, vec],
                 out_shape=[jax.ShapeDtypeStruct((T, H2), BF16), jax.ShapeDtypeStruct((1, H2), F32),
                            jax.ShapeDtypeStruct((1, H2), F32)],
                 compiler_params=_params("arbitrary"))(raw, bias.reshape(1, H2), a_log.reshape(1, H2), dcs, dtot, dxdtx)


def _cols(a, hg):
    T = a.shape[0]
    return a.reshape(T, 2, N_GROUPS, hg).transpose(1, 2, 0, 3)


def _rows(a, hg):
    T = a.shape[0]
    return a.reshape(T, 2, N_GROUPS, hg).transpose(1, 2, 3, 0)


def _uncols(a):
    T = a.shape[2]
    return a.transpose(2, 0, 1, 3).reshape(T, -1)


def _ssd_masks(d):
    r = lax.broadcasted_iota(jnp.int32, (CHUNK, CHUNK), 0)
    c = lax.broadcasted_iota(jnp.int32, (CHUNK, CHUNK), 1)
    return ((r >= c) & (d == 0)) | ((r <= c) & (d == 1))


def _head_expand(hg):
    r = lax.broadcasted_iota(jnp.int32, (hg, hg * HEAD_DIM), 0)
    c = lax.broadcasted_iota(jnp.int32, (hg, hg * HEAD_DIM), 1)
    return (c // HEAD_DIM == r).astype(F32)


def _head_select(hg):
    r = lax.broadcasted_iota(jnp.int32, (hg * HEAD_DIM, hg), 0)
    c = lax.broadcasted_iota(jnp.int32, (hg * HEAD_DIM, hg), 1)
    return (r // HEAD_DIM == c).astype(F32)


def _ssd_common(d, csc_ref, dtc_ref, hg):
    expand = _head_expand(hg)
    csx = _dot(csc_ref[...], expand, NN, HIGHEST)
    dtx = _dot(dtc_ref[...], expand, NN, HIGHEST)
    totx = jnp.where(d == 0, csx[CHUNK - 1:CHUNK, :], csx[0:1, :])
    return csx, dtx, totx


def _ssd_fwd(xbc, dtc, csc, csr, d_inner, name, comm=None):
    T = xbc.shape[0]
    nc = T // CHUNK
    gw = d_inner // N_GROUPS
    hg = gw // HEAD_DIM
    P, N = HEAD_DIM, D_STATE
    b_off = d_inner // N
    c_off = b_off + N_GROUPS

    def cidx(d, c):
        return c + d * (nc - 1 - 2 * c)

    def body(xs_ref, b_ref, c_ref, dtc_ref, csc_ref, csr_ref, y_ref, st_ref, h_ref):
        d = pl.program_id(0)
        c = pl.program_id(2)

        @pl.when(c == 0)
        def _():
            h_ref[...] = jnp.zeros_like(h_ref)

        Bb = b_ref[...].astype(BF16)
        Cb = c_ref[...].astype(BF16)
        S = _dot(Cb, Bb, NT)
        mask = _ssd_masks(d)
        csx, dtx, totx = _ssd_common(d, csc_ref, dtc_ref, hg)
        H = h_ref[...]
        st_ref[...] = H
        xdt = xs_ref[...] * dtx
        xdtb = xdt.astype(BF16)
        y_off = jnp.exp(csx) * _dot(Cb, H.astype(BF16), NN)
        for j in range(hg):
            sl = slice(j * P, (j + 1) * P)
            decay = jnp.exp(jnp.where(mask, csc_ref[:, j:j + 1] - csr_ref[j:j + 1, :], -jnp.inf))
            y_ref[:, sl] = _dot((S * decay).astype(BF16), xdtb[:, sl], NN) + y_off[:, sl]
        h_ref[...] = jnp.exp(totx) * H + _dot(Bb, (jnp.exp(totx - csx) * xdt).astype(BF16), TN)

    col = lambda d, g, c: (d, g, cidx(d, c), 0)
    return _call(
        body, comm=comm, name=name, grid=(2, N_GROUPS, nc),
        in_specs=[pl.BlockSpec((CHUNK, gw), lambda d, g, c: (cidx(d, c), g)),
                  pl.BlockSpec((CHUNK, N), lambda d, g, c: (cidx(d, c), b_off + g)),
                  pl.BlockSpec((CHUNK, N), lambda d, g, c: (cidx(d, c), c_off + g)),
                  pl.BlockSpec((None, None, CHUNK, hg), col),
                  pl.BlockSpec((None, None, CHUNK, hg), col),
                  pl.BlockSpec((None, None, hg, CHUNK), lambda d, g, c: (d, g, 0, cidx(d, c)))],
        out_specs=[pl.BlockSpec((None, CHUNK, gw), lambda d, g, c: (d, cidx(d, c), g)),
                   pl.BlockSpec((None, None, None, N, gw), lambda d, g, c: (d, cidx(d, c), g, 0, 0))],
        out_shape=[jax.ShapeDtypeStruct((2, T, d_inner), F32),
                   jax.ShapeDtypeStruct((2, nc, N_GROUPS, N, gw), F32)],
        scratch_shapes=[pltpu.VMEM((N, gw), F32)],
        compiler_params=_params("parallel", "parallel", "arbitrary"),
    )(xbc, xbc, xbc, dtc, csc, csr)


def _ssd_bwd(xbc, dtc, csc, csr, states, y2, dy, d_inner, name, comm=None):
    T = xbc.shape[0]
    nc = T // CHUNK
    gw = d_inner // N_GROUPS
    hg = gw // HEAD_DIM
    P, N = HEAD_DIM, D_STATE
    b_off = d_inner // N
    c_off = b_off + N_GROUPS

    def cidx(d, c):
        return (nc - 1 - c) + d * (2 * c - nc + 1)

    def body(xs_ref, b_ref, c_ref, dtc_ref, csc_ref, csr_ref, st_ref, y_ref, dy_ref,
             dxs_ref, db_ref, dc_ref, dcs_ref, dtot_ref, dxdtx_ref, dh_ref, dxdt_ref):
        d = pl.program_id(0)
        c = pl.program_id(2)

        @pl.when(c == 0)
        def _():
            dh_ref[...] = jnp.zeros_like(dh_ref)

        Bb = b_ref[...].astype(BF16)
        Cb = c_ref[...].astype(BF16)
        S = _dot(Cb, Bb, NT)
        mask = _ssd_masks(d)
        csx, dtx, totx = _ssd_common(d, csc_ref, dtc_ref, hg)
        select = _head_select(hg)
        X = xs_ref[...]
        xdt = X * dtx
        xdtb = xdt.astype(BF16)
        dY = dy_ref[...]
        dYb = dY.astype(BF16)
        Hp = st_ref[...]
        Hpb = Hp.astype(BF16)
        dH = dh_ref[...]
        dHb = dH.astype(BF16)
        e_tot = jnp.exp(totx)
        dCH = (jnp.exp(csx) * dY).astype(BF16)
        dC = _dot(dCH, Hpb, NT)
        dHp = _dot(Cb, dCH, TN)
        Q = _dot(Bb, dHb, NN)
        dte = jnp.exp(totx - csx)
        wx = dte * xdt
        dB = _dot(wx.astype(BF16), dHb, NT)
        ddte = Q * wx
        dS = jnp.zeros((CHUNK, CHUNK), F32)
        for j in range(hg):
            sl = slice(j * P, (j + 1) * P)
            decay = jnp.exp(jnp.where(mask, csc_ref[:, j:j + 1] - csr_ref[j:j + 1, :], -jnp.inf))
            dS = dS + _dot(dYb[:, sl], xdtb[:, sl], NT) * decay
            dxdt_ref[:, sl] = _dot((S * decay).astype(BF16), dYb[:, sl], TN)
        dxdt_diag = dxdt_ref[...]
        dxdt = dxdt_diag + dte * Q
        dcs_ref[...] = _dot(dYb.astype(F32) * y_ref[...] - xdtb.astype(F32) * dxdt_diag - ddte, select, NN, HIGHEST)
        dtot_row = (jnp.sum(ddte, axis=0, keepdims=True) + e_tot * jnp.sum(dH * Hp, axis=0, keepdims=True))
        dtot_ref[...] = jnp.zeros((CHUNK, hg), F32) + _dot(dtot_row, select, NN, HIGHEST)
        dxdtx_ref[...] = _dot(dxdt * X, select, NN, HIGHEST)
        dxs_ref[...] = dxdt * dtx
        dh_ref[...] = e_tot * dH + dHp
        dSb = dS.astype(BF16)
        dc_ref[...] = dC + _dot(dSb, Bb, NN)
        db_ref[...] = dB + _dot(dSb, Cb, TN)

    col = lambda d, g, c: (d, g, cidx(d, c), 0)
    colspec = pl.BlockSpec((None, None, CHUNK, hg), col)
    rowblk = pl.BlockSpec((None, CHUNK, gw), lambda d, g, c: (d, cidx(d, c), g))
    return _call(
        body, comm=comm, name=name, grid=(2, N_GROUPS, nc),
        in_specs=[pl.BlockSpec((CHUNK, gw), lambda d, g, c: (cidx(d, c), g)),
                  pl.BlockSpec((CHUNK, N), lambda d, g, c: (cidx(d, c), b_off + g)),
                  pl.BlockSpec((CHUNK, N), lambda d, g, c: (cidx(d, c), c_off + g)),
                  colspec, colspec,
                  pl.BlockSpec((None, None, hg, CHUNK), lambda d, g, c: (d, g, 0, cidx(d, c))),
                  pl.BlockSpec((None, None, None, N, gw), lambda d, g, c: (d, cidx(d, c), g, 0, 0)),
                  rowblk,
                  pl.BlockSpec((CHUNK, gw), lambda d, g, c: (cidx(d, c), g))],
        out_specs=[rowblk,
                   pl.BlockSpec((None, CHUNK, N), lambda d, g, c: (d, cidx(d, c), g)),
                   pl.BlockSpec((None, CHUNK, N), lambda d, g, c: (d, cidx(d, c), g)),
                   colspec, colspec, colspec],
        out_shape=[jax.ShapeDtypeStruct((2, T, d_inner), F32),
                   jax.ShapeDtypeStruct((2, T, N_GROUPS * N), F32),
                   jax.ShapeDtypeStruct((2, T, N_GROUPS * N), F32)]
        + [jax.ShapeDtypeStruct((2, N_GROUPS, T, hg), F32)] * 3,
        scratch_shapes=[pltpu.VMEM((N, gw), F32), pltpu.VMEM((CHUNK, gw), F32)],
        compiler_params=_params("parallel", "parallel", "arbitrary"),
    )(xbc, xbc, xbc, dtc, csc, csr, states, y2, dy)


def _gnorm_fwd(y2, xbc, zx, dvec, nw, d_inner, name, comm=None):
    T = xbc.shape[0]
    gw = d_inner // N_GROUPS
    tm = _tile(T, 128, 8)

    def body(y_ref, xs_ref, z_ref, d_ref, w_ref, o_ref):
        for g in range(N_GROUPS):
            sl = slice(g * gw, (g + 1) * gw)
            y = y_ref[0, :, sl] + y_ref[1, :, sl] + xs_ref[:, sl] * d_ref[:, sl]
            z = z_ref[:, sl]
            gy = y * (z * _sigmoid(z))
            rs = lax.rsqrt(jnp.mean(gy * gy, axis=-1, keepdims=True) + RMS_EPS)
            o_ref[:, sl] = (gy * rs * w_ref[:, sl]).astype(o_ref.dtype)

    row = pl.BlockSpec((tm, d_inner), lambda i: (i, 0))
    vec = pl.BlockSpec((1, d_inner), lambda i: (0, 0))
    return _call(body, comm=comm, name=name, grid=(T // tm,),
                 in_specs=[pl.BlockSpec((2, tm, d_inner), lambda i: (0, i, 0)), row, row, vec, vec],
                 out_specs=row, out_shape=jax.ShapeDtypeStruct((T, d_inner), BF16),
                 compiler_params=_params("parallel"))(y2, xbc, zx, dvec, nw)


def _gnorm_bwd(y2, xbc, zx, dvec, nw, dgn, sel, d_inner, name, comm=None):
    T = xbc.shape[0]
    gw = d_inner // N_GROUPS
    n_heads = d_inner // HEAD_DIM
    tm = _tile(T, 128, 8)
    n_tiles = T // tm

    def body(y_ref, xs_ref, z_ref, d_ref, w_ref, dg_ref, sel_ref, dy_ref, dxs_ref, dz_ref, dw_ref, dd_ref, dch_ref):
        i = pl.program_id(0)

        @pl.when(i == 0)
        def _():
            dw_ref[...] = jnp.zeros_like(dw_ref)
            dch_ref[...] = jnp.zeros_like(dch_ref)

        for g in range(N_GROUPS):
            sl = slice(g * gw, (g + 1) * gw)
            xs = xs_ref[:, sl]
            y = y_ref[0, :, sl] + y_ref[1, :, sl] + xs * d_ref[:, sl]
            z = z_ref[:, sl]
            sig = _sigmoid(z)
            sz = z * sig
            gy = y * sz
            rs = lax.rsqrt(jnp.mean(gy * gy, axis=-1, keepdims=True) + RMS_EPS)
            n = gy * rs
            dout = dg_ref[:, sl]
            dw_ref[:, sl] += jnp.sum(dout * n, axis=0, keepdims=True)
            dn = dout * w_ref[:, sl]
            dgy = rs * (dn - n * jnp.mean(dn * n, axis=-1, keepdims=True))
            dy = dgy * sz
            dy_ref[:, sl] = dy
            dz_ref[:, sl] = (dgy * y * (sig * (1.0 + z * (1.0 - sig)))).astype(dz_ref.dtype)
            dxs_ref[:, sl] = dy * d_ref[:, sl]
            dch_ref[:, sl] += jnp.sum(dy * xs, axis=0, keepdims=True)

        @pl.when(i == n_tiles - 1)
        def _():
            dd_ref[...] = _dot(dch_ref[...], sel_ref[...], NN, HIGHEST)

    row = pl.BlockSpec((tm, d_inner), lambda i: (i, 0))
    vec = pl.BlockSpec((1, d_inner), lambda i: (0, 0))
    hvec = pl.BlockSpec((1, n_heads), lambda i: (0, 0))
    return _call(body, comm=comm, name=name, grid=(n_tiles,),
                 in_specs=[pl.BlockSpec((2, tm, d_inner), lambda i: (0, i, 0)), row, row, vec, vec, row,
                           pl.BlockSpec((d_inner, n_heads), lambda i: (0, 0))],
                 out_specs=[row, row, row, vec, hvec],
                 out_shape=[jax.ShapeDtypeStruct((T, d_inner), F32), jax.ShapeDtypeStruct((T, d_inner), F32),
                            jax.ShapeDtypeStruct((T, d_inner), BF16),
                            jax.ShapeDtypeStruct((1, d_inner), F32), jax.ShapeDtypeStruct((1, n_heads), F32)],
                 scratch_shapes=[pltpu.VMEM((1, d_inner), F32)],
                 compiler_params=_params("arbitrary"))(y2, xbc, zx, dvec, nw, dgn, sel)


def _pool_counts(i, tt, T, win, rows, row0):
    t = i * tt + row0 + lax.broadcasted_iota(jnp.int32, (rows, 1), 0)
    start = t - win // 2
    lo = jnp.clip(start, 0, T)
    hi = jnp.clip(start + win, 0, T)
    return jnp.maximum(hi - lo, 1).astype(F32)


def _pool_features(ext_ref, i, tt, T, gi, gd):
    win = POOL_WINDOWS[gi]
    sl = slice(gi * gd, (gi + 1) * gd)
    acc = None
    for o in range(-(win // 2), win - win // 2):
        term = ext_ref[pl.ds(HALO + o, tt), sl]
        acc = term if acc is None else acc + term
    return acc / _pool_counts(i, tt, T, win, tt, 0) - ext_ref[pl.ds(HALO, tt), sl]


def _pool_fwd(u, w, bias, scale, name):
    T, D = u.shape
    ng = len(POOL_WINDOWS)
    gd = D // ng
    tt = _tile(T, 512, 8)
    nt = T // tt

    def body(cur, prev, nxt, w_ref, b_ref, s_ref, o_ref, ext):
        i = pl.program_id(0)
        _fill_ext(ext, cur, prev, nxt, i, nt, tt)
        for gi in range(ng):
            sl = slice(gi * gd, (gi + 1) * gd)
            m = _pool_features(ext, i, tt, T, gi, gd)
            pre = _dot(m.astype(BF16), w_ref[gi], NN) + b_ref[:, sl]
            o_ref[:, sl] = pre * s_ref[:, sl]

    vec = pl.BlockSpec((1, D), lambda i: (0, 0))
    return _call(body, name=name, grid=(nt,),
                 in_specs=_halo_specs(tt, D, lambda i: 0, nt, 0)
                 + [pl.BlockSpec((ng, gd, gd), lambda i: (0, 0, 0)), vec, vec],
                 out_specs=pl.BlockSpec((tt, D), lambda i: (i, 0)),
                 out_shape=jax.ShapeDtypeStruct((T, D), F32),
                 scratch_shapes=[pltpu.VMEM((tt + 2 * HALO, D), F32)],
                 compiler_params=_params("parallel"))(u, u, u, w, bias, scale)


def _pool_bwd_a(u, w, bias, scale, dy, name):
    T, D = u.shape
    ng = len(POOL_WINDOWS)
    gd = D // ng
    tt = _tile(T, 512, 8)
    nt = T // tt

    def body(cur, prev, nxt, w_ref, b_ref, s_ref, dy_ref, dm_ref, dw_ref, db_ref, ds_ref, ext):
        i = pl.program_id(0)

        @pl.when(i == 0)
        def _():
            dw_ref[...] = jnp.zeros_like(dw_ref)
            db_ref[...] = jnp.zeros_like(db_ref)
            ds_ref[...] = jnp.zeros_like(ds_ref)

        _fill_ext(ext, cur, prev, nxt, i, nt, tt)
        for gi in range(ng):
            sl = slice(gi * gd, (gi + 1) * gd)
            mb = _pool_features(ext, i, tt, T, gi, gd).astype(BF16)
            wg = w_ref[gi]
            pre = _dot(mb, wg, NN) + b_ref[:, sl]
            dy_ = dy_ref[:, sl]
            ds_ref[:, sl] += jnp.sum(dy_ * pre, axis=0, keepdims=True)
            dpre = dy_ * s_ref[:, sl]
            db_ref[:, sl] += jnp.sum(dpre, axis=0, keepdims=True)
            dpb = dpre.astype(BF16)
            dw_ref[gi] += _dot(mb, dpb, TN)
            dm_ref[:, sl] = _dot(dpb, wg, NT)

    vec = pl.BlockSpec((1, D), lambda i: (0, 0))
    row = pl.BlockSpec((tt, D), lambda i: (i, 0))
    wspec = pl.BlockSpec((ng, gd, gd), lambda i: (0, 0, 0))
    return _call(body, name=name, grid=(nt,),
                 in_specs=_halo_specs(tt, D, lambda i: 0, nt, 0) + [wspec, vec, vec, row],
                 out_specs=[row, wspec, vec, vec],
                 out_shape=[jax.ShapeDtypeStruct((T, D), F32), jax.ShapeDtypeStruct((ng, gd, gd), F32),
                            jax.ShapeDtypeStruct((1, D), F32), jax.ShapeDtypeStruct((1, D), F32)],
                 scratch_shapes=[pltpu.VMEM((tt + 2 * HALO, D), F32)],
                 compiler_params=_params("arbitrary"))(u, u, u, w, bias, scale, dy)


def _pool_bwd_b(dm, dy, alpha, name):
    T, D = dm.shape
    ng = len(POOL_WINDOWS)
    gd = D // ng
    tt = _tile(T, 512, 8)
    nt = T // tt

    def body(cur, prev, nxt, dy_ref, o_ref, ext):
        i = pl.program_id(0)
        _fill_ext(ext, cur, prev, nxt, i, nt, tt)
        for gi, win in enumerate(POOL_WINDOWS):
            sl = slice(gi * gd, (gi + 1) * gd)
            rows = tt + 2 * HALO
            ext[:, sl] = ext[:, sl] / _pool_counts(i, tt, T, win, rows, -HALO)
            acc = None
            for o in range(-(win // 2) + 1, win // 2 + 1):
                term = ext[pl.ds(HALO + o, tt), sl]
                acc = term if acc is None else acc + term
            o_ref[:, sl] = alpha * dy_ref[:, sl] + acc - cur[:, sl]

    row = pl.BlockSpec((tt, D), lambda i: (i, 0))
    return _call(body, name=name, grid=(nt,),
                 in_specs=_halo_specs(tt, D, lambda i: 0, nt, 0) + [row], out_specs=row,
                 out_shape=jax.ShapeDtypeStruct((T, D), F32),
                 scratch_shapes=[pltpu.VMEM((tt + 2 * HALO, D), F32)],
                 compiler_params=_params("parallel"))(dm, dm, dm, dy)


def _exchange(comm, name):
    ops, arrays, shapes, sems, in_place = _comm_plan(comm)
    n = len(ops)

    def body(*refs):
        copies = _comm_copies(ops, refs[:n], refs[n:2 * n], *refs[2 * n:])
        for cp in copies:
            cp.start()
        for cp in copies:
            cp.wait()

    any_spec = pl.BlockSpec(memory_space=pl.ANY)
    return _pallas(body, name=name, in_specs=[any_spec] * n, out_specs=[any_spec] * n, out_shape=shapes,
                   scratch_shapes=sems, input_output_aliases={a: a for a in in_place})(*arrays)


def _adamw(pieces, w, m, v, name):
    R, C = w.shape
    n_pieces = len(pieces)
    rp = R // n_pieces
    tr = _tile(rp, max(8, (1 << 18) // C // 8 * 8), 8)
    per = rp // tr

    def body(*refs):
        p_refs = refs[:n_pieces]
        w_ref, m_ref, v_ref, g_ref, d_ref, nm_ref, nv_ref = refs[n_pieces:]

        def step(p_ref):
            g = p_ref[0].astype(F32)
            for i in range(1, N_DEV):
                g = g + p_ref[i].astype(F32)
            mm = ADAM_B1 * m_ref[...] + (1.0 - ADAM_B1) * g
            vv = ADAM_B2 * v_ref[...] + (1.0 - ADAM_B2) * (g * g)
            m_hat = mm / (1.0 - ADAM_B1 ** ADAM_STEP)
            v_hat = vv / (1.0 - ADAM_B2 ** ADAM_STEP)
            g_ref[...] = g
            d_ref[...] = -ADAM_LR * (m_hat / (jnp.sqrt(v_hat) + ADAM_EPS) + ADAM_WD * w_ref[...])
            nm_ref[...] = mm
            nv_ref[...] = vv

        if n_pieces == 1:
            step(p_refs[0])
        else:
            for p in range(n_pieces):
                pl.when(pl.program_id(0) // per == p)(functools.partial(step, p_refs[p]))

    row = pl.BlockSpec((tr, C), lambda i: (i, 0))
    piece_specs = [pl.BlockSpec((N_DEV, tr, C), lambda i, p=p: (0, jnp.clip(i - p * per, 0, per - 1), 0))
                   for p in range(n_pieces)]
    return _call(body, name=name, grid=(R // tr,), in_specs=piece_specs + [row, row, row],
                 out_specs=[row] * 4, out_shape=[jax.ShapeDtypeStruct((R, C), F32)] * 4,
                 compiler_params=_params("parallel"))(*pieces, w, m, v)


def _pack(arrays):
    flat, meta, off = [], [], 0
    for a in arrays:
        flat.append(a.reshape(-1).astype(F32))
        meta.append((off, a.shape))
        off += a.size
    total = -(-off // (8 * LANES)) * (8 * LANES)
    flat.append(jnp.zeros((total - off,), F32))
    return jnp.concatenate(flat).reshape(total // LANES, LANES), meta


def _unpack(packed, meta):
    flat = packed.reshape(-1)
    return [flat[off:off + math.prod(shape)].reshape(shape) for off, shape in meta]


def kernel(x, ssd_in_proj, ssd_conv_w, ssd_conv_b, ssd_dt_bias, ssd_A_log, ssd_D, ssd_norm_w, ssd_out_proj, pool_w, pool_b, pool_scale, mlp_w1, mlp_w2, ln_mix_g, ln_mix_b, ln_ffn_g, ln_ffn_b, loss_target, m_ssd_in_proj, m_ssd_conv_w, m_ssd_conv_b, m_ssd_dt_bias, m_ssd_A_log, m_ssd_D, m_ssd_norm_w, m_ssd_out_proj, m_pool_w, m_pool_b, m_pool_scale, m_mlp_w1, m_mlp_w2, m_ln_mix_g, m_ln_mix_b, m_ln_ffn_g, m_ln_ffn_b, v_ssd_in_proj, v_ssd_conv_w, v_ssd_conv_b, v_ssd_dt_bias, v_ssd_A_log, v_ssd_D, v_ssd_norm_w, v_ssd_out_proj, v_pool_w, v_pool_b, v_pool_scale, v_mlp_w1, v_mlp_w2, v_ln_mix_g, v_ln_mix_b, v_ln_ffn_g, v_ln_ffn_b):
    T, D = x.shape[1], x.shape[2]
    depth = mlp_w1.shape[0]
    n_ssd, n_pool = ssd_in_proj.shape[0], pool_w.shape[0]
    d_inner = ssd_out_proj.shape[1] * N_DEV
    n_heads = d_inner // HEAD_DIM
    hg = n_heads // N_GROUPS
    d_bc = N_GROUPS * D_STATE
    d_xbc = d_inner + 2 * d_bc
    d_in_proj = ssd_in_proj.shape[2] * N_DEV
    d_ff = mlp_w1.shape[2] * N_DEV
    ng = len(POOL_WINDOWS)
    gd = D // ng
    alpha = (2.0 * depth) ** 0.25
    x0 = x.reshape(T, D)
    target = loss_target.reshape(T, D)

    assert depth == 4 and n_ssd == 2 and n_pool == 2, "the exchange schedules below are written for this stack"

    small_pack, small_meta = _pack([ssd_conv_w, pool_b, pool_scale])
    pw_rows = pool_w.shape[1] * pool_w.shape[2]
    in_b, out_b = ssd_in_proj.astype(BF16), ssd_out_proj.astype(BF16)
    pw_b = pool_w.reshape(n_pool, pw_rows, gd).astype(BF16)
    w1_b, w2_b = mlp_w1.astype(BF16), mlp_w2.astype(BF16)
    shard = {("in", 0): in_b[0], ("in", 1): in_b[1], ("out", 0): out_b[0], ("out", 1): out_b[1],
             ("pool", 0): pw_b[0], ("pool", 1): pw_b[1]}
    for i in range(depth):
        shard["w1", i], shard["w2", i] = w1_b[i], w2_b[i]

    def full_cols(g):
        return g.transpose(1, 0, 2).reshape(g.shape[1], -1)

    def full_rows(g):
        return g.reshape(-1, g.shape[-1])

    def full_pool(g):
        return g.reshape(N_DEV, ng, gd // N_DEV, gd).transpose(1, 0, 2, 3).reshape(ng, gd, gd)

    def slab_cols(g):
        return g.reshape(g.shape[0], N_DEV, -1).transpose(1, 0, 2)

    def slab_rows(g):
        return g.reshape(N_DEV, -1, g.shape[-1])

    def slab_pool(g):
        return g.astype(BF16).reshape(ng, N_DEV, gd // N_DEV, gd).transpose(1, 0, 2, 3).reshape(N_DEV, pw_rows, gd)

    to_full = {"in": full_cols, "out": full_rows, "pool": full_pool, "w1": full_cols, "w2": full_rows}

    gather1_on = {"in_proj_0": [("w1", 0), ("pool", 0)], "conv_fwd_0": [("out", 0)],
                  "ssd_fwd_0": [("w2", 0), ("w1", 1), ("w2", 1)], "out_proj_0": [("out", 1)], "mlp_up_0": [("in", 1)],
                  "mlp_down_0": [("w1", 2)], "mlp_up_1": [("w2", 2)], "mlp_down_1": [("w1", 3), ("pool", 1)],
                  "in_proj_2": [("w2", 3)]}
    gather2_on = {"conv_fwd_0": [("w1", 0), ("pool", 0)], "ssd_fwd_0": [("out", 0)],
                  "gnorm_fwd_0": [("w2", 0), ("w1", 1), ("w2", 1)], "mlp_up_0": [("out", 1)], "mlp_down_0": [("in", 1)],
                  "mlp_up_1": [("w1", 2)], "mlp_down_1": [("w2", 2)], "in_proj_2": [("w1", 3), ("pool", 1)],
                  "conv_fwd_2": [("w2", 3)]}
    hide_us = {"mlp_down_dx": 161, "mlp_down_dw": 163, "mlp_up_dx": 170, "mlp_up_dw": 164, "gnorm_bwd": 163,
               "ssd_bwd": 1104, "conv_bwd": 228, "in_proj_dw": 230, "in_proj_dx": 267}
    SLAB_BYTES_PER_US = 80e3 / (N_DEV - 1)
    OVERRUN_US, MIN_CARRIER_US = 45, 120
    n_pieces = {"in": 2, "out": 1, "pool": 1, "w1": 2, "w2": 2}
    W, half, G, R = {}, {}, {}, {}
    queue = []

    def produced(key, slabs):
        G[key] = slabs
        rows = slabs.shape[1] // n_pieces[key[0]]
        cost = rows * math.prod(slabs.shape[2:]) * slabs.dtype.itemsize / SLAB_BYTES_PER_US
        queue.extend((key, p * rows, rows, cost) for p in range(n_pieces[key[0]]))

    def take(budget):
        taken, used = [], 0.0
        while queue and ((not taken and budget >= MIN_CARRIER_US) or used + queue[0][3] <= budget + OVERRUN_US):
            taken.append(queue.pop(0))
            used += taken[-1][3]
        return taken

    def run(fn, *args, name, **kw):
        k1, k2 = gather1_on.get(name, []), gather2_on.get(name, [])
        pieces = take(hide_us.get(name.rsplit("_", 1)[0], 0)) if not (k1 or k2) else []
        comm = ([("gather1", shard[k]) for k in k1] + [("gather2", half[k]) for k in k2]
                + [("slabs", G[key], r0, rows) for key, r0, rows, _ in pieces])
        if not comm:
            return fn(*args, name=name, **kw)
        res, got = fn(*args, name=name, comm=comm, **kw)
        for k, g in zip(k1, got):
            half[k] = g
        for k, g in zip(k2, got[len(k1):]):
            W[k] = to_full[k[0]](g)
        for (key, r0, _, _), g in zip(pieces, got):
            R.setdefault(key, []).append((r0, g))
        return res

    half_in0, g_small = _exchange([("gather1", shard["in", 0]), ("gather", small_pack)], "gather_first")
    (g_in0,) = _exchange([("gather2", half_in0)], "gather_first_onward")
    W["in", 0] = full_cols(g_in0)
    smalls = [_unpack(g_small[k], small_meta) for k in range(N_DEV)]
    conv_w = jnp.concatenate([s[0] for s in smalls], axis=-1).reshape(n_ssd, CONV_WIDTH, d_xbc)
    pool_bias = jnp.concatenate([s[1] for s in smalls], axis=-1).reshape(n_pool, 1, D)
    pool_sc = jnp.concatenate([s[2] for s in smalls], axis=-1).reshape(n_pool, 1, D)

    sel = (jnp.arange(d_inner)[:, None] // HEAD_DIM == jnp.arange(n_heads)[None, :]).astype(F32)

    saved = []
    h, hb = x0, x0.astype(BF16)
    for i in range(depth):
        j = i // 2
        s = {}
        s["x0"], s["x0b"] = h, hb
        if i % 2 == 0:
            (zx,) = run(_matmul, hb, W["in", j], "nn", name=f"in_proj_{i}", outs=[F32], tn=1152)
            xbc = run(_conv_fwd, zx, conv_w[j], ssd_conv_b[j], d_inner, name=f"conv_fwd_{i}")
            raw = zx[:, d_inner + d_xbc:]
            dt, cs = _dt_fwd(raw, ssd_dt_bias[j], ssd_A_log[j], f"dt_fwd_{i}")
            dtc, csc, csr = _cols(dt, hg), _cols(cs, hg), _rows(cs, hg)
            y2, states = run(_ssd_fwd, xbc, dtc, csc, csr, d_inner, name=f"ssd_fwd_{i}")
            dvec = jnp.repeat(ssd_D[j], HEAD_DIM).reshape(1, d_inner)
            nw = ssd_norm_w[j].reshape(1, d_inner)
            gn = run(_gnorm_fwd, y2, xbc, zx, dvec, nw, d_inner, name=f"gnorm_fwd_{i}")
            (mix,) = run(_matmul, gn, W["out", j], "nn", name=f"out_proj_{i}", outs=[F32])
            s.update(zx=zx, xbc=xbc, raw=raw, dtc=dtc, csc=csc, csr=csr, y2=y2, states=states, dvec=dvec, nw=nw, gn=gn)
        else:
            mix = _pool_fwd(h, W["pool", j], pool_bias[j], pool_sc[j], f"pool_fwd_{i}")
        s["mix"] = mix
        x1, x1b = _ln_fwd(h, mix, ln_mix_g[i], ln_mix_b[i], alpha, f"ln_mix_fwd_{i}")
        u, hh = run(_matmul, x1b, W["w1", i], "nn", name=f"mlp_up_{i}", outs=[F32, BF16],
                    epilogue=lambda acc: (acc, jnp.square(jnp.maximum(acc, 0.0))))
        (m2,) = run(_matmul, hh, W["w2", i], "nn", name=f"mlp_down_{i}", outs=[F32])
        x2, x2b = _ln_fwd(x1, m2, ln_ffn_g[i], ln_ffn_b[i], alpha, f"ln_ffn_fwd_{i}")
        s.update(x1=x1, x1b=x1b, u=u, hh=hh, m2=m2)
        saved.append(s)
        h, hb = x2, x2b

    loss_row, dh = _loss_head(h, target)
    loss = lax.psum(loss_row[0, 0], ("x", "y", "c"))

    big = {"in": (ssd_in_proj, m_ssd_in_proj, v_ssd_in_proj), "out": (ssd_out_proj, m_ssd_out_proj, v_ssd_out_proj),
           "pool": (pool_w, m_pool_w, v_pool_w), "w1": (mlp_w1, m_mlp_w1, v_mlp_w1), "w2": (mlp_w2, m_mlp_w2, v_mlp_w2)}
    done = {}

    def update(pieces, w, m, v, name):
        shape = w.shape
        C = shape[-1]
        res = _adamw([p.reshape(N_DEV, -1, C) for p in pieces], w.reshape(-1, C), m.reshape(-1, C), v.reshape(-1, C), name)
        return [r.reshape(shape) for r in res]

    def settle():
        for key in list(R):
            if key not in done and len(R[key]) == n_pieces[key[0]]:
                w, m, v = big[key[0]]
                pieces = [g for _, g in sorted(R[key], key=lambda t: t[0])]
                done[key] = update(pieces, w[key[1]], m[key[1]], v[key[1]], f"adamw_{key[0]}_{key[1]}")

    gr = {k: [None] * depth for k in ("ln_mix_g", "ln_mix_b", "ln_ffn_g", "ln_ffn_b")}
    gs = {k: [None] * n_ssd for k in ("conv_w", "conv_b", "dt_bias", "A_log", "D", "norm_w")}
    gp = {k: [None] * n_pool for k in ("b", "scale")}
    for i in reversed(range(depth)):
        j = i // 2
        s = saved[i]
        ds2, ds2b, gr["ln_ffn_g"][i], gr["ln_ffn_b"][i] = _ln_bwd(s["x1"], s["m2"], ln_ffn_g[i], dh, alpha, f"ln_ffn_bwd_{i}")
        (du,) = run(_matmul, ds2b, W["w2", i], "nt", name=f"mlp_down_dx_{i}", outs=[BF16], extras=[s["u"]],
                    epilogue=lambda acc, u_: (acc * (2.0 * jnp.maximum(u_, 0.0)),))
        (g_w2,) = run(_matmul, s["hh"], ds2b, "tn", name=f"mlp_down_dw_{i}", outs=[BF16])
        produced(("w2", i), slab_rows(g_w2))
        (dx1,) = run(_matmul, du, W["w1", i], "nt", name=f"mlp_up_dx_{i}", outs=[F32], extras=[ds2],
                     epilogue=lambda acc, e: (acc + alpha * e,))
        (g_w1,) = run(_matmul, s["x1b"], du, "tn", name=f"mlp_up_dw_{i}", outs=[BF16])
        produced(("w1", i), slab_cols(g_w1))
        ds1, ds1b, gr["ln_mix_g"][i], gr["ln_mix_b"][i] = _ln_bwd(s["x0"], s["mix"], ln_mix_g[i], dx1, alpha, f"ln_mix_bwd_{i}")
        if i % 2 == 0:
            (dgn,) = run(_matmul, ds1b, W["out", j], "nt", name=f"out_proj_dx_{i}", outs=[F32])
            (g_out,) = _matmul(s["gn"], ds1b, "tn", name=f"out_proj_dw_{i}", outs=[BF16])
            produced(("out", j), slab_rows(g_out))
            dy, dxs_d, dz, gs["norm_w"][j], gs["D"][j] = run(
                _gnorm_bwd, s["y2"], s["xbc"], s["zx"], s["dvec"], s["nw"], dgn, sel, d_inner, name=f"gnorm_bwd_{i}")
            dxs, dB, dC, dcs, dtot, dxdtx = run(_ssd_bwd, s["xbc"], s["dtc"], s["csc"], s["csr"], s["states"], s["y2"], dy,
                                                d_inner, name=f"ssd_bwd_{i}")
            draw, gs["dt_bias"][j], gs["A_log"][j] = _dt_bwd(
                s["raw"], ssd_dt_bias[j], ssd_A_log[j], _uncols(dcs), _uncols(dtot), _uncols(dxdtx), f"dt_bwd_{i}")
            dpre = jnp.concatenate([
                _conv_dpre(s["zx"], conv_w[j], ssd_conv_b[j], d_inner, 0, dxs, dxs_d, f"conv_dpre_x_{i}"),
                _conv_dpre(s["zx"], conv_w[j], ssd_conv_b[j], d_inner, d_inner, dB, None, f"conv_dpre_b_{i}"),
                _conv_dpre(s["zx"], conv_w[j], ssd_conv_b[j], d_inner, d_inner + d_bc, dC, None, f"conv_dpre_c_{i}"),
            ], axis=1)
            din, gs["conv_w"][j], gs["conv_b"][j] = run(_conv_bwd, s["zx"], dpre, conv_w[j], d_inner, name=f"conv_bwd_{i}")
            dzx = jnp.concatenate([dz, din, draw], axis=1)
            (g_in,) = run(_matmul, s["x0b"], dzx, "tn", name=f"in_proj_dw_{i}", outs=[BF16], tn=1152)
            produced(("in", j), slab_cols(g_in))
            (dh,) = run(_matmul, dzx, W["in", j], "nt", name=f"in_proj_dx_{i}", outs=[F32], extras=[ds1], tk=1152,
                        epilogue=lambda acc, e: (acc + alpha * e,))
        else:
            dm, g_pw, gp["b"][j], gp["scale"][j] = _pool_bwd_a(
                s["x0"], W["pool", j], pool_bias[j], pool_sc[j], ds1, f"pool_bwd_a_{i}")
            produced(("pool", j), slab_pool(g_pw))
            dh = _pool_bwd_b(dm, ds1, alpha, f"pool_bwd_b_{i}")
        settle()
    grad_x = dh.reshape(x.shape)

    g_conv_w = jnp.stack(gs["conv_w"]).reshape(n_ssd, CONV_WIDTH, 1, N_DEV, d_xbc // N_DEV)
    g_pool_b = jnp.stack(gp["b"]).reshape(n_pool, ng, N_DEV, gd // N_DEV)
    g_pool_s = jnp.stack(gp["scale"]).reshape(n_pool, N_DEV, D // N_DEV)
    s_small = jnp.stack([_pack([g_conv_w[:, :, :, k], g_pool_b[:, :, k], g_pool_s[:, k]])[0] for k in range(N_DEV)])
    repl_grads = [jnp.stack(gs["conv_b"]).reshape(ssd_conv_b.shape), jnp.stack(gs["dt_bias"]).reshape(ssd_dt_bias.shape),
                  jnp.stack(gs["A_log"]).reshape(ssd_A_log.shape), jnp.stack(gs["D"]).reshape(ssd_D.shape),
                  jnp.stack(gs["norm_w"]).reshape(ssd_norm_w.shape),
                  jnp.stack(gr["ln_mix_g"]).reshape(ln_mix_g.shape), jnp.stack(gr["ln_mix_b"]).reshape(ln_mix_b.shape),
                  jnp.stack(gr["ln_ffn_g"]).reshape(ln_ffn_g.shape), jnp.stack(gr["ln_ffn_b"]).reshape(ln_ffn_b.shape)]
    repl_pack, repl_meta = _pack(repl_grads)
    s_repl = jnp.broadcast_to(repl_pack[None], (N_DEV,) + repl_pack.shape)
    left = list(queue)
    del queue[:]
    got = _exchange([("slabs", s_small, 0, s_small.shape[1]), ("slabs", s_repl, 0, s_repl.shape[1])]
                    + [("slabs", G[key], r0, rows) for key, r0, rows, _ in left], "exchange_last")
    r_small, r_repl = got[0], got[1]
    for (key, r0, _, _), g in zip(left, got[2:]):
        R.setdefault(key, []).append((r0, g))
    settle()

    upd = {}
    for nm, kind, count in (("ssd_in_proj", "in", n_ssd), ("ssd_out_proj", "out", n_ssd), ("pool_w", "pool", n_pool),
                            ("mlp_w1", "w1", depth), ("mlp_w2", "w2", depth)):
        upd[nm] = [jnp.stack([done[kind, l][q] for l in range(count)]) for q in range(4)]
    sm = update([r_small], small_pack, _pack([m_ssd_conv_w, m_pool_b, m_pool_scale])[0],
                _pack([v_ssd_conv_w, v_pool_b, v_pool_scale])[0], "adamw_small_sharded")
    for idx, nm in enumerate(["ssd_conv_w", "pool_b", "pool_scale"]):
        upd[nm] = [_unpack(r, small_meta)[idx] for r in sm]
    repl_names = ["ssd_conv_b", "ssd_dt_bias", "ssd_A_log", "ssd_D", "ssd_norm_w",
                  "ln_mix_g", "ln_mix_b", "ln_ffn_g", "ln_ffn_b"]
    repl_w = [ssd_conv_b, ssd_dt_bias, ssd_A_log, ssd_D, ssd_norm_w, ln_mix_g, ln_mix_b, ln_ffn_g, ln_ffn_b]
    repl_m = [m_ssd_conv_b, m_ssd_dt_bias, m_ssd_A_log, m_ssd_D, m_ssd_norm_w, m_ln_mix_g, m_ln_mix_b, m_ln_ffn_g, m_ln_ffn_b]
    repl_v = [v_ssd_conv_b, v_ssd_dt_bias, v_ssd_A_log, v_ssd_D, v_ssd_norm_w, v_ln_mix_g, v_ln_mix_b, v_ln_ffn_g, v_ln_ffn_b]
    rp = update([r_repl], _pack(repl_w)[0], _pack(repl_m)[0], _pack(repl_v)[0], "adamw_replicated")
    for idx, nm in enumerate(repl_names):
        upd[nm] = [_unpack(r, repl_meta)[idx] for r in rp]

    order = ["ssd_in_proj", "ssd_conv_w", "ssd_conv_b", "ssd_dt_bias", "ssd_A_log", "ssd_D", "ssd_norm_w",
             "ssd_out_proj", "pool_w", "pool_b", "pool_scale", "mlp_w1", "mlp_w2",
             "ln_mix_g", "ln_mix_b", "ln_ffn_g", "ln_ffn_b"]
    return (loss, grad_x, *[upd[n][0] for n in order], *[upd[n][1] for n in order],
            *[upd[n][2] for n in order], *[upd[n][3] for n in order])
```

```python
import functools
import math

import jax
import jax.numpy as jnp
from jax import lax
from jax.experimental import pallas as pl
from jax.experimental.pallas import tpu as pltpu

F32 = jnp.float32
BF16 = jnp.bfloat16

N_DEV = 8
HEAD_DIM = 64
N_GROUPS = 8
D_STATE = 128
CHUNK = 128
CONV_WIDTH = 5
POOL_WINDOWS = (2, 4, 8, 16)
HALO = 8
LN_EPS = 1e-5
RMS_EPS = 1e-5
ADAM_LR = 0.001
ADAM_B1 = 0.9
ADAM_B2 = 0.999
ADAM_EPS = 1e-08
ADAM_WD = 0.01
ADAM_STEP = 10
LANES = 128
VMEM_LIMIT_BYTES = 56 * 1024 * 1024
HIGHEST = lax.Precision.HIGHEST


def _pallas(body, **kw):
    return pl.pallas_call(body, **kw)


def _params(*sem):
    return pltpu.CompilerParams(dimension_semantics=sem, vmem_limit_bytes=VMEM_LIMIT_BYTES)


def _mesh_pos():
    return lax.axis_index("x"), lax.axis_index("y"), lax.axis_index("c")


def _peer(x, y, c, k):
    dx, dy, dc = (k >> 2) & 1, (k >> 1) & 1, k & 1
    px = (1 - x) if dx else x
    py = (1 - y) if dy else y
    pc = (1 - c) if dc else c
    return px, py, pc


def _comm_copies(ops, ins, outs, send_sems, recv_sems, local_sems):
    x, y, c = _mesh_pos()
    me = 4 * x + 2 * y + c
    copies = []
    for a, op in enumerate(ops):
        src, dst = ins[a], outs[a]

        def remote(k, s, d, to):
            return pltpu.make_async_remote_copy(src_ref=s, dst_ref=d, send_sem=send_sems.at[a, k - 1],
                                                recv_sem=recv_sems.at[a, k - 1], device_id=to,
                                                device_id_type=pl.DeviceIdType.MESH)

        if op[0] in ("gather", "gather1"):
            copies.append(pltpu.make_async_copy(src, dst.at[me], local_sems.at[a]))
            for k in (range(1, N_DEV) if op[0] == "gather" else (1, 2, 4, 6)):
                copies.append(remote(k, src, dst.at[me], _peer(x, y, c, k)))
        elif op[0] == "gather2":
            for k in (2, 4, 6):
                qx, qy, qc = _peer(x, y, c, k)
                slot = 4 * qx + 2 * qy + qc
                copies.append(remote(k, src.at[slot], dst.at[slot], _peer(x, y, c, 1)))
        else:
            rows = pl.ds(op[1], op[2])
            copies.append(pltpu.make_async_copy(src.at[me, rows], dst.at[me], local_sems.at[a]))
            for k in range(1, N_DEV):
                px, py, pc = _peer(x, y, c, k)
                copies.append(remote(k, src.at[4 * px + 2 * py + pc, rows], dst.at[me], (px, py, pc)))
    return copies


def _comm_plan(comm):
    ops = [(c[0],) + tuple(c[2:]) for c in comm]
    arrays = [c[1] for c in comm]
    shapes = []
    for op, a in zip(ops, arrays):
        if op[0] in ("gather", "gather1"):
            shapes.append(jax.ShapeDtypeStruct((N_DEV,) + a.shape, a.dtype))
        elif op[0] == "gather2":
            shapes.append(jax.ShapeDtypeStruct(a.shape, a.dtype))
        else:
            shapes.append(jax.ShapeDtypeStruct((N_DEV, op[2]) + a.shape[2:], a.dtype))
    n = len(ops)
    sems = [pltpu.SemaphoreType.DMA((n, N_DEV - 1)), pltpu.SemaphoreType.DMA((n, N_DEV - 1)),
            pltpu.SemaphoreType.DMA((n,))]
    in_place = [a for a, op in enumerate(ops) if op[0] == "gather2"]
    return ops, arrays, shapes, sems, in_place


def _call(body, *, comm=None, **kw):
    if not comm:
        return _pallas(body, **kw)
    ops, arrays, c_shape, c_sems, in_place = _comm_plan(comm)
    n = len(ops)
    grid = tuple(kw["grid"])
    single = not isinstance(kw["out_shape"], (list, tuple))
    out_shape = [kw["out_shape"]] if single else list(kw["out_shape"])
    out_specs = [kw["out_specs"]] if single else list(kw["out_specs"])
    in_specs = list(kw["in_specs"])
    scratch = list(kw.get("scratch_shapes", ()))
    n_in, n_out, n_scr = len(in_specs), len(out_shape), len(scratch)

    def wrapped(*refs):
        ins, refs = refs[:n_in], refs[n_in:]
        c_ins, refs = refs[:n], refs[n:]
        outs, refs = refs[:n_out], refs[n_out:]
        c_outs, refs = refs[:n], refs[n:]
        scr, sems = refs[:n_scr], refs[n_scr:]
        first = last = None
        for ax, size in enumerate(grid):
            i = pl.program_id(ax)
            first = (i == 0) if first is None else first & (i == 0)
            last = (i == size - 1) if last is None else last & (i == size - 1)

        @pl.when(first)
        def _():
            for cp in _comm_copies(ops, c_ins, c_outs, *sems):
                cp.start()

        body(*ins, *outs, *scr)

        @pl.when(last)
        def _():
            for cp in _comm_copies(ops, c_ins, c_outs, *sems):
                cp.wait()

    any_spec = pl.BlockSpec(memory_space=pl.ANY)
    call = _pallas(wrapped, name=kw["name"], grid=grid, in_specs=in_specs + [any_spec] * n,
                   out_specs=out_specs + [any_spec] * n, out_shape=out_shape + c_shape,
                   scratch_shapes=scratch + c_sems,
                   input_output_aliases={**kw.get("input_output_aliases", {}), **{n_in + a: n_out + a for a in in_place}},
                   compiler_params=_params(*(("arbitrary",) * len(grid))))

    def run(*args):
        res = call(*args, *arrays)
        own = res[:n_out]
        return (own[0] if single else list(own)), list(res[n_out:])

    return run


def _tile(dim, target, align=LANES):
    if dim <= target:
        return dim
    t = (target // align) * align
    while t >= align:
        if dim % t == 0:
            return t
        t -= align
    return dim


def _dot(a, b, dims, precision=None):
    return lax.dot_general(a, b, (dims, ((), ())), precision=precision, preferred_element_type=F32)


NN = ((1,), (0,))
NT = ((1,), (1,))
TN = ((0,), (0,))


def _sigmoid(x):
    return 1.0 / (1.0 + jnp.exp(-x))


def _matmul(a, b, mode, *, name, outs, epilogue=None, extras=(), tm=1024, tn=1024, tk=2048, comm=None):
    if mode == "nn":
        (M, K), (K2, N) = a.shape, b.shape
    elif mode == "nt":
        (M, K), (N, K2) = a.shape, b.shape
    else:
        (K, M), (K2, N) = a.shape, b.shape
    assert K == K2, (a.shape, b.shape, mode)
    tm, tn, tk = _tile(M, tm), _tile(N, tn), _tile(K, tk)
    nk = K // tk
    dims = {"nn": NN, "nt": NT, "tn": TN}[mode]
    a_spec = (pl.BlockSpec((tk, tm), lambda i, j, k: (k, i)) if mode == "tn"
              else pl.BlockSpec((tm, tk), lambda i, j, k: (i, k)))
    b_spec = (pl.BlockSpec((tn, tk), lambda i, j, k: (j, k)) if mode == "nt"
              else pl.BlockSpec((tk, tn), lambda i, j, k: (k, j)))
    mn_spec = pl.BlockSpec((tm, tn), lambda i, j, k: (i, j))
    n_extra, n_out = len(extras), len(outs)

    def finish(acc, extra_refs, out_refs):
        res = (acc,) if epilogue is None else epilogue(acc, *[r[...] for r in extra_refs])
        for o_ref, r in zip(out_refs, res):
            o_ref[...] = r.astype(o_ref.dtype)

    def body(*refs):
        a_ref, b_ref = refs[0], refs[1]
        extra_refs = refs[2:2 + n_extra]
        out_refs = refs[2 + n_extra:2 + n_extra + n_out]
        part = _dot(a_ref[...].astype(BF16), b_ref[...].astype(BF16), dims)
        if nk == 1:
            finish(part, extra_refs, out_refs)
            return
        acc_ref = refs[-1]
        k = pl.program_id(2)

        @pl.when(k == 0)
        def _():
            acc_ref[...] = part

        @pl.when((k > 0) & (k < nk - 1))
        def _():
            acc_ref[...] += part

        @pl.when(k == nk - 1)
        def _():
            finish(acc_ref[...] + part, extra_refs, out_refs)

    res = _call(
        body, comm=comm, name=name, grid=(M // tm, N // tn, nk),
        in_specs=[a_spec, b_spec] + [mn_spec] * n_extra,
        out_specs=[mn_spec] * n_out,
        out_shape=[jax.ShapeDtypeStruct((M, N), dt) for dt in outs],
        scratch_shapes=[pltpu.VMEM((tm, tn), F32)] if nk > 1 else [],
        compiler_params=_params("parallel", "parallel", "arbitrary"),
    )(a, b, *extras)
    return res


def _ln_fwd(x, f, g, b, alpha, name):
    T, D = x.shape
    tm = _tile(T, 256, 8)

    def body(x_ref, f_ref, g_ref, b_ref, y_ref, yb_ref):
        s = alpha * x_ref[...] + f_ref[...]
        mu = jnp.mean(s, axis=-1, keepdims=True)
        d = s - mu
        var = jnp.mean(d * d, axis=-1, keepdims=True)
        y = d * lax.rsqrt(var + LN_EPS) * g_ref[...] + b_ref[...]
        y_ref[...] = y
        yb_ref[...] = y.astype(BF16)

    row = pl.BlockSpec((tm, D), lambda i: (i, 0))
    vec = pl.BlockSpec((1, D), lambda i: (0, 0))
    return _call(body, name=name, grid=(T // tm,), in_specs=[row, row, vec, vec], out_specs=[row, row],
                 out_shape=[jax.ShapeDtypeStruct((T, D), F32), jax.ShapeDtypeStruct((T, D), BF16)],
                 compiler_params=_params("parallel"))(x, f, g.reshape(1, D), b.reshape(1, D))


def _ln_bwd(x, f, g, dy, alpha, name):
    T, D = x.shape
    tm = _tile(T, 256, 8)

    def body(x_ref, f_ref, g_ref, dy_ref, ds_ref, dsb_ref, dg_ref, db_ref):
        i = pl.program_id(0)

        @pl.when(i == 0)
        def _():
            dg_ref[...] = jnp.zeros_like(dg_ref)
            db_ref[...] = jnp.zeros_like(db_ref)

        s = alpha * x_ref[...] + f_ref[...]
        mu = jnp.mean(s, axis=-1, keepdims=True)
        d = s - mu
        var = jnp.mean(d * d, axis=-1, keepdims=True)
        rstd = lax.rsqrt(var + LN_EPS)
        xhat = d * rstd
        dy_ = dy_ref[...]
        dg_ref[...] += jnp.sum(dy_ * xhat, axis=0, keepdims=True)
        db_ref[...] += jnp.sum(dy_, axis=0, keepdims=True)
        dxh = dy_ * g_ref[...]
        m1 = jnp.mean(dxh, axis=-1, keepdims=True)
        m2 = jnp.mean(dxh * xhat, axis=-1, keepdims=True)
        ds = rstd * (dxh - m1 - xhat * m2)
        ds_ref[...] = ds
        dsb_ref[...] = ds.astype(BF16)

    row = pl.BlockSpec((tm, D), lambda i: (i, 0))
    vec = pl.BlockSpec((1, D), lambda i: (0, 0))
    return _call(body, name=name, grid=(T // tm,), in_specs=[row, row, vec, row], out_specs=[row, row, vec, vec],
                 out_shape=[jax.ShapeDtypeStruct((T, D), F32), jax.ShapeDtypeStruct((T, D), BF16),
                            jax.ShapeDtypeStruct((1, D), F32), jax.ShapeDtypeStruct((1, D), F32)],
                 compiler_params=_params("arbitrary"))(x, f, g.reshape(1, D), dy)


def _loss_head(y, target):
    T, D = y.shape
    tm = _tile(T, 256, 8)

    def body(y_ref, t_ref, loss_ref, dy_ref):
        i = pl.program_id(0)

        @pl.when(i == 0)
        def _():
            loss_ref[...] = jnp.zeros_like(loss_ref)

        err = y_ref[...] - t_ref[...]
        dy_ref[...] = err * (1.0 / D)
        per_tok = jnp.mean(err * err, axis=-1, keepdims=True)
        loss_ref[...] += 0.5 * jnp.sum(per_tok)

    row = pl.BlockSpec((tm, D), lambda i: (i, 0))
    return _call(body, name="loss_head", grid=(T // tm,), in_specs=[row, row],
                 out_specs=[pl.BlockSpec((1, LANES), lambda i: (0, 0)), row],
                 out_shape=[jax.ShapeDtypeStruct((1, LANES), F32), jax.ShapeDtypeStruct((T, D), F32)],
                 compiler_params=_params("arbitrary"))(y, target)


def _halo_specs(tt, cw, col_of, n_tiles, grid_rank_tokens_axis):
    per = tt // HALO
    ax = grid_rank_tokens_axis

    def cur(*g):
        return (g[ax], col_of(*g))

    def prev(*g):
        return (jnp.maximum(g[ax] * per - 1, 0), col_of(*g))

    def nxt(*g):
        return (jnp.minimum((g[ax] + 1) * per, n_tiles * per - 1), col_of(*g))

    return [pl.BlockSpec((tt, cw), cur), pl.BlockSpec((HALO, cw), prev), pl.BlockSpec((HALO, cw), nxt)]


def _fill_ext(ext_ref, cur_ref, prev_ref, next_ref, i, n_tiles, tt):
    ext_ref[pl.ds(0, HALO), :] = jnp.where(i > 0, prev_ref[...], 0.0)
    ext_ref[pl.ds(HALO, tt), :] = cur_ref[...]
    ext_ref[pl.ds(HALO + tt, HALO), :] = jnp.where(i < n_tiles - 1, next_ref[...], 0.0)


def _conv_pre(ext_ref, w, bias, tt, lo=0, n=None):
    n = tt if n is None else n
    pad = CONV_WIDTH // 2
    acc = None
    for k in range(CONV_WIDTH):
        term = ext_ref[pl.ds(HALO + lo + k - pad, n), :] * w[k:k + 1, :]
        acc = term if acc is None else acc + term
    return acc + bias


def _conv_fwd(zx, conv_w, conv_b, d_inner, name, comm=None):
    T = zx.shape[0]
    d_xbc = conv_w.shape[1]
    cw = _tile(d_xbc, 512)
    assert d_inner % cw == 0
    off = d_inner // cw
    tt = _tile(T, 512, 8)
    nt = T // tt

    def body(cur_ref, prev_ref, next_ref, w_ref, b_ref, o_ref, ext_ref):
        i = pl.program_id(1)
        _fill_ext(ext_ref, cur_ref, prev_ref, next_ref, i, nt, tt)
        pre = _conv_pre(ext_ref, w_ref[...], b_ref[...], tt)
        o_ref[...] = pre * _sigmoid(pre)

    specs = _halo_specs(tt, cw, lambda j, i: off + j, nt, 1)
    return _call(body, comm=comm, name=name, grid=(d_xbc // cw, nt),
                 in_specs=specs + [pl.BlockSpec((CONV_WIDTH, cw), lambda j, i: (0, j)),
                                   pl.BlockSpec((1, cw), lambda j, i: (0, j))],
                 out_specs=pl.BlockSpec((tt, cw), lambda j, i: (i, j)),
                 out_shape=jax.ShapeDtypeStruct((T, d_xbc), F32),
                 scratch_shapes=[pltpu.VMEM((tt + 2 * HALO, cw), F32)],
                 compiler_params=_params("parallel", "parallel"))(zx, zx, zx, conv_w, conv_b.reshape(1, d_xbc))


def _conv_dpre(zx, conv_w, conv_b, d_inner, col_lo, dirs, extra, name, into=None):
    T = zx.shape[0]
    ncols = dirs.shape[2]
    cw = _tile(ncols, 512)
    assert d_inner % cw == 0 and col_lo % cw == 0
    off_zx = (d_inner + col_lo) // cw
    off_w = col_lo // cw
    tt = _tile(T, 512, 8)
    nt = T // tt
    has_extra = extra is not None

    def body(cur_ref, prev_ref, next_ref, w_ref, b_ref, dirs_ref, *rest):
        o_ref, ext_ref = rest[-2], rest[-1]
        i = pl.program_id(1)
        _fill_ext(ext_ref, cur_ref, prev_ref, next_ref, i, nt, tt)
        pre = _conv_pre(ext_ref, w_ref[...], b_ref[...], tt)
        sig = _sigmoid(pre)
        dact = dirs_ref[0] + dirs_ref[1]
        if has_extra:
            dact = dact + rest[0][...]
        o_ref[...] = dact * (sig * (1.0 + pre * (1.0 - sig)))

    specs = _halo_specs(tt, cw, lambda j, i: off_zx + j, nt, 1)
    in_specs = specs + [pl.BlockSpec((CONV_WIDTH, cw), lambda j, i: (0, off_w + j)),
                        pl.BlockSpec((1, cw), lambda j, i: (0, off_w + j)),
                        pl.BlockSpec((2, tt, cw), lambda j, i: (0, i, j))]
    args = [zx, zx, zx, conv_w, conv_b.reshape(1, -1), dirs]
    if has_extra:
        in_specs.append(pl.BlockSpec((tt, cw), lambda j, i: (i, j)))
        args.append(extra)
    aliases = {}
    if into is not None:
        in_specs.append(pl.BlockSpec(memory_space=pl.ANY))
        args.append(into)
        aliases = {len(args) - 1: 0}
    return _call(body, name=name, grid=(ncols // cw, nt), in_specs=in_specs,
                 out_specs=pl.BlockSpec((tt, cw), lambda j, i: (i, off_w + j)),
                 out_shape=jax.ShapeDtypeStruct((T, conv_w.shape[1]), F32),
                 scratch_shapes=[pltpu.VMEM((tt + 2 * HALO, cw), F32)], input_output_aliases=aliases,
                 compiler_params=_params("parallel", "parallel"))(*args)


def _conv_bwd(zx, dpre, conv_w, d_inner, into, name, comm=None):
    T = zx.shape[0]
    d_xbc = conv_w.shape[1]
    cw = _tile(d_xbc, 512)
    off = d_inner // cw
    tt = _tile(T, 512, 8)
    nt = T // tt
    pad = CONV_WIDTH // 2

    def body(zc, zp, zn, dc, dp, dn, w_ref, into_ref, din_ref, dw_ref, db_ref, zext, dext):
        i = pl.program_id(1)

        @pl.when(i == 0)
        def _():
            dw_ref[...] = jnp.zeros_like(dw_ref)
            db_ref[...] = jnp.zeros_like(db_ref)

        _fill_ext(zext, zc, zp, zn, i, nt, tt)
        _fill_ext(dext, dc, dp, dn, i, nt, tt)
        w = w_ref[...]
        d = dc[...]
        acc = None
        for k in range(CONV_WIDTH):
            term = dext[pl.ds(HALO + pad - k, tt), :] * w[k:k + 1, :]
            acc = term if acc is None else acc + term
            dw_ref[k:k + 1, :] += jnp.sum(d * zext[pl.ds(HALO + k - pad, tt), :], axis=0, keepdims=True)
        din_ref[...] = acc.astype(din_ref.dtype)
        db_ref[...] += jnp.sum(d, axis=0, keepdims=True)

    zspecs = _halo_specs(tt, cw, lambda j, i: off + j, nt, 1)
    dspecs = _halo_specs(tt, cw, lambda j, i: j, nt, 1)
    return _call(body, comm=comm, name=name, grid=(d_xbc // cw, nt),
                 in_specs=zspecs + dspecs + [pl.BlockSpec((CONV_WIDTH, cw), lambda j, i: (0, j)),
                                             pl.BlockSpec(memory_space=pl.ANY)],
                 out_specs=[pl.BlockSpec((tt, cw), lambda j, i: (i, off + j)),
                            pl.BlockSpec((CONV_WIDTH, cw), lambda j, i: (0, j)),
                            pl.BlockSpec((1, cw), lambda j, i: (0, j))],
                 out_shape=[jax.ShapeDtypeStruct(into.shape, into.dtype), jax.ShapeDtypeStruct((CONV_WIDTH, d_xbc), F32),
                            jax.ShapeDtypeStruct((1, d_xbc), F32)],
                 scratch_shapes=[pltpu.VMEM((tt + 2 * HALO, cw), F32), pltpu.VMEM((tt + 2 * HALO, cw), F32)],
                 input_output_aliases={7: 0},
                 compiler_params=_params("parallel", "arbitrary"))(zx, zx, zx, dpre, dpre, dpre, conv_w, into)


def _tri(n):
    r = lax.broadcasted_iota(jnp.int32, (n, n), 0)
    c = lax.broadcasted_iota(jnp.int32, (n, n), 1)
    return (r >= c).astype(F32), (r <= c).astype(F32)


def _dt_fwd(raw, bias, a_log, name):
    T, H2 = raw.shape
    half = H2 // 2

    def body(raw_ref, bias_ref, alog_ref, dt_ref, cs_ref):
        x = raw_ref[...] + bias_ref[...]
        dt = jnp.maximum(x, 0.0) + jnp.log(1.0 + jnp.exp(-jnp.abs(x)))
        a = dt * (-jnp.exp(alog_ref[...]))
        lower, upper = _tri(CHUNK)
        cs_f = _dot(lower, a, NN, HIGHEST)
        cs_b = _dot(upper, a, NN, HIGHEST)
        lane = lax.broadcasted_iota(jnp.int32, (CHUNK, H2), 1)
        dt_ref[...] = dt
        cs_ref[...] = jnp.where(lane < half, cs_f, cs_b)

    row = pl.BlockSpec((CHUNK, H2), lambda c: (c, 0))
    vec = pl.BlockSpec((1, H2), lambda c: (0, 0))
    return _call(body, name=name, grid=(T // CHUNK,), in_specs=[row, vec, vec], out_specs=[row, row],
                 out_shape=[jax.ShapeDtypeStruct((T, H2), F32)] * 2,
                 compiler_params=_params("parallel"))(raw, bias.reshape(1, H2), a_log.reshape(1, H2))


def _dt_bwd(raw, bias, a_log, dcs, dtot, dxdtx, into, name):
    T, H2 = raw.shape
    half = H2 // 2
    assert into.shape[1] % H2 == 0
    last = into.shape[1] // H2 - 1

    def body(raw_ref, bias_ref, alog_ref, dcs_ref, dtot_ref, dx_ref, into_ref, draw_ref, dbias_ref, dalog_ref):
        c = pl.program_id(0)

        @pl.when(c == 0)
        def _():
            dbias_ref[...] = jnp.zeros_like(dbias_ref)
            dalog_ref[...] = jnp.zeros_like(dalog_ref)

        x = raw_ref[...] + bias_ref[...]
        dt = jnp.maximum(x, 0.0) + jnp.log(1.0 + jnp.exp(-jnp.abs(x)))
        A = -jnp.exp(alog_ref[...])
        lower, upper = _tri(CHUNK)
        g = dcs_ref[...]
        lane = lax.broadcasted_iota(jnp.int32, (CHUNK, H2), 1)
        da = jnp.where(lane < half, _dot(upper, g, NN, HIGHEST), _dot(lower, g, NN, HIGHEST)) + dtot_ref[...]
        ddt = da * A + dx_ref[...]
        draw = ddt * _sigmoid(x)
        draw_ref[...] = draw.astype(draw_ref.dtype)
        dbias_ref[...] += jnp.sum(draw, axis=0, keepdims=True)
        dalog_ref[...] += jnp.sum(da * dt, axis=0, keepdims=True) * A

    row = pl.BlockSpec((CHUNK, H2), lambda c: (c, 0))
    vec = pl.BlockSpec((1, H2), lambda c: (0, 0))
    return _call(body, name=name, grid=(T // CHUNK,),
                 in_specs=[row, vec, vec, row, row, row, pl.BlockSpec(memory_space=pl.ANY)],
                 out_specs=[pl.BlockSpec((CHUNK, H2), lambda c: (c, last)), vec, vec],
                 out_shape=[jax.ShapeDtypeStruct(into.shape, into.dtype), jax.ShapeDtypeStruct((1, H2), F32),
                            jax.ShapeDtypeStruct((1, H2), F32)],
                 input_output_aliases={6: 0},
                 compiler_params=_params("arbitrary"))(raw, bias.reshape(1, H2), a_log.reshape(1, H2), dcs, dtot, dxdtx,
                                                       into)


def _cols(a, hg):
    T = a.shape[0]
    return a.reshape(T, 2, N_GROUPS, hg).transpose(1, 2, 0, 3)


def _rows(a, hg):
    T = a.shape[0]
    return a.reshape(T, 2, N_GROUPS, hg).transpose(1, 2, 3, 0)


def _uncols(a):
    T = a.shape[2]
    return a.transpose(2, 0, 1, 3).reshape(T, -1)


def _ssd_masks(d):
    r = lax.broadcasted_iota(jnp.int32, (CHUNK, CHUNK), 0)
    c = lax.broadcasted_iota(jnp.int32, (CHUNK, CHUNK), 1)
    return ((r >= c) & (d == 0)) | ((r <= c) & (d == 1))


def _head_expand(hg):
    r = lax.broadcasted_iota(jnp.int32, (hg, hg * HEAD_DIM), 0)
    c = lax.broadcasted_iota(jnp.int32, (hg, hg * HEAD_DIM), 1)
    return (c // HEAD_DIM == r).astype(F32)


def _head_select(hg):
    r = lax.broadcasted_iota(jnp.int32, (hg * HEAD_DIM, hg), 0)
    c = lax.broadcasted_iota(jnp.int32, (hg * HEAD_DIM, hg), 1)
    return (r // HEAD_DIM == c).astype(F32)


def _ssd_common(d, csc_ref, dtc_ref, hg):
    expand = _head_expand(hg)
    csx = _dot(csc_ref[...], expand, NN, HIGHEST)
    dtx = _dot(dtc_ref[...], expand, NN, HIGHEST)
    totx = jnp.where(d == 0, csx[CHUNK - 1:CHUNK, :], csx[0:1, :])
    return csx, dtx, totx


def _ssd_fwd(xbc, dtc, csc, csr, d_inner, name, comm=None):
    T = xbc.shape[0]
    nc = T // CHUNK
    gw = d_inner // N_GROUPS
    hg = gw // HEAD_DIM
    P, N = HEAD_DIM, D_STATE
    b_off = d_inner // N
    c_off = b_off + N_GROUPS

    def cidx(d, c):
        return c + d * (nc - 1 - 2 * c)

    def body(xs_ref, b_ref, c_ref, dtc_ref, csc_ref, csr_ref, y_ref, st_ref, h_ref):
        d = pl.program_id(0)
        c = pl.program_id(2)

        @pl.when(c == 0)
        def _():
            h_ref[...] = jnp.zeros_like(h_ref)

        Bb = b_ref[...].astype(BF16)
        Cb = c_ref[...].astype(BF16)
        S = _dot(Cb, Bb, NT)
        mask = _ssd_masks(d)
        csx, dtx, totx = _ssd_common(d, csc_ref, dtc_ref, hg)
        H = h_ref[...]
        st_ref[...] = H
        xdt = xs_ref[...] * dtx
        xdtb = xdt.astype(BF16)
        y_off = jnp.exp(csx) * _dot(Cb, H.astype(BF16), NN)
        for j in range(hg):
            sl = slice(j * P, (j + 1) * P)
            decay = jnp.exp(jnp.where(mask, csc_ref[:, j:j + 1] - csr_ref[j:j + 1, :], -jnp.inf))
            y_ref[:, sl] = _dot((S * decay).astype(BF16), xdtb[:, sl], NN) + y_off[:, sl]
        h_ref[...] = jnp.exp(totx) * H + _dot(Bb, (jnp.exp(totx - csx) * xdt).astype(BF16), TN)

    col = lambda d, g, c: (d, g, cidx(d, c), 0)
    return _call(
        body, comm=comm, name=name, grid=(2, N_GROUPS, nc),
        in_specs=[pl.BlockSpec((CHUNK, gw), lambda d, g, c: (cidx(d, c), g)),
                  pl.BlockSpec((CHUNK, N), lambda d, g, c: (cidx(d, c), b_off + g)),
                  pl.BlockSpec((CHUNK, N), lambda d, g, c: (cidx(d, c), c_off + g)),
                  pl.BlockSpec((None, None, CHUNK, hg), col),
                  pl.BlockSpec((None, None, CHUNK, hg), col),
                  pl.BlockSpec((None, None, hg, CHUNK), lambda d, g, c: (d, g, 0, cidx(d, c)))],
        out_specs=[pl.BlockSpec((None, CHUNK, gw), lambda d, g, c: (d, cidx(d, c), g)),
                   pl.BlockSpec((None, None, None, N, gw), lambda d, g, c: (d, cidx(d, c), g, 0, 0))],
        out_shape=[jax.ShapeDtypeStruct((2, T, d_inner), F32),
                   jax.ShapeDtypeStruct((2, nc, N_GROUPS, N, gw), F32)],
        scratch_shapes=[pltpu.VMEM((N, gw), F32)],
        compiler_params=_params("parallel", "parallel", "arbitrary"),
    )(xbc, xbc, xbc, dtc, csc, csr)


def _ssd_bwd(xbc, dtc, csc, csr, states, y2, dy, d_inner, name, comm=None):
    T = xbc.shape[0]
    nc = T // CHUNK
    gw = d_inner // N_GROUPS
    hg = gw // HEAD_DIM
    P, N = HEAD_DIM, D_STATE
    b_off = d_inner // N
    c_off = b_off + N_GROUPS

    def cidx(d, c):
        return (nc - 1 - c) + d * (2 * c - nc + 1)

    def body(xs_ref, b_ref, c_ref, dtc_ref, csc_ref, csr_ref, st_ref, y_ref, dy_ref,
             dxs_ref, db_ref, dc_ref, dcs_ref, dtot_ref, dxdtx_ref, dh_ref, dxdt_ref):
        d = pl.program_id(0)
        c = pl.program_id(2)

        @pl.when(c == 0)
        def _():
            dh_ref[...] = jnp.zeros_like(dh_ref)

        Bb = b_ref[...].astype(BF16)
        Cb = c_ref[...].astype(BF16)
        S = _dot(Cb, Bb, NT)
        mask = _ssd_masks(d)
        csx, dtx, totx = _ssd_common(d, csc_ref, dtc_ref, hg)
        select = _head_select(hg)
        X = xs_ref[...]
        xdt = X * dtx
        xdtb = xdt.astype(BF16)
        dY = dy_ref[...]
        dYb = dY.astype(BF16)
        Hp = st_ref[...]
        Hpb = Hp.astype(BF16)
        dH = dh_ref[...]
        dHb = dH.astype(BF16)
        e_tot = jnp.exp(totx)
        dCH = (jnp.exp(csx) * dY).astype(BF16)
        dC = _dot(dCH, Hpb, NT)
        dHp = _dot(Cb, dCH, TN)
        Q = _dot(Bb, dHb, NN)
        dte = jnp.exp(totx - csx)
        wx = dte * xdt
        dB = _dot(wx.astype(BF16), dHb, NT)
        ddte = Q * wx
        dS = jnp.zeros((CHUNK, CHUNK), F32)
        for j in range(hg):
            sl = slice(j * P, (j + 1) * P)
            decay = jnp.exp(jnp.where(mask, csc_ref[:, j:j + 1] - csr_ref[j:j + 1, :], -jnp.inf))
            dS = dS + _dot(dYb[:, sl], xdtb[:, sl], NT) * decay
            dxdt_ref[:, sl] = _dot((S * decay).astype(BF16), dYb[:, sl], TN)
        dxdt_diag = dxdt_ref[...]
        dxdt = dxdt_diag + dte * Q
        dcs_ref[...] = _dot(dYb.astype(F32) * y_ref[...] - xdtb.astype(F32) * dxdt_diag - ddte, select, NN, HIGHEST)
        dtot_row = (jnp.sum(ddte, axis=0, keepdims=True) + e_tot * jnp.sum(dH * Hp, axis=0, keepdims=True))
        dtot_ref[...] = jnp.zeros((CHUNK, hg), F32) + _dot(dtot_row, select, NN, HIGHEST)
        dxdtx_ref[...] = _dot(dxdt * X, select, NN, HIGHEST)
        dxs_ref[...] = dxdt * dtx
        dh_ref[...] = e_tot * dH + dHp
        dSb = dS.astype(BF16)
        dc_ref[...] = dC + _dot(dSb, Bb, NN)
        db_ref[...] = dB + _dot(dSb, Cb, TN)

    col = lambda d, g, c: (d, g, cidx(d, c), 0)
    colspec = pl.BlockSpec((None, None, CHUNK, hg), col)
    rowblk = pl.BlockSpec((None, CHUNK, gw), lambda d, g, c: (d, cidx(d, c), g))
    return _call(
        body, comm=comm, name=name, grid=(2, N_GROUPS, nc),
        in_specs=[pl.BlockSpec((CHUNK, gw), lambda d, g, c: (cidx(d, c), g)),
                  pl.BlockSpec((CHUNK, N), lambda d, g, c: (cidx(d, c), b_off + g)),
                  pl.BlockSpec((CHUNK, N), lambda d, g, c: (cidx(d, c), c_off + g)),
                  colspec, colspec,
                  pl.BlockSpec((None, None, hg, CHUNK), lambda d, g, c: (d, g, 0, cidx(d, c))),
                  pl.BlockSpec((None, None, None, N, gw), lambda d, g, c: (d, cidx(d, c), g, 0, 0)),
                  rowblk,
                  pl.BlockSpec((CHUNK, gw), lambda d, g, c: (cidx(d, c), g))],
        out_specs=[rowblk,
                   pl.BlockSpec((None, CHUNK, N), lambda d, g, c: (d, cidx(d, c), g)),
                   pl.BlockSpec((None, CHUNK, N), lambda d, g, c: (d, cidx(d, c), g)),
                   colspec, colspec, colspec],
        out_shape=[jax.ShapeDtypeStruct((2, T, d_inner), F32),
                   jax.ShapeDtypeStruct((2, T, N_GROUPS * N), F32),
                   jax.ShapeDtypeStruct((2, T, N_GROUPS * N), F32)]
        + [jax.ShapeDtypeStruct((2, N_GROUPS, T, hg), F32)] * 3,
        scratch_shapes=[pltpu.VMEM((N, gw), F32), pltpu.VMEM((CHUNK, gw), F32)],
        compiler_params=_params("parallel", "parallel", "arbitrary"),
    )(xbc, xbc, xbc, dtc, csc, csr, states, y2, dy)


def _gnorm_fwd(y2, xbc, zx, dvec, nw, d_inner, name, comm=None):
    T = xbc.shape[0]
    gw = d_inner // N_GROUPS
    tm = _tile(T, 128, 8)

    def body(y_ref, xs_ref, z_ref, d_ref, w_ref, o_ref):
        for g in range(N_GROUPS):
            sl = slice(g * gw, (g + 1) * gw)
            y = y_ref[0, :, sl] + y_ref[1, :, sl] + xs_ref[:, sl] * d_ref[:, sl]
            z = z_ref[:, sl]
            gy = y * (z * _sigmoid(z))
            rs = lax.rsqrt(jnp.mean(gy * gy, axis=-1, keepdims=True) + RMS_EPS)
            o_ref[:, sl] = (gy * rs * w_ref[:, sl]).astype(o_ref.dtype)

    row = pl.BlockSpec((tm, d_inner), lambda i: (i, 0))
    vec = pl.BlockSpec((1, d_inner), lambda i: (0, 0))
    return _call(body, comm=comm, name=name, grid=(T // tm,),
                 in_specs=[pl.BlockSpec((2, tm, d_inner), lambda i: (0, i, 0)), row, row, vec, vec],
                 out_specs=row, out_shape=jax.ShapeDtypeStruct((T, d_inner), BF16),
                 compiler_params=_params("parallel"))(y2, xbc, zx, dvec, nw)


def _gnorm_bwd(y2, xbc, zx, dvec, nw, dgn, sel, d_inner, name, comm=None):
    T = xbc.shape[0]
    gw = d_inner // N_GROUPS
    n_heads = d_inner // HEAD_DIM
    tm = _tile(T, 128, 8)
    n_tiles = T // tm

    def body(y_ref, xs_ref, z_ref, d_ref, w_ref, dg_ref, sel_ref, dy_ref, dxs_ref, dz_ref, dw_ref, dd_ref, dch_ref):
        i = pl.program_id(0)

        @pl.when(i == 0)
        def _():
            dw_ref[...] = jnp.zeros_like(dw_ref)
            dch_ref[...] = jnp.zeros_like(dch_ref)

        for g in range(N_GROUPS):
            sl = slice(g * gw, (g + 1) * gw)
            xs = xs_ref[:, sl]
            y = y_ref[0, :, sl] + y_ref[1, :, sl] + xs * d_ref[:, sl]
            z = z_ref[:, sl]
            sig = _sigmoid(z)
            sz = z * sig
            gy = y * sz
            rs = lax.rsqrt(jnp.mean(gy * gy, axis=-1, keepdims=True) + RMS_EPS)
            n = gy * rs
            dout = dg_ref[:, sl]
            dw_ref[:, sl] += jnp.sum(dout * n, axis=0, keepdims=True)
            dn = dout * w_ref[:, sl]
            dgy = rs * (dn - n * jnp.mean(dn * n, axis=-1, keepdims=True))
            dy = dgy * sz
            dy_ref[:, sl] = dy
            dz_ref[:, sl] = (dgy * y * (sig * (1.0 + z * (1.0 - sig)))).astype(dz_ref.dtype)
            dxs_ref[:, sl] = dy * d_ref[:, sl]
            dch_ref[:, sl] += jnp.sum(dy * xs, axis=0, keepdims=True)

        @pl.when(i == n_tiles - 1)
        def _():
            dd_ref[...] = _dot(dch_ref[...], sel_ref[...], NN, HIGHEST)

    row = pl.BlockSpec((tm, d_inner), lambda i: (i, 0))
    vec = pl.BlockSpec((1, d_inner), lambda i: (0, 0))
    hvec = pl.BlockSpec((1, n_heads), lambda i: (0, 0))
    return _call(body, comm=comm, name=name, grid=(n_tiles,),
                 in_specs=[pl.BlockSpec((2, tm, d_inner), lambda i: (0, i, 0)), row, row, vec, vec, row,
                           pl.BlockSpec((d_inner, n_heads), lambda i: (0, 0))],
                 out_specs=[row, row, row, vec, hvec],
                 out_shape=[jax.ShapeDtypeStruct((T, d_inner), F32), jax.ShapeDtypeStruct((T, d_inner), F32),
                            jax.ShapeDtypeStruct(zx.shape, BF16),
                            jax.ShapeDtypeStruct((1, d_inner), F32), jax.ShapeDtypeStruct((1, n_heads), F32)],
                 scratch_shapes=[pltpu.VMEM((1, d_inner), F32)],
                 compiler_params=_params("arbitrary"))(y2, xbc, zx, dvec, nw, dgn, sel)


def _pool_counts(i, tt, T, win, rows, row0):
    t = i * tt + row0 + lax.broadcasted_iota(jnp.int32, (rows, 1), 0)
    start = t - win // 2
    lo = jnp.clip(start, 0, T)
    hi = jnp.clip(start + win, 0, T)
    return jnp.maximum(hi - lo, 1).astype(F32)


def _pool_features(ext_ref, i, tt, T, gi, gd):
    win = POOL_WINDOWS[gi]
    sl = slice(gi * gd, (gi + 1) * gd)
    acc = None
    for o in range(-(win // 2), win - win // 2):
        term = ext_ref[pl.ds(HALO + o, tt), sl]
        acc = term if acc is None else acc + term
    return acc / _pool_counts(i, tt, T, win, tt, 0) - ext_ref[pl.ds(HALO, tt), sl]


def _pool_fwd(u, w, bias, scale, name):
    T, D = u.shape
    ng = len(POOL_WINDOWS)
    gd = D // ng
    tt = _tile(T, 512, 8)
    nt = T // tt

    def body(cur, prev, nxt, w_ref, b_ref, s_ref, o_ref, ext):
        i = pl.program_id(0)
        _fill_ext(ext, cur, prev, nxt, i, nt, tt)
        for gi in range(ng):
            sl = slice(gi * gd, (gi + 1) * gd)
            m = _pool_features(ext, i, tt, T, gi, gd)
            pre = _dot(m.astype(BF16), w_ref[gi], NN) + b_ref[:, sl]
            o_ref[:, sl] = pre * s_ref[:, sl]

    vec = pl.BlockSpec((1, D), lambda i: (0, 0))
    return _call(body, name=name, grid=(nt,),
                 in_specs=_halo_specs(tt, D, lambda i: 0, nt, 0)
                 + [pl.BlockSpec((ng, gd, gd), lambda i: (0, 0, 0)), vec, vec],
                 out_specs=pl.BlockSpec((tt, D), lambda i: (i, 0)),
                 out_shape=jax.ShapeDtypeStruct((T, D), F32),
                 scratch_shapes=[pltpu.VMEM((tt + 2 * HALO, D), F32)],
                 compiler_params=_params("parallel"))(u, u, u, w, bias, scale)


def _pool_bwd_a(u, w, bias, scale, dy, name):
    T, D = u.shape
    ng = len(POOL_WINDOWS)
    gd = D // ng
    tt = _tile(T, 512, 8)
    nt = T // tt

    def body(cur, prev, nxt, w_ref, b_ref, s_ref, dy_ref, dm_ref, dw_ref, db_ref, ds_ref, ext):
        i = pl.program_id(0)

        @pl.when(i == 0)
        def _():
            dw_ref[...] = jnp.zeros_like(dw_ref)
            db_ref[...] = jnp.zeros_like(db_ref)
            ds_ref[...] = jnp.zeros_like(ds_ref)

        _fill_ext(ext, cur, prev, nxt, i, nt, tt)
        for gi in range(ng):
            sl = slice(gi * gd, (gi + 1) * gd)
            mb = _pool_features(ext, i, tt, T, gi, gd).astype(BF16)
            wg = w_ref[gi]
            pre = _dot(mb, wg, NN) + b_ref[:, sl]
            dy_ = dy_ref[:, sl]
            ds_ref[:, sl] += jnp.sum(dy_ * pre, axis=0, keepdims=True)
            dpre = dy_ * s_ref[:, sl]
            db_ref[:, sl] += jnp.sum(dpre, axis=0, keepdims=True)
            dpb = dpre.astype(BF16)
            dw_ref[gi] += _dot(mb, dpb, TN)
            dm_ref[:, sl] = _dot(dpb, wg, NT)

    vec = pl.BlockSpec((1, D), lambda i: (0, 0))
    row = pl.BlockSpec((tt, D), lambda i: (i, 0))
    wspec = pl.BlockSpec((ng, gd, gd), lambda i: (0, 0, 0))
    return _call(body, name=name, grid=(nt,),
                 in_specs=_halo_specs(tt, D, lambda i: 0, nt, 0) + [wspec, vec, vec, row],
                 out_specs=[row, wspec, vec, vec],
                 out_shape=[jax.ShapeDtypeStruct((T, D), F32), jax.ShapeDtypeStruct((ng, gd, gd), F32),
                            jax.ShapeDtypeStruct((1, D), F32), jax.ShapeDtypeStruct((1, D), F32)],
                 scratch_shapes=[pltpu.VMEM((tt + 2 * HALO, D), F32)],
                 compiler_params=_params("arbitrary"))(u, u, u, w, bias, scale, dy)


def _pool_bwd_b(dm, dy, alpha, name):
    T, D = dm.shape
    ng = len(POOL_WINDOWS)
    gd = D // ng
    tt = _tile(T, 512, 8)
    nt = T // tt

    def body(cur, prev, nxt, dy_ref, o_ref, ext):
        i = pl.program_id(0)
        _fill_ext(ext, cur, prev, nxt, i, nt, tt)
        for gi, win in enumerate(POOL_WINDOWS):
            sl = slice(gi * gd, (gi + 1) * gd)
            rows = tt + 2 * HALO
            ext[:, sl] = ext[:, sl] / _pool_counts(i, tt, T, win, rows, -HALO)
            acc = None
            for o in range(-(win // 2) + 1, win // 2 + 1):
                term = ext[pl.ds(HALO + o, tt), sl]
                acc = term if acc is None else acc + term
            o_ref[:, sl] = alpha * dy_ref[:, sl] + acc - cur[:, sl]

    row = pl.BlockSpec((tt, D), lambda i: (i, 0))
    return _call(body, name=name, grid=(nt,),
                 in_specs=_halo_specs(tt, D, lambda i: 0, nt, 0) + [row], out_specs=row,
                 out_shape=jax.ShapeDtypeStruct((T, D), F32),
                 scratch_shapes=[pltpu.VMEM((tt + 2 * HALO, D), F32)],
                 compiler_params=_params("parallel"))(dm, dm, dm, dy)


def _exchange(comm, name):
    ops, arrays, shapes, sems, in_place = _comm_plan(comm)
    n = len(ops)

    def body(*refs):
        copies = _comm_copies(ops, refs[:n], refs[n:2 * n], *refs[2 * n:])
        for cp in copies:
            cp.start()
        for cp in copies:
            cp.wait()

    any_spec = pl.BlockSpec(memory_space=pl.ANY)
    return _pallas(body, name=name, in_specs=[any_spec] * n, out_specs=[any_spec] * n, out_shape=shapes,
                   scratch_shapes=sems, input_output_aliases={a: a for a in in_place})(*arrays)


def _adamw(piece, w, m, v, row0, into, name):
    R, C = w.shape
    rows = piece.shape[1]
    tr = _tile(rows, max(8, (1 << 18) // C // 8 * 8), 8)
    assert row0 % tr == 0 and rows % tr == 0
    off = row0 // tr

    def body(p_ref, w_ref, m_ref, v_ref, *rest):
        g_ref, d_ref, nm_ref, nv_ref = rest[-4:]
        g = p_ref[0].astype(F32)
        for i in range(1, N_DEV):
            g = g + p_ref[i].astype(F32)
        mm = ADAM_B1 * m_ref[...] + (1.0 - ADAM_B1) * g
        vv = ADAM_B2 * v_ref[...] + (1.0 - ADAM_B2) * (g * g)
        m_hat = mm / (1.0 - ADAM_B1 ** ADAM_STEP)
        v_hat = vv / (1.0 - ADAM_B2 ** ADAM_STEP)
        g_ref[...] = g
        d_ref[...] = -ADAM_LR * (m_hat / (jnp.sqrt(v_hat) + ADAM_EPS) + ADAM_WD * w_ref[...])
        nm_ref[...] = mm
        nv_ref[...] = vv

    row = pl.BlockSpec((tr, C), lambda i: (off + i, 0))
    kept = [] if into is None else list(into)
    return _call(body, name=name, grid=(rows // tr,),
                 in_specs=[pl.BlockSpec((N_DEV, tr, C), lambda i: (0, i, 0)), row, row, row]
                 + [pl.BlockSpec(memory_space=pl.ANY)] * len(kept),
                 out_specs=[row] * 4, out_shape=[jax.ShapeDtypeStruct((R, C), F32)] * 4,
                 input_output_aliases={4 + q: q for q in range(len(kept))},
                 compiler_params=_params("parallel"))(piece, w, m, v, *kept)


def _pack(arrays):
    flat, meta, off = [], [], 0
    for a in arrays:
        flat.append(a.reshape(-1).astype(F32))
        meta.append((off, a.shape))
        off += a.size
    total = -(-off // (8 * LANES)) * (8 * LANES)
    flat.append(jnp.zeros((total - off,), F32))
    return jnp.concatenate(flat).reshape(total // LANES, LANES), meta


def _unpack(packed, meta):
    flat = packed.reshape(-1)
    return [flat[off:off + math.prod(shape)].reshape(shape) for off, shape in meta]


def kernel(x, ssd_in_proj, ssd_conv_w, ssd_conv_b, ssd_dt_bias, ssd_A_log, ssd_D, ssd_norm_w, ssd_out_proj, pool_w, pool_b, pool_scale, mlp_w1, mlp_w2, ln_mix_g, ln_mix_b, ln_ffn_g, ln_ffn_b, loss_target, m_ssd_in_proj, m_ssd_conv_w, m_ssd_conv_b, m_ssd_dt_bias, m_ssd_A_log, m_ssd_D, m_ssd_norm_w, m_ssd_out_proj, m_pool_w, m_pool_b, m_pool_scale, m_mlp_w1, m_mlp_w2, m_ln_mix_g, m_ln_mix_b, m_ln_ffn_g, m_ln_ffn_b, v_ssd_in_proj, v_ssd_conv_w, v_ssd_conv_b, v_ssd_dt_bias, v_ssd_A_log, v_ssd_D, v_ssd_norm_w, v_ssd_out_proj, v_pool_w, v_pool_b, v_pool_scale, v_mlp_w1, v_mlp_w2, v_ln_mix_g, v_ln_mix_b, v_ln_ffn_g, v_ln_ffn_b):
    T, D = x.shape[1], x.shape[2]
    depth = mlp_w1.shape[0]
    n_ssd, n_pool = ssd_in_proj.shape[0], pool_w.shape[0]
    d_inner = ssd_out_proj.shape[1] * N_DEV
    n_heads = d_inner // HEAD_DIM
    hg = n_heads // N_GROUPS
    d_bc = N_GROUPS * D_STATE
    d_xbc = d_inner + 2 * d_bc
    d_in_proj = ssd_in_proj.shape[2] * N_DEV
    d_ff = mlp_w1.shape[2] * N_DEV
    ng = len(POOL_WINDOWS)
    gd = D // ng
    alpha = (2.0 * depth) ** 0.25
    x0 = x.reshape(T, D)
    target = loss_target.reshape(T, D)

    assert depth == 4 and n_ssd == 2 and n_pool == 2, "the exchange schedules below are written for this stack"

    small_pack, small_meta = _pack([ssd_conv_w, pool_b, pool_scale])
    pw_rows = pool_w.shape[1] * pool_w.shape[2]
    in_b, out_b = ssd_in_proj.astype(BF16), ssd_out_proj.astype(BF16)
    pw_b = pool_w.reshape(n_pool, pw_rows, gd).astype(BF16)
    w1_b, w2_b = mlp_w1.astype(BF16), mlp_w2.astype(BF16)
    shard = {("in", 0): in_b[0], ("in", 1): in_b[1], ("out", 0): out_b[0], ("out", 1): out_b[1],
             ("pool", 0): pw_b[0], ("pool", 1): pw_b[1]}
    for i in range(depth):
        shard["w1", i], shard["w2", i] = w1_b[i], w2_b[i]

    def full_cols(g):
        return g.transpose(1, 0, 2).reshape(g.shape[1], -1)

    def full_rows(g):
        return g.reshape(-1, g.shape[-1])

    def full_pool(g):
        return g.reshape(N_DEV, ng, gd // N_DEV, gd).transpose(1, 0, 2, 3).reshape(ng, gd, gd)

    def slab_cols(g):
        return g.reshape(g.shape[0], N_DEV, -1).transpose(1, 0, 2)

    def slab_rows(g):
        return g.reshape(N_DEV, -1, g.shape[-1])

    def slab_pool(g):
        return g.astype(BF16).reshape(ng, N_DEV, gd // N_DEV, gd).transpose(1, 0, 2, 3).reshape(N_DEV, pw_rows, gd)

    to_full = {"in": full_cols, "out": full_rows, "pool": full_pool, "w1": full_cols, "w2": full_rows}

    gather1_on = {"in_proj_0": [("w1", 0), ("pool", 0)], "conv_fwd_0": [("out", 0)],
                  "ssd_fwd_0": [("w2", 0), ("w1", 1), ("w2", 1)], "out_proj_0": [("out", 1)], "mlp_up_0": [("in", 1)],
                  "mlp_down_0": [("w1", 2)], "mlp_up_1": [("w2", 2)], "mlp_down_1": [("w1", 3), ("pool", 1)],
                  "in_proj_2": [("w2", 3)]}
    gather2_on = {"conv_fwd_0": [("w1", 0), ("pool", 0)], "ssd_fwd_0": [("out", 0)],
                  "gnorm_fwd_0": [("w2", 0), ("w1", 1), ("w2", 1)], "mlp_up_0": [("out", 1)], "mlp_down_0": [("in", 1)],
                  "mlp_up_1": [("w1", 2)], "mlp_down_1": [("w2", 2)], "in_proj_2": [("w1", 3), ("pool", 1)],
                  "conv_fwd_2": [("w2", 3)]}
    hide_us = {"mlp_down_dx": 161, "mlp_down_dw": 163, "mlp_up_dx": 170, "mlp_up_dw": 164, "out_proj_dx": 83,
               "out_proj_dw": 85, "gnorm_bwd": 163, "ssd_bwd": 1104, "conv_bwd": 228, "in_proj_dw": 230, "in_proj_dx": 267}
    SLAB_BYTES_PER_US = 80e3 / (N_DEV - 1)
    OVERRUN_US, MIN_CARRIER_US = 15, 80
    n_pieces = {"in": 4, "out": 2, "pool": 1, "w1": 4, "w2": 4}
    W, half, G, R = {}, {}, {}, {}
    queue = []

    def produced(key, slabs):
        G[key] = slabs
        rows = slabs.shape[1] // n_pieces[key[0]]
        cost = rows * math.prod(slabs.shape[2:]) * slabs.dtype.itemsize / SLAB_BYTES_PER_US
        queue.extend((key, p * rows, rows, cost) for p in range(n_pieces[key[0]]))

    def take(budget):
        taken, used = [], 0.0
        while queue and ((not taken and budget >= MIN_CARRIER_US) or used + queue[0][3] <= budget + OVERRUN_US):
            taken.append(queue.pop(0))
            used += taken[-1][3]
        return taken

    def run(fn, *args, name, **kw):
        k1, k2 = gather1_on.get(name, []), gather2_on.get(name, [])
        pieces = take(hide_us.get(name.rsplit("_", 1)[0], 0)) if not (k1 or k2) else []
        comm = ([("gather1", shard[k]) for k in k1] + [("gather2", half[k]) for k in k2]
                + [("slabs", G[key], r0, rows) for key, r0, rows, _ in pieces])
        if not comm:
            return fn(*args, name=name, **kw)
        res, got = fn(*args, name=name, comm=comm, **kw)
        for k, g in zip(k1, got):
            half[k] = g
        for k, g in zip(k2, got[len(k1):]):
            W[k] = to_full[k[0]](g)
        for (key, r0, _, _), g in zip(pieces, got):
            R.setdefault(key, []).append((r0, g))
        return res

    half_in0, g_small = _exchange([("gather1", shard["in", 0]), ("gather", small_pack)], "gather_first")
    (g_in0,) = _exchange([("gather2", half_in0)], "gather_first_onward")
    W["in", 0] = full_cols(g_in0)
    smalls = [_unpack(g_small[k], small_meta) for k in range(N_DEV)]
    conv_w = jnp.concatenate([s[0] for s in smalls], axis=-1).reshape(n_ssd, CONV_WIDTH, d_xbc)
    pool_bias = jnp.concatenate([s[1] for s in smalls], axis=-1).reshape(n_pool, 1, D)
    pool_sc = jnp.concatenate([s[2] for s in smalls], axis=-1).reshape(n_pool, 1, D)

    sel = (jnp.arange(d_inner)[:, None] // HEAD_DIM == jnp.arange(n_heads)[None, :]).astype(F32)

    saved = []
    h, hb = x0, x0.astype(BF16)
    for i in range(depth):
        j = i // 2
        s = {}
        s["x0"], s["x0b"] = h, hb
        if i % 2 == 0:
            (zx,) = run(_matmul, hb, W["in", j], "nn", name=f"in_proj_{i}", outs=[F32], tn=1152)
            xbc = run(_conv_fwd, zx, conv_w[j], ssd_conv_b[j], d_inner, name=f"conv_fwd_{i}")
            raw = zx[:, d_inner + d_xbc:]
            dt, cs = _dt_fwd(raw, ssd_dt_bias[j], ssd_A_log[j], f"dt_fwd_{i}")
            dtc, csc, csr = _cols(dt, hg), _cols(cs, hg), _rows(cs, hg)
            y2, states = run(_ssd_fwd, xbc, dtc, csc, csr, d_inner, name=f"ssd_fwd_{i}")
            dvec = jnp.repeat(ssd_D[j], HEAD_DIM).reshape(1, d_inner)
            nw = ssd_norm_w[j].reshape(1, d_inner)
            gn = run(_gnorm_fwd, y2, xbc, zx, dvec, nw, d_inner, name=f"gnorm_fwd_{i}")
            (mix,) = run(_matmul, gn, W["out", j], "nn", name=f"out_proj_{i}", outs=[F32])
            s.update(zx=zx, xbc=xbc, raw=raw, dtc=dtc, csc=csc, csr=csr, y2=y2, states=states, dvec=dvec, nw=nw, gn=gn)
        else:
            mix = _pool_fwd(h, W["pool", j], pool_bias[j], pool_sc[j], f"pool_fwd_{i}")
        s["mix"] = mix
        x1, x1b = _ln_fwd(h, mix, ln_mix_g[i], ln_mix_b[i], alpha, f"ln_mix_fwd_{i}")
        u, hh = run(_matmul, x1b, W["w1", i], "nn", name=f"mlp_up_{i}", outs=[F32, BF16],
                    epilogue=lambda acc: (acc, jnp.square(jnp.maximum(acc, 0.0))))
        (m2,) = run(_matmul, hh, W["w2", i], "nn", name=f"mlp_down_{i}", outs=[F32])
        x2, x2b = _ln_fwd(x1, m2, ln_ffn_g[i], ln_ffn_b[i], alpha, f"ln_ffn_fwd_{i}")
        s.update(x1=x1, x1b=x1b, u=u, hh=hh, m2=m2)
        saved.append(s)
        h, hb = x2, x2b

    loss_row, dh = _loss_head(h, target)
    loss = lax.psum(loss_row[0, 0], ("x", "y", "c"))

    big = {"in": (ssd_in_proj, m_ssd_in_proj, v_ssd_in_proj), "out": (ssd_out_proj, m_ssd_out_proj, v_ssd_out_proj),
           "pool": (pool_w, m_pool_w, v_pool_w), "w1": (mlp_w1, m_mlp_w1, v_mlp_w1), "w2": (mlp_w2, m_mlp_w2, v_mlp_w2)}

    def update(kind, count):
        shape = big[kind][0].shape
        w, m, v = (a.reshape(-1, shape[-1]) for a in big[kind])
        per_layer = w.shape[0] // count
        outs = None
        for l in range(count):
            for r0, g in sorted(R[kind, l], key=lambda t: t[0]):
                outs = _adamw(g, w, m, v, l * per_layer + r0, outs, f"adamw_{kind}_{l}_{r0}")
        return [o.reshape(shape) for o in outs]

    gr = {k: [None] * depth for k in ("ln_mix_g", "ln_mix_b", "ln_ffn_g", "ln_ffn_b")}
    gs = {k: [None] * n_ssd for k in ("conv_w", "conv_b", "dt_bias", "A_log", "D", "norm_w")}
    gp = {k: [None] * n_pool for k in ("b", "scale")}
    for i in reversed(range(depth)):
        j = i // 2
        s = saved[i]
        ds2, ds2b, gr["ln_ffn_g"][i], gr["ln_ffn_b"][i] = _ln_bwd(s["x1"], s["m2"], ln_ffn_g[i], dh, alpha, f"ln_ffn_bwd_{i}")
        (du,) = run(_matmul, ds2b, W["w2", i], "nt", name=f"mlp_down_dx_{i}", outs=[BF16], extras=[s["u"]],
                    epilogue=lambda acc, u_: (acc * (2.0 * jnp.maximum(u_, 0.0)),))
        (g_w2,) = run(_matmul, s["hh"], ds2b, "tn", name=f"mlp_down_dw_{i}", outs=[BF16])
        produced(("w2", i), slab_rows(g_w2))
        (dx1,) = run(_matmul, du, W["w1", i], "nt", name=f"mlp_up_dx_{i}", outs=[F32], extras=[ds2],
                     epilogue=lambda acc, e: (acc + alpha * e,))
        (g_w1,) = run(_matmul, s["x1b"], du, "tn", name=f"mlp_up_dw_{i}", outs=[BF16])
        produced(("w1", i), slab_cols(g_w1))
        ds1, ds1b, gr["ln_mix_g"][i], gr["ln_mix_b"][i] = _ln_bwd(s["x0"], s["mix"], ln_mix_g[i], dx1, alpha, f"ln_mix_bwd_{i}")
        if i % 2 == 0:
            (dgn,) = run(_matmul, ds1b, W["out", j], "nt", name=f"out_proj_dx_{i}", outs=[F32])
            (g_out,) = _matmul(s["gn"], ds1b, "tn", name=f"out_proj_dw_{i}", outs=[BF16])
            produced(("out", j), slab_rows(g_out))
            dy, dxs_d, dzx, gs["norm_w"][j], gs["D"][j] = run(
                _gnorm_bwd, s["y2"], s["xbc"], s["zx"], s["dvec"], s["nw"], dgn, sel, d_inner, name=f"gnorm_bwd_{i}")
            dxs, dB, dC, dcs, dtot, dxdtx = run(_ssd_bwd, s["xbc"], s["dtc"], s["csc"], s["csr"], s["states"], s["y2"], dy,
                                                d_inner, name=f"ssd_bwd_{i}")
            dzx, gs["dt_bias"][j], gs["A_log"][j] = _dt_bwd(
                s["raw"], ssd_dt_bias[j], ssd_A_log[j], _uncols(dcs), _uncols(dtot), _uncols(dxdtx), dzx, f"dt_bwd_{i}")
            dpre = _conv_dpre(s["zx"], conv_w[j], ssd_conv_b[j], d_inner, 0, dxs, dxs_d, f"conv_dpre_x_{i}")
            dpre = _conv_dpre(s["zx"], conv_w[j], ssd_conv_b[j], d_inner, d_inner, dB, None, f"conv_dpre_b_{i}", into=dpre)
            dpre = _conv_dpre(s["zx"], conv_w[j], ssd_conv_b[j], d_inner, d_inner + d_bc, dC, None, f"conv_dpre_c_{i}",
                              into=dpre)
            dzx, gs["conv_w"][j], gs["conv_b"][j] = run(_conv_bwd, s["zx"], dpre, conv_w[j], d_inner, dzx,
                                                        name=f"conv_bwd_{i}")
            (g_in,) = run(_matmul, s["x0b"], dzx, "tn", name=f"in_proj_dw_{i}", outs=[BF16], tn=1152)
            produced(("in", j), slab_cols(g_in))
            (dh,) = run(_matmul, dzx, W["in", j], "nt", name=f"in_proj_dx_{i}", outs=[F32], extras=[ds1], tk=1152,
                        epilogue=lambda acc, e: (acc + alpha * e,))
        else:
            dm, g_pw, gp["b"][j], gp["scale"][j] = _pool_bwd_a(
                s["x0"], W["pool", j], pool_bias[j], pool_sc[j], ds1, f"pool_bwd_a_{i}")
            produced(("pool", j), slab_pool(g_pw))
            dh = _pool_bwd_b(dm, ds1, alpha, f"pool_bwd_b_{i}")
    grad_x = dh.reshape(x.shape)

    g_conv_w = jnp.stack(gs["conv_w"]).reshape(n_ssd, CONV_WIDTH, 1, N_DEV, d_xbc // N_DEV)
    g_pool_b = jnp.stack(gp["b"]).reshape(n_pool, ng, N_DEV, gd // N_DEV)
    g_pool_s = jnp.stack(gp["scale"]).reshape(n_pool, N_DEV, D // N_DEV)
    s_small = jnp.stack([_pack([g_conv_w[:, :, :, k], g_pool_b[:, :, k], g_pool_s[:, k]])[0] for k in range(N_DEV)])
    repl_grads = [jnp.stack(gs["conv_b"]).reshape(ssd_conv_b.shape), jnp.stack(gs["dt_bias"]).reshape(ssd_dt_bias.shape),
                  jnp.stack(gs["A_log"]).reshape(ssd_A_log.shape), jnp.stack(gs["D"]).reshape(ssd_D.shape),
                  jnp.stack(gs["norm_w"]).reshape(ssd_norm_w.shape),
                  jnp.stack(gr["ln_mix_g"]).reshape(ln_mix_g.shape), jnp.stack(gr["ln_mix_b"]).reshape(ln_mix_b.shape),
                  jnp.stack(gr["ln_ffn_g"]).reshape(ln_ffn_g.shape), jnp.stack(gr["ln_ffn_b"]).reshape(ln_ffn_b.shape)]
    repl_pack, repl_meta = _pack(repl_grads)
    s_repl = jnp.broadcast_to(repl_pack[None], (N_DEV,) + repl_pack.shape)
    left = list(queue)
    del queue[:]
    got = _exchange([("slabs", s_small, 0, s_small.shape[1]), ("slabs", s_repl, 0, s_repl.shape[1])]
                    + [("slabs", G[key], r0, rows) for key, r0, rows, _ in left], "exchange_last")
    r_small, r_repl = got[0], got[1]
    for (key, r0, _, _), g in zip(left, got[2:]):
        R.setdefault(key, []).append((r0, g))

    upd = {}
    for nm, kind, count in (("ssd_in_proj", "in", n_ssd), ("ssd_out_proj", "out", n_ssd), ("pool_w", "pool", n_pool),
                            ("mlp_w1", "w1", depth), ("mlp_w2", "w2", depth)):
        upd[nm] = update(kind, count)
    sm = _adamw(r_small, small_pack, _pack([m_ssd_conv_w, m_pool_b, m_pool_scale])[0],
                _pack([v_ssd_conv_w, v_pool_b, v_pool_scale])[0], 0, None, "adamw_small_sharded")
    for idx, nm in enumerate(["ssd_conv_w", "pool_b", "pool_scale"]):
        upd[nm] = [_unpack(r, small_meta)[idx] for r in sm]
    repl_names = ["ssd_conv_b", "ssd_dt_bias", "ssd_A_log", "ssd_D", "ssd_norm_w",
                  "ln_mix_g", "ln_mix_b", "ln_ffn_g", "ln_ffn_b"]
    repl_w = [ssd_conv_b, ssd_dt_bias, ssd_A_log, ssd_D, ssd_norm_w, ln_mix_g, ln_mix_b, ln_ffn_g, ln_ffn_b]
    repl_m = [m_ssd_conv_b, m_ssd_dt_bias, m_ssd_A_log, m_ssd_D, m_ssd_norm_w, m_ln_mix_g, m_ln_mix_b, m_ln_ffn_g, m_ln_ffn_b]
    repl_v = [v_ssd_conv_b, v_ssd_dt_bias, v_ssd_A_log, v_ssd_D, v_ssd_norm_w, v_ln_mix_g, v_ln_mix_b, v_ln_ffn_g, v_ln_ffn_b]
    rp = _adamw(r_repl, _pack(repl_w)[0], _pack(repl_m)[0], _pack(repl_v)[0], 0, None, "adamw_replicated")
    for idx, nm in enumerate(repl_names):
        upd[nm] = [_unpack(r, repl_meta)[idx] for r in rp]

    order = ["ssd_in_proj", "ssd_conv_w", "ssd_conv_b", "ssd_dt_bias", "ssd_A_log", "ssd_D", "ssd_norm_w",
             "ssd_out_proj", "pool_w", "pool_b", "pool_scale", "mlp_w1", "mlp_w2",
             "ln_mix_g", "ln_mix_b", "ln_ffn_g", "ln_ffn_b"]
    return (loss, grad_x, *[upd[n][0] for n in order], *[upd[n][1] for n in order],
            *[upd[n][2] for n in order], *[upd[n][3] for n in order])
```

```python
import functools
import math

import jax
import jax.numpy as jnp
from jax import lax
from jax.experimental import pallas as pl
from jax.experimental.pallas import tpu as pltpu

F32 = jnp.float32
BF16 = jnp.bfloat16

N_DEV = 8
HEAD_DIM = 64
N_GROUPS = 8
D_STATE = 128
CHUNK = 128
CONV_WIDTH = 5
POOL_WINDOWS = (2, 4, 8, 16)
HALO = 8
LN_EPS = 1e-5
RMS_EPS = 1e-5
ADAM_LR = 0.001
ADAM_B1 = 0.9
ADAM_B2 = 0.999
ADAM_EPS = 1e-08
ADAM_WD = 0.01
ADAM_STEP = 10
LANES = 128
VMEM_LIMIT_BYTES = 56 * 1024 * 1024
HIGHEST = lax.Precision.HIGHEST


def _pallas(body, **kw):
    return pl.pallas_call(body, **kw)


def _params(*sem):
    return pltpu.CompilerParams(dimension_semantics=sem, vmem_limit_bytes=VMEM_LIMIT_BYTES)


def _mesh_pos():
    return lax.axis_index("x"), lax.axis_index("y"), lax.axis_index("c")


def _peer(x, y, c, k):
    dx, dy, dc = (k >> 2) & 1, (k >> 1) & 1, k & 1
    px = (1 - x) if dx else x
    py = (1 - y) if dy else y
    pc = (1 - c) if dc else c
    return px, py, pc


def _comm_copies(ops, ins, outs, send_sems, recv_sems, local_sems):
    x, y, c = _mesh_pos()
    me = 4 * x + 2 * y + c
    copies = []
    for a, op in enumerate(ops):
        src, dst = ins[a], outs[a]

        def remote(k, s, d, to):
            return pltpu.make_async_remote_copy(src_ref=s, dst_ref=d, send_sem=send_sems.at[a, k - 1],
                                                recv_sem=recv_sems.at[a, k - 1], device_id=to,
                                                device_id_type=pl.DeviceIdType.MESH)

        if op[0] in ("gather", "gather1"):
            copies.append(pltpu.make_async_copy(src, dst.at[me], local_sems.at[a]))
            for k in (range(1, N_DEV) if op[0] == "gather" else (1, 2, 4, 6)):
                copies.append(remote(k, src, dst.at[me], _peer(x, y, c, k)))
        elif op[0] == "gather2":
            for k in (2, 4, 6):
                qx, qy, qc = _peer(x, y, c, k)
                slot = 4 * qx + 2 * qy + qc
                copies.append(remote(k, src.at[slot], dst.at[slot], _peer(x, y, c, 1)))
        else:
            rows = pl.ds(op[1], op[2])
            copies.append(pltpu.make_async_copy(src.at[me, rows], dst.at[me], local_sems.at[a]))
            for k in range(1, N_DEV):
                px, py, pc = _peer(x, y, c, k)
                copies.append(remote(k, src.at[4 * px + 2 * py + pc, rows], dst.at[me], (px, py, pc)))
    return copies


def _comm_plan(comm):
    ops = [(c[0],) + tuple(c[2:]) for c in comm]
    arrays = [c[1] for c in comm]
    shapes = []
    for op, a in zip(ops, arrays):
        if op[0] in ("gather", "gather1"):
            shapes.append(jax.ShapeDtypeStruct((N_DEV,) + a.shape, a.dtype))
        elif op[0] == "gather2":
            shapes.append(jax.ShapeDtypeStruct(a.shape, a.dtype))
        else:
            shapes.append(jax.ShapeDtypeStruct((N_DEV, op[2]) + a.shape[2:], a.dtype))
    n = len(ops)
    sems = [pltpu.SemaphoreType.DMA((n, N_DEV - 1)), pltpu.SemaphoreType.DMA((n, N_DEV - 1)),
            pltpu.SemaphoreType.DMA((n,))]
    in_place = [a for a, op in enumerate(ops) if op[0] == "gather2"]
    return ops, arrays, shapes, sems, in_place


def _call(body, *, comm=None, **kw):
    if not comm:
        return _pallas(body, **kw)
    ops, arrays, c_shape, c_sems, in_place = _comm_plan(comm)
    n = len(ops)
    grid = tuple(kw["grid"])
    single = not isinstance(kw["out_shape"], (list, tuple))
    out_shape = [kw["out_shape"]] if single else list(kw["out_shape"])
    out_specs = [kw["out_specs"]] if single else list(kw["out_specs"])
    in_specs = list(kw["in_specs"])
    scratch = list(kw.get("scratch_shapes", ()))
    n_in, n_out, n_scr = len(in_specs), len(out_shape), len(scratch)

    def wrapped(*refs):
        ins, refs = refs[:n_in], refs[n_in:]
        c_ins, refs = refs[:n], refs[n:]
        outs, refs = refs[:n_out], refs[n_out:]
        c_outs, refs = refs[:n], refs[n:]
        scr, sems = refs[:n_scr], refs[n_scr:]
        first = last = None
        for ax, size in enumerate(grid):
            i = pl.program_id(ax)
            first = (i == 0) if first is None else first & (i == 0)
            last = (i == size - 1) if last is None else last & (i == size - 1)

        @pl.when(first)
        def _():
            for cp in _comm_copies(ops, c_ins, c_outs, *sems):
                cp.start()

        body(*ins, *outs, *scr)

        @pl.when(last)
        def _():
            for cp in _comm_copies(ops, c_ins, c_outs, *sems):
                cp.wait()

    any_spec = pl.BlockSpec(memory_space=pl.ANY)
    call = _pallas(wrapped, name=kw["name"], grid=grid, in_specs=in_specs + [any_spec] * n,
                   out_specs=out_specs + [any_spec] * n, out_shape=out_shape + c_shape,
                   scratch_shapes=scratch + c_sems,
                   input_output_aliases={**kw.get("input_output_aliases", {}), **{n_in + a: n_out + a for a in in_place}},
                   compiler_params=_params(*(("arbitrary",) * len(grid))))

    def run(*args):
        res = call(*args, *arrays)
        own = res[:n_out]
        return (own[0] if single else list(own)), list(res[n_out:])

    return run


def _tile(dim, target, align=LANES):
    if dim <= target:
        return dim
    t = (target // align) * align
    while t >= align:
        if dim % t == 0:
            return t
        t -= align
    return dim


def _dot(a, b, dims, precision=None):
    return lax.dot_general(a, b, (dims, ((), ())), precision=precision, preferred_element_type=F32)


NN = ((1,), (0,))
NT = ((1,), (1,))
TN = ((0,), (0,))


def _sigmoid(x):
    return 1.0 / (1.0 + jnp.exp(-x))


def _matmul(a, b, mode, *, name, outs, epilogue=None, extras=(), tm=1024, tn=1024, tk=2048, comm=None,
            cols_by_device=False):
    if mode == "nn":
        (M, K), (K2, N) = a.shape, (b.shape[-2:] if not cols_by_device else (b.shape[1], N_DEV * b.shape[2]))
    elif mode == "nt":
        (M, K), (N, K2) = a.shape, (b.shape if not cols_by_device else (b.shape[1], N_DEV * b.shape[2]))
    else:
        (K, M), (K2, N) = a.shape, b.shape
    assert K == K2, (a.shape, b.shape, mode)
    per_dev = (K if mode == "nt" else N) // N_DEV
    if cols_by_device and mode == "nt":
        tk = min(tk, per_dev)
    elif cols_by_device:
        tn = min(tn, per_dev)
    tm, tn, tk = _tile(M, tm), _tile(N, tn), _tile(K, tk)
    nk = K // tk
    dims = {"nn": NN, "nt": NT, "tn": TN}[mode]
    a_spec = (pl.BlockSpec((tk, tm), lambda i, j, k: (k, i)) if mode == "tn"
              else pl.BlockSpec((tm, tk), lambda i, j, k: (i, k)))
    b_spec = (pl.BlockSpec((tn, tk), lambda i, j, k: (j, k)) if mode == "nt"
              else pl.BlockSpec((tk, tn), lambda i, j, k: (k, j)))
    mn_spec = pl.BlockSpec((tm, tn), lambda i, j, k: (i, j))
    out_spec, out_dims = mn_spec, (M, N)
    if cols_by_device and mode == "nn":
        r = per_dev // tn
        b_spec = pl.BlockSpec((None, tk, tn), lambda i, j, k: (j // r, k, j % r))
    elif cols_by_device and mode == "nt":
        r = per_dev // tk
        b_spec = pl.BlockSpec((None, tn, tk), lambda i, j, k: (k // r, j, k % r))
    elif cols_by_device:
        assert not extras and epilogue is None
        r = per_dev // tn
        out_spec, out_dims = pl.BlockSpec((None, tm, tn), lambda i, j, k: (j // r, i, j % r)), (N_DEV, M, per_dev)
    n_extra, n_out = len(extras), len(outs)

    def finish(acc, extra_refs, out_refs):
        res = (acc,) if epilogue is None else epilogue(acc, *[r[...] for r in extra_refs])
        for o_ref, r in zip(out_refs, res):
            o_ref[...] = r.astype(o_ref.dtype)

    def body(*refs):
        a_ref, b_ref = refs[0], refs[1]
        extra_refs = refs[2:2 + n_extra]
        out_refs = refs[2 + n_extra:2 + n_extra + n_out]
        part = _dot(a_ref[...].astype(BF16), b_ref[...].astype(BF16), dims)
        if nk == 1:
            finish(part, extra_refs, out_refs)
            return
        acc_ref = refs[-1]
        k = pl.program_id(2)

        @pl.when(k == 0)
        def _():
            acc_ref[...] = part

        @pl.when((k > 0) & (k < nk - 1))
        def _():
            acc_ref[...] += part

        @pl.when(k == nk - 1)
        def _():
            finish(acc_ref[...] + part, extra_refs, out_refs)

    res = _call(
        body, comm=comm, name=name, grid=(M // tm, N // tn, nk),
        in_specs=[a_spec, b_spec] + [mn_spec] * n_extra,
        out_specs=[out_spec] * n_out,
        out_shape=[jax.ShapeDtypeStruct(out_dims, dt) for dt in outs],
        scratch_shapes=[pltpu.VMEM((tm, tn), F32)] if nk > 1 else [],
        compiler_params=_params("parallel", "parallel", "arbitrary"),
    )(a, b, *extras)
    return res


def _ln_fwd(x, f, g, b, alpha, name):
    T, D = x.shape
    tm = _tile(T, 256, 8)

    def body(x_ref, f_ref, g_ref, b_ref, y_ref, yb_ref):
        s = alpha * x_ref[...] + f_ref[...]
        mu = jnp.mean(s, axis=-1, keepdims=True)
        d = s - mu
        var = jnp.mean(d * d, axis=-1, keepdims=True)
        y = d * lax.rsqrt(var + LN_EPS) * g_ref[...] + b_ref[...]
        y_ref[...] = y
        yb_ref[...] = y.astype(BF16)

    row = pl.BlockSpec((tm, D), lambda i: (i, 0))
    vec = pl.BlockSpec((1, D), lambda i: (0, 0))
    return _call(body, name=name, grid=(T // tm,), in_specs=[row, row, vec, vec], out_specs=[row, row],
                 out_shape=[jax.ShapeDtypeStruct((T, D), F32), jax.ShapeDtypeStruct((T, D), BF16)],
                 compiler_params=_params("parallel"))(x, f, g.reshape(1, D), b.reshape(1, D))


def _ln_bwd(x, f, g, dy, alpha, name):
    T, D = x.shape
    tm = _tile(T, 256, 8)

    def body(x_ref, f_ref, g_ref, dy_ref, ds_ref, dsb_ref, dg_ref, db_ref):
        i = pl.program_id(0)

        @pl.when(i == 0)
        def _():
            dg_ref[...] = jnp.zeros_like(dg_ref)
            db_ref[...] = jnp.zeros_like(db_ref)

        s = alpha * x_ref[...] + f_ref[...]
        mu = jnp.mean(s, axis=-1, keepdims=True)
        d = s - mu
        var = jnp.mean(d * d, axis=-1, keepdims=True)
        rstd = lax.rsqrt(var + LN_EPS)
        xhat = d * rstd
        dy_ = dy_ref[...]
        dg_ref[...] += jnp.sum(dy_ * xhat, axis=0, keepdims=True)
        db_ref[...] += jnp.sum(dy_, axis=0, keepdims=True)
        dxh = dy_ * g_ref[...]
        m1 = jnp.mean(dxh, axis=-1, keepdims=True)
        m2 = jnp.mean(dxh * xhat, axis=-1, keepdims=True)
        ds = rstd * (dxh - m1 - xhat * m2)
        ds_ref[...] = ds
        dsb_ref[...] = ds.astype(BF16)

    row = pl.BlockSpec((tm, D), lambda i: (i, 0))
    vec = pl.BlockSpec((1, D), lambda i: (0, 0))
    return _call(body, name=name, grid=(T // tm,), in_specs=[row, row, vec, row], out_specs=[row, row, vec, vec],
                 out_shape=[jax.ShapeDtypeStruct((T, D), F32), jax.ShapeDtypeStruct((T, D), BF16),
                            jax.ShapeDtypeStruct((1, D), F32), jax.ShapeDtypeStruct((1, D), F32)],
                 compiler_params=_params("arbitrary"))(x, f, g.reshape(1, D), dy)


def _loss_head(y, target):
    T, D = y.shape
    tm = _tile(T, 256, 8)

    def body(y_ref, t_ref, loss_ref, dy_ref):
        i = pl.program_id(0)

        @pl.when(i == 0)
        def _():
            loss_ref[...] = jnp.zeros_like(loss_ref)

        err = y_ref[...] - t_ref[...]
        dy_ref[...] = err * (1.0 / D)
        per_tok = jnp.mean(err * err, axis=-1, keepdims=True)
        loss_ref[...] += 0.5 * jnp.sum(per_tok)

    row = pl.BlockSpec((tm, D), lambda i: (i, 0))
    return _call(body, name="loss_head", grid=(T // tm,), in_specs=[row, row],
                 out_specs=[pl.BlockSpec((1, LANES), lambda i: (0, 0)), row],
                 out_shape=[jax.ShapeDtypeStruct((1, LANES), F32), jax.ShapeDtypeStruct((T, D), F32)],
                 compiler_params=_params("arbitrary"))(y, target)


def _halo_specs(tt, cw, col_of, n_tiles, grid_rank_tokens_axis):
    per = tt // HALO
    ax = grid_rank_tokens_axis

    def cur(*g):
        return (g[ax], col_of(*g))

    def prev(*g):
        return (jnp.maximum(g[ax] * per - 1, 0), col_of(*g))

    def nxt(*g):
        return (jnp.minimum((g[ax] + 1) * per, n_tiles * per - 1), col_of(*g))

    return [pl.BlockSpec((tt, cw), cur), pl.BlockSpec((HALO, cw), prev), pl.BlockSpec((HALO, cw), nxt)]


def _fill_ext(ext_ref, cur_ref, prev_ref, next_ref, i, n_tiles, tt):
    ext_ref[pl.ds(0, HALO), :] = jnp.where(i > 0, prev_ref[...], 0.0)
    ext_ref[pl.ds(HALO, tt), :] = cur_ref[...]
    ext_ref[pl.ds(HALO + tt, HALO), :] = jnp.where(i < n_tiles - 1, next_ref[...], 0.0)


def _conv_pre(ext_ref, w, bias, tt, lo=0, n=None):
    n = tt if n is None else n
    pad = CONV_WIDTH // 2
    acc = None
    for k in range(CONV_WIDTH):
        term = ext_ref[pl.ds(HALO + lo + k - pad, n), :] * w[k:k + 1, :]
        acc = term if acc is None else acc + term
    return acc + bias


def _conv_fwd(zx, conv_w, conv_b, d_inner, name, comm=None):
    T = zx.shape[0]
    d_xbc = conv_w.shape[1]
    cw = _tile(d_xbc, 512)
    assert d_inner % cw == 0
    off = d_inner // cw
    tt = _tile(T, 512, 8)
    nt = T // tt

    def body(cur_ref, prev_ref, next_ref, w_ref, b_ref, o_ref, ext_ref):
        i = pl.program_id(1)
        _fill_ext(ext_ref, cur_ref, prev_ref, next_ref, i, nt, tt)
        pre = _conv_pre(ext_ref, w_ref[...], b_ref[...], tt)
        o_ref[...] = pre * _sigmoid(pre)

    specs = _halo_specs(tt, cw, lambda j, i: off + j, nt, 1)
    return _call(body, comm=comm, name=name, grid=(d_xbc // cw, nt),
                 in_specs=specs + [pl.BlockSpec((CONV_WIDTH, cw), lambda j, i: (0, j)),
                                   pl.BlockSpec((1, cw), lambda j, i: (0, j))],
                 out_specs=pl.BlockSpec((tt, cw), lambda j, i: (i, j)),
                 out_shape=jax.ShapeDtypeStruct((T, d_xbc), F32),
                 scratch_shapes=[pltpu.VMEM((tt + 2 * HALO, cw), F32)],
                 compiler_params=_params("parallel", "parallel"))(zx, zx, zx, conv_w, conv_b.reshape(1, d_xbc))


def _conv_dpre(zx, conv_w, conv_b, d_inner, col_lo, dirs, extra, name, into=None):
    T = zx.shape[0]
    ncols = dirs.shape[2]
    cw = _tile(ncols, 512)
    assert d_inner % cw == 0 and col_lo % cw == 0
    off_zx = (d_inner + col_lo) // cw
    off_w = col_lo // cw
    tt = _tile(T, 512, 8)
    nt = T // tt
    has_extra = extra is not None

    def body(cur_ref, prev_ref, next_ref, w_ref, b_ref, dirs_ref, *rest):
        o_ref, ext_ref = rest[-2], rest[-1]
        i = pl.program_id(1)
        _fill_ext(ext_ref, cur_ref, prev_ref, next_ref, i, nt, tt)
        pre = _conv_pre(ext_ref, w_ref[...], b_ref[...], tt)
        sig = _sigmoid(pre)
        dact = dirs_ref[0] + dirs_ref[1]
        if has_extra:
            dact = dact + rest[0][...]
        o_ref[...] = dact * (sig * (1.0 + pre * (1.0 - sig)))

    specs = _halo_specs(tt, cw, lambda j, i: off_zx + j, nt, 1)
    in_specs = specs + [pl.BlockSpec((CONV_WIDTH, cw), lambda j, i: (0, off_w + j)),
                        pl.BlockSpec((1, cw), lambda j, i: (0, off_w + j)),
                        pl.BlockSpec((2, tt, cw), lambda j, i: (0, i, j))]
    args = [zx, zx, zx, conv_w, conv_b.reshape(1, -1), dirs]
    if has_extra:
        in_specs.append(pl.BlockSpec((tt, cw), lambda j, i: (i, j)))
        args.append(extra)
    aliases = {}
    if into is not None:
        in_specs.append(pl.BlockSpec(memory_space=pl.ANY))
        args.append(into)
        aliases = {len(args) - 1: 0}
    return _call(body, name=name, grid=(ncols // cw, nt), in_specs=in_specs,
                 out_specs=pl.BlockSpec((tt, cw), lambda j, i: (i, off_w + j)),
                 out_shape=jax.ShapeDtypeStruct((T, conv_w.shape[1]), F32),
                 scratch_shapes=[pltpu.VMEM((tt + 2 * HALO, cw), F32)], input_output_aliases=aliases,
                 compiler_params=_params("parallel", "parallel"))(*args)


def _conv_bwd(zx, dpre, conv_w, d_inner, into, name, comm=None):
    T = zx.shape[0]
    d_xbc = conv_w.shape[1]
    cw = _tile(d_xbc, 512)
    off = d_inner // cw
    tt = _tile(T, 512, 8)
    nt = T // tt
    pad = CONV_WIDTH // 2

    def body(zc, zp, zn, dc, dp, dn, w_ref, into_ref, din_ref, dw_ref, db_ref, zext, dext):
        i = pl.program_id(1)

        @pl.when(i == 0)
        def _():
            dw_ref[...] = jnp.zeros_like(dw_ref)
            db_ref[...] = jnp.zeros_like(db_ref)

        _fill_ext(zext, zc, zp, zn, i, nt, tt)
        _fill_ext(dext, dc, dp, dn, i, nt, tt)
        w = w_ref[...]
        d = dc[...]
        acc = None
        for k in range(CONV_WIDTH):
            term = dext[pl.ds(HALO + pad - k, tt), :] * w[k:k + 1, :]
            acc = term if acc is None else acc + term
            dw_ref[k:k + 1, :] += jnp.sum(d * zext[pl.ds(HALO + k - pad, tt), :], axis=0, keepdims=True)
        din_ref[...] = acc.astype(din_ref.dtype)
        db_ref[...] += jnp.sum(d, axis=0, keepdims=True)

    zspecs = _halo_specs(tt, cw, lambda j, i: off + j, nt, 1)
    dspecs = _halo_specs(tt, cw, lambda j, i: j, nt, 1)
    return _call(body, comm=comm, name=name, grid=(d_xbc // cw, nt),
                 in_specs=zspecs + dspecs + [pl.BlockSpec((CONV_WIDTH, cw), lambda j, i: (0, j)),
                                             pl.BlockSpec(memory_space=pl.ANY)],
                 out_specs=[pl.BlockSpec((tt, cw), lambda j, i: (i, off + j)),
                            pl.BlockSpec((CONV_WIDTH, cw), lambda j, i: (0, j)),
                            pl.BlockSpec((1, cw), lambda j, i: (0, j))],
                 out_shape=[jax.ShapeDtypeStruct(into.shape, into.dtype), jax.ShapeDtypeStruct((CONV_WIDTH, d_xbc), F32),
                            jax.ShapeDtypeStruct((1, d_xbc), F32)],
                 scratch_shapes=[pltpu.VMEM((tt + 2 * HALO, cw), F32), pltpu.VMEM((tt + 2 * HALO, cw), F32)],
                 input_output_aliases={7: 0},
                 compiler_params=_params("parallel", "arbitrary"))(zx, zx, zx, dpre, dpre, dpre, conv_w, into)


def _tri(n):
    r = lax.broadcasted_iota(jnp.int32, (n, n), 0)
    c = lax.broadcasted_iota(jnp.int32, (n, n), 1)
    return (r >= c).astype(F32), (r <= c).astype(F32)


def _dt_fwd(raw, bias, a_log, name):
    T, H2 = raw.shape
    half = H2 // 2

    def body(raw_ref, bias_ref, alog_ref, dt_ref, cs_ref):
        x = raw_ref[...] + bias_ref[...]
        dt = jnp.maximum(x, 0.0) + jnp.log(1.0 + jnp.exp(-jnp.abs(x)))
        a = dt * (-jnp.exp(alog_ref[...]))
        lower, upper = _tri(CHUNK)
        cs_f = _dot(lower, a, NN, HIGHEST)
        cs_b = _dot(upper, a, NN, HIGHEST)
        lane = lax.broadcasted_iota(jnp.int32, (CHUNK, H2), 1)
        dt_ref[...] = dt
        cs_ref[...] = jnp.where(lane < half, cs_f, cs_b)

    row = pl.BlockSpec((CHUNK, H2), lambda c: (c, 0))
    vec = pl.BlockSpec((1, H2), lambda c: (0, 0))
    return _call(body, name=name, grid=(T // CHUNK,), in_specs=[row, vec, vec], out_specs=[row, row],
                 out_shape=[jax.ShapeDtypeStruct((T, H2), F32)] * 2,
                 compiler_params=_params("parallel"))(raw, bias.reshape(1, H2), a_log.reshape(1, H2))


def _dt_bwd(raw, bias, a_log, dcs, dtot, dxdtx, into, name):
    T, H2 = raw.shape
    half = H2 // 2
    assert into.shape[1] % H2 == 0
    last = into.shape[1] // H2 - 1

    def body(raw_ref, bias_ref, alog_ref, dcs_ref, dtot_ref, dx_ref, into_ref, draw_ref, dbias_ref, dalog_ref):
        c = pl.program_id(0)

        @pl.when(c == 0)
        def _():
            dbias_ref[...] = jnp.zeros_like(dbias_ref)
            dalog_ref[...] = jnp.zeros_like(dalog_ref)

        x = raw_ref[...] + bias_ref[...]
        dt = jnp.maximum(x, 0.0) + jnp.log(1.0 + jnp.exp(-jnp.abs(x)))
        A = -jnp.exp(alog_ref[...])
        lower, upper = _tri(CHUNK)
        g = dcs_ref[...]
        lane = lax.broadcasted_iota(jnp.int32, (CHUNK, H2), 1)
        da = jnp.where(lane < half, _dot(upper, g, NN, HIGHEST), _dot(lower, g, NN, HIGHEST)) + dtot_ref[...]
        ddt = da * A + dx_ref[...]
        draw = ddt * _sigmoid(x)
        draw_ref[...] = draw.astype(draw_ref.dtype)
        dbias_ref[...] += jnp.sum(draw, axis=0, keepdims=True)
        dalog_ref[...] += jnp.sum(da * dt, axis=0, keepdims=True) * A

    row = pl.BlockSpec((CHUNK, H2), lambda c: (c, 0))
    vec = pl.BlockSpec((1, H2), lambda c: (0, 0))
    return _call(body, name=name, grid=(T // CHUNK,),
                 in_specs=[row, vec, vec, row, row, row, pl.BlockSpec(memory_space=pl.ANY)],
                 out_specs=[pl.BlockSpec((CHUNK, H2), lambda c: (c, last)), vec, vec],
                 out_shape=[jax.ShapeDtypeStruct(into.shape, into.dtype), jax.ShapeDtypeStruct((1, H2), F32),
                            jax.ShapeDtypeStruct((1, H2), F32)],
                 input_output_aliases={6: 0},
                 compiler_params=_params("arbitrary"))(raw, bias.reshape(1, H2), a_log.reshape(1, H2), dcs, dtot, dxdtx,
                                                       into)


def _cols(a, hg):
    T = a.shape[0]
    return a.reshape(T, 2, N_GROUPS, hg).transpose(1, 2, 0, 3)


def _rows(a, hg):
    T = a.shape[0]
    return a.reshape(T, 2, N_GROUPS, hg).transpose(1, 2, 3, 0)


def _uncols(a):
    T = a.shape[2]
    return a.transpose(2, 0, 1, 3).reshape(T, -1)


def _ssd_masks(d):
    r = lax.broadcasted_iota(jnp.int32, (CHUNK, CHUNK), 0)
    c = lax.broadcasted_iota(jnp.int32, (CHUNK, CHUNK), 1)
    return ((r >= c) & (d == 0)) | ((r <= c) & (d == 1))


def _head_expand(hg):
    r = lax.broadcasted_iota(jnp.int32, (hg, hg * HEAD_DIM), 0)
    c = lax.broadcasted_iota(jnp.int32, (hg, hg * HEAD_DIM), 1)
    return (c // HEAD_DIM == r).astype(F32)


def _head_select(hg):
    r = lax.broadcasted_iota(jnp.int32, (hg * HEAD_DIM, hg), 0)
    c = lax.broadcasted_iota(jnp.int32, (hg * HEAD_DIM, hg), 1)
    return (r // HEAD_DIM == c).astype(F32)


def _ssd_common(d, csc_ref, dtc_ref, hg):
    expand = _head_expand(hg)
    csx = _dot(csc_ref[...], expand, NN, HIGHEST)
    dtx = _dot(dtc_ref[...], expand, NN, HIGHEST)
    totx = jnp.where(d == 0, csx[CHUNK - 1:CHUNK, :], csx[0:1, :])
    return csx, dtx, totx


def _ssd_fwd(xbc, dtc, csc, csr, d_inner, name, comm=None):
    T = xbc.shape[0]
    nc = T // CHUNK
    gw = d_inner // N_GROUPS
    hg = gw // HEAD_DIM
    P, N = HEAD_DIM, D_STATE
    b_off = d_inner // N
    c_off = b_off + N_GROUPS

    def cidx(d, c):
        return c + d * (nc - 1 - 2 * c)

    def body(xs_ref, b_ref, c_ref, dtc_ref, csc_ref, csr_ref, y_ref, st_ref, h_ref):
        d = pl.program_id(0)
        c = pl.program_id(2)

        @pl.when(c == 0)
        def _():
            h_ref[...] = jnp.zeros_like(h_ref)

        Bb = b_ref[...].astype(BF16)
        Cb = c_ref[...].astype(BF16)
        S = _dot(Cb, Bb, NT)
        mask = _ssd_masks(d)
        csx, dtx, totx = _ssd_common(d, csc_ref, dtc_ref, hg)
        H = h_ref[...]
        st_ref[...] = H
        xdt = xs_ref[...] * dtx
        xdtb = xdt.astype(BF16)
        y_off = jnp.exp(csx) * _dot(Cb, H.astype(BF16), NN)
        for j in range(hg):
            sl = slice(j * P, (j + 1) * P)
            decay = jnp.exp(jnp.where(mask, csc_ref[:, j:j + 1] - csr_ref[j:j + 1, :], -jnp.inf))
            y_ref[:, sl] = _dot((S * decay).astype(BF16), xdtb[:, sl], NN) + y_off[:, sl]
        h_ref[...] = jnp.exp(totx) * H + _dot(Bb, (jnp.exp(totx - csx) * xdt).astype(BF16), TN)

    col = lambda d, g, c: (d, g, cidx(d, c), 0)
    return _call(
        body, comm=comm, name=name, grid=(2, N_GROUPS, nc),
        in_specs=[pl.BlockSpec((CHUNK, gw), lambda d, g, c: (cidx(d, c), g)),
                  pl.BlockSpec((CHUNK, N), lambda d, g, c: (cidx(d, c), b_off + g)),
                  pl.BlockSpec((CHUNK, N), lambda d, g, c: (cidx(d, c), c_off + g)),
                  pl.BlockSpec((None, None, CHUNK, hg), col),
                  pl.BlockSpec((None, None, CHUNK, hg), col),
                  pl.BlockSpec((None, None, hg, CHUNK), lambda d, g, c: (d, g, 0, cidx(d, c)))],
        out_specs=[pl.BlockSpec((None, CHUNK, gw), lambda d, g, c: (d, cidx(d, c), g)),
                   pl.BlockSpec((None, None, None, N, gw), lambda d, g, c: (d, cidx(d, c), g, 0, 0))],
        out_shape=[jax.ShapeDtypeStruct((2, T, d_inner), F32),
                   jax.ShapeDtypeStruct((2, nc, N_GROUPS, N, gw), F32)],
        scratch_shapes=[pltpu.VMEM((N, gw), F32)],
        compiler_params=_params("parallel", "parallel", "arbitrary"),
    )(xbc, xbc, xbc, dtc, csc, csr)


def _ssd_bwd(xbc, dtc, csc, csr, states, y2, dy, d_inner, name, comm=None):
    T = xbc.shape[0]
    nc = T // CHUNK
    gw = d_inner // N_GROUPS
    hg = gw // HEAD_DIM
    P, N = HEAD_DIM, D_STATE
    b_off = d_inner // N
    c_off = b_off + N_GROUPS

    def cidx(d, c):
        return (nc - 1 - c) + d * (2 * c - nc + 1)

    def body(xs_ref, b_ref, c_ref, dtc_ref, csc_ref, csr_ref, st_ref, y_ref, dy_ref,
             dxs_ref, db_ref, dc_ref, dcs_ref, dtot_ref, dxdtx_ref, dh_ref, dxdt_ref):
        d = pl.program_id(0)
        c = pl.program_id(2)

        @pl.when(c == 0)
        def _():
            dh_ref[...] = jnp.zeros_like(dh_ref)

        Bb = b_ref[...].astype(BF16)
        Cb = c_ref[...].astype(BF16)
        S = _dot(Cb, Bb, NT)
        mask = _ssd_masks(d)
        csx, dtx, totx = _ssd_common(d, csc_ref, dtc_ref, hg)
        select = _head_select(hg)
        X = xs_ref[...]
        xdt = X * dtx
        xdtb = xdt.astype(BF16)
        dY = dy_ref[...]
        dYb = dY.astype(BF16)
        Hp = st_ref[...]
        Hpb = Hp.astype(BF16)
        dH = dh_ref[...]
        dHb = dH.astype(BF16)
        e_tot = jnp.exp(totx)
        dCH = (jnp.exp(csx) * dY).astype(BF16)
        dC = _dot(dCH, Hpb, NT)
        dHp = _dot(Cb, dCH, TN)
        Q = _dot(Bb, dHb, NN)
        dte = jnp.exp(totx - csx)
        wx = dte * xdt
        dB = _dot(wx.astype(BF16), dHb, NT)
        ddte = Q * wx
        dS = jnp.zeros((CHUNK, CHUNK), F32)
        for j in range(hg):
            sl = slice(j * P, (j + 1) * P)
            decay = jnp.exp(jnp.where(mask, csc_ref[:, j:j + 1] - csr_ref[j:j + 1, :], -jnp.inf))
            dS = dS + _dot(dYb[:, sl], xdtb[:, sl], NT) * decay
            dxdt_ref[:, sl] = _dot((S * decay).astype(BF16), dYb[:, sl], TN)
        dxdt_diag = dxdt_ref[...]
        dxdt = dxdt_diag + dte * Q
        dcs_ref[...] = _dot(dYb.astype(F32) * y_ref[...] - xdtb.astype(F32) * dxdt_diag - ddte, select, NN, HIGHEST)
        dtot_row = (jnp.sum(ddte, axis=0, keepdims=True) + e_tot * jnp.sum(dH * Hp, axis=0, keepdims=True))
        dtot_ref[...] = jnp.zeros((CHUNK, hg), F32) + _dot(dtot_row, select, NN, HIGHEST)
        dxdtx_ref[...] = _dot(dxdt * X, select, NN, HIGHEST)
        dxs_ref[...] = dxdt * dtx
        dh_ref[...] = e_tot * dH + dHp
        dSb = dS.astype(BF16)
        dc_ref[...] = dC + _dot(dSb, Bb, NN)
        db_ref[...] = dB + _dot(dSb, Cb, TN)

    col = lambda d, g, c: (d, g, cidx(d, c), 0)
    colspec = pl.BlockSpec((None, None, CHUNK, hg), col)
    rowblk = pl.BlockSpec((None, CHUNK, gw), lambda d, g, c: (d, cidx(d, c), g))
    return _call(
        body, comm=comm, name=name, grid=(2, N_GROUPS, nc),
        in_specs=[pl.BlockSpec((CHUNK, gw), lambda d, g, c: (cidx(d, c), g)),
                  pl.BlockSpec((CHUNK, N), lambda d, g, c: (cidx(d, c), b_off + g)),
                  pl.BlockSpec((CHUNK, N), lambda d, g, c: (cidx(d, c), c_off + g)),
                  colspec, colspec,
                  pl.BlockSpec((None, None, hg, CHUNK), lambda d, g, c: (d, g, 0, cidx(d, c))),
                  pl.BlockSpec((None, None, None, N, gw), lambda d, g, c: (d, cidx(d, c), g, 0, 0)),
                  rowblk,
                  pl.BlockSpec((CHUNK, gw), lambda d, g, c: (cidx(d, c), g))],
        out_specs=[rowblk,
                   pl.BlockSpec((None, CHUNK, N), lambda d, g, c: (d, cidx(d, c), g)),
                   pl.BlockSpec((None, CHUNK, N), lambda d, g, c: (d, cidx(d, c), g)),
                   colspec, colspec, colspec],
        out_shape=[jax.ShapeDtypeStruct((2, T, d_inner), F32),
                   jax.ShapeDtypeStruct((2, T, N_GROUPS * N), F32),
                   jax.ShapeDtypeStruct((2, T, N_GROUPS * N), F32)]
        + [jax.ShapeDtypeStruct((2, N_GROUPS, T, hg), F32)] * 3,
        scratch_shapes=[pltpu.VMEM((N, gw), F32), pltpu.VMEM((CHUNK, gw), F32)],
        compiler_params=_params("parallel", "parallel", "arbitrary"),
    )(xbc, xbc, xbc, dtc, csc, csr, states, y2, dy)


def _gnorm_fwd(y2, xbc, zx, dvec, nw, d_inner, name, comm=None):
    T = xbc.shape[0]
    gw = d_inner // N_GROUPS
    tm = _tile(T, 128, 8)

    def body(y_ref, xs_ref, z_ref, d_ref, w_ref, o_ref):
        for g in range(N_GROUPS):
            sl = slice(g * gw, (g + 1) * gw)
            y = y_ref[0, :, sl] + y_ref[1, :, sl] + xs_ref[:, sl] * d_ref[:, sl]
            z = z_ref[:, sl]
            gy = y * (z * _sigmoid(z))
            rs = lax.rsqrt(jnp.mean(gy * gy, axis=-1, keepdims=True) + RMS_EPS)
            o_ref[:, sl] = (gy * rs * w_ref[:, sl]).astype(o_ref.dtype)

    row = pl.BlockSpec((tm, d_inner), lambda i: (i, 0))
    vec = pl.BlockSpec((1, d_inner), lambda i: (0, 0))
    return _call(body, comm=comm, name=name, grid=(T // tm,),
                 in_specs=[pl.BlockSpec((2, tm, d_inner), lambda i: (0, i, 0)), row, row, vec, vec],
                 out_specs=row, out_shape=jax.ShapeDtypeStruct((T, d_inner), BF16),
                 compiler_params=_params("parallel"))(y2, xbc, zx, dvec, nw)


def _gnorm_bwd(y2, xbc, zx, dvec, nw, dgn, sel, d_inner, name, comm=None):
    T = xbc.shape[0]
    gw = d_inner // N_GROUPS
    n_heads = d_inner // HEAD_DIM
    tm = _tile(T, 128, 8)
    n_tiles = T // tm

    def body(y_ref, xs_ref, z_ref, d_ref, w_ref, dg_ref, sel_ref, dy_ref, dxs_ref, dz_ref, dw_ref, dd_ref, dch_ref):
        i = pl.program_id(0)

        @pl.when(i == 0)
        def _():
            dw_ref[...] = jnp.zeros_like(dw_ref)
            dch_ref[...] = jnp.zeros_like(dch_ref)

        for g in range(N_GROUPS):
            sl = slice(g * gw, (g + 1) * gw)
            xs = xs_ref[:, sl]
            y = y_ref[0, :, sl] + y_ref[1, :, sl] + xs * d_ref[:, sl]
            z = z_ref[:, sl]
            sig = _sigmoid(z)
            sz = z * sig
            gy = y * sz
            rs = lax.rsqrt(jnp.mean(gy * gy, axis=-1, keepdims=True) + RMS_EPS)
            n = gy * rs
            dout = dg_ref[:, sl]
            dw_ref[:, sl] += jnp.sum(dout * n, axis=0, keepdims=True)
            dn = dout * w_ref[:, sl]
            dgy = rs * (dn - n * jnp.mean(dn * n, axis=-1, keepdims=True))
            dy = dgy * sz
            dy_ref[:, sl] = dy
            dz_ref[:, sl] = (dgy * y * (sig * (1.0 + z * (1.0 - sig)))).astype(dz_ref.dtype)
            dxs_ref[:, sl] = dy * d_ref[:, sl]
            dch_ref[:, sl] += jnp.sum(dy * xs, axis=0, keepdims=True)

        @pl.when(i == n_tiles - 1)
        def _():
            dd_ref[...] = _dot(dch_ref[...], sel_ref[...], NN, HIGHEST)

    row = pl.BlockSpec((tm, d_inner), lambda i: (i, 0))
    vec = pl.BlockSpec((1, d_inner), lambda i: (0, 0))
    hvec = pl.BlockSpec((1, n_heads), lambda i: (0, 0))
    return _call(body, comm=comm, name=name, grid=(n_tiles,),
                 in_specs=[pl.BlockSpec((2, tm, d_inner), lambda i: (0, i, 0)), row, row, vec, vec, row,
                           pl.BlockSpec((d_inner, n_heads), lambda i: (0, 0))],
                 out_specs=[row, row, row, vec, hvec],
                 out_shape=[jax.ShapeDtypeStruct((T, d_inner), F32), jax.ShapeDtypeStruct((T, d_inner), F32),
                            jax.ShapeDtypeStruct(zx.shape, BF16),
                            jax.ShapeDtypeStruct((1, d_inner), F32), jax.ShapeDtypeStruct((1, n_heads), F32)],
                 scratch_shapes=[pltpu.VMEM((1, d_inner), F32)],
                 compiler_params=_params("arbitrary"))(y2, xbc, zx, dvec, nw, dgn, sel)


def _pool_counts(i, tt, T, win, rows, row0):
    t = i * tt + row0 + lax.broadcasted_iota(jnp.int32, (rows, 1), 0)
    start = t - win // 2
    lo = jnp.clip(start, 0, T)
    hi = jnp.clip(start + win, 0, T)
    return jnp.maximum(hi - lo, 1).astype(F32)


def _pool_features(ext_ref, i, tt, T, gi, gd):
    win = POOL_WINDOWS[gi]
    sl = slice(gi * gd, (gi + 1) * gd)
    acc = None
    for o in range(-(win // 2), win - win // 2):
        term = ext_ref[pl.ds(HALO + o, tt), sl]
        acc = term if acc is None else acc + term
    return acc / _pool_counts(i, tt, T, win, tt, 0) - ext_ref[pl.ds(HALO, tt), sl]


def _pool_fwd(u, w, bias, scale, name):
    T, D = u.shape
    ng = len(POOL_WINDOWS)
    gd = D // ng
    tt = _tile(T, 512, 8)
    nt = T // tt

    def body(cur, prev, nxt, w_ref, b_ref, s_ref, o_ref, ext):
        i = pl.program_id(0)
        _fill_ext(ext, cur, prev, nxt, i, nt, tt)
        for gi in range(ng):
            sl = slice(gi * gd, (gi + 1) * gd)
            m = _pool_features(ext, i, tt, T, gi, gd)
            pre = _dot(m.astype(BF16), w_ref[gi], NN) + b_ref[:, sl]
            o_ref[:, sl] = pre * s_ref[:, sl]

    vec = pl.BlockSpec((1, D), lambda i: (0, 0))
    return _call(body, name=name, grid=(nt,),
                 in_specs=_halo_specs(tt, D, lambda i: 0, nt, 0)
                 + [pl.BlockSpec((ng, gd, gd), lambda i: (0, 0, 0)), vec, vec],
                 out_specs=pl.BlockSpec((tt, D), lambda i: (i, 0)),
                 out_shape=jax.ShapeDtypeStruct((T, D), F32),
                 scratch_shapes=[pltpu.VMEM((tt + 2 * HALO, D), F32)],
                 compiler_params=_params("parallel"))(u, u, u, w, bias, scale)


def _pool_bwd_a(u, w, bias, scale, dy, name):
    T, D = u.shape
    ng = len(POOL_WINDOWS)
    gd = D // ng
    tt = _tile(T, 512, 8)
    nt = T // tt

    def body(cur, prev, nxt, w_ref, b_ref, s_ref, dy_ref, dm_ref, dw_ref, db_ref, ds_ref, ext):
        i = pl.program_id(0)

        @pl.when(i == 0)
        def _():
            dw_ref[...] = jnp.zeros_like(dw_ref)
            db_ref[...] = jnp.zeros_like(db_ref)
            ds_ref[...] = jnp.zeros_like(ds_ref)

        _fill_ext(ext, cur, prev, nxt, i, nt, tt)
        for gi in range(ng):
            sl = slice(gi * gd, (gi + 1) * gd)
            mb = _pool_features(ext, i, tt, T, gi, gd).astype(BF16)
            wg = w_ref[gi]
            pre = _dot(mb, wg, NN) + b_ref[:, sl]
            dy_ = dy_ref[:, sl]
            ds_ref[:, sl] += jnp.sum(dy_ * pre, axis=0, keepdims=True)
            dpre = dy_ * s_ref[:, sl]
            db_ref[:, sl] += jnp.sum(dpre, axis=0, keepdims=True)
            dpb = dpre.astype(BF16)
            dw_ref[gi] += _dot(mb, dpb, TN)
            dm_ref[:, sl] = _dot(dpb, wg, NT)

    vec = pl.BlockSpec((1, D), lambda i: (0, 0))
    row = pl.BlockSpec((tt, D), lambda i: (i, 0))
    wspec = pl.BlockSpec((ng, gd, gd), lambda i: (0, 0, 0))
    return _call(body, name=name, grid=(nt,),
                 in_specs=_halo_specs(tt, D, lambda i: 0, nt, 0) + [wspec, vec, vec, row],
                 out_specs=[row, wspec, vec, vec],
                 out_shape=[jax.ShapeDtypeStruct((T, D), F32), jax.ShapeDtypeStruct((ng, gd, gd), F32),
                            jax.ShapeDtypeStruct((1, D), F32), jax.ShapeDtypeStruct((1, D), F32)],
                 scratch_shapes=[pltpu.VMEM((tt + 2 * HALO, D), F32)],
                 compiler_params=_params("arbitrary"))(u, u, u, w, bias, scale, dy)


def _pool_bwd_b(dm, dy, alpha, name):
    T, D = dm.shape
    ng = len(POOL_WINDOWS)
    gd = D // ng
    tt = _tile(T, 512, 8)
    nt = T // tt

    def body(cur, prev, nxt, dy_ref, o_ref, ext):
        i = pl.program_id(0)
        _fill_ext(ext, cur, prev, nxt, i, nt, tt)
        for gi, win in enumerate(POOL_WINDOWS):
            sl = slice(gi * gd, (gi + 1) * gd)
            rows = tt + 2 * HALO
            ext[:, sl] = ext[:, sl] / _pool_counts(i, tt, T, win, rows, -HALO)
            acc = None
            for o in range(-(win // 2) + 1, win // 2 + 1):
                term = ext[pl.ds(HALO + o, tt), sl]
                acc = term if acc is None else acc + term
            o_ref[:, sl] = alpha * dy_ref[:, sl] + acc - cur[:, sl]

    row = pl.BlockSpec((tt, D), lambda i: (i, 0))
    return _call(body, name=name, grid=(nt,),
                 in_specs=_halo_specs(tt, D, lambda i: 0, nt, 0) + [row], out_specs=row,
                 out_shape=jax.ShapeDtypeStruct((T, D), F32),
                 scratch_shapes=[pltpu.VMEM((tt + 2 * HALO, D), F32)],
                 compiler_params=_params("parallel"))(dm, dm, dm, dy)


def _exchange(comm, name):
    ops, arrays, shapes, sems, in_place = _comm_plan(comm)
    n = len(ops)

    def body(*refs):
        copies = _comm_copies(ops, refs[:n], refs[n:2 * n], *refs[2 * n:])
        for cp in copies:
            cp.start()
        for cp in copies:
            cp.wait()

    any_spec = pl.BlockSpec(memory_space=pl.ANY)
    return _pallas(body, name=name, in_specs=[any_spec] * n, out_specs=[any_spec] * n, out_shape=shapes,
                   scratch_shapes=sems, input_output_aliases={a: a for a in in_place})(*arrays)


def _adamw(piece, w, m, v, row0, into, name):
    R, C = w.shape
    rows = piece.shape[1]
    tr = _tile(rows, max(8, (1 << 18) // C // 8 * 8), 8)
    assert row0 % tr == 0 and rows % tr == 0
    off = row0 // tr

    def body(p_ref, w_ref, m_ref, v_ref, *rest):
        g_ref, d_ref, nm_ref, nv_ref = rest[-4:]
        g = p_ref[0].astype(F32)
        for i in range(1, N_DEV):
            g = g + p_ref[i].astype(F32)
        mm = ADAM_B1 * m_ref[...] + (1.0 - ADAM_B1) * g
        vv = ADAM_B2 * v_ref[...] + (1.0 - ADAM_B2) * (g * g)
        m_hat = mm / (1.0 - ADAM_B1 ** ADAM_STEP)
        v_hat = vv / (1.0 - ADAM_B2 ** ADAM_STEP)
        g_ref[...] = g
        d_ref[...] = -ADAM_LR * (m_hat / (jnp.sqrt(v_hat) + ADAM_EPS) + ADAM_WD * w_ref[...])
        nm_ref[...] = mm
        nv_ref[...] = vv

    row = pl.BlockSpec((tr, C), lambda i: (off + i, 0))
    kept = [] if into is None else list(into)
    return _call(body, name=name, grid=(rows // tr,),
                 in_specs=[pl.BlockSpec((N_DEV, tr, C), lambda i: (0, i, 0)), row, row, row]
                 + [pl.BlockSpec(memory_space=pl.ANY)] * len(kept),
                 out_specs=[row] * 4, out_shape=[jax.ShapeDtypeStruct((R, C), F32)] * 4,
                 input_output_aliases={4 + q: q for q in range(len(kept))},
                 compiler_params=_params("parallel"))(piece, w, m, v, *kept)


def _pack(arrays):
    flat, meta, off = [], [], 0
    for a in arrays:
        flat.append(a.reshape(-1).astype(F32))
        meta.append((off, a.shape))
        off += a.size
    total = -(-off // (8 * LANES)) * (8 * LANES)
    flat.append(jnp.zeros((total - off,), F32))
    return jnp.concatenate(flat).reshape(total // LANES, LANES), meta


def _unpack(packed, meta):
    flat = packed.reshape(-1)
    return [flat[off:off + math.prod(shape)].reshape(shape) for off, shape in meta]


def kernel(x, ssd_in_proj, ssd_conv_w, ssd_conv_b, ssd_dt_bias, ssd_A_log, ssd_D, ssd_norm_w, ssd_out_proj, pool_w, pool_b, pool_scale, mlp_w1, mlp_w2, ln_mix_g, ln_mix_b, ln_ffn_g, ln_ffn_b, loss_target, m_ssd_in_proj, m_ssd_conv_w, m_ssd_conv_b, m_ssd_dt_bias, m_ssd_A_log, m_ssd_D, m_ssd_norm_w, m_ssd_out_proj, m_pool_w, m_pool_b, m_pool_scale, m_mlp_w1, m_mlp_w2, m_ln_mix_g, m_ln_mix_b, m_ln_ffn_g, m_ln_ffn_b, v_ssd_in_proj, v_ssd_conv_w, v_ssd_conv_b, v_ssd_dt_bias, v_ssd_A_log, v_ssd_D, v_ssd_norm_w, v_ssd_out_proj, v_pool_w, v_pool_b, v_pool_scale, v_mlp_w1, v_mlp_w2, v_ln_mix_g, v_ln_mix_b, v_ln_ffn_g, v_ln_ffn_b):
    T, D = x.shape[1], x.shape[2]
    depth = mlp_w1.shape[0]
    n_ssd, n_pool = ssd_in_proj.shape[0], pool_w.shape[0]
    d_inner = ssd_out_proj.shape[1] * N_DEV
    n_heads = d_inner // HEAD_DIM
    hg = n_heads // N_GROUPS
    d_bc = N_GROUPS * D_STATE
    d_xbc = d_inner + 2 * d_bc
    d_in_proj = ssd_in_proj.shape[2] * N_DEV
    d_ff = mlp_w1.shape[2] * N_DEV
    ng = len(POOL_WINDOWS)
    gd = D // ng
    alpha = (2.0 * depth) ** 0.25
    x0 = x.reshape(T, D)
    target = loss_target.reshape(T, D)

    assert depth == 4 and n_ssd == 2 and n_pool == 2, "the exchange schedules below are written for this stack"

    small_pack, small_meta = _pack([ssd_conv_w, pool_b, pool_scale])
    pw_rows = pool_w.shape[1] * pool_w.shape[2]
    in_b, out_b = ssd_in_proj.astype(BF16), ssd_out_proj.astype(BF16)
    pw_b = pool_w.reshape(n_pool, pw_rows, gd).astype(BF16)
    w1_b, w2_b = mlp_w1.astype(BF16), mlp_w2.astype(BF16)
    shard = {("in", 0): in_b[0], ("in", 1): in_b[1], ("out", 0): out_b[0], ("out", 1): out_b[1],
             ("pool", 0): pw_b[0], ("pool", 1): pw_b[1]}
    for i in range(depth):
        shard["w1", i], shard["w2", i] = w1_b[i], w2_b[i]

    def full_cols(g):
        return g.transpose(1, 0, 2).reshape(g.shape[1], -1)

    def full_rows(g):
        return g.reshape(-1, g.shape[-1])

    def full_pool(g):
        return g.reshape(N_DEV, ng, gd // N_DEV, gd).transpose(1, 0, 2, 3).reshape(ng, gd, gd)

    def slab_cols(g):
        return g.reshape(g.shape[0], N_DEV, -1).transpose(1, 0, 2)

    def slab_rows(g):
        return g.reshape(N_DEV, -1, g.shape[-1])

    def slab_pool(g):
        return g.astype(BF16).reshape(ng, N_DEV, gd // N_DEV, gd).transpose(1, 0, 2, 3).reshape(N_DEV, pw_rows, gd)

    to_full = {"in": full_cols, "out": full_rows, "pool": full_pool, "w1": lambda g: g, "w2": full_rows}

    gather1_on = {"in_proj_0": [("w1", 0), ("pool", 0)], "conv_fwd_0": [("out", 0)],
                  "ssd_fwd_0": [("w2", 0), ("w1", 1), ("w2", 1)], "out_proj_0": [("out", 1)], "mlp_up_0": [("in", 1)],
                  "mlp_down_0": [("w1", 2)], "mlp_up_1": [("w2", 2)], "mlp_down_1": [("w1", 3), ("pool", 1)],
                  "in_proj_2": [("w2", 3)]}
    gather2_on = {"conv_fwd_0": [("w1", 0), ("pool", 0)], "ssd_fwd_0": [("out", 0)],
                  "gnorm_fwd_0": [("w2", 0), ("w1", 1), ("w2", 1)], "mlp_up_0": [("out", 1)], "mlp_down_0": [("in", 1)],
                  "mlp_up_1": [("w1", 2)], "mlp_down_1": [("w2", 2)], "in_proj_2": [("w1", 3), ("pool", 1)],
                  "conv_fwd_2": [("w2", 3)]}
    hide_us = {"mlp_down_dx": 161, "mlp_down_dw": 163, "mlp_up_dx": 170, "mlp_up_dw": 164, "out_proj_dx": 83,
               "out_proj_dw": 85, "gnorm_bwd": 163, "ssd_bwd": 1104, "conv_bwd": 228, "in_proj_dw": 230, "in_proj_dx": 267}
    SLAB_BYTES_PER_US = 70e3 / (N_DEV - 1)
    OVERRUN_US, MIN_CARRIER_US = 0, 80
    n_pieces = {"in": 4, "out": 2, "pool": 1, "w1": 4, "w2": 4}
    W, half, G, R = {}, {}, {}, {}
    queue = []

    def produced(key, slabs):
        G[key] = slabs
        rows = slabs.shape[1] // n_pieces[key[0]]
        cost = rows * math.prod(slabs.shape[2:]) * slabs.dtype.itemsize / SLAB_BYTES_PER_US
        queue.extend((key, p * rows, rows, cost) for p in range(n_pieces[key[0]]))

    def take(budget):
        taken, used = [], 0.0
        while queue and ((not taken and budget >= MIN_CARRIER_US) or used + queue[0][3] <= budget + OVERRUN_US):
            taken.append(queue.pop(0))
            used += taken[-1][3]
        return taken

    def run(fn, *args, name, **kw):
        k1, k2 = gather1_on.get(name, []), gather2_on.get(name, [])
        pieces = take(hide_us.get(name.rsplit("_", 1)[0], 0)) if not (k1 or k2) else []
        comm = ([("gather1", shard[k]) for k in k1] + [("gather2", half[k]) for k in k2]
                + [("slabs", G[key], r0, rows) for key, r0, rows, _ in pieces])
        if not comm:
            return fn(*args, name=name, **kw)
        res, got = fn(*args, name=name, comm=comm, **kw)
        for k, g in zip(k1, got):
            half[k] = g
        for k, g in zip(k2, got[len(k1):]):
            W[k] = to_full[k[0]](g)
        for (key, r0, _, _), g in zip(pieces, got):
            R.setdefault(key, []).append((r0, g))
        return res

    half_in0, g_small = _exchange([("gather1", shard["in", 0]), ("gather", small_pack)], "gather_first")
    (g_in0,) = _exchange([("gather2", half_in0)], "gather_first_onward")
    W["in", 0] = full_cols(g_in0)
    smalls = [_unpack(g_small[k], small_meta) for k in range(N_DEV)]
    conv_w = jnp.concatenate([s[0] for s in smalls], axis=-1).reshape(n_ssd, CONV_WIDTH, d_xbc)
    pool_bias = jnp.concatenate([s[1] for s in smalls], axis=-1).reshape(n_pool, 1, D)
    pool_sc = jnp.concatenate([s[2] for s in smalls], axis=-1).reshape(n_pool, 1, D)

    sel = (jnp.arange(d_inner)[:, None] // HEAD_DIM == jnp.arange(n_heads)[None, :]).astype(F32)

    saved = []
    h, hb = x0, x0.astype(BF16)
    for i in range(depth):
        j = i // 2
        s = {}
        s["x0"], s["x0b"] = h, hb
        if i % 2 == 0:
            (zx,) = run(_matmul, hb, W["in", j], "nn", name=f"in_proj_{i}", outs=[F32], tn=1152)
            xbc = run(_conv_fwd, zx, conv_w[j], ssd_conv_b[j], d_inner, name=f"conv_fwd_{i}")
            raw = zx[:, d_inner + d_xbc:]
            dt, cs = _dt_fwd(raw, ssd_dt_bias[j], ssd_A_log[j], f"dt_fwd_{i}")
            dtc, csc, csr = _cols(dt, hg), _cols(cs, hg), _rows(cs, hg)
            y2, states = run(_ssd_fwd, xbc, dtc, csc, csr, d_inner, name=f"ssd_fwd_{i}")
            dvec = jnp.repeat(ssd_D[j], HEAD_DIM).reshape(1, d_inner)
            nw = ssd_norm_w[j].reshape(1, d_inner)
            gn = run(_gnorm_fwd, y2, xbc, zx, dvec, nw, d_inner, name=f"gnorm_fwd_{i}")
            (mix,) = run(_matmul, gn, W["out", j], "nn", name=f"out_proj_{i}", outs=[F32])
            s.update(zx=zx, xbc=xbc, raw=raw, dtc=dtc, csc=csc, csr=csr, y2=y2, states=states, dvec=dvec, nw=nw, gn=gn)
        else:
            mix = _pool_fwd(h, W["pool", j], pool_bias[j], pool_sc[j], f"pool_fwd_{i}")
        s["mix"] = mix
        x1, x1b = _ln_fwd(h, mix, ln_mix_g[i], ln_mix_b[i], alpha, f"ln_mix_fwd_{i}")
        u, hh = run(_matmul, x1b, W["w1", i], "nn", name=f"mlp_up_{i}", outs=[F32, BF16],
                    epilogue=lambda acc: (acc, jnp.square(jnp.maximum(acc, 0.0))), cols_by_device=True)
        (m2,) = run(_matmul, hh, W["w2", i], "nn", name=f"mlp_down_{i}", outs=[F32])
        x2, x2b = _ln_fwd(x1, m2, ln_ffn_g[i], ln_ffn_b[i], alpha, f"ln_ffn_fwd_{i}")
        s.update(x1=x1, x1b=x1b, u=u, hh=hh, m2=m2)
        saved.append(s)
        h, hb = x2, x2b

    loss_row, dh = _loss_head(h, target)
    loss = lax.psum(loss_row[0, 0], ("x", "y", "c"))

    big = {"in": (ssd_in_proj, m_ssd_in_proj, v_ssd_in_proj), "out": (ssd_out_proj, m_ssd_out_proj, v_ssd_out_proj),
           "pool": (pool_w, m_pool_w, v_pool_w), "w1": (mlp_w1, m_mlp_w1, v_mlp_w1), "w2": (mlp_w2, m_mlp_w2, v_mlp_w2)}

    def update(kind, count):
        shape = big[kind][0].shape
        w, m, v = (a.reshape(-1, shape[-1]) for a in big[kind])
        per_layer = w.shape[0] // count
        outs = None
        for l in range(count):
            for r0, g in sorted(R[kind, l], key=lambda t: t[0]):
                outs = _adamw(g, w, m, v, l * per_layer + r0, outs, f"adamw_{kind}_{l}_{r0}")
        return [o.reshape(shape) for o in outs]

    gr = {k: [None] * depth for k in ("ln_mix_g", "ln_mix_b", "ln_ffn_g", "ln_ffn_b")}
    gs = {k: [None] * n_ssd for k in ("conv_w", "conv_b", "dt_bias", "A_log", "D", "norm_w")}
    gp = {k: [None] * n_pool for k in ("b", "scale")}
    for i in reversed(range(depth)):
        j = i // 2
        s = saved[i]
        ds2, ds2b, gr["ln_ffn_g"][i], gr["ln_ffn_b"][i] = _ln_bwd(s["x1"], s["m2"], ln_ffn_g[i], dh, alpha, f"ln_ffn_bwd_{i}")
        (du,) = run(_matmul, ds2b, W["w2", i], "nt", name=f"mlp_down_dx_{i}", outs=[BF16], extras=[s["u"]],
                    epilogue=lambda acc, u_: (acc * (2.0 * jnp.maximum(u_, 0.0)),))
        (g_w2,) = run(_matmul, s["hh"], ds2b, "tn", name=f"mlp_down_dw_{i}", outs=[BF16])
        produced(("w2", i), slab_rows(g_w2))
        (dx1,) = run(_matmul, du, W["w1", i], "nt", name=f"mlp_up_dx_{i}", outs=[F32], extras=[ds2],
                     epilogue=lambda acc, e: (acc + alpha * e,), cols_by_device=True)
        (g_w1,) = run(_matmul, s["x1b"], du, "tn", name=f"mlp_up_dw_{i}", outs=[BF16], cols_by_device=True)
        produced(("w1", i), g_w1)
        ds1, ds1b, gr["ln_mix_g"][i], gr["ln_mix_b"][i] = _ln_bwd(s["x0"], s["mix"], ln_mix_g[i], dx1, alpha, f"ln_mix_bwd_{i}")
        if i % 2 == 0:
            (dgn,) = run(_matmul, ds1b, W["out", j], "nt", name=f"out_proj_dx_{i}", outs=[F32])
            (g_out,) = run(_matmul, s["gn"], ds1b, "tn", name=f"out_proj_dw_{i}", outs=[BF16])
            produced(("out", j), slab_rows(g_out))
            dy, dxs_d, dzx, gs["norm_w"][j], gs["D"][j] = run(
                _gnorm_bwd, s["y2"], s["xbc"], s["zx"], s["dvec"], s["nw"], dgn, sel, d_inner, name=f"gnorm_bwd_{i}")
            dxs, dB, dC, dcs, dtot, dxdtx = run(_ssd_bwd, s["xbc"], s["dtc"], s["csc"], s["csr"], s["states"], s["y2"], dy,
                                                d_inner, name=f"ssd_bwd_{i}")
            dzx, gs["dt_bias"][j], gs["A_log"][j] = _dt_bwd(
                s["raw"], ssd_dt_bias[j], ssd_A_log[j], _uncols(dcs), _uncols(dtot), _uncols(dxdtx), dzx, f"dt_bwd_{i}")
            dpre = _conv_dpre(s["zx"], conv_w[j], ssd_conv_b[j], d_inner, 0, dxs, dxs_d, f"conv_dpre_x_{i}")
            dpre = _conv_dpre(s["zx"], conv_w[j], ssd_conv_b[j], d_inner, d_inner, dB, None, f"conv_dpre_b_{i}", into=dpre)
            dpre = _conv_dpre(s["zx"], conv_w[j], ssd_conv_b[j], d_inner, d_inner + d_bc, dC, None, f"conv_dpre_c_{i}",
                              into=dpre)
            dzx, gs["conv_w"][j], gs["conv_b"][j] = run(_conv_bwd, s["zx"], dpre, conv_w[j], d_inner, dzx,
                                                        name=f"conv_bwd_{i}")
            (g_in,) = run(_matmul, s["x0b"], dzx, "tn", name=f"in_proj_dw_{i}", outs=[BF16], tn=1152)
            produced(("in", j), slab_cols(g_in))
            (dh,) = run(_matmul, dzx, W["in", j], "nt", name=f"in_proj_dx_{i}", outs=[F32], extras=[ds1], tk=1152,
                        epilogue=lambda acc, e: (acc + alpha * e,))
        else:
            dm, g_pw, gp["b"][j], gp["scale"][j] = _pool_bwd_a(
                s["x0"], W["pool", j], pool_bias[j], pool_sc[j], ds1, f"pool_bwd_a_{i}")
            produced(("pool", j), slab_pool(g_pw))
            dh = _pool_bwd_b(dm, ds1, alpha, f"pool_bwd_b_{i}")
    grad_x = dh.reshape(x.shape)

    g_conv_w = jnp.stack(gs["conv_w"]).reshape(n_ssd, CONV_WIDTH, 1, N_DEV, d_xbc // N_DEV)
    g_pool_b = jnp.stack(gp["b"]).reshape(n_pool, ng, N_DEV, gd // N_DEV)
    g_pool_s = jnp.stack(gp["scale"]).reshape(n_pool, N_DEV, D // N_DEV)
    s_small = jnp.stack([_pack([g_conv_w[:, :, :, k], g_pool_b[:, :, k], g_pool_s[:, k]])[0] for k in range(N_DEV)])
    repl_grads = [jnp.stack(gs["conv_b"]).reshape(ssd_conv_b.shape), jnp.stack(gs["dt_bias"]).reshape(ssd_dt_bias.shape),
                  jnp.stack(gs["A_log"]).reshape(ssd_A_log.shape), jnp.stack(gs["D"]).reshape(ssd_D.shape),
                  jnp.stack(gs["norm_w"]).reshape(ssd_norm_w.shape),
                  jnp.stack(gr["ln_mix_g"]).reshape(ln_mix_g.shape), jnp.stack(gr["ln_mix_b"]).reshape(ln_mix_b.shape),
                  jnp.stack(gr["ln_ffn_g"]).reshape(ln_ffn_g.shape), jnp.stack(gr["ln_ffn_b"]).reshape(ln_ffn_b.shape)]
    repl_pack, repl_meta = _pack(repl_grads)
    s_repl = jnp.broadcast_to(repl_pack[None], (N_DEV,) + repl_pack.shape)
    left = list(queue)
    del queue[:]
    got = _exchange([("slabs", s_small, 0, s_small.shape[1]), ("slabs", s_repl, 0, s_repl.shape[1])]
                    + [("slabs", G[key], r0, rows) for key, r0, rows, _ in left], "exchange_last")
    r_small, r_repl = got[0], got[1]
    for (key, r0, _, _), g in zip(left, got[2:]):
        R.setdefault(key, []).append((r0, g))

    upd = {}
    for nm, kind, count in (("ssd_in_proj", "in", n_ssd), ("ssd_out_proj", "out", n_ssd), ("pool_w", "pool", n_pool),
                            ("mlp_w1", "w1", depth), ("mlp_w2", "w2", depth)):
        upd[nm] = update(kind, count)
    sm = _adamw(r_small, small_pack, _pack([m_ssd_conv_w, m_pool_b, m_pool_scale])[0],
                _pack([v_ssd_conv_w, v_pool_b, v_pool_scale])[0], 0, None, "adamw_small_sharded")
    for idx, nm in enumerate(["ssd_conv_w", "pool_b", "pool_scale"]):
        upd[nm] = [_unpack(r, small_meta)[idx] for r in sm]
    repl_names = ["ssd_conv_b", "ssd_dt_bias", "ssd_A_log", "ssd_D", "ssd_norm_w",
                  "ln_mix_g", "ln_mix_b", "ln_ffn_g", "ln_ffn_b"]
    repl_w = [ssd_conv_b, ssd_dt_bias, ssd_A_log, ssd_D, ssd_norm_w, ln_mix_g, ln_mix_b, ln_ffn_g, ln_ffn_b]
    repl_m = [m_ssd_conv_b, m_ssd_dt_bias, m_ssd_A_log, m_ssd_D, m_ssd_norm_w, m_ln_mix_g, m_ln_mix_b, m_ln_ffn_g, m_ln_ffn_b]
    repl_v = [v_ssd_conv_b, v_ssd_dt_bias, v_ssd_A_log, v_ssd_D, v_ssd_norm_w, v_ln_mix_g, v_ln_mix_b, v_ln_ffn_g, v_ln_ffn_b]
    rp = _adamw(r_repl, _pack(repl_w)[0], _pack(repl_m)[0], _pack(repl_v)[0], 0, None, "adamw_replicated")
    for idx, nm in enumerate(repl_names):
        upd[nm] = [_unpack(r, repl_meta)[idx] for r in rp]

    order = ["ssd_in_proj", "ssd_conv_w", "ssd_conv_b", "ssd_dt_bias", "ssd_A_log", "ssd_D", "ssd_norm_w",
             "ssd_out_proj", "pool_w", "pool_b", "pool_scale", "mlp_w1", "mlp_w2",
             "ln_mix_g", "ln_mix_b", "ln_ffn_g", "ln_ffn_b"]
    return (loss, grad_x, *[upd[n][0] for n in order], *[upd[n][1] for n in order],
            *[upd[n][2] for n in order], *[upd[n][3] for n in order])
```

```python
import functools
import math

import jax
import jax.numpy as jnp
from jax import lax
from jax.experimental import pallas as pl
from jax.experimental.pallas import tpu as pltpu

F32 = jnp.float32
BF16 = jnp.bfloat16

N_DEV = 8
HEAD_DIM = 64
N_GROUPS = 8
D_STATE = 128
CHUNK = 128
CONV_WIDTH = 5
POOL_WINDOWS = (2, 4, 8, 16)
HALO = 8
LN_EPS = 1e-5
RMS_EPS = 1e-5
ADAM_LR = 0.001
ADAM_B1 = 0.9
ADAM_B2 = 0.999
ADAM_EPS = 1e-08
ADAM_WD = 0.01
ADAM_STEP = 10
LANES = 128
VMEM_LIMIT_BYTES = 56 * 1024 * 1024
HIGHEST = lax.Precision.HIGHEST


def _pallas(body, **kw):
    return pl.pallas_call(body, **kw)


def _params(*sem):
    return pltpu.CompilerParams(dimension_semantics=sem, vmem_limit_bytes=VMEM_LIMIT_BYTES)


def _mesh_pos():
    return lax.axis_index("x"), lax.axis_index("y"), lax.axis_index("c")


def _peer(x, y, c, k):
    dx, dy, dc = (k >> 2) & 1, (k >> 1) & 1, k & 1
    px = (1 - x) if dx else x
    py = (1 - y) if dy else y
    pc = (1 - c) if dc else c
    return px, py, pc


def _comm_copies(ops, ins, outs, send_sems, recv_sems, local_sems):
    x, y, c = _mesh_pos()
    me = 4 * x + 2 * y + c
    copies = []
    for a, op in enumerate(ops):
        src, dst = ins[a], outs[a]

        def remote(k, s, d, to):
            return pltpu.make_async_remote_copy(src_ref=s, dst_ref=d, send_sem=send_sems.at[a, k - 1],
                                                recv_sem=recv_sems.at[a, k - 1], device_id=to,
                                                device_id_type=pl.DeviceIdType.MESH)

        if op[0] in ("gather", "gather1"):
            copies.append(pltpu.make_async_copy(src, dst.at[me], local_sems.at[a]))
            for k in (range(1, N_DEV) if op[0] == "gather" else (1, 2, 4, 6)):
                copies.append(remote(k, src, dst.at[me], _peer(x, y, c, k)))
        elif op[0] == "gather2":
            for k in (2, 4, 6):
                qx, qy, qc = _peer(x, y, c, k)
                slot = 4 * qx + 2 * qy + qc
                copies.append(remote(k, src.at[slot], dst.at[slot], _peer(x, y, c, 1)))
        else:
            rows = pl.ds(op[1], op[2])
            copies.append(pltpu.make_async_copy(src.at[me, rows], dst.at[me], local_sems.at[a]))
            for k in range(1, N_DEV):
                px, py, pc = _peer(x, y, c, k)
                copies.append(remote(k, src.at[4 * px + 2 * py + pc, rows], dst.at[me], (px, py, pc)))
    return copies


def _comm_plan(comm):
    ops = [(c[0],) + tuple(c[2:]) for c in comm]
    arrays = [c[1] for c in comm]
    shapes = []
    for op, a in zip(ops, arrays):
        if op[0] in ("gather", "gather1"):
            shapes.append(jax.ShapeDtypeStruct((N_DEV,) + a.shape, a.dtype))
        elif op[0] == "gather2":
            shapes.append(jax.ShapeDtypeStruct(a.shape, a.dtype))
        else:
            shapes.append(jax.ShapeDtypeStruct((N_DEV, op[2]) + a.shape[2:], a.dtype))
    n = len(ops)
    sems = [pltpu.SemaphoreType.DMA((n, N_DEV - 1)), pltpu.SemaphoreType.DMA((n, N_DEV - 1)),
            pltpu.SemaphoreType.DMA((n,))]
    in_place = [a for a, op in enumerate(ops) if op[0] == "gather2"]
    return ops, arrays, shapes, sems, in_place


def _call(body, *, comm=None, **kw):
    if not comm:
        return _pallas(body, **kw)
    ops, arrays, c_shape, c_sems, in_place = _comm_plan(comm)
    n = len(ops)
    grid = tuple(kw["grid"])
    single = not isinstance(kw["out_shape"], (list, tuple))
    out_shape = [kw["out_shape"]] if single else list(kw["out_shape"])
    out_specs = [kw["out_specs"]] if single else list(kw["out_specs"])
    in_specs = list(kw["in_specs"])
    scratch = list(kw.get("scratch_shapes", ()))
    n_in, n_out, n_scr = len(in_specs), len(out_shape), len(scratch)

    def wrapped(*refs):
        ins, refs = refs[:n_in], refs[n_in:]
        c_ins, refs = refs[:n], refs[n:]
        outs, refs = refs[:n_out], refs[n_out:]
        c_outs, refs = refs[:n], refs[n:]
        scr, sems = refs[:n_scr], refs[n_scr:]
        first = last = None
        for ax, size in enumerate(grid):
            i = pl.program_id(ax)
            first = (i == 0) if first is None else first & (i == 0)
            last = (i == size - 1) if last is None else last & (i == size - 1)

        @pl.when(first)
        def _():
            for cp in _comm_copies(ops, c_ins, c_outs, *sems):
                cp.start()

        body(*ins, *outs, *scr)

        @pl.when(last)
        def _():
            for cp in _comm_copies(ops, c_ins, c_outs, *sems):
                cp.wait()

    any_spec = pl.BlockSpec(memory_space=pl.ANY)
    call = _pallas(wrapped, name=kw["name"], grid=grid, in_specs=in_specs + [any_spec] * n,
                   out_specs=out_specs + [any_spec] * n, out_shape=out_shape + c_shape,
                   scratch_shapes=scratch + c_sems,
                   input_output_aliases={**kw.get("input_output_aliases", {}), **{n_in + a: n_out + a for a in in_place}},
                   compiler_params=_params(*(("arbitrary",) * len(grid))))

    def run(*args):
        res = call(*args, *arrays)
        own = res[:n_out]
        return (own[0] if single else list(own)), list(res[n_out:])

    return run


def _tile(dim, target, align=LANES):
    if dim <= target:
        return dim
    t = (target // align) * align
    while t >= align:
        if dim % t == 0:
            return t
        t -= align
    return dim


def _dot(a, b, dims, precision=None):
    return lax.dot_general(a, b, (dims, ((), ())), precision=precision, preferred_element_type=F32)


NN = ((1,), (0,))
NT = ((1,), (1,))
TN = ((0,), (0,))


def _sigmoid(x):
    return 1.0 / (1.0 + jnp.exp(-x))


def _matmul(a, b, mode, *, name, outs, epilogue=None, extras=(), tm=1024, tn=1024, tk=2048, comm=None,
            cols_by_device=False):
    if mode == "nn":
        (M, K), (K2, N) = a.shape, (b.shape[-2:] if not cols_by_device else (b.shape[1], N_DEV * b.shape[2]))
    elif mode == "nt":
        (M, K), (N, K2) = a.shape, (b.shape if not cols_by_device else (b.shape[1], N_DEV * b.shape[2]))
    else:
        (K, M), (K2, N) = a.shape, b.shape
    assert K == K2, (a.shape, b.shape, mode)
    per_dev = (K if mode == "nt" else N) // N_DEV
    if cols_by_device and mode == "nt":
        tk = min(tk, per_dev)
    elif cols_by_device:
        tn = min(tn, per_dev)
    tm, tn, tk = _tile(M, tm), _tile(N, tn), _tile(K, tk)
    nk = K // tk
    dims = {"nn": NN, "nt": NT, "tn": TN}[mode]
    a_spec = (pl.BlockSpec((tk, tm), lambda i, j, k: (k, i)) if mode == "tn"
              else pl.BlockSpec((tm, tk), lambda i, j, k: (i, k)))
    b_spec = (pl.BlockSpec((tn, tk), lambda i, j, k: (j, k)) if mode == "nt"
              else pl.BlockSpec((tk, tn), lambda i, j, k: (k, j)))
    mn_spec = pl.BlockSpec((tm, tn), lambda i, j, k: (i, j))
    out_spec, out_dims = mn_spec, (M, N)
    if cols_by_device and mode == "nn":
        r = per_dev // tn
        b_spec = pl.BlockSpec((None, tk, tn), lambda i, j, k: (j // r, k, j % r))
    elif cols_by_device and mode == "nt":
        r = per_dev // tk
        b_spec = pl.BlockSpec((None, tn, tk), lambda i, j, k: (k // r, j, k % r))
    elif cols_by_device:
        assert not extras and epilogue is None
        r = per_dev // tn
        out_spec, out_dims = pl.BlockSpec((None, tm, tn), lambda i, j, k: (j // r, i, j % r)), (N_DEV, M, per_dev)
    n_extra, n_out = len(extras), len(outs)

    def finish(acc, extra_refs, out_refs):
        res = (acc,) if epilogue is None else epilogue(acc, *[r[...] for r in extra_refs])
        for o_ref, r in zip(out_refs, res):
            o_ref[...] = r.astype(o_ref.dtype)

    def body(*refs):
        a_ref, b_ref = refs[0], refs[1]
        extra_refs = refs[2:2 + n_extra]
        out_refs = refs[2 + n_extra:2 + n_extra + n_out]
        part = _dot(a_ref[...].astype(BF16), b_ref[...].astype(BF16), dims)
        if nk == 1:
            finish(part, extra_refs, out_refs)
            return
        acc_ref = refs[-1]
        k = pl.program_id(2)

        @pl.when(k == 0)
        def _():
            acc_ref[...] = part

        @pl.when((k > 0) & (k < nk - 1))
        def _():
            acc_ref[...] += part

        @pl.when(k == nk - 1)
        def _():
            finish(acc_ref[...] + part, extra_refs, out_refs)

    res = _call(
        body, comm=comm, name=name, grid=(M // tm, N // tn, nk),
        in_specs=[a_spec, b_spec] + [mn_spec] * n_extra,
        out_specs=[out_spec] * n_out,
        out_shape=[jax.ShapeDtypeStruct(out_dims, dt) for dt in outs],
        scratch_shapes=[pltpu.VMEM((tm, tn), F32)] if nk > 1 else [],
        compiler_params=_params("parallel", "parallel", "arbitrary"),
    )(a, b, *extras)
    return res


def _ln_fwd(x, f, g, b, alpha, name):
    T, D = x.shape
    tm = _tile(T, 256, 8)

    def body(x_ref, f_ref, g_ref, b_ref, y_ref, yb_ref):
        s = alpha * x_ref[...] + f_ref[...]
        mu = jnp.mean(s, axis=-1, keepdims=True)
        d = s - mu
        var = jnp.mean(d * d, axis=-1, keepdims=True)
        y = d * lax.rsqrt(var + LN_EPS) * g_ref[...] + b_ref[...]
        y_ref[...] = y
        yb_ref[...] = y.astype(BF16)

    row = pl.BlockSpec((tm, D), lambda i: (i, 0))
    vec = pl.BlockSpec((1, D), lambda i: (0, 0))
    return _call(body, name=name, grid=(T // tm,), in_specs=[row, row, vec, vec], out_specs=[row, row],
                 out_shape=[jax.ShapeDtypeStruct((T, D), F32), jax.ShapeDtypeStruct((T, D), BF16)],
                 compiler_params=_params("parallel"))(x, f, g.reshape(1, D), b.reshape(1, D))


def _ln_bwd(x, f, g, dy, alpha, name):
    T, D = x.shape
    tm = _tile(T, 256, 8)

    def body(x_ref, f_ref, g_ref, dy_ref, ds_ref, dsb_ref, dg_ref, db_ref):
        i = pl.program_id(0)

        @pl.when(i == 0)
        def _():
            dg_ref[...] = jnp.zeros_like(dg_ref)
            db_ref[...] = jnp.zeros_like(db_ref)

        s = alpha * x_ref[...] + f_ref[...]
        mu = jnp.mean(s, axis=-1, keepdims=True)
        d = s - mu
        var = jnp.mean(d * d, axis=-1, keepdims=True)
        rstd = lax.rsqrt(var + LN_EPS)
        xhat = d * rstd
        dy_ = dy_ref[...]
        dg_ref[...] += jnp.sum(dy_ * xhat, axis=0, keepdims=True)
        db_ref[...] += jnp.sum(dy_, axis=0, keepdims=True)
        dxh = dy_ * g_ref[...]
        m1 = jnp.mean(dxh, axis=-1, keepdims=True)
        m2 = jnp.mean(dxh * xhat, axis=-1, keepdims=True)
        ds = rstd * (dxh - m1 - xhat * m2)
        ds_ref[...] = ds
        dsb_ref[...] = ds.astype(BF16)

    row = pl.BlockSpec((tm, D), lambda i: (i, 0))
    vec = pl.BlockSpec((1, D), lambda i: (0, 0))
    return _call(body, name=name, grid=(T // tm,), in_specs=[row, row, vec, row], out_specs=[row, row, vec, vec],
                 out_shape=[jax.ShapeDtypeStruct((T, D), F32), jax.ShapeDtypeStruct((T, D), BF16),
                            jax.ShapeDtypeStruct((1, D), F32), jax.ShapeDtypeStruct((1, D), F32)],
                 compiler_params=_params("arbitrary"))(x, f, g.reshape(1, D), dy)


def _loss_head(y, target):
    T, D = y.shape
    tm = _tile(T, 256, 8)

    def body(y_ref, t_ref, loss_ref, dy_ref):
        i = pl.program_id(0)

        @pl.when(i == 0)
        def _():
            loss_ref[...] = jnp.zeros_like(loss_ref)

        err = y_ref[...] - t_ref[...]
        dy_ref[...] = err * (1.0 / D)
        per_tok = jnp.mean(err * err, axis=-1, keepdims=True)
        loss_ref[...] += 0.5 * jnp.sum(per_tok)

    row = pl.BlockSpec((tm, D), lambda i: (i, 0))
    return _call(body, name="loss_head", grid=(T // tm,), in_specs=[row, row],
                 out_specs=[pl.BlockSpec((1, LANES), lambda i: (0, 0)), row],
                 out_shape=[jax.ShapeDtypeStruct((1, LANES), F32), jax.ShapeDtypeStruct((T, D), F32)],
                 compiler_params=_params("arbitrary"))(y, target)


def _halo_specs(tt, cw, col_of, n_tiles, grid_rank_tokens_axis):
    per = tt // HALO
    ax = grid_rank_tokens_axis

    def cur(*g):
        return (g[ax], col_of(*g))

    def prev(*g):
        return (jnp.maximum(g[ax] * per - 1, 0), col_of(*g))

    def nxt(*g):
        return (jnp.minimum((g[ax] + 1) * per, n_tiles * per - 1), col_of(*g))

    return [pl.BlockSpec((tt, cw), cur), pl.BlockSpec((HALO, cw), prev), pl.BlockSpec((HALO, cw), nxt)]


def _fill_ext(ext_ref, cur_ref, prev_ref, next_ref, i, n_tiles, tt):
    ext_ref[pl.ds(0, HALO), :] = jnp.where(i > 0, prev_ref[...], 0.0)
    ext_ref[pl.ds(HALO, tt), :] = cur_ref[...]
    ext_ref[pl.ds(HALO + tt, HALO), :] = jnp.where(i < n_tiles - 1, next_ref[...], 0.0)


def _conv_pre(ext_ref, w, bias, tt, lo=0, n=None):
    n = tt if n is None else n
    pad = CONV_WIDTH // 2
    acc = None
    for k in range(CONV_WIDTH):
        term = ext_ref[pl.ds(HALO + lo + k - pad, n), :] * w[k:k + 1, :]
        acc = term if acc is None else acc + term
    return acc + bias


def _conv_fwd(zx, conv_w, conv_b, d_inner, name, comm=None):
    T = zx.shape[0]
    d_xbc = conv_w.shape[1]
    cw = _tile(d_xbc, 512)
    assert d_inner % cw == 0
    off = d_inner // cw
    tt = _tile(T, 512, 8)
    nt = T // tt

    def body(cur_ref, prev_ref, next_ref, w_ref, b_ref, o_ref, ext_ref):
        i = pl.program_id(1)
        _fill_ext(ext_ref, cur_ref, prev_ref, next_ref, i, nt, tt)
        pre = _conv_pre(ext_ref, w_ref[...], b_ref[...], tt)
        o_ref[...] = pre * _sigmoid(pre)

    specs = _halo_specs(tt, cw, lambda j, i: off + j, nt, 1)
    return _call(body, comm=comm, name=name, grid=(d_xbc // cw, nt),
                 in_specs=specs + [pl.BlockSpec((CONV_WIDTH, cw), lambda j, i: (0, j)),
                                   pl.BlockSpec((1, cw), lambda j, i: (0, j))],
                 out_specs=pl.BlockSpec((tt, cw), lambda j, i: (i, j)),
                 out_shape=jax.ShapeDtypeStruct((T, d_xbc), F32),
                 scratch_shapes=[pltpu.VMEM((tt + 2 * HALO, cw), F32)],
                 compiler_params=_params("parallel", "parallel"))(zx, zx, zx, conv_w, conv_b.reshape(1, d_xbc))


def _conv_dpre(zx, conv_w, conv_b, d_inner, col_lo, dirs, extra, name, into=None, extra_scale=None):
    T = zx.shape[0]
    ncols = dirs.shape[2]
    cw = _tile(ncols, 512)
    assert d_inner % cw == 0 and col_lo % cw == 0
    off_zx = (d_inner + col_lo) // cw
    off_w = col_lo // cw
    tt = _tile(T, 512, 8)
    nt = T // tt
    has_extra = extra is not None

    def body(cur_ref, prev_ref, next_ref, w_ref, b_ref, dirs_ref, *rest):
        o_ref, ext_ref = rest[-2], rest[-1]
        i = pl.program_id(1)
        _fill_ext(ext_ref, cur_ref, prev_ref, next_ref, i, nt, tt)
        pre = _conv_pre(ext_ref, w_ref[...], b_ref[...], tt)
        sig = _sigmoid(pre)
        dact = dirs_ref[0] + dirs_ref[1]
        if has_extra:
            dact = dact + rest[0][...] * rest[1][...]
        o_ref[...] = dact * (sig * (1.0 + pre * (1.0 - sig)))

    specs = _halo_specs(tt, cw, lambda j, i: off_zx + j, nt, 1)
    in_specs = specs + [pl.BlockSpec((CONV_WIDTH, cw), lambda j, i: (0, off_w + j)),
                        pl.BlockSpec((1, cw), lambda j, i: (0, off_w + j)),
                        pl.BlockSpec((2, tt, cw), lambda j, i: (0, i, j))]
    args = [zx, zx, zx, conv_w, conv_b.reshape(1, -1), dirs]
    if has_extra:
        in_specs += [pl.BlockSpec((tt, cw), lambda j, i: (i, j)), pl.BlockSpec((1, cw), lambda j, i: (0, j))]
        args += [extra, extra_scale]
    aliases = {}
    if into is not None:
        in_specs.append(pl.BlockSpec(memory_space=pl.ANY))
        args.append(into)
        aliases = {len(args) - 1: 0}
    return _call(body, name=name, grid=(ncols // cw, nt), in_specs=in_specs,
                 out_specs=pl.BlockSpec((tt, cw), lambda j, i: (i, off_w + j)),
                 out_shape=jax.ShapeDtypeStruct((T, conv_w.shape[1]), F32),
                 scratch_shapes=[pltpu.VMEM((tt + 2 * HALO, cw), F32)], input_output_aliases=aliases,
                 compiler_params=_params("parallel", "parallel"))(*args)


def _conv_bwd(zx, dpre, conv_w, d_inner, into, name, comm=None):
    T = zx.shape[0]
    d_xbc = conv_w.shape[1]
    cw = _tile(d_xbc, 512)
    off = d_inner // cw
    tt = _tile(T, 512, 8)
    nt = T // tt
    pad = CONV_WIDTH // 2

    def body(zc, zp, zn, dc, dp, dn, w_ref, into_ref, din_ref, dw_ref, db_ref, zext, dext):
        i = pl.program_id(1)

        @pl.when(i == 0)
        def _():
            dw_ref[...] = jnp.zeros_like(dw_ref)
            db_ref[...] = jnp.zeros_like(db_ref)

        _fill_ext(zext, zc, zp, zn, i, nt, tt)
        _fill_ext(dext, dc, dp, dn, i, nt, tt)
        w = w_ref[...]
        d = dc[...]
        acc = None
        for k in range(CONV_WIDTH):
            term = dext[pl.ds(HALO + pad - k, tt), :] * w[k:k + 1, :]
            acc = term if acc is None else acc + term
            dw_ref[k:k + 1, :] += jnp.sum(d * zext[pl.ds(HALO + k - pad, tt), :], axis=0, keepdims=True)
        din_ref[...] = acc.astype(din_ref.dtype)
        db_ref[...] += jnp.sum(d, axis=0, keepdims=True)

    zspecs = _halo_specs(tt, cw, lambda j, i: off + j, nt, 1)
    dspecs = _halo_specs(tt, cw, lambda j, i: j, nt, 1)
    return _call(body, comm=comm, name=name, grid=(d_xbc // cw, nt),
                 in_specs=zspecs + dspecs + [pl.BlockSpec((CONV_WIDTH, cw), lambda j, i: (0, j)),
                                             pl.BlockSpec(memory_space=pl.ANY)],
                 out_specs=[pl.BlockSpec((tt, cw), lambda j, i: (i, off + j)),
                            pl.BlockSpec((CONV_WIDTH, cw), lambda j, i: (0, j)),
                            pl.BlockSpec((1, cw), lambda j, i: (0, j))],
                 out_shape=[jax.ShapeDtypeStruct(into.shape, into.dtype), jax.ShapeDtypeStruct((CONV_WIDTH, d_xbc), F32),
                            jax.ShapeDtypeStruct((1, d_xbc), F32)],
                 scratch_shapes=[pltpu.VMEM((tt + 2 * HALO, cw), F32), pltpu.VMEM((tt + 2 * HALO, cw), F32)],
                 input_output_aliases={7: 0},
                 compiler_params=_params("parallel", "arbitrary"))(zx, zx, zx, dpre, dpre, dpre, conv_w, into)


def _tri(n):
    r = lax.broadcasted_iota(jnp.int32, (n, n), 0)
    c = lax.broadcasted_iota(jnp.int32, (n, n), 1)
    return (r >= c).astype(F32), (r <= c).astype(F32)


def _dt_fwd(raw, bias, a_log, name):
    T, H2 = raw.shape
    half = H2 // 2

    def body(raw_ref, bias_ref, alog_ref, dt_ref, cs_ref):
        x = raw_ref[...] + bias_ref[...]
        dt = jnp.maximum(x, 0.0) + jnp.log(1.0 + jnp.exp(-jnp.abs(x)))
        a = dt * (-jnp.exp(alog_ref[...]))
        lower, upper = _tri(CHUNK)
        cs_f = _dot(lower, a, NN, HIGHEST)
        cs_b = _dot(upper, a, NN, HIGHEST)
        lane = lax.broadcasted_iota(jnp.int32, (CHUNK, H2), 1)
        dt_ref[...] = dt
        cs_ref[...] = jnp.where(lane < half, cs_f, cs_b)

    row = pl.BlockSpec((CHUNK, H2), lambda c: (c, 0))
    vec = pl.BlockSpec((1, H2), lambda c: (0, 0))
    return _call(body, name=name, grid=(T // CHUNK,), in_specs=[row, vec, vec], out_specs=[row, row],
                 out_shape=[jax.ShapeDtypeStruct((T, H2), F32)] * 2,
                 compiler_params=_params("parallel"))(raw, bias.reshape(1, H2), a_log.reshape(1, H2))


def _dt_bwd(raw, bias, a_log, dcs, dtot, dxdtx, into, name):
    T, H2 = raw.shape
    half = H2 // 2
    assert into.shape[1] % H2 == 0
    last = into.shape[1] // H2 - 1

    def body(raw_ref, bias_ref, alog_ref, dcs_ref, dtot_ref, dx_ref, into_ref, draw_ref, dbias_ref, dalog_ref):
        c = pl.program_id(0)

        @pl.when(c == 0)
        def _():
            dbias_ref[...] = jnp.zeros_like(dbias_ref)
            dalog_ref[...] = jnp.zeros_like(dalog_ref)

        x = raw_ref[...] + bias_ref[...]
        dt = jnp.maximum(x, 0.0) + jnp.log(1.0 + jnp.exp(-jnp.abs(x)))
        A = -jnp.exp(alog_ref[...])
        lower, upper = _tri(CHUNK)
        g = dcs_ref[...]
        lane = lax.broadcasted_iota(jnp.int32, (CHUNK, H2), 1)
        da = jnp.where(lane < half, _dot(upper, g, NN, HIGHEST), _dot(lower, g, NN, HIGHEST)) + dtot_ref[...]
        ddt = da * A + dx_ref[...]
        draw = ddt * _sigmoid(x)
        draw_ref[...] = draw.astype(draw_ref.dtype)
        dbias_ref[...] += jnp.sum(draw, axis=0, keepdims=True)
        dalog_ref[...] += jnp.sum(da * dt, axis=0, keepdims=True) * A

    row = pl.BlockSpec((CHUNK, H2), lambda c: (c, 0))
    vec = pl.BlockSpec((1, H2), lambda c: (0, 0))
    return _call(body, name=name, grid=(T // CHUNK,),
                 in_specs=[row, vec, vec, row, row, row, pl.BlockSpec(memory_space=pl.ANY)],
                 out_specs=[pl.BlockSpec((CHUNK, H2), lambda c: (c, last)), vec, vec],
                 out_shape=[jax.ShapeDtypeStruct(into.shape, into.dtype), jax.ShapeDtypeStruct((1, H2), F32),
                            jax.ShapeDtypeStruct((1, H2), F32)],
                 input_output_aliases={6: 0},
                 compiler_params=_params("arbitrary"))(raw, bias.reshape(1, H2), a_log.reshape(1, H2), dcs, dtot, dxdtx,
                                                       into)


def _cols(a, hg):
    T = a.shape[0]
    return a.reshape(T, 2, N_GROUPS, hg).transpose(1, 2, 0, 3)


def _rows(a, hg):
    T = a.shape[0]
    return a.reshape(T, 2, N_GROUPS, hg).transpose(1, 2, 3, 0)


def _uncols(a):
    T = a.shape[2]
    return a.transpose(2, 0, 1, 3).reshape(T, -1)


def _ssd_masks(d):
    r = lax.broadcasted_iota(jnp.int32, (CHUNK, CHUNK), 0)
    c = lax.broadcasted_iota(jnp.int32, (CHUNK, CHUNK), 1)
    return ((r >= c) & (d == 0)) | ((r <= c) & (d == 1))


def _head_expand(hg):
    r = lax.broadcasted_iota(jnp.int32, (hg, hg * HEAD_DIM), 0)
    c = lax.broadcasted_iota(jnp.int32, (hg, hg * HEAD_DIM), 1)
    return (c // HEAD_DIM == r).astype(F32)


def _head_select(hg):
    r = lax.broadcasted_iota(jnp.int32, (hg * HEAD_DIM, hg), 0)
    c = lax.broadcasted_iota(jnp.int32, (hg * HEAD_DIM, hg), 1)
    return (r // HEAD_DIM == c).astype(F32)


def _ssd_common(d, csc_ref, dtc_ref, hg):
    expand = _head_expand(hg)
    csx = _dot(csc_ref[...], expand, NN, HIGHEST)
    dtx = _dot(dtc_ref[...], expand, NN, HIGHEST)
    totx = jnp.where(d == 0, csx[CHUNK - 1:CHUNK, :], csx[0:1, :])
    return csx, dtx, totx


def _ssd_fwd(xbc, dtc, csc, csr, d_inner, name, comm=None):
    T = xbc.shape[0]
    nc = T // CHUNK
    gw = d_inner // N_GROUPS
    hg = gw // HEAD_DIM
    P, N = HEAD_DIM, D_STATE
    b_off = d_inner // N
    c_off = b_off + N_GROUPS

    def cidx(d, c):
        return c + d * (nc - 1 - 2 * c)

    def body(xs_ref, b_ref, c_ref, dtc_ref, csc_ref, csr_ref, y_ref, st_ref, h_ref):
        d = pl.program_id(0)
        c = pl.program_id(2)

        @pl.when(c == 0)
        def _():
            h_ref[...] = jnp.zeros_like(h_ref)

        Bb = b_ref[...].astype(BF16)
        Cb = c_ref[...].astype(BF16)
        S = _dot(Cb, Bb, NT)
        mask = _ssd_masks(d)
        csx, dtx, totx = _ssd_common(d, csc_ref, dtc_ref, hg)
        H = h_ref[...]
        st_ref[...] = H
        xdt = xs_ref[...] * dtx
        xdtb = xdt.astype(BF16)
        y_off = jnp.exp(csx) * _dot(Cb, H.astype(BF16), NN)
        for j in range(hg):
            sl = slice(j * P, (j + 1) * P)
            decay = jnp.exp(jnp.where(mask, csc_ref[:, j:j + 1] - csr_ref[j:j + 1, :], -jnp.inf))
            y_ref[:, sl] = _dot((S * decay).astype(BF16), xdtb[:, sl], NN) + y_off[:, sl]
        h_ref[...] = jnp.exp(totx) * H + _dot(Bb, (jnp.exp(totx - csx) * xdt).astype(BF16), TN)

    col = lambda d, g, c: (d, g, cidx(d, c), 0)
    return _call(
        body, comm=comm, name=name, grid=(2, N_GROUPS, nc),
        in_specs=[pl.BlockSpec((CHUNK, gw), lambda d, g, c: (cidx(d, c), g)),
                  pl.BlockSpec((CHUNK, N), lambda d, g, c: (cidx(d, c), b_off + g)),
                  pl.BlockSpec((CHUNK, N), lambda d, g, c: (cidx(d, c), c_off + g)),
                  pl.BlockSpec((None, None, CHUNK, hg), col),
                  pl.BlockSpec((None, None, CHUNK, hg), col),
                  pl.BlockSpec((None, None, hg, CHUNK), lambda d, g, c: (d, g, 0, cidx(d, c)))],
        out_specs=[pl.BlockSpec((None, CHUNK, gw), lambda d, g, c: (d, cidx(d, c), g)),
                   pl.BlockSpec((None, None, None, N, gw), lambda d, g, c: (d, cidx(d, c), g, 0, 0))],
        out_shape=[jax.ShapeDtypeStruct((2, T, d_inner), F32),
                   jax.ShapeDtypeStruct((2, nc, N_GROUPS, N, gw), F32)],
        scratch_shapes=[pltpu.VMEM((N, gw), F32)],
        compiler_params=_params("parallel", "parallel", "arbitrary"),
    )(xbc, xbc, xbc, dtc, csc, csr)


def _ssd_bwd(xbc, dtc, csc, csr, states, y2, dy, d_inner, name, comm=None):
    T = xbc.shape[0]
    nc = T // CHUNK
    gw = d_inner // N_GROUPS
    hg = gw // HEAD_DIM
    P, N = HEAD_DIM, D_STATE
    b_off = d_inner // N
    c_off = b_off + N_GROUPS

    def cidx(d, c):
        return (nc - 1 - c) + d * (2 * c - nc + 1)

    def body(xs_ref, b_ref, c_ref, dtc_ref, csc_ref, csr_ref, st_ref, y_ref, dy_ref,
             dxs_ref, db_ref, dc_ref, dcs_ref, dtot_ref, dxdtx_ref, dh_ref, dxdt_ref):
        d = pl.program_id(0)
        c = pl.program_id(2)

        @pl.when(c == 0)
        def _():
            dh_ref[...] = jnp.zeros_like(dh_ref)

        Bb = b_ref[...].astype(BF16)
        Cb = c_ref[...].astype(BF16)
        S = _dot(Cb, Bb, NT)
        mask = _ssd_masks(d)
        csx, dtx, totx = _ssd_common(d, csc_ref, dtc_ref, hg)
        select = _head_select(hg)
        X = xs_ref[...]
        xdt = X * dtx
        xdtb = xdt.astype(BF16)
        dY = dy_ref[...]
        dYb = dY.astype(BF16)
        Hp = st_ref[...]
        Hpb = Hp.astype(BF16)
        dH = dh_ref[...]
        dHb = dH.astype(BF16)
        e_tot = jnp.exp(totx)
        dCH = (jnp.exp(csx) * dY).astype(BF16)
        dC = _dot(dCH, Hpb, NT)
        dHp = _dot(Cb, dCH, TN)
        Q = _dot(Bb, dHb, NN)
        dte = jnp.exp(totx - csx)
        wx = dte * xdt
        dB = _dot(wx.astype(BF16), dHb, NT)
        ddte = Q * wx
        dS = jnp.zeros((CHUNK, CHUNK), F32)
        for j in range(hg):
            sl = slice(j * P, (j + 1) * P)
            decay = jnp.exp(jnp.where(mask, csc_ref[:, j:j + 1] - csr_ref[j:j + 1, :], -jnp.inf))
            dS = dS + _dot(dYb[:, sl], xdtb[:, sl], NT) * decay
            dxdt_ref[:, sl] = _dot((S * decay).astype(BF16), dYb[:, sl], TN)
        dxdt_diag = dxdt_ref[...]
        dxdt = dxdt_diag + dte * Q
        dcs_ref[...] = _dot(dYb.astype(F32) * y_ref[...] - xdtb.astype(F32) * dxdt_diag - ddte, select, NN, HIGHEST)
        dtot_row = (jnp.sum(ddte, axis=0, keepdims=True) + e_tot * jnp.sum(dH * Hp, axis=0, keepdims=True))
        dtot_ref[...] = jnp.zeros((CHUNK, hg), F32) + _dot(dtot_row, select, NN, HIGHEST)
        dxdtx_ref[...] = _dot(dxdt * X, select, NN, HIGHEST)
        dxs_ref[...] = dxdt * dtx
        dh_ref[...] = e_tot * dH + dHp
        dSb = dS.astype(BF16)
        dc_ref[...] = dC + _dot(dSb, Bb, NN)
        db_ref[...] = dB + _dot(dSb, Cb, TN)

    col = lambda d, g, c: (d, g, cidx(d, c), 0)
    colspec = pl.BlockSpec((None, None, CHUNK, hg), col)
    rowblk = pl.BlockSpec((None, CHUNK, gw), lambda d, g, c: (d, cidx(d, c), g))
    return _call(
        body, comm=comm, name=name, grid=(2, N_GROUPS, nc),
        in_specs=[pl.BlockSpec((CHUNK, gw), lambda d, g, c: (cidx(d, c), g)),
                  pl.BlockSpec((CHUNK, N), lambda d, g, c: (cidx(d, c), b_off + g)),
                  pl.BlockSpec((CHUNK, N), lambda d, g, c: (cidx(d, c), c_off + g)),
                  colspec, colspec,
                  pl.BlockSpec((None, None, hg, CHUNK), lambda d, g, c: (d, g, 0, cidx(d, c))),
                  pl.BlockSpec((None, None, None, N, gw), lambda d, g, c: (d, cidx(d, c), g, 0, 0)),
                  rowblk,
                  pl.BlockSpec((CHUNK, gw), lambda d, g, c: (cidx(d, c), g))],
        out_specs=[rowblk,
                   pl.BlockSpec((None, CHUNK, N), lambda d, g, c: (d, cidx(d, c), g)),
                   pl.BlockSpec((None, CHUNK, N), lambda d, g, c: (d, cidx(d, c), g)),
                   colspec, colspec, colspec],
        out_shape=[jax.ShapeDtypeStruct((2, T, d_inner), F32),
                   jax.ShapeDtypeStruct((2, T, N_GROUPS * N), F32),
                   jax.ShapeDtypeStruct((2, T, N_GROUPS * N), F32)]
        + [jax.ShapeDtypeStruct((2, N_GROUPS, T, hg), F32)] * 3,
        scratch_shapes=[pltpu.VMEM((N, gw), F32), pltpu.VMEM((CHUNK, gw), F32)],
        compiler_params=_params("parallel", "parallel", "arbitrary"),
    )(xbc, xbc, xbc, dtc, csc, csr, states, y2, dy)


def _gnorm_fwd(y2, xbc, zx, dvec, nw, d_inner, name, comm=None):
    T = xbc.shape[0]
    gw = d_inner // N_GROUPS
    tm = _tile(T, 128, 8)

    def body(y_ref, xs_ref, z_ref, d_ref, w_ref, o_ref):
        for g in range(N_GROUPS):
            sl = slice(g * gw, (g + 1) * gw)
            y = y_ref[0, :, sl] + y_ref[1, :, sl] + xs_ref[:, sl] * d_ref[:, sl]
            z = z_ref[:, sl]
            gy = y * (z * _sigmoid(z))
            rs = lax.rsqrt(jnp.mean(gy * gy, axis=-1, keepdims=True) + RMS_EPS)
            o_ref[:, sl] = (gy * rs * w_ref[:, sl]).astype(o_ref.dtype)

    row = pl.BlockSpec((tm, d_inner), lambda i: (i, 0))
    vec = pl.BlockSpec((1, d_inner), lambda i: (0, 0))
    return _call(body, comm=comm, name=name, grid=(T // tm,),
                 in_specs=[pl.BlockSpec((2, tm, d_inner), lambda i: (0, i, 0)), row, row, vec, vec],
                 out_specs=row, out_shape=jax.ShapeDtypeStruct((T, d_inner), BF16),
                 compiler_params=_params("parallel"))(y2, xbc, zx, dvec, nw)


def _gnorm_bwd(y2, xbc, zx, dvec, nw, dgn, sel, d_inner, name, comm=None):
    T = xbc.shape[0]
    gw = d_inner // N_GROUPS
    n_heads = d_inner // HEAD_DIM
    tm = _tile(T, 128, 8)
    n_tiles = T // tm

    def body(y_ref, xs_ref, z_ref, d_ref, w_ref, dg_ref, sel_ref, dy_ref, dz_ref, dw_ref, dd_ref, dch_ref):
        i = pl.program_id(0)

        @pl.when(i == 0)
        def _():
            dw_ref[...] = jnp.zeros_like(dw_ref)
            dch_ref[...] = jnp.zeros_like(dch_ref)

        for g in range(N_GROUPS):
            sl = slice(g * gw, (g + 1) * gw)
            xs = xs_ref[:, sl]
            y = y_ref[0, :, sl] + y_ref[1, :, sl] + xs * d_ref[:, sl]
            z = z_ref[:, sl]
            sig = _sigmoid(z)
            sz = z * sig
            gy = y * sz
            rs = lax.rsqrt(jnp.mean(gy * gy, axis=-1, keepdims=True) + RMS_EPS)
            n = gy * rs
            dout = dg_ref[:, sl]
            dw_ref[:, sl] += jnp.sum(dout * n, axis=0, keepdims=True)
            dn = dout * w_ref[:, sl]
            dgy = rs * (dn - n * jnp.mean(dn * n, axis=-1, keepdims=True))
            dy = dgy * sz
            dy_ref[:, sl] = dy
            dz_ref[:, sl] = (dgy * y * (sig * (1.0 + z * (1.0 - sig)))).astype(dz_ref.dtype)
            dch_ref[:, sl] += jnp.sum(dy * xs, axis=0, keepdims=True)

        @pl.when(i == n_tiles - 1)
        def _():
            dd_ref[...] = _dot(dch_ref[...], sel_ref[...], NN, HIGHEST)

    row = pl.BlockSpec((tm, d_inner), lambda i: (i, 0))
    vec = pl.BlockSpec((1, d_inner), lambda i: (0, 0))
    hvec = pl.BlockSpec((1, n_heads), lambda i: (0, 0))
    return _call(body, comm=comm, name=name, grid=(n_tiles,),
                 in_specs=[pl.BlockSpec((2, tm, d_inner), lambda i: (0, i, 0)), row, row, vec, vec, row,
                           pl.BlockSpec((d_inner, n_heads), lambda i: (0, 0))],
                 out_specs=[row, row, vec, hvec],
                 out_shape=[jax.ShapeDtypeStruct((T, d_inner), F32), jax.ShapeDtypeStruct(zx.shape, BF16),
                            jax.ShapeDtypeStruct((1, d_inner), F32), jax.ShapeDtypeStruct((1, n_heads), F32)],
                 scratch_shapes=[pltpu.VMEM((1, d_inner), F32)],
                 compiler_params=_params("arbitrary"))(y2, xbc, zx, dvec, nw, dgn, sel)


def _pool_counts(i, tt, T, win, rows, row0):
    t = i * tt + row0 + lax.broadcasted_iota(jnp.int32, (rows, 1), 0)
    start = t - win // 2
    lo = jnp.clip(start, 0, T)
    hi = jnp.clip(start + win, 0, T)
    return jnp.maximum(hi - lo, 1).astype(F32)


def _pool_features(ext_ref, i, tt, T, gi, gd):
    win = POOL_WINDOWS[gi]
    sl = slice(gi * gd, (gi + 1) * gd)
    acc = None
    for o in range(-(win // 2), win - win // 2):
        term = ext_ref[pl.ds(HALO + o, tt), sl]
        acc = term if acc is None else acc + term
    return acc / _pool_counts(i, tt, T, win, tt, 0) - ext_ref[pl.ds(HALO, tt), sl]


def _pool_fwd(u, w, bias, scale, name):
    T, D = u.shape
    ng = len(POOL_WINDOWS)
    gd = D // ng
    tt = _tile(T, 512, 8)
    nt = T // tt

    def body(cur, prev, nxt, w_ref, b_ref, s_ref, o_ref, ext):
        i = pl.program_id(0)
        _fill_ext(ext, cur, prev, nxt, i, nt, tt)
        for gi in range(ng):
            sl = slice(gi * gd, (gi + 1) * gd)
            m = _pool_features(ext, i, tt, T, gi, gd)
            pre = _dot(m.astype(BF16), w_ref[gi], NN) + b_ref[:, sl]
            o_ref[:, sl] = pre * s_ref[:, sl]

    vec = pl.BlockSpec((1, D), lambda i: (0, 0))
    return _call(body, name=name, grid=(nt,),
                 in_specs=_halo_specs(tt, D, lambda i: 0, nt, 0)
                 + [pl.BlockSpec((ng, gd, gd), lambda i: (0, 0, 0)), vec, vec],
                 out_specs=pl.BlockSpec((tt, D), lambda i: (i, 0)),
                 out_shape=jax.ShapeDtypeStruct((T, D), F32),
                 scratch_shapes=[pltpu.VMEM((tt + 2 * HALO, D), F32)],
                 compiler_params=_params("parallel"))(u, u, u, w, bias, scale)


def _pool_bwd_a(u, w, bias, scale, dy, name):
    T, D = u.shape
    ng = len(POOL_WINDOWS)
    gd = D // ng
    tt = _tile(T, 512, 8)
    nt = T // tt

    def body(cur, prev, nxt, w_ref, b_ref, s_ref, dy_ref, dm_ref, dw_ref, db_ref, ds_ref, ext):
        i = pl.program_id(0)

        @pl.when(i == 0)
        def _():
            dw_ref[...] = jnp.zeros_like(dw_ref)
            db_ref[...] = jnp.zeros_like(db_ref)
            ds_ref[...] = jnp.zeros_like(ds_ref)

        _fill_ext(ext, cur, prev, nxt, i, nt, tt)
        for gi in range(ng):
            sl = slice(gi * gd, (gi + 1) * gd)
            mb = _pool_features(ext, i, tt, T, gi, gd).astype(BF16)
            wg = w_ref[gi]
            pre = _dot(mb, wg, NN) + b_ref[:, sl]
            dy_ = dy_ref[:, sl]
            ds_ref[:, sl] += jnp.sum(dy_ * pre, axis=0, keepdims=True)
            dpre = dy_ * s_ref[:, sl]
            db_ref[:, sl] += jnp.sum(dpre, axis=0, keepdims=True)
            dpb = dpre.astype(BF16)
            dw_ref[gi] += _dot(mb, dpb, TN)
            dm_ref[:, sl] = _dot(dpb, wg, NT)

    vec = pl.BlockSpec((1, D), lambda i: (0, 0))
    row = pl.BlockSpec((tt, D), lambda i: (i, 0))
    wspec = pl.BlockSpec((ng, gd, gd), lambda i: (0, 0, 0))
    return _call(body, name=name, grid=(nt,),
                 in_specs=_halo_specs(tt, D, lambda i: 0, nt, 0) + [wspec, vec, vec, row],
                 out_specs=[row, wspec, vec, vec],
                 out_shape=[jax.ShapeDtypeStruct((T, D), F32), jax.ShapeDtypeStruct((ng, gd, gd), F32),
                            jax.ShapeDtypeStruct((1, D), F32), jax.ShapeDtypeStruct((1, D), F32)],
                 scratch_shapes=[pltpu.VMEM((tt + 2 * HALO, D), F32)],
                 compiler_params=_params("arbitrary"))(u, u, u, w, bias, scale, dy)


def _pool_bwd_b(dm, dy, alpha, name):
    T, D = dm.shape
    ng = len(POOL_WINDOWS)
    gd = D // ng
    tt = _tile(T, 512, 8)
    nt = T // tt

    def body(cur, prev, nxt, dy_ref, o_ref, ext):
        i = pl.program_id(0)
        _fill_ext(ext, cur, prev, nxt, i, nt, tt)
        for gi, win in enumerate(POOL_WINDOWS):
            sl = slice(gi * gd, (gi + 1) * gd)
            rows = tt + 2 * HALO
            ext[:, sl] = ext[:, sl] / _pool_counts(i, tt, T, win, rows, -HALO)
            acc = None
            for o in range(-(win // 2) + 1, win // 2 + 1):
                term = ext[pl.ds(HALO + o, tt), sl]
                acc = term if acc is None else acc + term
            o_ref[:, sl] = alpha * dy_ref[:, sl] + acc - cur[:, sl]

    row = pl.BlockSpec((tt, D), lambda i: (i, 0))
    return _call(body, name=name, grid=(nt,),
                 in_specs=_halo_specs(tt, D, lambda i: 0, nt, 0) + [row], out_specs=row,
                 out_shape=jax.ShapeDtypeStruct((T, D), F32),
                 scratch_shapes=[pltpu.VMEM((tt + 2 * HALO, D), F32)],
                 compiler_params=_params("parallel"))(dm, dm, dm, dy)


def _exchange(comm, name):
    ops, arrays, shapes, sems, in_place = _comm_plan(comm)
    n = len(ops)

    def body(*refs):
        copies = _comm_copies(ops, refs[:n], refs[n:2 * n], *refs[2 * n:])
        for cp in copies:
            cp.start()
        for cp in copies:
            cp.wait()

    any_spec = pl.BlockSpec(memory_space=pl.ANY)
    return _pallas(body, name=name, in_specs=[any_spec] * n, out_specs=[any_spec] * n, out_shape=shapes,
                   scratch_shapes=sems, input_output_aliases={a: a for a in in_place})(*arrays)


def _adamw(piece, w, m, v, row0, into, name):
    R, C = w.shape
    rows = piece.shape[1]
    tr = _tile(rows, max(8, (1 << 18) // C // 8 * 8), 8)
    assert row0 % tr == 0 and rows % tr == 0
    off = row0 // tr

    def body(p_ref, w_ref, m_ref, v_ref, *rest):
        g_ref, d_ref, nm_ref, nv_ref = rest[-4:]
        g = p_ref[0].astype(F32)
        for i in range(1, N_DEV):
            g = g + p_ref[i].astype(F32)
        mm = ADAM_B1 * m_ref[...] + (1.0 - ADAM_B1) * g
        vv = ADAM_B2 * v_ref[...] + (1.0 - ADAM_B2) * (g * g)
        m_hat = mm / (1.0 - ADAM_B1 ** ADAM_STEP)
        v_hat = vv / (1.0 - ADAM_B2 ** ADAM_STEP)
        g_ref[...] = g
        d_ref[...] = -ADAM_LR * (m_hat / (jnp.sqrt(v_hat) + ADAM_EPS) + ADAM_WD * w_ref[...])
        nm_ref[...] = mm
        nv_ref[...] = vv

    row = pl.BlockSpec((tr, C), lambda i: (off + i, 0))
    kept = [] if into is None else list(into)
    return _call(body, name=name, grid=(rows // tr,),
                 in_specs=[pl.BlockSpec((N_DEV, tr, C), lambda i: (0, i, 0)), row, row, row]
                 + [pl.BlockSpec(memory_space=pl.ANY)] * len(kept),
                 out_specs=[row] * 4, out_shape=[jax.ShapeDtypeStruct((R, C), F32)] * 4,
                 input_output_aliases={4 + q: q for q in range(len(kept))},
                 compiler_params=_params("parallel"))(piece, w, m, v, *kept)


def _pack(arrays):
    flat, meta, off = [], [], 0
    for a in arrays:
        flat.append(a.reshape(-1).astype(F32))
        meta.append((off, a.shape))
        off += a.size
    total = -(-off // (8 * LANES)) * (8 * LANES)
    flat.append(jnp.zeros((total - off,), F32))
    return jnp.concatenate(flat).reshape(total // LANES, LANES), meta


def _unpack(packed, meta):
    flat = packed.reshape(-1)
    return [flat[off:off + math.prod(shape)].reshape(shape) for off, shape in meta]


def kernel(x, ssd_in_proj, ssd_conv_w, ssd_conv_b, ssd_dt_bias, ssd_A_log, ssd_D, ssd_norm_w, ssd_out_proj, pool_w, pool_b, pool_scale, mlp_w1, mlp_w2, ln_mix_g, ln_mix_b, ln_ffn_g, ln_ffn_b, loss_target, m_ssd_in_proj, m_ssd_conv_w, m_ssd_conv_b, m_ssd_dt_bias, m_ssd_A_log, m_ssd_D, m_ssd_norm_w, m_ssd_out_proj, m_pool_w, m_pool_b, m_pool_scale, m_mlp_w1, m_mlp_w2, m_ln_mix_g, m_ln_mix_b, m_ln_ffn_g, m_ln_ffn_b, v_ssd_in_proj, v_ssd_conv_w, v_ssd_conv_b, v_ssd_dt_bias, v_ssd_A_log, v_ssd_D, v_ssd_norm_w, v_ssd_out_proj, v_pool_w, v_pool_b, v_pool_scale, v_mlp_w1, v_mlp_w2, v_ln_mix_g, v_ln_mix_b, v_ln_ffn_g, v_ln_ffn_b):
    T, D = x.shape[1], x.shape[2]
    depth = mlp_w1.shape[0]
    n_ssd, n_pool = ssd_in_proj.shape[0], pool_w.shape[0]
    d_inner = ssd_out_proj.shape[1] * N_DEV
    n_heads = d_inner // HEAD_DIM
    hg = n_heads // N_GROUPS
    d_bc = N_GROUPS * D_STATE
    d_xbc = d_inner + 2 * d_bc
    d_in_proj = ssd_in_proj.shape[2] * N_DEV
    d_ff = mlp_w1.shape[2] * N_DEV
    ng = len(POOL_WINDOWS)
    gd = D // ng
    alpha = (2.0 * depth) ** 0.25
    x0 = x.reshape(T, D)
    target = loss_target.reshape(T, D)

    assert depth == 4 and n_ssd == 2 and n_pool == 2, "the exchange schedules below are written for this stack"

    small_pack, small_meta = _pack([ssd_conv_w, pool_b, pool_scale])
    pw_rows = pool_w.shape[1] * pool_w.shape[2]
    in_b, out_b = ssd_in_proj.astype(BF16), ssd_out_proj.astype(BF16)
    pw_b = pool_w.reshape(n_pool, pw_rows, gd).astype(BF16)
    w1_b, w2_b = mlp_w1.astype(BF16), mlp_w2.astype(BF16)
    shard = {("in", 0): in_b[0], ("in", 1): in_b[1], ("out", 0): out_b[0], ("out", 1): out_b[1],
             ("pool", 0): pw_b[0], ("pool", 1): pw_b[1]}
    for i in range(depth):
        shard["w1", i], shard["w2", i] = w1_b[i], w2_b[i]

    def full_cols(g):
        return g.transpose(1, 0, 2).reshape(g.shape[1], -1)

    def full_rows(g):
        return g.reshape(-1, g.shape[-1])

    def full_pool(g):
        return g.reshape(N_DEV, ng, gd // N_DEV, gd).transpose(1, 0, 2, 3).reshape(ng, gd, gd)

    def slab_cols(g):
        return g.reshape(g.shape[0], N_DEV, -1).transpose(1, 0, 2)

    def slab_rows(g):
        return g.reshape(N_DEV, -1, g.shape[-1])

    def slab_pool(g):
        return g.astype(BF16).reshape(ng, N_DEV, gd // N_DEV, gd).transpose(1, 0, 2, 3).reshape(N_DEV, pw_rows, gd)

    to_full = {"in": full_cols, "out": full_rows, "pool": full_pool, "w1": lambda g: g, "w2": full_rows}

    gather1_on = {"in_proj_0": [("w1", 0), ("pool", 0)], "conv_fwd_0": [("out", 0)],
                  "ssd_fwd_0": [("w2", 0), ("w1", 1), ("w2", 1)], "out_proj_0": [("out", 1)], "mlp_up_0": [("in", 1)],
                  "mlp_down_0": [("w1", 2)], "mlp_up_1": [("w2", 2)], "mlp_down_1": [("w1", 3), ("pool", 1)],
                  "in_proj_2": [("w2", 3)]}
    gather2_on = {"conv_fwd_0": [("w1", 0), ("pool", 0)], "ssd_fwd_0": [("out", 0)],
                  "gnorm_fwd_0": [("w2", 0), ("w1", 1), ("w2", 1)], "mlp_up_0": [("out", 1)], "mlp_down_0": [("in", 1)],
                  "mlp_up_1": [("w1", 2)], "mlp_down_1": [("w2", 2)], "in_proj_2": [("w1", 3), ("pool", 1)],
                  "conv_fwd_2": [("w2", 3)]}
    hide_us = {"mlp_down_dx": 161, "mlp_down_dw": 163, "mlp_up_dx": 170, "mlp_up_dw": 164, "out_proj_dx": 83,
               "out_proj_dw": 85, "gnorm_bwd": 163, "ssd_bwd": 1104, "conv_bwd": 228, "in_proj_dw": 230, "in_proj_dx": 267}
    SLAB_BYTES_PER_US = 70e3 / (N_DEV - 1)
    OVERRUN_US, MIN_CARRIER_US = 0, 80
    n_pieces = {"in": 4, "out": 2, "pool": 1, "w1": 4, "w2": 4}
    W, half, G, R = {}, {}, {}, {}
    queue = []

    def produced(key, slabs):
        G[key] = slabs
        rows = slabs.shape[1] // n_pieces[key[0]]
        cost = rows * math.prod(slabs.shape[2:]) * slabs.dtype.itemsize / SLAB_BYTES_PER_US
        queue.extend((key, p * rows, rows, cost) for p in range(n_pieces[key[0]]))

    def take(budget):
        taken, used = [], 0.0
        while queue and ((not taken and budget >= MIN_CARRIER_US) or used + queue[0][3] <= budget + OVERRUN_US):
            taken.append(queue.pop(0))
            used += taken[-1][3]
        return taken

    def run(fn, *args, name, **kw):
        k1, k2 = gather1_on.get(name, []), gather2_on.get(name, [])
        budget = float("inf") if name == "in_proj_dx_0" else hide_us.get(name.rsplit("_", 1)[0], 0)
        pieces = take(budget) if not (k1 or k2) else []
        comm = ([("gather1", shard[k]) for k in k1] + [("gather2", half[k]) for k in k2]
                + [("slabs", G[key], r0, rows) for key, r0, rows, _ in pieces])
        if not comm:
            return fn(*args, name=name, **kw)
        res, got = fn(*args, name=name, comm=comm, **kw)
        for k, g in zip(k1, got):
            half[k] = g
        for k, g in zip(k2, got[len(k1):]):
            W[k] = to_full[k[0]](g)
        for (key, r0, _, _), g in zip(pieces, got):
            R.setdefault(key, []).append((r0, g))
        return res

    half_in0, g_small = _exchange([("gather1", shard["in", 0]), ("gather", small_pack)], "gather_first")
    (g_in0,) = _exchange([("gather2", half_in0)], "gather_first_onward")
    W["in", 0] = full_cols(g_in0)
    smalls = [_unpack(g_small[k], small_meta) for k in range(N_DEV)]
    conv_w = jnp.concatenate([s[0] for s in smalls], axis=-1).reshape(n_ssd, CONV_WIDTH, d_xbc)
    pool_bias = jnp.concatenate([s[1] for s in smalls], axis=-1).reshape(n_pool, 1, D)
    pool_sc = jnp.concatenate([s[2] for s in smalls], axis=-1).reshape(n_pool, 1, D)

    sel = (jnp.arange(d_inner)[:, None] // HEAD_DIM == jnp.arange(n_heads)[None, :]).astype(F32)

    saved = []
    h, hb = x0, x0.astype(BF16)
    for i in range(depth):
        j = i // 2
        s = {}
        s["x0"], s["x0b"] = h, hb
        if i % 2 == 0:
            (zx,) = run(_matmul, hb, W["in", j], "nn", name=f"in_proj_{i}", outs=[F32], tn=1152)
            xbc = run(_conv_fwd, zx, conv_w[j], ssd_conv_b[j], d_inner, name=f"conv_fwd_{i}")
            raw = zx[:, d_inner + d_xbc:]
            dt, cs = _dt_fwd(raw, ssd_dt_bias[j], ssd_A_log[j], f"dt_fwd_{i}")
            dtc, csc, csr = _cols(dt, hg), _cols(cs, hg), _rows(cs, hg)
            y2, states = run(_ssd_fwd, xbc, dtc, csc, csr, d_inner, name=f"ssd_fwd_{i}")
            dvec = jnp.repeat(ssd_D[j], HEAD_DIM).reshape(1, d_inner)
            nw = ssd_norm_w[j].reshape(1, d_inner)
            gn = run(_gnorm_fwd, y2, xbc, zx, dvec, nw, d_inner, name=f"gnorm_fwd_{i}")
            (mix,) = run(_matmul, gn, W["out", j], "nn", name=f"out_proj_{i}", outs=[F32])
            s.update(zx=zx, xbc=xbc, raw=raw, dtc=dtc, csc=csc, csr=csr, y2=y2, states=states, dvec=dvec, nw=nw, gn=gn)
        else:
            mix = _pool_fwd(h, W["pool", j], pool_bias[j], pool_sc[j], f"pool_fwd_{i}")
        s["mix"] = mix
        x1, x1b = _ln_fwd(h, mix, ln_mix_g[i], ln_mix_b[i], alpha, f"ln_mix_fwd_{i}")
        u, hh = run(_matmul, x1b, W["w1", i], "nn", name=f"mlp_up_{i}", outs=[F32, BF16],
                    epilogue=lambda acc: (acc, jnp.square(jnp.maximum(acc, 0.0))), cols_by_device=True)
        (m2,) = run(_matmul, hh, W["w2", i], "nn", name=f"mlp_down_{i}", outs=[F32])
        x2, x2b = _ln_fwd(x1, m2, ln_ffn_g[i], ln_ffn_b[i], alpha, f"ln_ffn_fwd_{i}")
        s.update(x1=x1, x1b=x1b, u=u, hh=hh, m2=m2)
        saved.append(s)
        h, hb = x2, x2b

    loss_row, dh = _loss_head(h, target)
    loss = lax.psum(loss_row[0, 0], ("x", "y", "c"))

    big = {"in": (ssd_in_proj, m_ssd_in_proj, v_ssd_in_proj), "out": (ssd_out_proj, m_ssd_out_proj, v_ssd_out_proj),
           "pool": (pool_w, m_pool_w, v_pool_w), "w1": (mlp_w1, m_mlp_w1, v_mlp_w1), "w2": (mlp_w2, m_mlp_w2, v_mlp_w2)}

    def update(kind, count):
        shape = big[kind][0].shape
        w, m, v = (a.reshape(-1, shape[-1]) for a in big[kind])
        per_layer = w.shape[0] // count
        outs = None
        for l in range(count):
            for r0, g in sorted(R[kind, l], key=lambda t: t[0]):
                outs = _adamw(g, w, m, v, l * per_layer + r0, outs, f"adamw_{kind}_{l}_{r0}")
        return [o.reshape(shape) for o in outs]

    gr = {k: [None] * depth for k in ("ln_mix_g", "ln_mix_b", "ln_ffn_g", "ln_ffn_b")}
    gs = {k: [None] * n_ssd for k in ("conv_w", "conv_b", "dt_bias", "A_log", "D", "norm_w")}
    gp = {k: [None] * n_pool for k in ("b", "scale")}
    for i in reversed(range(depth)):
        j = i // 2
        s = saved[i]
        ds2, ds2b, gr["ln_ffn_g"][i], gr["ln_ffn_b"][i] = _ln_bwd(s["x1"], s["m2"], ln_ffn_g[i], dh, alpha, f"ln_ffn_bwd_{i}")
        (du,) = run(_matmul, ds2b, W["w2", i], "nt", name=f"mlp_down_dx_{i}", outs=[BF16], extras=[s["u"]],
                    epilogue=lambda acc, u_: (acc * (2.0 * jnp.maximum(u_, 0.0)),))
        (g_w2,) = run(_matmul, s["hh"], ds2b, "tn", name=f"mlp_down_dw_{i}", outs=[BF16])
        produced(("w2", i), slab_rows(g_w2))
        (dx1,) = run(_matmul, du, W["w1", i], "nt", name=f"mlp_up_dx_{i}", outs=[F32], extras=[ds2],
                     epilogue=lambda acc, e: (acc + alpha * e,), cols_by_device=True)
        (g_w1,) = run(_matmul, s["x1b"], du, "tn", name=f"mlp_up_dw_{i}", outs=[BF16], cols_by_device=True)
        produced(("w1", i), g_w1)
        ds1, ds1b, gr["ln_mix_g"][i], gr["ln_mix_b"][i] = _ln_bwd(s["x0"], s["mix"], ln_mix_g[i], dx1, alpha, f"ln_mix_bwd_{i}")
        if i % 2 == 0:
            (dgn,) = run(_matmul, ds1b, W["out", j], "nt", name=f"out_proj_dx_{i}", outs=[F32])
            (g_out,) = run(_matmul, s["gn"], ds1b, "tn", name=f"out_proj_dw_{i}", outs=[BF16])
            produced(("out", j), slab_rows(g_out))
            dy, dzx, gs["norm_w"][j], gs["D"][j] = run(
                _gnorm_bwd, s["y2"], s["xbc"], s["zx"], s["dvec"], s["nw"], dgn, sel, d_inner, name=f"gnorm_bwd_{i}")
            dxs, dB, dC, dcs, dtot, dxdtx = run(_ssd_bwd, s["xbc"], s["dtc"], s["csc"], s["csr"], s["states"], s["y2"], dy,
                                                d_inner, name=f"ssd_bwd_{i}")
            dzx, gs["dt_bias"][j], gs["A_log"][j] = _dt_bwd(
                s["raw"], ssd_dt_bias[j], ssd_A_log[j], _uncols(dcs), _uncols(dtot), _uncols(dxdtx), dzx, f"dt_bwd_{i}")
            dpre = _conv_dpre(s["zx"], conv_w[j], ssd_conv_b[j], d_inner, 0, dxs, dy, f"conv_dpre_x_{i}",
                              extra_scale=s["dvec"])
            dpre = _conv_dpre(s["zx"], conv_w[j], ssd_conv_b[j], d_inner, d_inner, dB, None, f"conv_dpre_b_{i}", into=dpre)
            dpre = _conv_dpre(s["zx"], conv_w[j], ssd_conv_b[j], d_inner, d_inner + d_bc, dC, None, f"conv_dpre_c_{i}",
                              into=dpre)
            dzx, gs["conv_w"][j], gs["conv_b"][j] = run(_conv_bwd, s["zx"], dpre, conv_w[j], d_inner, dzx,
                                                        name=f"conv_bwd_{i}")
            (g_in,) = run(_matmul, s["x0b"], dzx, "tn", name=f"in_proj_dw_{i}", outs=[BF16], tn=1152)
            produced(("in", j), slab_cols(g_in))
            (dh,) = run(_matmul, dzx, W["in", j], "nt", name=f"in_proj_dx_{i}", outs=[F32], extras=[ds1], tk=1152,
                        epilogue=lambda acc, e: (acc + alpha * e,))
        else:
            dm, g_pw, gp["b"][j], gp["scale"][j] = _pool_bwd_a(
                s["x0"], W["pool", j], pool_bias[j], pool_sc[j], ds1, f"pool_bwd_a_{i}")
            produced(("pool", j), slab_pool(g_pw))
            dh = _pool_bwd_b(dm, ds1, alpha, f"pool_bwd_b_{i}")
    grad_x = dh.reshape(x.shape)

    g_conv_w = jnp.stack(gs["conv_w"]).reshape(n_ssd, CONV_WIDTH, 1, N_DEV, d_xbc // N_DEV)
    g_pool_b = jnp.stack(gp["b"]).reshape(n_pool, ng, N_DEV, gd // N_DEV)
    g_pool_s = jnp.stack(gp["scale"]).reshape(n_pool, N_DEV, D // N_DEV)
    s_small = jnp.stack([_pack([g_conv_w[:, :, :, k], g_pool_b[:, :, k], g_pool_s[:, k]])[0] for k in range(N_DEV)])
    repl_grads = [jnp.stack(gs["conv_b"]).reshape(ssd_conv_b.shape), jnp.stack(gs["dt_bias"]).reshape(ssd_dt_bias.shape),
                  jnp.stack(gs["A_log"]).reshape(ssd_A_log.shape), jnp.stack(gs["D"]).reshape(ssd_D.shape),
                  jnp.stack(gs["norm_w"]).reshape(ssd_norm_w.shape),
                  jnp.stack(gr["ln_mix_g"]).reshape(ln_mix_g.shape), jnp.stack(gr["ln_mix_b"]).reshape(ln_mix_b.shape),
                  jnp.stack(gr["ln_ffn_g"]).reshape(ln_ffn_g.shape), jnp.stack(gr["ln_ffn_b"]).reshape(ln_ffn_b.shape)]
    repl_pack, repl_meta = _pack(repl_grads)
    s_repl = jnp.broadcast_to(repl_pack[None], (N_DEV,) + repl_pack.shape)
    left = list(queue)
    del queue[:]
    got = _exchange([("slabs", s_small, 0, s_small.shape[1]), ("slabs", s_repl, 0, s_repl.shape[1])]
                    + [("slabs", G[key], r0, rows) for key, r0, rows, _ in left], "exchange_last")
    r_small, r_repl = got[0], got[1]
    for (key, r0, _, _), g in zip(left, got[2:]):
        R.setdefault(key, []).append((r0, g))

    upd = {}
    for nm, kind, count in (("ssd_in_proj", "in", n_ssd), ("ssd_out_proj", "out", n_ssd), ("pool_w", "pool", n_pool),
                            ("mlp_w1", "w1", depth), ("mlp_w2", "w2", depth)):
        upd[nm] = update(kind, count)
    sm = _adamw(r_small, small_pack, _pack([m_ssd_conv_w, m_pool_b, m_pool_scale])[0],
                _pack([v_ssd_conv_w, v_pool_b, v_pool_scale])[0], 0, None, "adamw_small_sharded")
    for idx, nm in enumerate(["ssd_conv_w", "pool_b", "pool_scale"]):
        upd[nm] = [_unpack(r, small_meta)[idx] for r in sm]
    repl_names = ["ssd_conv_b", "ssd_dt_bias", "ssd_A_log", "ssd_D", "ssd_norm_w",
                  "ln_mix_g", "ln_mix_b", "ln_ffn_g", "ln_ffn_b"]
    repl_w = [ssd_conv_b, ssd_dt_bias, ssd_A_log, ssd_D, ssd_norm_w, ln_mix_g, ln_mix_b, ln_ffn_g, ln_ffn_b]
    repl_m = [m_ssd_conv_b, m_ssd_dt_bias, m_ssd_A_log, m_ssd_D, m_ssd_norm_w, m_ln_mix_g, m_ln_mix_b, m_ln_ffn_g, m_ln_ffn_b]
    repl_v = [v_ssd_conv_b, v_ssd_dt_bias, v_ssd_A_log, v_ssd_D, v_ssd_norm_w, v_ln_mix_g, v_ln_mix_b, v_ln_ffn_g, v_ln_ffn_b]
    rp = _adamw(r_repl, _pack(repl_w)[0], _pack(repl_m)[0], _pack(repl_v)[0], 0, None, "adamw_replicated")
    for idx, nm in enumerate(repl_names):
        upd[nm] = [_unpack(r, repl_meta)[idx] for r in rp]

    order = ["ssd_in_proj", "ssd_conv_w", "ssd_conv_b", "ssd_dt_bias", "ssd_A_log", "ssd_D", "ssd_norm_w",
             "ssd_out_proj", "pool_w", "pool_b", "pool_scale", "mlp_w1", "mlp_w2",
             "ln_mix_g", "ln_mix_b", "ln_ffn_g", "ln_ffn_b"]
    return (loss, grad_x, *[upd[n][0] for n in order], *[upd[n][1] for n in order],
            *[upd[n][2] for n in order], *[upd[n][3] for n in order])
```

```python
import functools
import math

import jax
import jax.numpy as jnp
from jax import lax
from jax.experimental import pallas as pl
from jax.experimental.pallas import tpu as pltpu

F32 = jnp.float32
BF16 = jnp.bfloat16

N_DEV = 8
HEAD_DIM = 64
N_GROUPS = 8
D_STATE = 128
CHUNK = 128
CONV_WIDTH = 5
POOL_WINDOWS = (2, 4, 8, 16)
HALO = 8
LN_EPS = 1e-5
RMS_EPS = 1e-5
ADAM_LR = 0.001
ADAM_B1 = 0.9
ADAM_B2 = 0.999
ADAM_EPS = 1e-08
ADAM_WD = 0.01
ADAM_STEP = 10
LANES = 128
VMEM_LIMIT_BYTES = 56 * 1024 * 1024
HIGHEST = lax.Precision.HIGHEST


def _pallas(body, **kw):
    return pl.pallas_call(body, **kw)


def _params(*sem):
    return pltpu.CompilerParams(dimension_semantics=sem, vmem_limit_bytes=VMEM_LIMIT_BYTES)


def _mesh_pos():
    return lax.axis_index("x"), lax.axis_index("y"), lax.axis_index("c")


def _peer(x, y, c, k):
    dx, dy, dc = (k >> 2) & 1, (k >> 1) & 1, k & 1
    px = (1 - x) if dx else x
    py = (1 - y) if dy else y
    pc = (1 - c) if dc else c
    return px, py, pc


def _comm_copies(ops, ins, outs, send_sems, recv_sems, local_sems):
    x, y, c = _mesh_pos()
    me = 4 * x + 2 * y + c
    copies = []
    for a, op in enumerate(ops):
        src, dst = ins[a], outs[a]

        def remote(k, s, d, to):
            return pltpu.make_async_remote_copy(src_ref=s, dst_ref=d, send_sem=send_sems.at[a, k - 1],
                                                recv_sem=recv_sems.at[a, k - 1], device_id=to,
                                                device_id_type=pl.DeviceIdType.MESH)

        if op[0] in ("gather", "gather1"):
            copies.append(pltpu.make_async_copy(src, dst.at[me], local_sems.at[a]))
            for k in (range(1, N_DEV) if op[0] == "gather" else (1, 2, 4, 6)):
                copies.append(remote(k, src, dst.at[me], _peer(x, y, c, k)))
        elif op[0] == "gather2":
            for k in (2, 4, 6):
                qx, qy, qc = _peer(x, y, c, k)
                slot = 4 * qx + 2 * qy + qc
                copies.append(remote(k, src.at[slot], dst.at[slot], _peer(x, y, c, 1)))
        else:
            rows = pl.ds(op[1], op[2])
            copies.append(pltpu.make_async_copy(src.at[me, rows], dst.at[me], local_sems.at[a]))
            for k in range(1, N_DEV):
                px, py, pc = _peer(x, y, c, k)
                copies.append(remote(k, src.at[4 * px + 2 * py + pc, rows], dst.at[me], (px, py, pc)))
    return copies


def _comm_plan(comm):
    ops = [(c[0],) + tuple(c[2:]) for c in comm]
    arrays = [c[1] for c in comm]
    shapes = []
    for op, a in zip(ops, arrays):
        if op[0] in ("gather", "gather1"):
            shapes.append(jax.ShapeDtypeStruct((N_DEV,) + a.shape, a.dtype))
        elif op[0] == "gather2":
            shapes.append(jax.ShapeDtypeStruct(a.shape, a.dtype))
        else:
            shapes.append(jax.ShapeDtypeStruct((N_DEV, op[2]) + a.shape[2:], a.dtype))
    n = len(ops)
    sems = [pltpu.SemaphoreType.DMA((n, N_DEV - 1)), pltpu.SemaphoreType.DMA((n, N_DEV - 1)),
            pltpu.SemaphoreType.DMA((n,))]
    in_place = [a for a, op in enumerate(ops) if op[0] == "gather2"]
    return ops, arrays, shapes, sems, in_place


def _call(body, *, comm=None, **kw):
    if not comm:
        return _pallas(body, **kw)
    ops, arrays, c_shape, c_sems, in_place = _comm_plan(comm)
    n = len(ops)
    grid = tuple(kw["grid"])
    single = not isinstance(kw["out_shape"], (list, tuple))
    out_shape = [kw["out_shape"]] if single else list(kw["out_shape"])
    out_specs = [kw["out_specs"]] if single else list(kw["out_specs"])
    in_specs = list(kw["in_specs"])
    scratch = list(kw.get("scratch_shapes", ()))
    n_in, n_out, n_scr = len(in_specs), len(out_shape), len(scratch)

    def wrapped(*refs):
        ins, refs = refs[:n_in], refs[n_in:]
        c_ins, refs = refs[:n], refs[n:]
        outs, refs = refs[:n_out], refs[n_out:]
        c_outs, refs = refs[:n], refs[n:]
        scr, sems = refs[:n_scr], refs[n_scr:]
        first = last = None
        for ax, size in enumerate(grid):
            i = pl.program_id(ax)
            first = (i == 0) if first is None else first & (i == 0)
            last = (i == size - 1) if last is None else last & (i == size - 1)

        @pl.when(first)
        def _():
            for cp in _comm_copies(ops, c_ins, c_outs, *sems):
                cp.start()

        body(*ins, *outs, *scr)

        @pl.when(last)
        def _():
            for cp in _comm_copies(ops, c_ins, c_outs, *sems):
                cp.wait()

    any_spec = pl.BlockSpec(memory_space=pl.ANY)
    call = _pallas(wrapped, name=kw["name"], grid=grid, in_specs=in_specs + [any_spec] * n,
                   out_specs=out_specs + [any_spec] * n, out_shape=out_shape + c_shape,
                   scratch_shapes=scratch + c_sems,
                   input_output_aliases={**kw.get("input_output_aliases", {}), **{n_in + a: n_out + a for a in in_place}},
                   compiler_params=_params(*(("arbitrary",) * len(grid))))

    def run(*args):
        res = call(*args, *arrays)
        own = res[:n_out]
        return (own[0] if single else list(own)), list(res[n_out:])

    return run


def _tile(dim, target, align=LANES):
    if dim <= target:
        return dim
    t = (target // align) * align
    while t >= align:
        if dim % t == 0:
            return t
        t -= align
    return dim


def _dot(a, b, dims, precision=None):
    return lax.dot_general(a, b, (dims, ((), ())), precision=precision, preferred_element_type=F32)


NN = ((1,), (0,))
NT = ((1,), (1,))
TN = ((0,), (0,))


def _sigmoid(x):
    return 1.0 / (1.0 + jnp.exp(-x))


def _matmul(a, b, mode, *, name, outs, epilogue=None, extras=(), tm=1024, tn=1024, tk=2048, comm=None,
            cols_by_device=False):
    if mode == "nn":
        (M, K), (K2, N) = a.shape, (b.shape[-2:] if not cols_by_device else (b.shape[1], N_DEV * b.shape[2]))
    elif mode == "nt":
        (M, K), (N, K2) = a.shape, (b.shape if not cols_by_device else (b.shape[1], N_DEV * b.shape[2]))
    else:
        (K, M), (K2, N) = a.shape, b.shape
    assert K == K2, (a.shape, b.shape, mode)
    per_dev = (K if mode == "nt" else N) // N_DEV
    if cols_by_device and mode == "nt":
        tk = min(tk, per_dev)
    elif cols_by_device:
        tn = min(tn, per_dev)
    tm, tn, tk = _tile(M, tm), _tile(N, tn), _tile(K, tk)
    nk = K // tk
    dims = {"nn": NN, "nt": NT, "tn": TN}[mode]
    a_spec = (pl.BlockSpec((tk, tm), lambda i, j, k: (k, i)) if mode == "tn"
              else pl.BlockSpec((tm, tk), lambda i, j, k: (i, k)))
    b_spec = (pl.BlockSpec((tn, tk), lambda i, j, k: (j, k)) if mode == "nt"
              else pl.BlockSpec((tk, tn), lambda i, j, k: (k, j)))
    mn_spec = pl.BlockSpec((tm, tn), lambda i, j, k: (i, j))
    out_spec, out_dims = mn_spec, (M, N)
    if cols_by_device and mode == "nn":
        r = per_dev // tn
        b_spec = pl.BlockSpec((None, tk, tn), lambda i, j, k: (j // r, k, j % r))
    elif cols_by_device and mode == "nt":
        r = per_dev // tk
        b_spec = pl.BlockSpec((None, tn, tk), lambda i, j, k: (k // r, j, k % r))
    elif cols_by_device:
        assert not extras and epilogue is None
        r = per_dev // tn
        out_spec, out_dims = pl.BlockSpec((None, tm, tn), lambda i, j, k: (j // r, i, j % r)), (N_DEV, M, per_dev)
    n_extra, n_out = len(extras), len(outs)

    def finish(acc, extra_refs, out_refs):
        res = (acc,) if epilogue is None else epilogue(acc, *[r[...] for r in extra_refs])
        for o_ref, r in zip(out_refs, res):
            o_ref[...] = r.astype(o_ref.dtype)

    def body(*refs):
        a_ref, b_ref = refs[0], refs[1]
        extra_refs = refs[2:2 + n_extra]
        out_refs = refs[2 + n_extra:2 + n_extra + n_out]
        part = _dot(a_ref[...].astype(BF16), b_ref[...].astype(BF16), dims)
        if nk == 1:
            finish(part, extra_refs, out_refs)
            return
        acc_ref = refs[-1]
        k = pl.program_id(2)

        @pl.when(k == 0)
        def _():
            acc_ref[...] = part

        @pl.when((k > 0) & (k < nk - 1))
        def _():
            acc_ref[...] += part

        @pl.when(k == nk - 1)
        def _():
            finish(acc_ref[...] + part, extra_refs, out_refs)

    res = _call(
        body, comm=comm, name=name, grid=(M // tm, N // tn, nk),
        in_specs=[a_spec, b_spec] + [mn_spec] * n_extra,
        out_specs=[out_spec] * n_out,
        out_shape=[jax.ShapeDtypeStruct(out_dims, dt) for dt in outs],
        scratch_shapes=[pltpu.VMEM((tm, tn), F32)] if nk > 1 else [],
        compiler_params=_params("parallel", "parallel", "arbitrary"),
    )(a, b, *extras)
    return res


def _ln_fwd(x, f, g, b, alpha, name):
    T, D = x.shape
    tm = _tile(T, 256, 8)

    def body(x_ref, f_ref, g_ref, b_ref, y_ref, yb_ref):
        s = alpha * x_ref[...] + f_ref[...]
        mu = jnp.mean(s, axis=-1, keepdims=True)
        d = s - mu
        var = jnp.mean(d * d, axis=-1, keepdims=True)
        y = d * lax.rsqrt(var + LN_EPS) * g_ref[...] + b_ref[...]
        y_ref[...] = y
        yb_ref[...] = y.astype(BF16)

    row = pl.BlockSpec((tm, D), lambda i: (i, 0))
    vec = pl.BlockSpec((1, D), lambda i: (0, 0))
    return _call(body, name=name, grid=(T // tm,), in_specs=[row, row, vec, vec], out_specs=[row, row],
                 out_shape=[jax.ShapeDtypeStruct((T, D), F32), jax.ShapeDtypeStruct((T, D), BF16)],
                 compiler_params=_params("parallel"))(x, f, g.reshape(1, D), b.reshape(1, D))


def _ln_bwd(x, f, g, dy, alpha, name):
    T, D = x.shape
    tm = _tile(T, 256, 8)

    def body(x_ref, f_ref, g_ref, dy_ref, ds_ref, dsb_ref, dg_ref, db_ref):
        i = pl.program_id(0)

        @pl.when(i == 0)
        def _():
            dg_ref[...] = jnp.zeros_like(dg_ref)
            db_ref[...] = jnp.zeros_like(db_ref)

        s = alpha * x_ref[...] + f_ref[...]
        mu = jnp.mean(s, axis=-1, keepdims=True)
        d = s - mu
        var = jnp.mean(d * d, axis=-1, keepdims=True)
        rstd = lax.rsqrt(var + LN_EPS)
        xhat = d * rstd
        dy_ = dy_ref[...]
        dg_ref[...] += jnp.sum(dy_ * xhat, axis=0, keepdims=True)
        db_ref[...] += jnp.sum(dy_, axis=0, keepdims=True)
        dxh = dy_ * g_ref[...]
        m1 = jnp.mean(dxh, axis=-1, keepdims=True)
        m2 = jnp.mean(dxh * xhat, axis=-1, keepdims=True)
        ds = rstd * (dxh - m1 - xhat * m2)
        ds_ref[...] = ds
        dsb_ref[...] = ds.astype(BF16)

    row = pl.BlockSpec((tm, D), lambda i: (i, 0))
    vec = pl.BlockSpec((1, D), lambda i: (0, 0))
    return _call(body, name=name, grid=(T // tm,), in_specs=[row, row, vec, row], out_specs=[row, row, vec, vec],
                 out_shape=[jax.ShapeDtypeStruct((T, D), F32), jax.ShapeDtypeStruct((T, D), BF16),
                            jax.ShapeDtypeStruct((1, D), F32), jax.ShapeDtypeStruct((1, D), F32)],
                 compiler_params=_params("arbitrary"))(x, f, g.reshape(1, D), dy)


def _loss_head(y, target):
    T, D = y.shape
    tm = _tile(T, 256, 8)

    def body(y_ref, t_ref, loss_ref, dy_ref):
        i = pl.program_id(0)

        @pl.when(i == 0)
        def _():
            loss_ref[...] = jnp.zeros_like(loss_ref)

        err = y_ref[...] - t_ref[...]
        dy_ref[...] = err * (1.0 / D)
        per_tok = jnp.mean(err * err, axis=-1, keepdims=True)
        loss_ref[...] += 0.5 * jnp.sum(per_tok)

    row = pl.BlockSpec((tm, D), lambda i: (i, 0))
    return _call(body, name="loss_head", grid=(T // tm,), in_specs=[row, row],
                 out_specs=[pl.BlockSpec((1, LANES), lambda i: (0, 0)), row],
                 out_shape=[jax.ShapeDtypeStruct((1, LANES), F32), jax.ShapeDtypeStruct((T, D), F32)],
                 compiler_params=_params("arbitrary"))(y, target)


def _halo_specs(tt, cw, col_of, n_tiles, grid_rank_tokens_axis):
    per = tt // HALO
    ax = grid_rank_tokens_axis

    def cur(*g):
        return (g[ax], col_of(*g))

    def prev(*g):
        return (jnp.maximum(g[ax] * per - 1, 0), col_of(*g))

    def nxt(*g):
        return (jnp.minimum((g[ax] + 1) * per, n_tiles * per - 1), col_of(*g))

    return [pl.BlockSpec((tt, cw), cur), pl.BlockSpec((HALO, cw), prev), pl.BlockSpec((HALO, cw), nxt)]


def _fill_ext(ext_ref, cur_ref, prev_ref, next_ref, i, n_tiles, tt):
    ext_ref[pl.ds(0, HALO), :] = jnp.where(i > 0, prev_ref[...], 0.0)
    ext_ref[pl.ds(HALO, tt), :] = cur_ref[...]
    ext_ref[pl.ds(HALO + tt, HALO), :] = jnp.where(i < n_tiles - 1, next_ref[...], 0.0)


def _conv_pre(ext_ref, w, bias, tt, lo=0, n=None):
    n = tt if n is None else n
    pad = CONV_WIDTH // 2
    acc = None
    for k in range(CONV_WIDTH):
        term = ext_ref[pl.ds(HALO + lo + k - pad, n), :] * w[k:k + 1, :]
        acc = term if acc is None else acc + term
    return acc + bias


def _conv_fwd(zx, conv_w, conv_b, d_inner, name, comm=None):
    T = zx.shape[0]
    d_xbc = conv_w.shape[1]
    cw = _tile(d_xbc, 512)
    assert d_inner % cw == 0
    off = d_inner // cw
    tt = _tile(T, 512, 8)
    nt = T // tt

    def body(cur_ref, prev_ref, next_ref, w_ref, b_ref, o_ref, ext_ref):
        i = pl.program_id(1)
        _fill_ext(ext_ref, cur_ref, prev_ref, next_ref, i, nt, tt)
        pre = _conv_pre(ext_ref, w_ref[...], b_ref[...], tt)
        o_ref[...] = pre * _sigmoid(pre)

    specs = _halo_specs(tt, cw, lambda j, i: off + j, nt, 1)
    return _call(body, comm=comm, name=name, grid=(d_xbc // cw, nt),
                 in_specs=specs + [pl.BlockSpec((CONV_WIDTH, cw), lambda j, i: (0, j)),
                                   pl.BlockSpec((1, cw), lambda j, i: (0, j))],
                 out_specs=pl.BlockSpec((tt, cw), lambda j, i: (i, j)),
                 out_shape=jax.ShapeDtypeStruct((T, d_xbc), F32),
                 scratch_shapes=[pltpu.VMEM((tt + 2 * HALO, cw), F32)],
                 compiler_params=_params("parallel", "parallel"))(zx, zx, zx, conv_w, conv_b.reshape(1, d_xbc))


def _conv_dpre(zx, conv_w, conv_b, d_inner, col_lo, dirs, extra, name, into=None, extra_scale=None):
    T = zx.shape[0]
    ncols = dirs.shape[2]
    cw = _tile(ncols, 512)
    assert d_inner % cw == 0 and col_lo % cw == 0
    off_zx = (d_inner + col_lo) // cw
    off_w = col_lo // cw
    tt = _tile(T, 512, 8)
    nt = T // tt
    has_extra = extra is not None

    def body(cur_ref, prev_ref, next_ref, w_ref, b_ref, dirs_ref, *rest):
        o_ref, ext_ref = rest[-2], rest[-1]
        i = pl.program_id(1)
        _fill_ext(ext_ref, cur_ref, prev_ref, next_ref, i, nt, tt)
        pre = _conv_pre(ext_ref, w_ref[...], b_ref[...], tt)
        sig = _sigmoid(pre)
        dact = dirs_ref[0] + dirs_ref[1]
        if has_extra:
            dact = dact + rest[0][...] * rest[1][...]
        o_ref[...] = dact * (sig * (1.0 + pre * (1.0 - sig)))

    specs = _halo_specs(tt, cw, lambda j, i: off_zx + j, nt, 1)
    in_specs = specs + [pl.BlockSpec((CONV_WIDTH, cw), lambda j, i: (0, off_w + j)),
                        pl.BlockSpec((1, cw), lambda j, i: (0, off_w + j)),
                        pl.BlockSpec((2, tt, cw), lambda j, i: (0, i, j))]
    args = [zx, zx, zx, conv_w, conv_b.reshape(1, -1), dirs]
    if has_extra:
        in_specs += [pl.BlockSpec((tt, cw), lambda j, i: (i, j)), pl.BlockSpec((1, cw), lambda j, i: (0, j))]
        args += [extra, extra_scale]
    aliases = {}
    if into is not None:
        in_specs.append(pl.BlockSpec(memory_space=pl.ANY))
        args.append(into)
        aliases = {len(args) - 1: 0}
    return _call(body, name=name, grid=(ncols // cw, nt), in_specs=in_specs,
                 out_specs=pl.BlockSpec((tt, cw), lambda j, i: (i, off_w + j)),
                 out_shape=jax.ShapeDtypeStruct((T, conv_w.shape[1]), F32),
                 scratch_shapes=[pltpu.VMEM((tt + 2 * HALO, cw), F32)], input_output_aliases=aliases,
                 compiler_params=_params("parallel", "parallel"))(*args)


def _conv_bwd(zx, dpre, conv_w, d_inner, into, name, comm=None):
    T = zx.shape[0]
    d_xbc = conv_w.shape[1]
    cw = _tile(d_xbc, 512)
    off = d_inner // cw
    tt = _tile(T, 512, 8)
    nt = T // tt
    pad = CONV_WIDTH // 2

    def body(zc, zp, zn, dc, dp, dn, w_ref, into_ref, din_ref, dw_ref, db_ref, zext, dext):
        i = pl.program_id(1)

        @pl.when(i == 0)
        def _():
            dw_ref[...] = jnp.zeros_like(dw_ref)
            db_ref[...] = jnp.zeros_like(db_ref)

        _fill_ext(zext, zc, zp, zn, i, nt, tt)
        _fill_ext(dext, dc, dp, dn, i, nt, tt)
        w = w_ref[...]
        d = dc[...]
        acc = None
        for k in range(CONV_WIDTH):
            term = dext[pl.ds(HALO + pad - k, tt), :] * w[k:k + 1, :]
            acc = term if acc is None else acc + term
            dw_ref[k:k + 1, :] += jnp.sum(d * zext[pl.ds(HALO + k - pad, tt), :], axis=0, keepdims=True)
        din_ref[...] = acc.astype(din_ref.dtype)
        db_ref[...] += jnp.sum(d, axis=0, keepdims=True)

    zspecs = _halo_specs(tt, cw, lambda j, i: off + j, nt, 1)
    dspecs = _halo_specs(tt, cw, lambda j, i: j, nt, 1)
    return _call(body, comm=comm, name=name, grid=(d_xbc // cw, nt),
                 in_specs=zspecs + dspecs + [pl.BlockSpec((CONV_WIDTH, cw), lambda j, i: (0, j)),
                                             pl.BlockSpec(memory_space=pl.ANY)],
                 out_specs=[pl.BlockSpec((tt, cw), lambda j, i: (i, off + j)),
                            pl.BlockSpec((CONV_WIDTH, cw), lambda j, i: (0, j)),
                            pl.BlockSpec((1, cw), lambda j, i: (0, j))],
                 out_shape=[jax.ShapeDtypeStruct(into.shape, into.dtype), jax.ShapeDtypeStruct((CONV_WIDTH, d_xbc), F32),
                            jax.ShapeDtypeStruct((1, d_xbc), F32)],
                 scratch_shapes=[pltpu.VMEM((tt + 2 * HALO, cw), F32), pltpu.VMEM((tt + 2 * HALO, cw), F32)],
                 input_output_aliases={7: 0},
                 compiler_params=_params("parallel", "arbitrary"))(zx, zx, zx, dpre, dpre, dpre, conv_w, into)


def _tri(n):
    r = lax.broadcasted_iota(jnp.int32, (n, n), 0)
    c = lax.broadcasted_iota(jnp.int32, (n, n), 1)
    return (r >= c).astype(F32), (r <= c).astype(F32)


def _dt_fwd(raw, bias, a_log, name):
    T, H2 = raw.shape
    half = H2 // 2

    def body(raw_ref, bias_ref, alog_ref, dt_ref, cs_ref):
        x = raw_ref[...] + bias_ref[...]
        dt = jnp.maximum(x, 0.0) + jnp.log(1.0 + jnp.exp(-jnp.abs(x)))
        a = dt * (-jnp.exp(alog_ref[...]))
        lower, upper = _tri(CHUNK)
        cs_f = _dot(lower, a, NN, HIGHEST)
        cs_b = _dot(upper, a, NN, HIGHEST)
        lane = lax.broadcasted_iota(jnp.int32, (CHUNK, H2), 1)
        dt_ref[...] = dt
        cs_ref[...] = jnp.where(lane < half, cs_f, cs_b)

    row = pl.BlockSpec((CHUNK, H2), lambda c: (c, 0))
    vec = pl.BlockSpec((1, H2), lambda c: (0, 0))
    return _call(body, name=name, grid=(T // CHUNK,), in_specs=[row, vec, vec], out_specs=[row, row],
                 out_shape=[jax.ShapeDtypeStruct((T, H2), F32)] * 2,
                 compiler_params=_params("parallel"))(raw, bias.reshape(1, H2), a_log.reshape(1, H2))


def _dt_bwd(raw, bias, a_log, dcs, dtot, dxdtx, into, name):
    T, H2 = raw.shape
    half = H2 // 2
    assert into.shape[1] % H2 == 0
    last = into.shape[1] // H2 - 1

    def body(raw_ref, bias_ref, alog_ref, dcs_ref, dtot_ref, dx_ref, into_ref, draw_ref, dbias_ref, dalog_ref):
        c = pl.program_id(0)

        @pl.when(c == 0)
        def _():
            dbias_ref[...] = jnp.zeros_like(dbias_ref)
            dalog_ref[...] = jnp.zeros_like(dalog_ref)

        x = raw_ref[...] + bias_ref[...]
        dt = jnp.maximum(x, 0.0) + jnp.log(1.0 + jnp.exp(-jnp.abs(x)))
        A = -jnp.exp(alog_ref[...])
        lower, upper = _tri(CHUNK)
        g = dcs_ref[...]
        lane = lax.broadcasted_iota(jnp.int32, (CHUNK, H2), 1)
        da = jnp.where(lane < half, _dot(upper, g, NN, HIGHEST), _dot(lower, g, NN, HIGHEST)) + dtot_ref[...]
        ddt = da * A + dx_ref[...]
        draw = ddt * _sigmoid(x)
        draw_ref[...] = draw.astype(draw_ref.dtype)
        dbias_ref[...] += jnp.sum(draw, axis=0, keepdims=True)
        dalog_ref[...] += jnp.sum(da * dt, axis=0, keepdims=True) * A

    row = pl.BlockSpec((CHUNK, H2), lambda c: (c, 0))
    vec = pl.BlockSpec((1, H2), lambda c: (0, 0))
    return _call(body, name=name, grid=(T // CHUNK,),
                 in_specs=[row, vec, vec, row, row, row, pl.BlockSpec(memory_space=pl.ANY)],
                 out_specs=[pl.BlockSpec((CHUNK, H2), lambda c: (c, last)), vec, vec],
                 out_shape=[jax.ShapeDtypeStruct(into.shape, into.dtype), jax.ShapeDtypeStruct((1, H2), F32),
                            jax.ShapeDtypeStruct((1, H2), F32)],
                 input_output_aliases={6: 0},
                 compiler_params=_params("arbitrary"))(raw, bias.reshape(1, H2), a_log.reshape(1, H2), dcs, dtot, dxdtx,
                                                       into)


def _cols(a, hg):
    T = a.shape[0]
    return a.reshape(T, 2, N_GROUPS, hg).transpose(1, 2, 0, 3)


def _rows(a, hg):
    T = a.shape[0]
    return a.reshape(T, 2, N_GROUPS, hg).transpose(1, 2, 3, 0)


def _uncols(a):
    T = a.shape[2]
    return a.transpose(2, 0, 1, 3).reshape(T, -1)


def _ssd_masks(d):
    r = lax.broadcasted_iota(jnp.int32, (CHUNK, CHUNK), 0)
    c = lax.broadcasted_iota(jnp.int32, (CHUNK, CHUNK), 1)
    return ((r >= c) & (d == 0)) | ((r <= c) & (d == 1))


def _head_expand(hg):
    r = lax.broadcasted_iota(jnp.int32, (hg, hg * HEAD_DIM), 0)
    c = lax.broadcasted_iota(jnp.int32, (hg, hg * HEAD_DIM), 1)
    return (c // HEAD_DIM == r).astype(F32)


def _head_select(hg):
    r = lax.broadcasted_iota(jnp.int32, (hg * HEAD_DIM, hg), 0)
    c = lax.broadcasted_iota(jnp.int32, (hg * HEAD_DIM, hg), 1)
    return (r // HEAD_DIM == c).astype(F32)


def _ssd_common(d, csc_ref, dtc_ref, hg):
    expand = _head_expand(hg)
    csx = _dot(csc_ref[...], expand, NN, HIGHEST)
    dtx = _dot(dtc_ref[...], expand, NN, HIGHEST)
    totx = jnp.where(d == 0, csx[CHUNK - 1:CHUNK, :], csx[0:1, :])
    return csx, dtx, totx


def _ssd_fwd(xbc, dtc, csc, csr, d_inner, name, comm=None):
    T = xbc.shape[0]
    nc = T // CHUNK
    gw = d_inner // N_GROUPS
    hg = gw // HEAD_DIM
    P, N = HEAD_DIM, D_STATE
    b_off = d_inner // N
    c_off = b_off + N_GROUPS

    def cidx(d, c):
        return c + d * (nc - 1 - 2 * c)

    def body(xs_ref, b_ref, c_ref, dtc_ref, csc_ref, csr_ref, y_ref, st_ref, h_ref):
        d = pl.program_id(0)
        c = pl.program_id(2)

        @pl.when(c == 0)
        def _():
            h_ref[...] = jnp.zeros_like(h_ref)

        Bb = b_ref[...].astype(BF16)
        Cb = c_ref[...].astype(BF16)
        S = _dot(Cb, Bb, NT)
        mask = _ssd_masks(d)
        csx, dtx, totx = _ssd_common(d, csc_ref, dtc_ref, hg)
        H = h_ref[...]
        st_ref[...] = H
        xdt = xs_ref[...] * dtx
        xdtb = xdt.astype(BF16)
        y_off = jnp.exp(csx) * _dot(Cb, H.astype(BF16), NN)
        for j in range(hg):
            sl = slice(j * P, (j + 1) * P)
            decay = jnp.exp(jnp.where(mask, csc_ref[:, j:j + 1] - csr_ref[j:j + 1, :], -jnp.inf))
            y_ref[:, sl] = _dot((S * decay).astype(BF16), xdtb[:, sl], NN) + y_off[:, sl]
        h_ref[...] = jnp.exp(totx) * H + _dot(Bb, (jnp.exp(totx - csx) * xdt).astype(BF16), TN)

    col = lambda d, g, c: (d, g, cidx(d, c), 0)
    return _call(
        body, comm=comm, name=name, grid=(2, N_GROUPS, nc),
        in_specs=[pl.BlockSpec((CHUNK, gw), lambda d, g, c: (cidx(d, c), g)),
                  pl.BlockSpec((CHUNK, N), lambda d, g, c: (cidx(d, c), b_off + g)),
                  pl.BlockSpec((CHUNK, N), lambda d, g, c: (cidx(d, c), c_off + g)),
                  pl.BlockSpec((None, None, CHUNK, hg), col),
                  pl.BlockSpec((None, None, CHUNK, hg), col),
                  pl.BlockSpec((None, None, hg, CHUNK), lambda d, g, c: (d, g, 0, cidx(d, c)))],
        out_specs=[pl.BlockSpec((None, CHUNK, gw), lambda d, g, c: (d, cidx(d, c), g)),
                   pl.BlockSpec((None, None, None, N, gw), lambda d, g, c: (d, cidx(d, c), g, 0, 0))],
        out_shape=[jax.ShapeDtypeStruct((2, T, d_inner), F32),
                   jax.ShapeDtypeStruct((2, nc, N_GROUPS, N, gw), F32)],
        scratch_shapes=[pltpu.VMEM((N, gw), F32)],
        compiler_params=_params("parallel", "parallel", "arbitrary"),
    )(xbc, xbc, xbc, dtc, csc, csr)


def _ssd_bwd(xbc, dtc, csc, csr, states, y2, dy, d_inner, name, comm=None):
    T = xbc.shape[0]
    nc = T // CHUNK
    gw = d_inner // N_GROUPS
    hg = gw // HEAD_DIM
    P, N = HEAD_DIM, D_STATE
    b_off = d_inner // N
    c_off = b_off + N_GROUPS

    def cidx(d, c):
        return (nc - 1 - c) + d * (2 * c - nc + 1)

    def body(xs_ref, b_ref, c_ref, dtc_ref, csc_ref, csr_ref, st_ref, y_ref, dy_ref,
             dxs_ref, db_ref, dc_ref, dcs_ref, dtot_ref, dxdtx_ref, dh_ref, dxdt_ref):
        d = pl.program_id(0)
        c = pl.program_id(2)

        @pl.when(c == 0)
        def _():
            dh_ref[...] = jnp.zeros_like(dh_ref)

        Bb = b_ref[...].astype(BF16)
        Cb = c_ref[...].astype(BF16)
        S = _dot(Cb, Bb, NT)
        mask = _ssd_masks(d)
        csx, dtx, totx = _ssd_common(d, csc_ref, dtc_ref, hg)
        select = _head_select(hg)
        X = xs_ref[...]
        xdt = X * dtx
        xdtb = xdt.astype(BF16)
        dY = dy_ref[...]
        dYb = dY.astype(BF16)
        Hp = st_ref[...]
        Hpb = Hp.astype(BF16)
        dH = dh_ref[...]
        dHb = dH.astype(BF16)
        e_tot = jnp.exp(totx)
        dCH = (jnp.exp(csx) * dY).astype(BF16)
        dC = _dot(dCH, Hpb, NT)
        dHp = _dot(Cb, dCH, TN)
        Q = _dot(Bb, dHb, NN)
        dte = jnp.exp(totx - csx)
        wx = dte * xdt
        dB = _dot(wx.astype(BF16), dHb, NT)
        ddte = Q * wx
        dS = jnp.zeros((CHUNK, CHUNK), F32)
        for j in range(hg):
            sl = slice(j * P, (j + 1) * P)
            decay = jnp.exp(jnp.where(mask, csc_ref[:, j:j + 1] - csr_ref[j:j + 1, :], -jnp.inf))
            dS = dS + _dot(dYb[:, sl], xdtb[:, sl], NT) * decay
            dxdt_ref[:, sl] = _dot((S * decay).astype(BF16), dYb[:, sl], TN)
        dxdt_diag = dxdt_ref[...]
        dxdt = dxdt_diag + dte * Q
        dcs_ref[...] = _dot(dYb.astype(F32) * y_ref[...] - xdtb.astype(F32) * dxdt_diag - ddte, select, NN, HIGHEST)
        dtot_row = (jnp.sum(ddte, axis=0, keepdims=True) + e_tot * jnp.sum(dH * Hp, axis=0, keepdims=True))
        dtot_ref[...] = jnp.zeros((CHUNK, hg), F32) + _dot(dtot_row, select, NN, HIGHEST)
        dxdtx_ref[...] = _dot(dxdt * X, select, NN, HIGHEST)
        dxs_ref[...] = dxdt * dtx
        dh_ref[...] = e_tot * dH + dHp
        dSb = dS.astype(BF16)
        dc_ref[...] = dC + _dot(dSb, Bb, NN)
        db_ref[...] = dB + _dot(dSb, Cb, TN)

    col = lambda d, g, c: (d, g, cidx(d, c), 0)
    colspec = pl.BlockSpec((None, None, CHUNK, hg), col)
    rowblk = pl.BlockSpec((None, CHUNK, gw), lambda d, g, c: (d, cidx(d, c), g))
    return _call(
        body, comm=comm, name=name, grid=(2, N_GROUPS, nc),
        in_specs=[pl.BlockSpec((CHUNK, gw), lambda d, g, c: (cidx(d, c), g)),
                  pl.BlockSpec((CHUNK, N), lambda d, g, c: (cidx(d, c), b_off + g)),
                  pl.BlockSpec((CHUNK, N), lambda d, g, c: (cidx(d, c), c_off + g)),
                  colspec, colspec,
                  pl.BlockSpec((None, None, hg, CHUNK), lambda d, g, c: (d, g, 0, cidx(d, c))),
                  pl.BlockSpec((None, None, None, N, gw), lambda d, g, c: (d, cidx(d, c), g, 0, 0)),
                  rowblk,
                  pl.BlockSpec((CHUNK, gw), lambda d, g, c: (cidx(d, c), g))],
        out_specs=[rowblk,
                   pl.BlockSpec((None, CHUNK, N), lambda d, g, c: (d, cidx(d, c), g)),
                   pl.BlockSpec((None, CHUNK, N), lambda d, g, c: (d, cidx(d, c), g)),
                   colspec, colspec, colspec],
        out_shape=[jax.ShapeDtypeStruct((2, T, d_inner), F32),
                   jax.ShapeDtypeStruct((2, T, N_GROUPS * N), F32),
                   jax.ShapeDtypeStruct((2, T, N_GROUPS * N), F32)]
        + [jax.ShapeDtypeStruct((2, N_GROUPS, T, hg), F32)] * 3,
        scratch_shapes=[pltpu.VMEM((N, gw), F32), pltpu.VMEM((CHUNK, gw), F32)],
        compiler_params=_params("parallel", "parallel", "arbitrary"),
    )(xbc, xbc, xbc, dtc, csc, csr, states, y2, dy)


def _gnorm_fwd(y2, xbc, zx, dvec, nw, d_inner, name, comm=None):
    T = xbc.shape[0]
    gw = d_inner // N_GROUPS
    tm = _tile(T, 128, 8)

    def body(y_ref, xs_ref, z_ref, d_ref, w_ref, o_ref):
        for g in range(N_GROUPS):
            sl = slice(g * gw, (g + 1) * gw)
            y = y_ref[0, :, sl] + y_ref[1, :, sl] + xs_ref[:, sl] * d_ref[:, sl]
            z = z_ref[:, sl]
            gy = y * (z * _sigmoid(z))
            rs = lax.rsqrt(jnp.mean(gy * gy, axis=-1, keepdims=True) + RMS_EPS)
            o_ref[:, sl] = (gy * rs * w_ref[:, sl]).astype(o_ref.dtype)

    row = pl.BlockSpec((tm, d_inner), lambda i: (i, 0))
    vec = pl.BlockSpec((1, d_inner), lambda i: (0, 0))
    return _call(body, comm=comm, name=name, grid=(T // tm,),
                 in_specs=[pl.BlockSpec((2, tm, d_inner), lambda i: (0, i, 0)), row, row, vec, vec],
                 out_specs=row, out_shape=jax.ShapeDtypeStruct((T, d_inner), BF16),
                 compiler_params=_params("parallel"))(y2, xbc, zx, dvec, nw)


def _gnorm_bwd(y2, xbc, zx, dvec, nw, dgn, sel, d_inner, name, comm=None):
    T = xbc.shape[0]
    gw = d_inner // N_GROUPS
    n_heads = d_inner // HEAD_DIM
    tm = _tile(T, 128, 8)
    n_tiles = T // tm

    def body(y_ref, xs_ref, z_ref, d_ref, w_ref, dg_ref, sel_ref, dy_ref, dz_ref, dw_ref, dd_ref, dch_ref):
        i = pl.program_id(0)

        @pl.when(i == 0)
        def _():
            dw_ref[...] = jnp.zeros_like(dw_ref)
            dch_ref[...] = jnp.zeros_like(dch_ref)

        for g in range(N_GROUPS):
            sl = slice(g * gw, (g + 1) * gw)
            xs = xs_ref[:, sl]
            y = y_ref[0, :, sl] + y_ref[1, :, sl] + xs * d_ref[:, sl]
            z = z_ref[:, sl]
            sig = _sigmoid(z)
            sz = z * sig
            gy = y * sz
            rs = lax.rsqrt(jnp.mean(gy * gy, axis=-1, keepdims=True) + RMS_EPS)
            n = gy * rs
            dout = dg_ref[:, sl]
            dw_ref[:, sl] += jnp.sum(dout * n, axis=0, keepdims=True)
            dn = dout * w_ref[:, sl]
            dgy = rs * (dn - n * jnp.mean(dn * n, axis=-1, keepdims=True))
            dy = dgy * sz
            dy_ref[:, sl] = dy
            dz_ref[:, sl] = (dgy * y * (sig * (1.0 + z * (1.0 - sig)))).astype(dz_ref.dtype)
            dch_ref[:, sl] += jnp.sum(dy * xs, axis=0, keepdims=True)

        @pl.when(i == n_tiles - 1)
        def _():
            dd_ref[...] = _dot(dch_ref[...], sel_ref[...], NN, HIGHEST)

    row = pl.BlockSpec((tm, d_inner), lambda i: (i, 0))
    vec = pl.BlockSpec((1, d_inner), lambda i: (0, 0))
    hvec = pl.BlockSpec((1, n_heads), lambda i: (0, 0))
    return _call(body, comm=comm, name=name, grid=(n_tiles,),
                 in_specs=[pl.BlockSpec((2, tm, d_inner), lambda i: (0, i, 0)), row, row, vec, vec, row,
                           pl.BlockSpec((d_inner, n_heads), lambda i: (0, 0))],
                 out_specs=[row, row, vec, hvec],
                 out_shape=[jax.ShapeDtypeStruct((T, d_inner), F32), jax.ShapeDtypeStruct(zx.shape, BF16),
                            jax.ShapeDtypeStruct((1, d_inner), F32), jax.ShapeDtypeStruct((1, n_heads), F32)],
                 scratch_shapes=[pltpu.VMEM((1, d_inner), F32)],
                 compiler_params=_params("arbitrary"))(y2, xbc, zx, dvec, nw, dgn, sel)


def _pool_counts(i, tt, T, win, rows, row0):
    t = i * tt + row0 + lax.broadcasted_iota(jnp.int32, (rows, 1), 0)
    start = t - win // 2
    lo = jnp.clip(start, 0, T)
    hi = jnp.clip(start + win, 0, T)
    return jnp.maximum(hi - lo, 1).astype(F32)


def _pool_features(ext_ref, i, tt, T, gi, gd):
    win = POOL_WINDOWS[gi]
    sl = slice(gi * gd, (gi + 1) * gd)
    acc = None
    for o in range(-(win // 2), win - win // 2):
        term = ext_ref[pl.ds(HALO + o, tt), sl]
        acc = term if acc is None else acc + term
    return acc / _pool_counts(i, tt, T, win, tt, 0) - ext_ref[pl.ds(HALO, tt), sl]


def _pool_fwd(u, w, bias, scale, name):
    T, D = u.shape
    ng = len(POOL_WINDOWS)
    gd = D // ng
    tt = _tile(T, 512, 8)
    nt = T // tt

    def body(cur, prev, nxt, w_ref, b_ref, s_ref, o_ref, ext):
        i = pl.program_id(0)
        _fill_ext(ext, cur, prev, nxt, i, nt, tt)
        for gi in range(ng):
            sl = slice(gi * gd, (gi + 1) * gd)
            m = _pool_features(ext, i, tt, T, gi, gd)
            pre = _dot(m.astype(BF16), w_ref[gi], NN) + b_ref[:, sl]
            o_ref[:, sl] = pre * s_ref[:, sl]

    vec = pl.BlockSpec((1, D), lambda i: (0, 0))
    return _call(body, name=name, grid=(nt,),
                 in_specs=_halo_specs(tt, D, lambda i: 0, nt, 0)
                 + [pl.BlockSpec((ng, gd, gd), lambda i: (0, 0, 0)), vec, vec],
                 out_specs=pl.BlockSpec((tt, D), lambda i: (i, 0)),
                 out_shape=jax.ShapeDtypeStruct((T, D), F32),
                 scratch_shapes=[pltpu.VMEM((tt + 2 * HALO, D), F32)],
                 compiler_params=_params("parallel"))(u, u, u, w, bias, scale)


def _pool_bwd_a(u, w, bias, scale, dy, name):
    T, D = u.shape
    ng = len(POOL_WINDOWS)
    gd = D // ng
    tt = _tile(T, 512, 8)
    nt = T // tt

    def body(cur, prev, nxt, w_ref, b_ref, s_ref, dy_ref, dm_ref, dw_ref, db_ref, ds_ref, ext):
        i = pl.program_id(0)

        @pl.when(i == 0)
        def _():
            dw_ref[...] = jnp.zeros_like(dw_ref)
            db_ref[...] = jnp.zeros_like(db_ref)
            ds_ref[...] = jnp.zeros_like(ds_ref)

        _fill_ext(ext, cur, prev, nxt, i, nt, tt)
        for gi in range(ng):
            sl = slice(gi * gd, (gi + 1) * gd)
            mb = _pool_features(ext, i, tt, T, gi, gd).astype(BF16)
            wg = w_ref[gi]
            pre = _dot(mb, wg, NN) + b_ref[:, sl]
            dy_ = dy_ref[:, sl]
            ds_ref[:, sl] += jnp.sum(dy_ * pre, axis=0, keepdims=True)
            dpre = dy_ * s_ref[:, sl]
            db_ref[:, sl] += jnp.sum(dpre, axis=0, keepdims=True)
            dpb = dpre.astype(BF16)
            dw_ref[gi] += _dot(mb, dpb, TN)
            dm_ref[:, sl] = _dot(dpb, wg, NT)

    vec = pl.BlockSpec((1, D), lambda i: (0, 0))
    row = pl.BlockSpec((tt, D), lambda i: (i, 0))
    wspec = pl.BlockSpec((ng, gd, gd), lambda i: (0, 0, 0))
    return _call(body, name=name, grid=(nt,),
                 in_specs=_halo_specs(tt, D, lambda i: 0, nt, 0) + [wspec, vec, vec, row],
                 out_specs=[row, wspec, vec, vec],
                 out_shape=[jax.ShapeDtypeStruct((T, D), F32), jax.ShapeDtypeStruct((ng, gd, gd), F32),
                            jax.ShapeDtypeStruct((1, D), F32), jax.ShapeDtypeStruct((1, D), F32)],
                 scratch_shapes=[pltpu.VMEM((tt + 2 * HALO, D), F32)],
                 compiler_params=_params("arbitrary"))(u, u, u, w, bias, scale, dy)


def _pool_bwd_b(dm, dy, alpha, name):
    T, D = dm.shape
    ng = len(POOL_WINDOWS)
    gd = D // ng
    tt = _tile(T, 512, 8)
    nt = T // tt

    def body(cur, prev, nxt, dy_ref, o_ref, ext):
        i = pl.program_id(0)
        _fill_ext(ext, cur, prev, nxt, i, nt, tt)
        for gi, win in enumerate(POOL_WINDOWS):
            sl = slice(gi * gd, (gi + 1) * gd)
            rows = tt + 2 * HALO
            ext[:, sl] = ext[:, sl] / _pool_counts(i, tt, T, win, rows, -HALO)
            acc = None
            for o in range(-(win // 2) + 1, win // 2 + 1):
                term = ext[pl.ds(HALO + o, tt), sl]
                acc = term if acc is None else acc + term
            o_ref[:, sl] = alpha * dy_ref[:, sl] + acc - cur[:, sl]

    row = pl.BlockSpec((tt, D), lambda i: (i, 0))
    return _call(body, name=name, grid=(nt,),
                 in_specs=_halo_specs(tt, D, lambda i: 0, nt, 0) + [row], out_specs=row,
                 out_shape=jax.ShapeDtypeStruct((T, D), F32),
                 scratch_shapes=[pltpu.VMEM((tt + 2 * HALO, D), F32)],
                 compiler_params=_params("parallel"))(dm, dm, dm, dy)


def _exchange(comm, name):
    ops, arrays, shapes, sems, in_place = _comm_plan(comm)
    n = len(ops)

    def body(*refs):
        copies = _comm_copies(ops, refs[:n], refs[n:2 * n], *refs[2 * n:])
        for cp in copies:
            cp.start()
        for cp in copies:
            cp.wait()

    any_spec = pl.BlockSpec(memory_space=pl.ANY)
    return _pallas(body, name=name, in_specs=[any_spec] * n, out_specs=[any_spec] * n, out_shape=shapes,
                   scratch_shapes=sems, input_output_aliases={a: a for a in in_place})(*arrays)


def _adamw(piece, w, m, v, row0, into, name):
    R, C = w.shape
    rows = piece.shape[1]
    tr = _tile(rows, max(8, (1 << 18) // C // 8 * 8), 8)
    assert row0 % tr == 0 and rows % tr == 0
    off = row0 // tr

    def body(p_ref, w_ref, m_ref, v_ref, *rest):
        g_ref, d_ref, nm_ref, nv_ref = rest[-4:]
        g = p_ref[0].astype(F32)
        for i in range(1, N_DEV):
            g = g + p_ref[i].astype(F32)
        mm = ADAM_B1 * m_ref[...] + (1.0 - ADAM_B1) * g
        vv = ADAM_B2 * v_ref[...] + (1.0 - ADAM_B2) * (g * g)
        m_hat = mm / (1.0 - ADAM_B1 ** ADAM_STEP)
        v_hat = vv / (1.0 - ADAM_B2 ** ADAM_STEP)
        g_ref[...] = g
        d_ref[...] = -ADAM_LR * (m_hat / (jnp.sqrt(v_hat) + ADAM_EPS) + ADAM_WD * w_ref[...])
        nm_ref[...] = mm
        nv_ref[...] = vv

    row = pl.BlockSpec((tr, C), lambda i: (off + i, 0))
    kept = [] if into is None else list(into)
    return _call(body, name=name, grid=(rows // tr,),
                 in_specs=[pl.BlockSpec((N_DEV, tr, C), lambda i: (0, i, 0)), row, row, row]
                 + [pl.BlockSpec(memory_space=pl.ANY)] * len(kept),
                 out_specs=[row] * 4, out_shape=[jax.ShapeDtypeStruct((R, C), F32)] * 4,
                 input_output_aliases={4 + q: q for q in range(len(kept))},
                 compiler_params=_params("parallel"))(piece, w, m, v, *kept)


def _pack(arrays):
    flat, meta, off = [], [], 0
    for a in arrays:
        flat.append(a.reshape(-1).astype(F32))
        meta.append((off, a.shape))
        off += a.size
    total = -(-off // (8 * LANES)) * (8 * LANES)
    flat.append(jnp.zeros((total - off,), F32))
    return jnp.concatenate(flat).reshape(total // LANES, LANES), meta


def _unpack(packed, meta):
    flat = packed.reshape(-1)
    return [flat[off:off + math.prod(shape)].reshape(shape) for off, shape in meta]


def kernel(x, ssd_in_proj, ssd_conv_w, ssd_conv_b, ssd_dt_bias, ssd_A_log, ssd_D, ssd_norm_w, ssd_out_proj, pool_w, pool_b, pool_scale, mlp_w1, mlp_w2, ln_mix_g, ln_mix_b, ln_ffn_g, ln_ffn_b, loss_target, m_ssd_in_proj, m_ssd_conv_w, m_ssd_conv_b, m_ssd_dt_bias, m_ssd_A_log, m_ssd_D, m_ssd_norm_w, m_ssd_out_proj, m_pool_w, m_pool_b, m_pool_scale, m_mlp_w1, m_mlp_w2, m_ln_mix_g, m_ln_mix_b, m_ln_ffn_g, m_ln_ffn_b, v_ssd_in_proj, v_ssd_conv_w, v_ssd_conv_b, v_ssd_dt_bias, v_ssd_A_log, v_ssd_D, v_ssd_norm_w, v_ssd_out_proj, v_pool_w, v_pool_b, v_pool_scale, v_mlp_w1, v_mlp_w2, v_ln_mix_g, v_ln_mix_b, v_ln_ffn_g, v_ln_ffn_b):
    T, D = x.shape[1], x.shape[2]
    depth = mlp_w1.shape[0]
    n_ssd, n_pool = ssd_in_proj.shape[0], pool_w.shape[0]
    d_inner = ssd_out_proj.shape[1] * N_DEV
    n_heads = d_inner // HEAD_DIM
    hg = n_heads // N_GROUPS
    d_bc = N_GROUPS * D_STATE
    d_xbc = d_inner + 2 * d_bc
    d_in_proj = ssd_in_proj.shape[2] * N_DEV
    d_ff = mlp_w1.shape[2] * N_DEV
    ng = len(POOL_WINDOWS)
    gd = D // ng
    alpha = (2.0 * depth) ** 0.25
    x0 = x.reshape(T, D)
    target = loss_target.reshape(T, D)

    assert depth == 4 and n_ssd == 2 and n_pool == 2, "the exchange schedules below are written for this stack"

    small_pack, small_meta = _pack([ssd_conv_w, pool_b, pool_scale])
    pw_rows = pool_w.shape[1] * pool_w.shape[2]
    in_b, out_b = ssd_in_proj.astype(BF16), ssd_out_proj.astype(BF16)
    pw_b = pool_w.reshape(n_pool, pw_rows, gd).astype(BF16)
    w1_b, w2_b = mlp_w1.astype(BF16), mlp_w2.astype(BF16)
    shard = {("in", 0): in_b[0], ("in", 1): in_b[1], ("out", 0): out_b[0], ("out", 1): out_b[1],
             ("pool", 0): pw_b[0], ("pool", 1): pw_b[1]}
    for i in range(depth):
        shard["w1", i], shard["w2", i] = w1_b[i], w2_b[i]

    def full_cols(g):
        return g.transpose(1, 0, 2).reshape(g.shape[1], -1)

    def full_rows(g):
        return g.reshape(-1, g.shape[-1])

    def full_pool(g):
        return g.reshape(N_DEV, ng, gd // N_DEV, gd).transpose(1, 0, 2, 3).reshape(ng, gd, gd)

    def slab_cols(g):
        return g.reshape(g.shape[0], N_DEV, -1).transpose(1, 0, 2)

    def slab_rows(g):
        return g.reshape(N_DEV, -1, g.shape[-1])

    def slab_pool(g):
        return g.astype(BF16).reshape(ng, N_DEV, gd // N_DEV, gd).transpose(1, 0, 2, 3).reshape(N_DEV, pw_rows, gd)

    to_full = {"in": full_cols, "out": full_rows, "pool": full_pool, "w1": lambda g: g, "w2": full_rows}

    gather1_on = {"in_proj_0": [("w1", 0), ("pool", 0)], "conv_fwd_0": [("out", 0)],
                  "ssd_fwd_0": [("w2", 0), ("w1", 1), ("in", 1)], "mlp_up_0": [("w2", 1)],
                  "mlp_down_0": [("w1", 2)], "mlp_up_1": [("w2", 2)], "mlp_down_1": [("w1", 3), ("pool", 1)],
                  "in_proj_2": [("out", 1)], "ssd_fwd_2": [("w2", 3)]}
    gather2_on = {"conv_fwd_0": [("w1", 0), ("pool", 0)], "ssd_fwd_0": [("out", 0)],
                  "gnorm_fwd_0": [("w2", 0), ("w1", 1), ("in", 1)], "mlp_down_0": [("w2", 1)],
                  "mlp_up_1": [("w1", 2)], "mlp_down_1": [("w2", 2)], "in_proj_2": [("w1", 3), ("pool", 1)],
                  "conv_fwd_2": [("out", 1)], "gnorm_fwd_2": [("w2", 3)]}
    hide_us = {"mlp_down_dx": 161, "mlp_down_dw": 163, "mlp_up_dx": 170, "mlp_up_dw": 164, "out_proj_dx": 83,
               "out_proj_dw": 85, "gnorm_bwd": 163, "ssd_bwd": 1104, "conv_bwd": 228, "in_proj_dw": 230, "in_proj_dx": 267}
    SLAB_BYTES_PER_US = 70e3 / (N_DEV - 1)
    OVERRUN_US, MIN_CARRIER_US = 0, 80
    n_pieces = {"in": 4, "out": 2, "pool": 1, "w1": 4, "w2": 4}
    W, half, G, R = {}, {}, {}, {}
    queue = []

    def produced(key, slabs):
        G[key] = slabs
        rows = slabs.shape[1] // n_pieces[key[0]]
        cost = rows * math.prod(slabs.shape[2:]) * slabs.dtype.itemsize / SLAB_BYTES_PER_US
        queue.extend((key, p * rows, rows, cost) for p in range(n_pieces[key[0]]))

    def take(budget):
        taken, used = [], 0.0
        while queue and ((not taken and budget >= MIN_CARRIER_US) or used + queue[0][3] <= budget + OVERRUN_US):
            taken.append(queue.pop(0))
            used += taken[-1][3]
        return taken

    def run(fn, *args, name, **kw):
        k1, k2 = gather1_on.get(name, []), gather2_on.get(name, [])
        budget = float("inf") if name == "in_proj_dx_0" else hide_us.get(name.rsplit("_", 1)[0], 0)
        pieces = take(budget) if not (k1 or k2) else []
        comm = ([("gather1", shard[k]) for k in k1] + [("gather2", half[k]) for k in k2]
                + [("slabs", G[key], r0, rows) for key, r0, rows, _ in pieces])
        if not comm:
            return fn(*args, name=name, **kw)
        res, got = fn(*args, name=name, comm=comm, **kw)
        for k, g in zip(k1, got):
            half[k] = g
        for k, g in zip(k2, got[len(k1):]):
            W[k] = to_full[k[0]](g)
        for (key, r0, _, _), g in zip(pieces, got):
            R.setdefault(key, []).append((r0, g))
        return res

    half_in0, g_small = _exchange([("gather1", shard["in", 0]), ("gather", small_pack)], "gather_first")
    (g_in0,) = _exchange([("gather2", half_in0)], "gather_first_onward")
    W["in", 0] = full_cols(g_in0)
    smalls = [_unpack(g_small[k], small_meta) for k in range(N_DEV)]
    conv_w = jnp.concatenate([s[0] for s in smalls], axis=-1).reshape(n_ssd, CONV_WIDTH, d_xbc)
    pool_bias = jnp.concatenate([s[1] for s in smalls], axis=-1).reshape(n_pool, 1, D)
    pool_sc = jnp.concatenate([s[2] for s in smalls], axis=-1).reshape(n_pool, 1, D)

    sel = (jnp.arange(d_inner)[:, None] // HEAD_DIM == jnp.arange(n_heads)[None, :]).astype(F32)

    saved = []
    h, hb = x0, x0.astype(BF16)
    for i in range(depth):
        j = i // 2
        s = {}
        s["x0"], s["x0b"] = h, hb
        if i % 2 == 0:
            (zx,) = run(_matmul, hb, W["in", j], "nn", name=f"in_proj_{i}", outs=[F32], tn=1152)
            xbc = run(_conv_fwd, zx, conv_w[j], ssd_conv_b[j], d_inner, name=f"conv_fwd_{i}")
            raw = zx[:, d_inner + d_xbc:]
            dt, cs = _dt_fwd(raw, ssd_dt_bias[j], ssd_A_log[j], f"dt_fwd_{i}")
            dtc, csc, csr = _cols(dt, hg), _cols(cs, hg), _rows(cs, hg)
            y2, states = run(_ssd_fwd, xbc, dtc, csc, csr, d_inner, name=f"ssd_fwd_{i}")
            dvec = jnp.repeat(ssd_D[j], HEAD_DIM).reshape(1, d_inner)
            nw = ssd_norm_w[j].reshape(1, d_inner)
            gn = run(_gnorm_fwd, y2, xbc, zx, dvec, nw, d_inner, name=f"gnorm_fwd_{i}")
            (mix,) = run(_matmul, gn, W["out", j], "nn", name=f"out_proj_{i}", outs=[F32])
            s.update(zx=zx, xbc=xbc, raw=raw, dtc=dtc, csc=csc, csr=csr, y2=y2, states=states, dvec=dvec, nw=nw, gn=gn)
        else:
            mix = _pool_fwd(h, W["pool", j], pool_bias[j], pool_sc[j], f"pool_fwd_{i}")
        s["mix"] = mix
        x1, x1b = _ln_fwd(h, mix, ln_mix_g[i], ln_mix_b[i], alpha, f"ln_mix_fwd_{i}")
        u, hh = run(_matmul, x1b, W["w1", i], "nn", name=f"mlp_up_{i}", outs=[F32, BF16],
                    epilogue=lambda acc: (acc, jnp.square(jnp.maximum(acc, 0.0))), cols_by_device=True)
        (m2,) = run(_matmul, hh, W["w2", i], "nn", name=f"mlp_down_{i}", outs=[F32])
        x2, x2b = _ln_fwd(x1, m2, ln_ffn_g[i], ln_ffn_b[i], alpha, f"ln_ffn_fwd_{i}")
        s.update(x1=x1, x1b=x1b, u=u, hh=hh, m2=m2)
        saved.append(s)
        h, hb = x2, x2b

    loss_row, dh = _loss_head(h, target)
    loss = lax.psum(loss_row[0, 0], ("x", "y", "c"))

    big = {"in": (ssd_in_proj, m_ssd_in_proj, v_ssd_in_proj), "out": (ssd_out_proj, m_ssd_out_proj, v_ssd_out_proj),
           "pool": (pool_w, m_pool_w, v_pool_w), "w1": (mlp_w1, m_mlp_w1, v_mlp_w1), "w2": (mlp_w2, m_mlp_w2, v_mlp_w2)}

    def update(kind, count):
        shape = big[kind][0].shape
        w, m, v = (a.reshape(-1, shape[-1]) for a in big[kind])
        per_layer = w.shape[0] // count
        outs = None
        for l in range(count):
            for r0, g in sorted(R[kind, l], key=lambda t: t[0]):
                outs = _adamw(g, w, m, v, l * per_layer + r0, outs, f"adamw_{kind}_{l}_{r0}")
        return [o.reshape(shape) for o in outs]

    gr = {k: [None] * depth for k in ("ln_mix_g", "ln_mix_b", "ln_ffn_g", "ln_ffn_b")}
    gs = {k: [None] * n_ssd for k in ("conv_w", "conv_b", "dt_bias", "A_log", "D", "norm_w")}
    gp = {k: [None] * n_pool for k in ("b", "scale")}
    for i in reversed(range(depth)):
        j = i // 2
        s = saved[i]
        ds2, ds2b, gr["ln_ffn_g"][i], gr["ln_ffn_b"][i] = _ln_bwd(s["x1"], s["m2"], ln_ffn_g[i], dh, alpha, f"ln_ffn_bwd_{i}")
        (du,) = run(_matmul, ds2b, W["w2", i], "nt", name=f"mlp_down_dx_{i}", outs=[BF16], extras=[s["u"]],
                    epilogue=lambda acc, u_: (acc * (2.0 * jnp.maximum(u_, 0.0)),))
        (g_w2,) = run(_matmul, s["hh"], ds2b, "tn", name=f"mlp_down_dw_{i}", outs=[BF16])
        produced(("w2", i), slab_rows(g_w2))
        (dx1,) = run(_matmul, du, W["w1", i], "nt", name=f"mlp_up_dx_{i}", outs=[F32], extras=[ds2],
                     epilogue=lambda acc, e: (acc + alpha * e,), cols_by_device=True)
        (g_w1,) = run(_matmul, s["x1b"], du, "tn", name=f"mlp_up_dw_{i}", outs=[BF16], cols_by_device=True)
        produced(("w1", i), g_w1)
        ds1, ds1b, gr["ln_mix_g"][i], gr["ln_mix_b"][i] = _ln_bwd(s["x0"], s["mix"], ln_mix_g[i], dx1, alpha, f"ln_mix_bwd_{i}")
        if i % 2 == 0:
            (dgn,) = run(_matmul, ds1b, W["out", j], "nt", name=f"out_proj_dx_{i}", outs=[F32])
            (g_out,) = run(_matmul, s["gn"], ds1b, "tn", name=f"out_proj_dw_{i}", outs=[BF16])
            produced(("out", j), slab_rows(g_out))
            dy, dzx, gs["norm_w"][j], gs["D"][j] = run(
                _gnorm_bwd, s["y2"], s["xbc"], s["zx"], s["dvec"], s["nw"], dgn, sel, d_inner, name=f"gnorm_bwd_{i}")
            dxs, dB, dC, dcs, dtot, dxdtx = run(_ssd_bwd, s["xbc"], s["dtc"], s["csc"], s["csr"], s["states"], s["y2"], dy,
                                                d_inner, name=f"ssd_bwd_{i}")
            dzx, gs["dt_bias"][j], gs["A_log"][j] = _dt_bwd(
                s["raw"], ssd_dt_bias[j], ssd_A_log[j], _uncols(dcs), _uncols(dtot), _uncols(dxdtx), dzx, f"dt_bwd_{i}")
            dpre = _conv_dpre(s["zx"], conv_w[j], ssd_conv_b[j], d_inner, 0, dxs, dy, f"conv_dpre_x_{i}",
                              extra_scale=s["dvec"])
            dpre = _conv_dpre(s["zx"], conv_w[j], ssd_conv_b[j], d_inner, d_inner, dB, None, f"conv_dpre_b_{i}", into=dpre)
            dpre = _conv_dpre(s["zx"], conv_w[j], ssd_conv_b[j], d_inner, d_inner + d_bc, dC, None, f"conv_dpre_c_{i}",
                              into=dpre)
            dzx, gs["conv_w"][j], gs["conv_b"][j] = run(_conv_bwd, s["zx"], dpre, conv_w[j], d_inner, dzx,
                                                        name=f"conv_bwd_{i}")
            (g_in,) = run(_matmul, s["x0b"], dzx, "tn", name=f"in_proj_dw_{i}", outs=[BF16], tn=1152)
            produced(("in", j), slab_cols(g_in))
            (dh,) = run(_matmul, dzx, W["in", j], "nt", name=f"in_proj_dx_{i}", outs=[F32], extras=[ds1], tk=1152,
                        epilogue=lambda acc, e: (acc + alpha * e,))
        else:
            dm, g_pw, gp["b"][j], gp["scale"][j] = _pool_bwd_a(
                s["x0"], W["pool", j], pool_bias[j], pool_sc[j], ds1, f"pool_bwd_a_{i}")
            produced(("pool", j), slab_pool(g_pw))
            dh = _pool_bwd_b(dm, ds1, alpha, f"pool_bwd_b_{i}")
    grad_x = dh.reshape(x.shape)

    g_conv_w = jnp.stack(gs["conv_w"]).reshape(n_ssd, CONV_WIDTH, 1, N_DEV, d_xbc // N_DEV)
    g_pool_b = jnp.stack(gp["b"]).reshape(n_pool, ng, N_DEV, gd // N_DEV)
    g_pool_s = jnp.stack(gp["scale"]).reshape(n_pool, N_DEV, D // N_DEV)
    s_small = jnp.stack([_pack([g_conv_w[:, :, :, k], g_pool_b[:, :, k], g_pool_s[:, k]])[0] for k in range(N_DEV)])
    repl_grads = [jnp.stack(gs["conv_b"]).reshape(ssd_conv_b.shape), jnp.stack(gs["dt_bias"]).reshape(ssd_dt_bias.shape),
                  jnp.stack(gs["A_log"]).reshape(ssd_A_log.shape), jnp.stack(gs["D"]).reshape(ssd_D.shape),
                  jnp.stack(gs["norm_w"]).reshape(ssd_norm_w.shape),
                  jnp.stack(gr["ln_mix_g"]).reshape(ln_mix_g.shape), jnp.stack(gr["ln_mix_b"]).reshape(ln_mix_b.shape),
                  jnp.stack(gr["ln_ffn_g"]).reshape(ln_ffn_g.shape), jnp.stack(gr["ln_ffn_b"]).reshape(ln_ffn_b.shape)]
    repl_pack, repl_meta = _pack(repl_grads)
    s_repl = jnp.broadcast_to(repl_pack[None], (N_DEV,) + repl_pack.shape)
    left = list(queue)
    del queue[:]
    got = _exchange([("slabs", s_small, 0, s_small.shape[1]), ("slabs", s_repl, 0, s_repl.shape[1])]
                    + [("slabs", G[key], r0, rows) for key, r0, rows, _ in left], "exchange_last")
    r_small, r_repl = got[0], got[1]
    for (key, r0, _, _), g in zip(left, got[2:]):
        R.setdefault(key, []).append((r0, g))

    upd = {}
    for nm, kind, count in (("ssd_in_proj", "in", n_ssd), ("ssd_out_proj", "out", n_ssd), ("pool_w", "pool", n_pool),
                            ("mlp_w1", "w1", depth), ("mlp_w2", "w2", depth)):
        upd[nm] = update(kind, count)
    sm = _adamw(r_small, small_pack, _pack([m_ssd_conv_w, m_pool_b, m_pool_scale])[0],
                _pack([v_ssd_conv_w, v_pool_b, v_pool_scale])[0], 0, None, "adamw_small_sharded")
    for idx, nm in enumerate(["ssd_conv_w", "pool_b", "pool_scale"]):
        upd[nm] = [_unpack(r, small_meta)[idx] for r in sm]
    repl_names = ["ssd_conv_b", "ssd_dt_bias", "ssd_A_log", "ssd_D", "ssd_norm_w",
                  "ln_mix_g", "ln_mix_b", "ln_ffn_g", "ln_ffn_b"]
    repl_w = [ssd_conv_b, ssd_dt_bias, ssd_A_log, ssd_D, ssd_norm_w, ln_mix_g, ln_mix_b, ln_ffn_g, ln_ffn_b]
    repl_m = [m_ssd_conv_b, m_ssd_dt_bias, m_ssd_A_log, m_ssd_D, m_ssd_norm_w, m_ln_mix_g, m_ln_mix_b, m_ln_ffn_g, m_ln_ffn_b]
    repl_v = [v_ssd_conv_b, v_ssd_dt_bias, v_ssd_A_log, v_ssd_D, v_ssd_norm_w, v_ln_mix_g, v_ln_mix_b, v_ln_ffn_g, v_ln_ffn_b]
    rp = _adamw(r_repl, _pack(repl_w)[0], _pack(repl_m)[0], _pack(repl_v)[0], 0, None, "adamw_replicated")
    for idx, nm in enumerate(repl_names):
        upd[nm] = [_unpack(r, repl_meta)[idx] for r in rp]

    order = ["ssd_in_proj", "ssd_conv_w", "ssd_conv_b", "ssd_dt_bias", "ssd_A_log", "ssd_D", "ssd_norm_w",
             "ssd_out_proj", "pool_w", "pool_b", "pool_scale", "mlp_w1", "mlp_w2",
             "ln_mix_g", "ln_mix_b", "ln_ffn_g", "ln_ffn_b"]
    return (loss, grad_x, *[upd[n][0] for n in order], *[upd[n][1] for n in order],
            *[upd[n][2] for n in order], *[upd[n][3] for n in order])
```

```python
import functools
import math

import jax
import jax.numpy as jnp
from jax import lax
from jax.experimental import pallas as pl
from jax.experimental.pallas import tpu as pltpu

F32 = jnp.float32
BF16 = jnp.bfloat16

N_DEV = 8
HEAD_DIM = 64
N_GROUPS = 8
D_STATE = 128
CHUNK = 128
GROUPS_PER_STEP = 2
CONV_WIDTH = 5
POOL_WINDOWS = (2, 4, 8, 16)
HALO = 8
LN_EPS = 1e-5
RMS_EPS = 1e-5
ADAM_LR = 0.001
ADAM_B1 = 0.9
ADAM_B2 = 0.999
ADAM_EPS = 1e-08
ADAM_WD = 0.01
ADAM_STEP = 10
LANES = 128
VMEM_LIMIT_BYTES = 56 * 1024 * 1024
HIGHEST = lax.Precision.HIGHEST


def _pallas(body, **kw):
    return pl.pallas_call(body, **kw)


def _params(*sem):
    return pltpu.CompilerParams(dimension_semantics=sem, vmem_limit_bytes=VMEM_LIMIT_BYTES)


def _mesh_pos():
    return lax.axis_index("x"), lax.axis_index("y"), lax.axis_index("c")


def _peer(x, y, c, k):
    dx, dy, dc = (k >> 2) & 1, (k >> 1) & 1, k & 1
    px = (1 - x) if dx else x
    py = (1 - y) if dy else y
    pc = (1 - c) if dc else c
    return px, py, pc


def _comm_copies(ops, ins, outs, send_sems, recv_sems, local_sems):
    x, y, c = _mesh_pos()
    me = 4 * x + 2 * y + c
    copies = []
    for a, op in enumerate(ops):
        src, dst = ins[a], outs[a]

        def remote(k, s, d, to):
            return pltpu.make_async_remote_copy(src_ref=s, dst_ref=d, send_sem=send_sems.at[a, k - 1],
                                                recv_sem=recv_sems.at[a, k - 1], device_id=to,
                                                device_id_type=pl.DeviceIdType.MESH)

        if op[0] in ("gather", "gather1"):
            copies.append(pltpu.make_async_copy(src, dst.at[me], local_sems.at[a]))
            for k in (range(1, N_DEV) if op[0] == "gather" else (1, 2, 4, 6)):
                copies.append(remote(k, src, dst.at[me], _peer(x, y, c, k)))
        elif op[0] == "gather2":
            for k in (2, 4, 6):
                qx, qy, qc = _peer(x, y, c, k)
                slot = 4 * qx + 2 * qy + qc
                copies.append(remote(k, src.at[slot], dst.at[slot], _peer(x, y, c, 1)))
        else:
            rows = pl.ds(op[1], op[2])
            copies.append(pltpu.make_async_copy(src.at[me, rows], dst.at[me], local_sems.at[a]))
            for k in range(1, N_DEV):
                px, py, pc = _peer(x, y, c, k)
                copies.append(remote(k, src.at[4 * px + 2 * py + pc, rows], dst.at[me], (px, py, pc)))
    return copies


def _comm_plan(comm):
    ops = [(c[0],) + tuple(c[2:]) for c in comm]
    arrays = [c[1] for c in comm]
    shapes = []
    for op, a in zip(ops, arrays):
        if op[0] in ("gather", "gather1"):
            shapes.append(jax.ShapeDtypeStruct((N_DEV,) + a.shape, a.dtype))
        elif op[0] == "gather2":
            shapes.append(jax.ShapeDtypeStruct(a.shape, a.dtype))
        else:
            shapes.append(jax.ShapeDtypeStruct((N_DEV, op[2]) + a.shape[2:], a.dtype))
    n = len(ops)
    sems = [pltpu.SemaphoreType.DMA((n, N_DEV - 1)), pltpu.SemaphoreType.DMA((n, N_DEV - 1)),
            pltpu.SemaphoreType.DMA((n,))]
    in_place = [a for a, op in enumerate(ops) if op[0] == "gather2"]
    return ops, arrays, shapes, sems, in_place


def _call(body, *, comm=None, **kw):
    if not comm:
        return _pallas(body, **kw)
    ops, arrays, c_shape, c_sems, in_place = _comm_plan(comm)
    n = len(ops)
    grid = tuple(kw["grid"])
    single = not isinstance(kw["out_shape"], (list, tuple))
    out_shape = [kw["out_shape"]] if single else list(kw["out_shape"])
    out_specs = [kw["out_specs"]] if single else list(kw["out_specs"])
    in_specs = list(kw["in_specs"])
    scratch = list(kw.get("scratch_shapes", ()))
    n_in, n_out, n_scr = len(in_specs), len(out_shape), len(scratch)

    def wrapped(*refs):
        ins, refs = refs[:n_in], refs[n_in:]
        c_ins, refs = refs[:n], refs[n:]
        outs, refs = refs[:n_out], refs[n_out:]
        c_outs, refs = refs[:n], refs[n:]
        scr, sems = refs[:n_scr], refs[n_scr:]
        first = last = None
        for ax, size in enumerate(grid):
            i = pl.program_id(ax)
            first = (i == 0) if first is None else first & (i == 0)
            last = (i == size - 1) if last is None else last & (i == size - 1)

        @pl.when(first)
        def _():
            for cp in _comm_copies(ops, c_ins, c_outs, *sems):
                cp.start()

        body(*ins, *outs, *scr)

        @pl.when(last)
        def _():
            for cp in _comm_copies(ops, c_ins, c_outs, *sems):
                cp.wait()

    any_spec = pl.BlockSpec(memory_space=pl.ANY)
    call = _pallas(wrapped, name=kw["name"], grid=grid, in_specs=in_specs + [any_spec] * n,
                   out_specs=out_specs + [any_spec] * n, out_shape=out_shape + c_shape,
                   scratch_shapes=scratch + c_sems,
                   input_output_aliases={**kw.get("input_output_aliases", {}), **{n_in + a: n_out + a for a in in_place}},
                   compiler_params=_params(*(("arbitrary",) * len(grid))))

    def run(*args):
        res = call(*args, *arrays)
        own = res[:n_out]
        return (own[0] if single else list(own)), list(res[n_out:])

    return run


def _tile(dim, target, align=LANES):
    if dim <= target:
        return dim
    t = (target // align) * align
    while t >= align:
        if dim % t == 0:
            return t
        t -= align
    return dim


def _dot(a, b, dims, precision=None):
    return lax.dot_general(a, b, (dims, ((), ())), precision=precision, preferred_element_type=F32)


NN = ((1,), (0,))
NT = ((1,), (1,))
TN = ((0,), (0,))


def _sigmoid(x):
    return 1.0 / (1.0 + jnp.exp(-x))


def _matmul(a, b, mode, *, name, outs, epilogue=None, extras=(), tm=1024, tn=1024, tk=2048, comm=None,
            cols_by_device=False):
    if mode == "nn":
        (M, K), (K2, N) = a.shape, (b.shape[-2:] if not cols_by_device else (b.shape[1], N_DEV * b.shape[2]))
    elif mode == "nt":
        (M, K), (N, K2) = a.shape, (b.shape if not cols_by_device else (b.shape[1], N_DEV * b.shape[2]))
    else:
        (K, M), (K2, N) = a.shape, b.shape
    assert K == K2, (a.shape, b.shape, mode)
    per_dev = (K if mode == "nt" else N) // N_DEV
    if cols_by_device and mode == "nt":
        tk = min(tk, per_dev)
    elif cols_by_device:
        tn = min(tn, per_dev)
    tm, tn, tk = _tile(M, tm), _tile(N, tn), _tile(K, tk)
    nk = K // tk
    dims = {"nn": NN, "nt": NT, "tn": TN}[mode]
    a_spec = (pl.BlockSpec((tk, tm), lambda i, j, k: (k, i)) if mode == "tn"
              else pl.BlockSpec((tm, tk), lambda i, j, k: (i, k)))
    b_spec = (pl.BlockSpec((tn, tk), lambda i, j, k: (j, k)) if mode == "nt"
              else pl.BlockSpec((tk, tn), lambda i, j, k: (k, j)))
    mn_spec = pl.BlockSpec((tm, tn), lambda i, j, k: (i, j))
    out_spec, out_dims = mn_spec, (M, N)
    if cols_by_device and mode == "nn":
        r = per_dev // tn
        b_spec = pl.BlockSpec((None, tk, tn), lambda i, j, k: (j // r, k, j % r))
    elif cols_by_device and mode == "nt":
        r = per_dev // tk
        b_spec = pl.BlockSpec((None, tn, tk), lambda i, j, k: (k // r, j, k % r))
    elif cols_by_device:
        assert not extras and epilogue is None
        r = per_dev // tn
        out_spec, out_dims = pl.BlockSpec((None, tm, tn), lambda i, j, k: (j // r, i, j % r)), (N_DEV, M, per_dev)
    n_extra, n_out = len(extras), len(outs)

    def finish(acc, extra_refs, out_refs):
        res = (acc,) if epilogue is None else epilogue(acc, *[r[...] for r in extra_refs])
        for o_ref, r in zip(out_refs, res):
            o_ref[...] = r.astype(o_ref.dtype)

    def body(*refs):
        a_ref, b_ref = refs[0], refs[1]
        extra_refs = refs[2:2 + n_extra]
        out_refs = refs[2 + n_extra:2 + n_extra + n_out]
        part = _dot(a_ref[...].astype(BF16), b_ref[...].astype(BF16), dims)
        if nk == 1:
            finish(part, extra_refs, out_refs)
            return
        acc_ref = refs[-1]
        k = pl.program_id(2)

        @pl.when(k == 0)
        def _():
            acc_ref[...] = part

        @pl.when((k > 0) & (k < nk - 1))
        def _():
            acc_ref[...] += part

        @pl.when(k == nk - 1)
        def _():
            finish(acc_ref[...] + part, extra_refs, out_refs)

    res = _call(
        body, comm=comm, name=name, grid=(M // tm, N // tn, nk),
        in_specs=[a_spec, b_spec] + [mn_spec] * n_extra,
        out_specs=[out_spec] * n_out,
        out_shape=[jax.ShapeDtypeStruct(out_dims, dt) for dt in outs],
        scratch_shapes=[pltpu.VMEM((tm, tn), F32)] if nk > 1 else [],
        compiler_params=_params("parallel", "parallel", "arbitrary"),
    )(a, b, *extras)
    return res


def _ln_fwd(x, f, g, b, alpha, name):
    T, D = x.shape
    tm = _tile(T, 256, 8)

    def body(x_ref, f_ref, g_ref, b_ref, y_ref, yb_ref):
        s = alpha * x_ref[...] + f_ref[...]
        mu = jnp.mean(s, axis=-1, keepdims=True)
        d = s - mu
        var = jnp.mean(d * d, axis=-1, keepdims=True)
        y = d * lax.rsqrt(var + LN_EPS) * g_ref[...] + b_ref[...]
        y_ref[...] = y
        yb_ref[...] = y.astype(BF16)

    row = pl.BlockSpec((tm, D), lambda i: (i, 0))
    vec = pl.BlockSpec((1, D), lambda i: (0, 0))
    return _call(body, name=name, grid=(T // tm,), in_specs=[row, row, vec, vec], out_specs=[row, row],
                 out_shape=[jax.ShapeDtypeStruct((T, D), F32), jax.ShapeDtypeStruct((T, D), BF16)],
                 compiler_params=_params("parallel"))(x, f, g.reshape(1, D), b.reshape(1, D))


def _ln_bwd(x, f, g, dy, alpha, name):
    T, D = x.shape
    tm = _tile(T, 256, 8)

    def body(x_ref, f_ref, g_ref, dy_ref, ds_ref, dsb_ref, dg_ref, db_ref):
        i = pl.program_id(0)

        @pl.when(i == 0)
        def _():
            dg_ref[...] = jnp.zeros_like(dg_ref)
            db_ref[...] = jnp.zeros_like(db_ref)

        s = alpha * x_ref[...] + f_ref[...]
        mu = jnp.mean(s, axis=-1, keepdims=True)
        d = s - mu
        var = jnp.mean(d * d, axis=-1, keepdims=True)
        rstd = lax.rsqrt(var + LN_EPS)
        xhat = d * rstd
        dy_ = dy_ref[...]
        dg_ref[...] += jnp.sum(dy_ * xhat, axis=0, keepdims=True)
        db_ref[...] += jnp.sum(dy_, axis=0, keepdims=True)
        dxh = dy_ * g_ref[...]
        m1 = jnp.mean(dxh, axis=-1, keepdims=True)
        m2 = jnp.mean(dxh * xhat, axis=-1, keepdims=True)
        ds = rstd * (dxh - m1 - xhat * m2)
        ds_ref[...] = ds
        dsb_ref[...] = ds.astype(BF16)

    row = pl.BlockSpec((tm, D), lambda i: (i, 0))
    vec = pl.BlockSpec((1, D), lambda i: (0, 0))
    return _call(body, name=name, grid=(T // tm,), in_specs=[row, row, vec, row], out_specs=[row, row, vec, vec],
                 out_shape=[jax.ShapeDtypeStruct((T, D), F32), jax.ShapeDtypeStruct((T, D), BF16),
                            jax.ShapeDtypeStruct((1, D), F32), jax.ShapeDtypeStruct((1, D), F32)],
                 compiler_params=_params("arbitrary"))(x, f, g.reshape(1, D), dy)


def _loss_head(y, target):
    T, D = y.shape
    tm = _tile(T, 256, 8)

    def body(y_ref, t_ref, loss_ref, dy_ref):
        i = pl.program_id(0)

        @pl.when(i == 0)
        def _():
            loss_ref[...] = jnp.zeros_like(loss_ref)

        err = y_ref[...] - t_ref[...]
        dy_ref[...] = err * (1.0 / D)
        per_tok = jnp.mean(err * err, axis=-1, keepdims=True)
        loss_ref[...] += 0.5 * jnp.sum(per_tok)

    row = pl.BlockSpec((tm, D), lambda i: (i, 0))
    return _call(body, name="loss_head", grid=(T // tm,), in_specs=[row, row],
                 out_specs=[pl.BlockSpec((1, LANES), lambda i: (0, 0)), row],
                 out_shape=[jax.ShapeDtypeStruct((1, LANES), F32), jax.ShapeDtypeStruct((T, D), F32)],
                 compiler_params=_params("arbitrary"))(y, target)


def _halo_specs(tt, cw, col_of, n_tiles, grid_rank_tokens_axis):
    per = tt // HALO
    ax = grid_rank_tokens_axis

    def cur(*g):
        return (g[ax], col_of(*g))

    def prev(*g):
        return (jnp.maximum(g[ax] * per - 1, 0), col_of(*g))

    def nxt(*g):
        return (jnp.minimum((g[ax] + 1) * per, n_tiles * per - 1), col_of(*g))

    return [pl.BlockSpec((tt, cw), cur), pl.BlockSpec((HALO, cw), prev), pl.BlockSpec((HALO, cw), nxt)]


def _fill_ext(ext_ref, cur_ref, prev_ref, next_ref, i, n_tiles, tt):
    ext_ref[pl.ds(0, HALO), :] = jnp.where(i > 0, prev_ref[...], 0.0)
    ext_ref[pl.ds(HALO, tt), :] = cur_ref[...]
    ext_ref[pl.ds(HALO + tt, HALO), :] = jnp.where(i < n_tiles - 1, next_ref[...], 0.0)


def _conv_pre(ext_ref, w, bias, tt, lo=0, n=None):
    n = tt if n is None else n
    pad = CONV_WIDTH // 2
    acc = None
    for k in range(CONV_WIDTH):
        term = ext_ref[pl.ds(HALO + lo + k - pad, n), :] * w[k:k + 1, :]
        acc = term if acc is None else acc + term
    return acc + bias


def _conv_fwd(zx, conv_w, conv_b, d_inner, name, comm=None):
    T = zx.shape[0]
    d_xbc = conv_w.shape[1]
    cw = _tile(d_xbc, 512)
    assert d_inner % cw == 0
    off = d_inner // cw
    tt = _tile(T, 512, 8)
    nt = T // tt

    def body(cur_ref, prev_ref, next_ref, w_ref, b_ref, o_ref, ext_ref):
        i = pl.program_id(1)
        _fill_ext(ext_ref, cur_ref, prev_ref, next_ref, i, nt, tt)
        pre = _conv_pre(ext_ref, w_ref[...], b_ref[...], tt)
        o_ref[...] = pre * _sigmoid(pre)

    specs = _halo_specs(tt, cw, lambda j, i: off + j, nt, 1)
    return _call(body, comm=comm, name=name, grid=(d_xbc // cw, nt),
                 in_specs=specs + [pl.BlockSpec((CONV_WIDTH, cw), lambda j, i: (0, j)),
                                   pl.BlockSpec((1, cw), lambda j, i: (0, j))],
                 out_specs=pl.BlockSpec((tt, cw), lambda j, i: (i, j)),
                 out_shape=jax.ShapeDtypeStruct((T, d_xbc), F32),
                 scratch_shapes=[pltpu.VMEM((tt + 2 * HALO, cw), F32)],
                 compiler_params=_params("parallel", "parallel"))(zx, zx, zx, conv_w, conv_b.reshape(1, d_xbc))


def _conv_dpre(zx, conv_w, conv_b, d_inner, col_lo, dirs, extra, name, into=None, extra_scale=None):
    T = zx.shape[0]
    ncols = dirs.shape[2]
    cw = _tile(ncols, 512)
    assert d_inner % cw == 0 and col_lo % cw == 0
    off_zx = (d_inner + col_lo) // cw
    off_w = col_lo // cw
    tt = _tile(T, 512, 8)
    nt = T // tt
    has_extra = extra is not None

    def body(cur_ref, prev_ref, next_ref, w_ref, b_ref, dirs_ref, *rest):
        o_ref, ext_ref = rest[-2], rest[-1]
        i = pl.program_id(1)
        _fill_ext(ext_ref, cur_ref, prev_ref, next_ref, i, nt, tt)
        pre = _conv_pre(ext_ref, w_ref[...], b_ref[...], tt)
        sig = _sigmoid(pre)
        dact = dirs_ref[0] + dirs_ref[1]
        if has_extra:
            dact = dact + rest[0][...] * rest[1][...]
        o_ref[...] = dact * (sig * (1.0 + pre * (1.0 - sig)))

    specs = _halo_specs(tt, cw, lambda j, i: off_zx + j, nt, 1)
    in_specs = specs + [pl.BlockSpec((CONV_WIDTH, cw), lambda j, i: (0, off_w + j)),
                        pl.BlockSpec((1, cw), lambda j, i: (0, off_w + j)),
                        pl.BlockSpec((2, tt, cw), lambda j, i: (0, i, j))]
    args = [zx, zx, zx, conv_w, conv_b.reshape(1, -1), dirs]
    if has_extra:
        in_specs += [pl.BlockSpec((tt, cw), lambda j, i: (i, j)), pl.BlockSpec((1, cw), lambda j, i: (0, j))]
        args += [extra, extra_scale]
    aliases = {}
    if into is not None:
        in_specs.append(pl.BlockSpec(memory_space=pl.ANY))
        args.append(into)
        aliases = {len(args) - 1: 0}
    return _call(body, name=name, grid=(ncols // cw, nt), in_specs=in_specs,
                 out_specs=pl.BlockSpec((tt, cw), lambda j, i: (i, off_w + j)),
                 out_shape=jax.ShapeDtypeStruct((T, conv_w.shape[1]), F32),
                 scratch_shapes=[pltpu.VMEM((tt + 2 * HALO, cw), F32)], input_output_aliases=aliases,
                 compiler_params=_params("parallel", "parallel"))(*args)


def _conv_bwd(zx, dpre, conv_w, d_inner, into, name, comm=None):
    T = zx.shape[0]
    d_xbc = conv_w.shape[1]
    cw = _tile(d_xbc, 512)
    off = d_inner // cw
    tt = _tile(T, 512, 8)
    nt = T // tt
    pad = CONV_WIDTH // 2

    def body(zc, zp, zn, dc, dp, dn, w_ref, into_ref, din_ref, dw_ref, db_ref, zext, dext):
        i = pl.program_id(1)

        @pl.when(i == 0)
        def _():
            dw_ref[...] = jnp.zeros_like(dw_ref)
            db_ref[...] = jnp.zeros_like(db_ref)

        _fill_ext(zext, zc, zp, zn, i, nt, tt)
        _fill_ext(dext, dc, dp, dn, i, nt, tt)
        w = w_ref[...]
        d = dc[...]
        acc = None
        for k in range(CONV_WIDTH):
            term = dext[pl.ds(HALO + pad - k, tt), :] * w[k:k + 1, :]
            acc = term if acc is None else acc + term
            dw_ref[k:k + 1, :] += jnp.sum(d * zext[pl.ds(HALO + k - pad, tt), :], axis=0, keepdims=True)
        din_ref[...] = acc.astype(din_ref.dtype)
        db_ref[...] += jnp.sum(d, axis=0, keepdims=True)

    zspecs = _halo_specs(tt, cw, lambda j, i: off + j, nt, 1)
    dspecs = _halo_specs(tt, cw, lambda j, i: j, nt, 1)
    return _call(body, comm=comm, name=name, grid=(d_xbc // cw, nt),
                 in_specs=zspecs + dspecs + [pl.BlockSpec((CONV_WIDTH, cw), lambda j, i: (0, j)),
                                             pl.BlockSpec(memory_space=pl.ANY)],
                 out_specs=[pl.BlockSpec((tt, cw), lambda j, i: (i, off + j)),
                            pl.BlockSpec((CONV_WIDTH, cw), lambda j, i: (0, j)),
                            pl.BlockSpec((1, cw), lambda j, i: (0, j))],
                 out_shape=[jax.ShapeDtypeStruct(into.shape, into.dtype), jax.ShapeDtypeStruct((CONV_WIDTH, d_xbc), F32),
                            jax.ShapeDtypeStruct((1, d_xbc), F32)],
                 scratch_shapes=[pltpu.VMEM((tt + 2 * HALO, cw), F32), pltpu.VMEM((tt + 2 * HALO, cw), F32)],
                 input_output_aliases={7: 0},
                 compiler_params=_params("parallel", "arbitrary"))(zx, zx, zx, dpre, dpre, dpre, conv_w, into)


def _tri(n):
    r = lax.broadcasted_iota(jnp.int32, (n, n), 0)
    c = lax.broadcasted_iota(jnp.int32, (n, n), 1)
    return (r >= c).astype(F32), (r <= c).astype(F32)


def _dt_fwd(raw, bias, a_log, name):
    T, H2 = raw.shape
    half = H2 // 2

    def body(raw_ref, bias_ref, alog_ref, dt_ref, cs_ref):
        x = raw_ref[...] + bias_ref[...]
        dt = jnp.maximum(x, 0.0) + jnp.log(1.0 + jnp.exp(-jnp.abs(x)))
        a = dt * (-jnp.exp(alog_ref[...]))
        lower, upper = _tri(CHUNK)
        cs_f = _dot(lower, a, NN, HIGHEST)
        cs_b = _dot(upper, a, NN, HIGHEST)
        lane = lax.broadcasted_iota(jnp.int32, (CHUNK, H2), 1)
        dt_ref[...] = dt
        cs_ref[...] = jnp.where(lane < half, cs_f, cs_b)

    row = pl.BlockSpec((CHUNK, H2), lambda c: (c, 0))
    vec = pl.BlockSpec((1, H2), lambda c: (0, 0))
    return _call(body, name=name, grid=(T // CHUNK,), in_specs=[row, vec, vec], out_specs=[row, row],
                 out_shape=[jax.ShapeDtypeStruct((T, H2), F32)] * 2,
                 compiler_params=_params("parallel"))(raw, bias.reshape(1, H2), a_log.reshape(1, H2))


def _dt_bwd(raw, bias, a_log, dcs, dtot, dxdtx, into, name):
    T, H2 = raw.shape
    half = H2 // 2
    assert into.shape[1] % H2 == 0
    last = into.shape[1] // H2 - 1

    def body(raw_ref, bias_ref, alog_ref, dcs_ref, dtot_ref, dx_ref, into_ref, draw_ref, dbias_ref, dalog_ref):
        c = pl.program_id(0)

        @pl.when(c == 0)
        def _():
            dbias_ref[...] = jnp.zeros_like(dbias_ref)
            dalog_ref[...] = jnp.zeros_like(dalog_ref)

        x = raw_ref[...] + bias_ref[...]
        dt = jnp.maximum(x, 0.0) + jnp.log(1.0 + jnp.exp(-jnp.abs(x)))
        A = -jnp.exp(alog_ref[...])
        lower, upper = _tri(CHUNK)
        g = dcs_ref[...]
        lane = lax.broadcasted_iota(jnp.int32, (CHUNK, H2), 1)
        da = jnp.where(lane < half, _dot(upper, g, NN, HIGHEST), _dot(lower, g, NN, HIGHEST)) + dtot_ref[...]
        ddt = da * A + dx_ref[...]
        draw = ddt * _sigmoid(x)
        draw_ref[...] = draw.astype(draw_ref.dtype)
        dbias_ref[...] += jnp.sum(draw, axis=0, keepdims=True)
        dalog_ref[...] += jnp.sum(da * dt, axis=0, keepdims=True) * A

    row = pl.BlockSpec((CHUNK, H2), lambda c: (c, 0))
    vec = pl.BlockSpec((1, H2), lambda c: (0, 0))
    return _call(body, name=name, grid=(T // CHUNK,),
                 in_specs=[row, vec, vec, row, row, row, pl.BlockSpec(memory_space=pl.ANY)],
                 out_specs=[pl.BlockSpec((CHUNK, H2), lambda c: (c, last)), vec, vec],
                 out_shape=[jax.ShapeDtypeStruct(into.shape, into.dtype), jax.ShapeDtypeStruct((1, H2), F32),
                            jax.ShapeDtypeStruct((1, H2), F32)],
                 input_output_aliases={6: 0},
                 compiler_params=_params("arbitrary"))(raw, bias.reshape(1, H2), a_log.reshape(1, H2), dcs, dtot, dxdtx,
                                                       into)


def _cols(a, hg):
    T = a.shape[0]
    return a.reshape(T, 2, N_GROUPS, hg).transpose(1, 2, 0, 3)


def _rows(a, hg):
    T = a.shape[0]
    return a.reshape(T, 2, N_GROUPS, hg).transpose(1, 2, 3, 0)


def _uncols(a):
    T = a.shape[2]
    return a.transpose(2, 0, 1, 3).reshape(T, -1)


def _ssd_masks(d):
    r = lax.broadcasted_iota(jnp.int32, (CHUNK, CHUNK), 0)
    c = lax.broadcasted_iota(jnp.int32, (CHUNK, CHUNK), 1)
    return ((r >= c) & (d == 0)) | ((r <= c) & (d == 1))


def _head_expand(hg):
    r = lax.broadcasted_iota(jnp.int32, (hg, hg * HEAD_DIM), 0)
    c = lax.broadcasted_iota(jnp.int32, (hg, hg * HEAD_DIM), 1)
    return (c // HEAD_DIM == r).astype(F32)


def _head_select(hg):
    r = lax.broadcasted_iota(jnp.int32, (hg * HEAD_DIM, hg), 0)
    c = lax.broadcasted_iota(jnp.int32, (hg * HEAD_DIM, hg), 1)
    return (r // HEAD_DIM == c).astype(F32)


def _ssd_common(d, csc_ref, dtc_ref, hg):
    expand = _head_expand(hg)
    csx = _dot(csc_ref[...], expand, NN, HIGHEST)
    dtx = _dot(dtc_ref[...], expand, NN, HIGHEST)
    totx = jnp.where(d == 0, csx[CHUNK - 1:CHUNK, :], csx[0:1, :])
    return csx, dtx, totx


def _ssd_fwd(xbc, dtc, csc, csr, d_inner, name, comm=None):
    T = xbc.shape[0]
    nc = T // CHUNK
    gw = d_inner // N_GROUPS
    hg = gw // HEAD_DIM
    P, N = HEAD_DIM, D_STATE
    b_off = d_inner // N
    c_off = b_off + N_GROUPS

    def cidx(d, c):
        return c + d * (nc - 1 - 2 * c)

    def body(xs_ref, b_ref, c_ref, dtc_ref, csc_ref, csr_ref, y_ref, st_ref, h_ref):
        d = pl.program_id(0)

        @pl.when(pl.program_id(2) == 0)
        def _():
            h_ref[...] = jnp.zeros_like(h_ref)

        for q in range(GROUPS_PER_STEP):
            group(xs_ref.at[:, pl.ds(q * gw, gw)], b_ref.at[:, pl.ds(q * N, N)], c_ref.at[:, pl.ds(q * N, N)],
                  dtc_ref.at[q], csc_ref.at[q], csr_ref.at[q], y_ref.at[:, pl.ds(q * gw, gw)], st_ref.at[q], h_ref.at[q], d)

    def group(xs_ref, b_ref, c_ref, dtc_ref, csc_ref, csr_ref, y_ref, st_ref, h_ref, d):
        Bb = b_ref[...].astype(BF16)
        Cb = c_ref[...].astype(BF16)
        S = _dot(Cb, Bb, NT)
        mask = _ssd_masks(d)
        csx, dtx, totx = _ssd_common(d, csc_ref, dtc_ref, hg)
        H = h_ref[...]
        st_ref[...] = H
        xdt = xs_ref[...] * dtx
        xdtb = xdt.astype(BF16)
        y_off = jnp.exp(csx) * _dot(Cb, H.astype(BF16), NN)
        for j in range(hg):
            sl = slice(j * P, (j + 1) * P)
            decay = jnp.exp(jnp.where(mask, csc_ref[:, j:j + 1] - csr_ref[j:j + 1, :], -jnp.inf))
            y_ref[:, sl] = _dot((S * decay).astype(BF16), xdtb[:, sl], NN) + y_off[:, sl]
        h_ref[...] = jnp.exp(totx) * H + _dot(Bb, (jnp.exp(totx - csx) * xdt).astype(BF16), TN)

    gps = GROUPS_PER_STEP
    assert N_GROUPS % gps == 0 and b_off % gps == 0 and c_off % gps == 0
    col = lambda d, g, c: (d, g, cidx(d, c), 0)
    return _call(
        body, comm=comm, name=name, grid=(2, N_GROUPS // gps, nc),
        in_specs=[pl.BlockSpec((CHUNK, gps * gw), lambda d, g, c: (cidx(d, c), g)),
                  pl.BlockSpec((CHUNK, gps * N), lambda d, g, c: (cidx(d, c), b_off // gps + g)),
                  pl.BlockSpec((CHUNK, gps * N), lambda d, g, c: (cidx(d, c), c_off // gps + g)),
                  pl.BlockSpec((None, gps, CHUNK, hg), col),
                  pl.BlockSpec((None, gps, CHUNK, hg), col),
                  pl.BlockSpec((None, gps, hg, CHUNK), lambda d, g, c: (d, g, 0, cidx(d, c)))],
        out_specs=[pl.BlockSpec((None, CHUNK, gps * gw), lambda d, g, c: (d, cidx(d, c), g)),
                   pl.BlockSpec((None, None, gps, N, gw), lambda d, g, c: (d, cidx(d, c), g, 0, 0))],
        out_shape=[jax.ShapeDtypeStruct((2, T, d_inner), F32),
                   jax.ShapeDtypeStruct((2, nc, N_GROUPS, N, gw), F32)],
        scratch_shapes=[pltpu.VMEM((gps, N, gw), F32)],
        compiler_params=_params("parallel", "parallel", "arbitrary"),
    )(xbc, xbc, xbc, dtc, csc, csr)


def _ssd_bwd(xbc, dtc, csc, csr, states, y2, dy, d_inner, name, comm=None):
    T = xbc.shape[0]
    nc = T // CHUNK
    gw = d_inner // N_GROUPS
    hg = gw // HEAD_DIM
    P, N = HEAD_DIM, D_STATE
    b_off = d_inner // N
    c_off = b_off + N_GROUPS

    def cidx(d, c):
        return (nc - 1 - c) + d * (2 * c - nc + 1)

    def body(xs_ref, b_ref, c_ref, dtc_ref, csc_ref, csr_ref, st_ref, y_ref, dy_ref,
             dxs_ref, db_ref, dc_ref, dcs_ref, dtot_ref, dxdtx_ref, dh_ref, dxdt_ref):
        d = pl.program_id(0)

        @pl.when(pl.program_id(2) == 0)
        def _():
            dh_ref[...] = jnp.zeros_like(dh_ref)

        for q in range(GROUPS_PER_STEP):
            wide = lambda r: r.at[:, pl.ds(q * gw, gw)]
            state = lambda r: r.at[:, pl.ds(q * N, N)]
            group(wide(xs_ref), state(b_ref), state(c_ref), dtc_ref.at[q], csc_ref.at[q], csr_ref.at[q], st_ref.at[q],
                  wide(y_ref), wide(dy_ref), wide(dxs_ref), state(db_ref), state(dc_ref), dcs_ref.at[q], dtot_ref.at[q],
                  dxdtx_ref.at[q], dh_ref.at[q], dxdt_ref.at[q], d)

    def group(xs_ref, b_ref, c_ref, dtc_ref, csc_ref, csr_ref, st_ref, y_ref, dy_ref,
              dxs_ref, db_ref, dc_ref, dcs_ref, dtot_ref, dxdtx_ref, dh_ref, dxdt_ref, d):
        Bb = b_ref[...].astype(BF16)
        Cb = c_ref[...].astype(BF16)
        S = _dot(Cb, Bb, NT)
        mask = _ssd_masks(d)
        csx, dtx, totx = _ssd_common(d, csc_ref, dtc_ref, hg)
        select = _head_select(hg)
        X = xs_ref[...]
        xdt = X * dtx
        xdtb = xdt.astype(BF16)
        dY = dy_ref[...]
        dYb = dY.astype(BF16)
        Hp = st_ref[...]
        Hpb = Hp.astype(BF16)
        dH = dh_ref[...]
        dHb = dH.astype(BF16)
        e_tot = jnp.exp(totx)
        dCH = (jnp.exp(csx) * dY).astype(BF16)
        dC = _dot(dCH, Hpb, NT)
        dHp = _dot(Cb, dCH, TN)
        Q = _dot(Bb, dHb, NN)
        dte = jnp.exp(totx - csx)
        wx = dte * xdt
        dB = _dot(wx.astype(BF16), dHb, NT)
        ddte = Q * wx
        dS = jnp.zeros((CHUNK, CHUNK), F32)
        for j in range(hg):
            sl = slice(j * P, (j + 1) * P)
            decay = jnp.exp(jnp.where(mask, csc_ref[:, j:j + 1] - csr_ref[j:j + 1, :], -jnp.inf))
            dS = dS + _dot(dYb[:, sl], xdtb[:, sl], NT) * decay
            dxdt_ref[:, sl] = _dot((S * decay).astype(BF16), dYb[:, sl], TN)
        dxdt_diag = dxdt_ref[...]
        dxdt = dxdt_diag + dte * Q
        dcs_ref[...] = _dot(dYb.astype(F32) * y_ref[...] - xdtb.astype(F32) * dxdt_diag - ddte, select, NN, HIGHEST)
        dtot_row = (jnp.sum(ddte, axis=0, keepdims=True) + e_tot * jnp.sum(dH * Hp, axis=0, keepdims=True))
        dtot_ref[...] = jnp.zeros((CHUNK, hg), F32) + _dot(dtot_row, select, NN, HIGHEST)
        dxdtx_ref[...] = _dot(dxdt * X, select, NN, HIGHEST)
        dxs_ref[...] = dxdt * dtx
        dh_ref[...] = e_tot * dH + dHp
        dSb = dS.astype(BF16)
        dc_ref[...] = dC + _dot(dSb, Bb, NN)
        db_ref[...] = dB + _dot(dSb, Cb, TN)

    col = lambda d, g, c: (d, g, cidx(d, c), 0)
    gps = GROUPS_PER_STEP
    assert N_GROUPS % gps == 0 and b_off % gps == 0 and c_off % gps == 0
    colspec = pl.BlockSpec((None, gps, CHUNK, hg), col)
    rowblk = pl.BlockSpec((None, CHUNK, gps * gw), lambda d, g, c: (d, cidx(d, c), g))
    return _call(
        body, comm=comm, name=name, grid=(2, N_GROUPS // gps, nc),
        in_specs=[pl.BlockSpec((CHUNK, gps * gw), lambda d, g, c: (cidx(d, c), g)),
                  pl.BlockSpec((CHUNK, gps * N), lambda d, g, c: (cidx(d, c), b_off // gps + g)),
                  pl.BlockSpec((CHUNK, gps * N), lambda d, g, c: (cidx(d, c), c_off // gps + g)),
                  colspec, colspec,
                  pl.BlockSpec((None, gps, hg, CHUNK), lambda d, g, c: (d, g, 0, cidx(d, c))),
                  pl.BlockSpec((None, None, gps, N, gw), lambda d, g, c: (d, cidx(d, c), g, 0, 0)),
                  rowblk,
                  pl.BlockSpec((CHUNK, gps * gw), lambda d, g, c: (cidx(d, c), g))],
        out_specs=[rowblk,
                   pl.BlockSpec((None, CHUNK, gps * N), lambda d, g, c: (d, cidx(d, c), g)),
                   pl.BlockSpec((None, CHUNK, gps * N), lambda d, g, c: (d, cidx(d, c), g)),
                   colspec, colspec, colspec],
        out_shape=[jax.ShapeDtypeStruct((2, T, d_inner), F32),
                   jax.ShapeDtypeStruct((2, T, N_GROUPS * N), F32),
                   jax.ShapeDtypeStruct((2, T, N_GROUPS * N), F32)]
        + [jax.ShapeDtypeStruct((2, N_GROUPS, T, hg), F32)] * 3,
        scratch_shapes=[pltpu.VMEM((gps, N, gw), F32), pltpu.VMEM((gps, CHUNK, gw), F32)],
        compiler_params=_params("parallel", "parallel", "arbitrary"),
    )(xbc, xbc, xbc, dtc, csc, csr, states, y2, dy)


def _gnorm_fwd(y2, xbc, zx, dvec, nw, d_inner, name, comm=None):
    T = xbc.shape[0]
    gw = d_inner // N_GROUPS
    tm = _tile(T, 128, 8)

    def body(y_ref, xs_ref, z_ref, d_ref, w_ref, o_ref):
        for g in range(N_GROUPS):
            sl = slice(g * gw, (g + 1) * gw)
            y = y_ref[0, :, sl] + y_ref[1, :, sl] + xs_ref[:, sl] * d_ref[:, sl]
            z = z_ref[:, sl]
            gy = y * (z * _sigmoid(z))
            rs = lax.rsqrt(jnp.mean(gy * gy, axis=-1, keepdims=True) + RMS_EPS)
            o_ref[:, sl] = (gy * rs * w_ref[:, sl]).astype(o_ref.dtype)

    row = pl.BlockSpec((tm, d_inner), lambda i: (i, 0))
    vec = pl.BlockSpec((1, d_inner), lambda i: (0, 0))
    return _call(body, comm=comm, name=name, grid=(T // tm,),
                 in_specs=[pl.BlockSpec((2, tm, d_inner), lambda i: (0, i, 0)), row, row, vec, vec],
                 out_specs=row, out_shape=jax.ShapeDtypeStruct((T, d_inner), BF16),
                 compiler_params=_params("parallel"))(y2, xbc, zx, dvec, nw)


def _gnorm_bwd(y2, xbc, zx, dvec, nw, dgn, sel, d_inner, name, comm=None):
    T = xbc.shape[0]
    gw = d_inner // N_GROUPS
    n_heads = d_inner // HEAD_DIM
    tm = _tile(T, 128, 8)
    n_tiles = T // tm

    def body(y_ref, xs_ref, z_ref, d_ref, w_ref, dg_ref, sel_ref, dy_ref, dz_ref, dw_ref, dd_ref, dch_ref):
        i = pl.program_id(0)

        @pl.when(i == 0)
        def _():
            dw_ref[...] = jnp.zeros_like(dw_ref)
            dch_ref[...] = jnp.zeros_like(dch_ref)

        for g in range(N_GROUPS):
            sl = slice(g * gw, (g + 1) * gw)
            xs = xs_ref[:, sl]
            y = y_ref[0, :, sl] + y_ref[1, :, sl] + xs * d_ref[:, sl]
            z = z_ref[:, sl]
            sig = _sigmoid(z)
            sz = z * sig
            gy = y * sz
            rs = lax.rsqrt(jnp.mean(gy * gy, axis=-1, keepdims=True) + RMS_EPS)
            n = gy * rs
            dout = dg_ref[:, sl]
            dw_ref[:, sl] += jnp.sum(dout * n, axis=0, keepdims=True)
            dn = dout * w_ref[:, sl]
            dgy = rs * (dn - n * jnp.mean(dn * n, axis=-1, keepdims=True))
            dy = dgy * sz
            dy_ref[:, sl] = dy
            dz_ref[:, sl] = (dgy * y * (sig * (1.0 + z * (1.0 - sig)))).astype(dz_ref.dtype)
            dch_ref[:, sl] += jnp.sum(dy * xs, axis=0, keepdims=True)

        @pl.when(i == n_tiles - 1)
        def _():
            dd_ref[...] = _dot(dch_ref[...], sel_ref[...], NN, HIGHEST)

    row = pl.BlockSpec((tm, d_inner), lambda i: (i, 0))
    vec = pl.BlockSpec((1, d_inner), lambda i: (0, 0))
    hvec = pl.BlockSpec((1, n_heads), lambda i: (0, 0))
    return _call(body, comm=comm, name=name, grid=(n_tiles,),
                 in_specs=[pl.BlockSpec((2, tm, d_inner), lambda i: (0, i, 0)), row, row, vec, vec, row,
                           pl.BlockSpec((d_inner, n_heads), lambda i: (0, 0))],
                 out_specs=[row, row, vec, hvec],
                 out_shape=[jax.ShapeDtypeStruct((T, d_inner), F32), jax.ShapeDtypeStruct(zx.shape, BF16),
                            jax.ShapeDtypeStruct((1, d_inner), F32), jax.ShapeDtypeStruct((1, n_heads), F32)],
                 scratch_shapes=[pltpu.VMEM((1, d_inner), F32)],
                 compiler_params=_params("arbitrary"))(y2, xbc, zx, dvec, nw, dgn, sel)


def _pool_counts(i, tt, T, win, rows, row0):
    t = i * tt + row0 + lax.broadcasted_iota(jnp.int32, (rows, 1), 0)
    start = t - win // 2
    lo = jnp.clip(start, 0, T)
    hi = jnp.clip(start + win, 0, T)
    return jnp.maximum(hi - lo, 1).astype(F32)


def _pool_features(ext_ref, i, tt, T, gi, gd):
    win = POOL_WINDOWS[gi]
    sl = slice(gi * gd, (gi + 1) * gd)
    acc = None
    for o in range(-(win // 2), win - win // 2):
        term = ext_ref[pl.ds(HALO + o, tt), sl]
        acc = term if acc is None else acc + term
    return acc / _pool_counts(i, tt, T, win, tt, 0) - ext_ref[pl.ds(HALO, tt), sl]


def _pool_fwd(u, w, bias, scale, name):
    T, D = u.shape
    ng = len(POOL_WINDOWS)
    gd = D // ng
    tt = _tile(T, 512, 8)
    nt = T // tt

    def body(cur, prev, nxt, w_ref, b_ref, s_ref, o_ref, ext):
        i = pl.program_id(0)
        _fill_ext(ext, cur, prev, nxt, i, nt, tt)
        for gi in range(ng):
            sl = slice(gi * gd, (gi + 1) * gd)
            m = _pool_features(ext, i, tt, T, gi, gd)
            pre = _dot(m.astype(BF16), w_ref[gi], NN) + b_ref[:, sl]
            o_ref[:, sl] = pre * s_ref[:, sl]

    vec = pl.BlockSpec((1, D), lambda i: (0, 0))
    return _call(body, name=name, grid=(nt,),
                 in_specs=_halo_specs(tt, D, lambda i: 0, nt, 0)
                 + [pl.BlockSpec((ng, gd, gd), lambda i: (0, 0, 0)), vec, vec],
                 out_specs=pl.BlockSpec((tt, D), lambda i: (i, 0)),
                 out_shape=jax.ShapeDtypeStruct((T, D), F32),
                 scratch_shapes=[pltpu.VMEM((tt + 2 * HALO, D), F32)],
                 compiler_params=_params("parallel"))(u, u, u, w, bias, scale)


def _pool_bwd_a(u, w, bias, scale, dy, name):
    T, D = u.shape
    ng = len(POOL_WINDOWS)
    gd = D // ng
    tt = _tile(T, 512, 8)
    nt = T // tt

    def body(cur, prev, nxt, w_ref, b_ref, s_ref, dy_ref, dm_ref, dw_ref, db_ref, ds_ref, ext):
        i = pl.program_id(0)

        @pl.when(i == 0)
        def _():
            dw_ref[...] = jnp.zeros_like(dw_ref)
            db_ref[...] = jnp.zeros_like(db_ref)
            ds_ref[...] = jnp.zeros_like(ds_ref)

        _fill_ext(ext, cur, prev, nxt, i, nt, tt)
        for gi in range(ng):
            sl = slice(gi * gd, (gi + 1) * gd)
            mb = _pool_features(ext, i, tt, T, gi, gd).astype(BF16)
            wg = w_ref[gi]
            pre = _dot(mb, wg, NN) + b_ref[:, sl]
            dy_ = dy_ref[:, sl]
            ds_ref[:, sl] += jnp.sum(dy_ * pre, axis=0, keepdims=True)
            dpre = dy_ * s_ref[:, sl]
            db_ref[:, sl] += jnp.sum(dpre, axis=0, keepdims=True)
            dpb = dpre.astype(BF16)
            dw_ref[gi] += _dot(mb, dpb, TN)
            dm_ref[:, sl] = _dot(dpb, wg, NT)

    vec = pl.BlockSpec((1, D), lambda i: (0, 0))
    row = pl.BlockSpec((tt, D), lambda i: (i, 0))
    wspec = pl.BlockSpec((ng, gd, gd), lambda i: (0, 0, 0))
    return _call(body, name=name, grid=(nt,),
                 in_specs=_halo_specs(tt, D, lambda i: 0, nt, 0) + [wspec, vec, vec, row],
                 out_specs=[row, wspec, vec, vec],
                 out_shape=[jax.ShapeDtypeStruct((T, D), F32), jax.ShapeDtypeStruct((ng, gd, gd), F32),
                            jax.ShapeDtypeStruct((1, D), F32), jax.ShapeDtypeStruct((1, D), F32)],
                 scratch_shapes=[pltpu.VMEM((tt + 2 * HALO, D), F32)],
                 compiler_params=_params("arbitrary"))(u, u, u, w, bias, scale, dy)


def _pool_bwd_b(dm, dy, alpha, name):
    T, D = dm.shape
    ng = len(POOL_WINDOWS)
    gd = D // ng
    tt = _tile(T, 512, 8)
    nt = T // tt

    def body(cur, prev, nxt, dy_ref, o_ref, ext):
        i = pl.program_id(0)
        _fill_ext(ext, cur, prev, nxt, i, nt, tt)
        for gi, win in enumerate(POOL_WINDOWS):
            sl = slice(gi * gd, (gi + 1) * gd)
            rows = tt + 2 * HALO
            ext[:, sl] = ext[:, sl] / _pool_counts(i, tt, T, win, rows, -HALO)
            acc = None
            for o in range(-(win // 2) + 1, win // 2 + 1):
                term = ext[pl.ds(HALO + o, tt), sl]
                acc = term if acc is None else acc + term
            o_ref[:, sl] = alpha * dy_ref[:, sl] + acc - cur[:, sl]

    row = pl.BlockSpec((tt, D), lambda i: (i, 0))
    return _call(body, name=name, grid=(nt,),
                 in_specs=_halo_specs(tt, D, lambda i: 0, nt, 0) + [row], out_specs=row,
                 out_shape=jax.ShapeDtypeStruct((T, D), F32),
                 scratch_shapes=[pltpu.VMEM((tt + 2 * HALO, D), F32)],
                 compiler_params=_params("parallel"))(dm, dm, dm, dy)


def _exchange(comm, name):
    ops, arrays, shapes, sems, in_place = _comm_plan(comm)
    n = len(ops)

    def body(*refs):
        copies = _comm_copies(ops, refs[:n], refs[n:2 * n], *refs[2 * n:])
        for cp in copies:
            cp.start()
        for cp in copies:
            cp.wait()

    any_spec = pl.BlockSpec(memory_space=pl.ANY)
    return _pallas(body, name=name, in_specs=[any_spec] * n, out_specs=[any_spec] * n, out_shape=shapes,
                   scratch_shapes=sems, input_output_aliases={a: a for a in in_place})(*arrays)


def _adamw(piece, w, m, v, row0, into, name):
    R, C = w.shape
    rows = piece.shape[1]
    tr = _tile(rows, max(8, (1 << 18) // C // 8 * 8), 8)
    assert row0 % tr == 0 and rows % tr == 0
    off = row0 // tr

    def body(p_ref, w_ref, m_ref, v_ref, *rest):
        g_ref, d_ref, nm_ref, nv_ref = rest[-4:]
        g = p_ref[0].astype(F32)
        for i in range(1, N_DEV):
            g = g + p_ref[i].astype(F32)
        mm = ADAM_B1 * m_ref[...] + (1.0 - ADAM_B1) * g
        vv = ADAM_B2 * v_ref[...] + (1.0 - ADAM_B2) * (g * g)
        m_hat = mm / (1.0 - ADAM_B1 ** ADAM_STEP)
        v_hat = vv / (1.0 - ADAM_B2 ** ADAM_STEP)
        g_ref[...] = g
        d_ref[...] = -ADAM_LR * (m_hat / (jnp.sqrt(v_hat) + ADAM_EPS) + ADAM_WD * w_ref[...])
        nm_ref[...] = mm
        nv_ref[...] = vv

    row = pl.BlockSpec((tr, C), lambda i: (off + i, 0))
    kept = [] if into is None else list(into)
    return _call(body, name=name, grid=(rows // tr,),
                 in_specs=[pl.BlockSpec((N_DEV, tr, C), lambda i: (0, i, 0)), row, row, row]
                 + [pl.BlockSpec(memory_space=pl.ANY)] * len(kept),
                 out_specs=[row] * 4, out_shape=[jax.ShapeDtypeStruct((R, C), F32)] * 4,
                 input_output_aliases={4 + q: q for q in range(len(kept))},
                 compiler_params=_params("parallel"))(piece, w, m, v, *kept)


def _pack(arrays):
    flat, meta, off = [], [], 0
    for a in arrays:
        flat.append(a.reshape(-1).astype(F32))
        meta.append((off, a.shape))
        off += a.size
    total = -(-off // (8 * LANES)) * (8 * LANES)
    flat.append(jnp.zeros((total - off,), F32))
    return jnp.concatenate(flat).reshape(total // LANES, LANES), meta


def _unpack(packed, meta):
    flat = packed.reshape(-1)
    return [flat[off:off + math.prod(shape)].reshape(shape) for off, shape in meta]


def kernel(x, ssd_in_proj, ssd_conv_w, ssd_conv_b, ssd_dt_bias, ssd_A_log, ssd_D, ssd_norm_w, ssd_out_proj, pool_w, pool_b, pool_scale, mlp_w1, mlp_w2, ln_mix_g, ln_mix_b, ln_ffn_g, ln_ffn_b, loss_target, m_ssd_in_proj, m_ssd_conv_w, m_ssd_conv_b, m_ssd_dt_bias, m_ssd_A_log, m_ssd_D, m_ssd_norm_w, m_ssd_out_proj, m_pool_w, m_pool_b, m_pool_scale, m_mlp_w1, m_mlp_w2, m_ln_mix_g, m_ln_mix_b, m_ln_ffn_g, m_ln_ffn_b, v_ssd_in_proj, v_ssd_conv_w, v_ssd_conv_b, v_ssd_dt_bias, v_ssd_A_log, v_ssd_D, v_ssd_norm_w, v_ssd_out_proj, v_pool_w, v_pool_b, v_pool_scale, v_mlp_w1, v_mlp_w2, v_ln_mix_g, v_ln_mix_b, v_ln_ffn_g, v_ln_ffn_b):
    T, D = x.shape[1], x.shape[2]
    depth = mlp_w1.shape[0]
    n_ssd, n_pool = ssd_in_proj.shape[0], pool_w.shape[0]
    d_inner = ssd_out_proj.shape[1] * N_DEV
    n_heads = d_inner // HEAD_DIM
    hg = n_heads // N_GROUPS
    d_bc = N_GROUPS * D_STATE
    d_xbc = d_inner + 2 * d_bc
    d_in_proj = ssd_in_proj.shape[2] * N_DEV
    d_ff = mlp_w1.shape[2] * N_DEV
    ng = len(POOL_WINDOWS)
    gd = D // ng
    alpha = (2.0 * depth) ** 0.25
    x0 = x.reshape(T, D)
    target = loss_target.reshape(T, D)

    assert depth == 4 and n_ssd == 2 and n_pool == 2, "the exchange schedules below are written for this stack"

    small_pack, small_meta = _pack([ssd_conv_w, pool_b, pool_scale])
    pw_rows = pool_w.shape[1] * pool_w.shape[2]
    in_b, out_b = ssd_in_proj.astype(BF16), ssd_out_proj.astype(BF16)
    pw_b = pool_w.reshape(n_pool, pw_rows, gd).astype(BF16)
    w1_b, w2_b = mlp_w1.astype(BF16), mlp_w2.astype(BF16)
    shard = {("in", 0): in_b[0], ("in", 1): in_b[1], ("out", 0): out_b[0], ("out", 1): out_b[1],
             ("pool", 0): pw_b[0], ("pool", 1): pw_b[1]}
    for i in range(depth):
        shard["w1", i], shard["w2", i] = w1_b[i], w2_b[i]

    def full_cols(g):
        return g.transpose(1, 0, 2).reshape(g.shape[1], -1)

    def full_rows(g):
        return g.reshape(-1, g.shape[-1])

    def full_pool(g):
        return g.reshape(N_DEV, ng, gd // N_DEV, gd).transpose(1, 0, 2, 3).reshape(ng, gd, gd)

    def slab_cols(g):
        return g.reshape(g.shape[0], N_DEV, -1).transpose(1, 0, 2)

    def slab_rows(g):
        return g.reshape(N_DEV, -1, g.shape[-1])

    def slab_pool(g):
        return g.astype(BF16).reshape(ng, N_DEV, gd // N_DEV, gd).transpose(1, 0, 2, 3).reshape(N_DEV, pw_rows, gd)

    to_full = {"in": full_cols, "out": full_rows, "pool": full_pool, "w1": lambda g: g, "w2": full_rows}

    gather1_on = {"in_proj_0": [("w1", 0), ("pool", 0)], "conv_fwd_0": [("out", 0)],
                  "ssd_fwd_0": [("w2", 0), ("w1", 1), ("in", 1)], "mlp_up_0": [("w2", 1)],
                  "mlp_down_0": [("w1", 2)], "mlp_up_1": [("w2", 2)], "mlp_down_1": [("w1", 3), ("pool", 1)],
                  "in_proj_2": [("out", 1)], "ssd_fwd_2": [("w2", 3)]}
    gather2_on = {"conv_fwd_0": [("w1", 0), ("pool", 0)], "ssd_fwd_0": [("out", 0)],
                  "gnorm_fwd_0": [("w2", 0), ("w1", 1), ("in", 1)], "mlp_down_0": [("w2", 1)],
                  "mlp_up_1": [("w1", 2)], "mlp_down_1": [("w2", 2)], "in_proj_2": [("w1", 3), ("pool", 1)],
                  "conv_fwd_2": [("out", 1)], "gnorm_fwd_2": [("w2", 3)]}
    hide_us = {"mlp_down_dx": 161, "mlp_down_dw": 163, "mlp_up_dx": 170, "mlp_up_dw": 164, "out_proj_dx": 83,
               "out_proj_dw": 85, "gnorm_bwd": 163, "ssd_bwd": 1104, "conv_bwd": 228, "in_proj_dw": 230, "in_proj_dx": 267}
    SLAB_BYTES_PER_US = 70e3 / (N_DEV - 1)
    OVERRUN_US, MIN_CARRIER_US = 0, 80
    n_pieces = {"in": 4, "out": 2, "pool": 1, "w1": 4, "w2": 4}
    W, half, G, R = {}, {}, {}, {}
    queue = []

    def produced(key, slabs):
        G[key] = slabs
        rows = slabs.shape[1] // n_pieces[key[0]]
        cost = rows * math.prod(slabs.shape[2:]) * slabs.dtype.itemsize / SLAB_BYTES_PER_US
        queue.extend((key, p * rows, rows, cost) for p in range(n_pieces[key[0]]))

    def take(budget):
        taken, used = [], 0.0
        while queue and ((not taken and budget >= MIN_CARRIER_US) or used + queue[0][3] <= budget + OVERRUN_US):
            taken.append(queue.pop(0))
            used += taken[-1][3]
        return taken

    def run(fn, *args, name, **kw):
        k1, k2 = gather1_on.get(name, []), gather2_on.get(name, [])
        budget = float("inf") if name == "in_proj_dx_0" else hide_us.get(name.rsplit("_", 1)[0], 0)
        pieces = take(budget) if not (k1 or k2) else []
        comm = ([("gather1", shard[k]) for k in k1] + [("gather2", half[k]) for k in k2]
                + [("slabs", G[key], r0, rows) for key, r0, rows, _ in pieces])
        if not comm:
            return fn(*args, name=name, **kw)
        res, got = fn(*args, name=name, comm=comm, **kw)
        for k, g in zip(k1, got):
            half[k] = g
        for k, g in zip(k2, got[len(k1):]):
            W[k] = to_full[k[0]](g)
        for (key, r0, _, _), g in zip(pieces, got):
            R.setdefault(key, []).append((r0, g))
        return res

    half_in0, g_small = _exchange([("gather1", shard["in", 0]), ("gather", small_pack)], "gather_first")
    (g_in0,) = _exchange([("gather2", half_in0)], "gather_first_onward")
    W["in", 0] = full_cols(g_in0)
    smalls = [_unpack(g_small[k], small_meta) for k in range(N_DEV)]
    conv_w = jnp.concatenate([s[0] for s in smalls], axis=-1).reshape(n_ssd, CONV_WIDTH, d_xbc)
    pool_bias = jnp.concatenate([s[1] for s in smalls], axis=-1).reshape(n_pool, 1, D)
    pool_sc = jnp.concatenate([s[2] for s in smalls], axis=-1).reshape(n_pool, 1, D)

    sel = (jnp.arange(d_inner)[:, None] // HEAD_DIM == jnp.arange(n_heads)[None, :]).astype(F32)

    saved = []
    h, hb = x0, x0.astype(BF16)
    for i in range(depth):
        j = i // 2
        s = {}
        s["x0"], s["x0b"] = h, hb
        if i % 2 == 0:
            (zx,) = run(_matmul, hb, W["in", j], "nn", name=f"in_proj_{i}", outs=[F32], tn=1152)
            xbc = run(_conv_fwd, zx, conv_w[j], ssd_conv_b[j], d_inner, name=f"conv_fwd_{i}")
            raw = zx[:, d_inner + d_xbc:]
            dt, cs = _dt_fwd(raw, ssd_dt_bias[j], ssd_A_log[j], f"dt_fwd_{i}")
            dtc, csc, csr = _cols(dt, hg), _cols(cs, hg), _rows(cs, hg)
            y2, states = run(_ssd_fwd, xbc, dtc, csc, csr, d_inner, name=f"ssd_fwd_{i}")
            dvec = jnp.repeat(ssd_D[j], HEAD_DIM).reshape(1, d_inner)
            nw = ssd_norm_w[j].reshape(1, d_inner)
            gn = run(_gnorm_fwd, y2, xbc, zx, dvec, nw, d_inner, name=f"gnorm_fwd_{i}")
            (mix,) = run(_matmul, gn, W["out", j], "nn", name=f"out_proj_{i}", outs=[F32])
            s.update(zx=zx, xbc=xbc, raw=raw, dtc=dtc, csc=csc, csr=csr, y2=y2, states=states, dvec=dvec, nw=nw, gn=gn)
        else:
            mix = _pool_fwd(h, W["pool", j], pool_bias[j], pool_sc[j], f"pool_fwd_{i}")
        s["mix"] = mix
        x1, x1b = _ln_fwd(h, mix, ln_mix_g[i], ln_mix_b[i], alpha, f"ln_mix_fwd_{i}")
        u, hh = run(_matmul, x1b, W["w1", i], "nn", name=f"mlp_up_{i}", outs=[F32, BF16],
                    epilogue=lambda acc: (acc, jnp.square(jnp.maximum(acc, 0.0))), cols_by_device=True)
        (m2,) = run(_matmul, hh, W["w2", i], "nn", name=f"mlp_down_{i}", outs=[F32])
        x2, x2b = _ln_fwd(x1, m2, ln_ffn_g[i], ln_ffn_b[i], alpha, f"ln_ffn_fwd_{i}")
        s.update(x1=x1, x1b=x1b, u=u, hh=hh, m2=m2)
        saved.append(s)
        h, hb = x2, x2b

    loss_row, dh = _loss_head(h, target)
    loss = lax.psum(loss_row[0, 0], ("x", "y", "c"))

    big = {"in": (ssd_in_proj, m_ssd_in_proj, v_ssd_in_proj), "out": (ssd_out_proj, m_ssd_out_proj, v_ssd_out_proj),
           "pool": (pool_w, m_pool_w, v_pool_w), "w1": (mlp_w1, m_mlp_w1, v_mlp_w1), "w2": (mlp_w2, m_mlp_w2, v_mlp_w2)}

    def update(kind, count):
        shape = big[kind][0].shape
        w, m, v = (a.reshape(-1, shape[-1]) for a in big[kind])
        per_layer = w.shape[0] // count
        outs = None
        for l in range(count):
            for r0, g in sorted(R[kind, l], key=lambda t: t[0]):
                outs = _adamw(g, w, m, v, l * per_layer + r0, outs, f"adamw_{kind}_{l}_{r0}")
        return [o.reshape(shape) for o in outs]

    gr = {k: [None] * depth for k in ("ln_mix_g", "ln_mix_b", "ln_ffn_g", "ln_ffn_b")}
    gs = {k: [None] * n_ssd for k in ("conv_w", "conv_b", "dt_bias", "A_log", "D", "norm_w")}
    gp = {k: [None] * n_pool for k in ("b", "scale")}
    for i in reversed(range(depth)):
        j = i // 2
        s = saved[i]
        ds2, ds2b, gr["ln_ffn_g"][i], gr["ln_ffn_b"][i] = _ln_bwd(s["x1"], s["m2"], ln_ffn_g[i], dh, alpha, f"ln_ffn_bwd_{i}")
        (du,) = run(_matmul, ds2b, W["w2", i], "nt", name=f"mlp_down_dx_{i}", outs=[BF16], extras=[s["u"]],
                    epilogue=lambda acc, u_: (acc * (2.0 * jnp.maximum(u_, 0.0)),))
        (g_w2,) = run(_matmul, s["hh"], ds2b, "tn", name=f"mlp_down_dw_{i}", outs=[BF16])
        produced(("w2", i), slab_rows(g_w2))
        (dx1,) = run(_matmul, du, W["w1", i], "nt", name=f"mlp_up_dx_{i}", outs=[F32], extras=[ds2],
                     epilogue=lambda acc, e: (acc + alpha * e,), cols_by_device=True)
        (g_w1,) = run(_matmul, s["x1b"], du, "tn", name=f"mlp_up_dw_{i}", outs=[BF16], cols_by_device=True)
        produced(("w1", i), g_w1)
        ds1, ds1b, gr["ln_mix_g"][i], gr["ln_mix_b"][i] = _ln_bwd(s["x0"], s["mix"], ln_mix_g[i], dx1, alpha, f"ln_mix_bwd_{i}")
        if i % 2 == 0:
            (dgn,) = run(_matmul, ds1b, W["out", j], "nt", name=f"out_proj_dx_{i}", outs=[F32])
            (g_out,) = run(_matmul, s["gn"], ds1b, "tn", name=f"out_proj_dw_{i}", outs=[BF16])
            produced(("out", j), slab_rows(g_out))
            dy, dzx, gs["norm_w"][j], gs["D"][j] = run(
                _gnorm_bwd, s["y2"], s["xbc"], s["zx"], s["dvec"], s["nw"], dgn, sel, d_inner, name=f"gnorm_bwd_{i}")
            dxs, dB, dC, dcs, dtot, dxdtx = run(_ssd_bwd, s["xbc"], s["dtc"], s["csc"], s["csr"], s["states"], s["y2"], dy,
                                                d_inner, name=f"ssd_bwd_{i}")
            dzx, gs["dt_bias"][j], gs["A_log"][j] = _dt_bwd(
                s["raw"], ssd_dt_bias[j], ssd_A_log[j], _uncols(dcs), _uncols(dtot), _uncols(dxdtx), dzx, f"dt_bwd_{i}")
            dpre = _conv_dpre(s["zx"], conv_w[j], ssd_conv_b[j], d_inner, 0, dxs, dy, f"conv_dpre_x_{i}",
                              extra_scale=s["dvec"])
            dpre = _conv_dpre(s["zx"], conv_w[j], ssd_conv_b[j], d_inner, d_inner, dB, None, f"conv_dpre_b_{i}", into=dpre)
            dpre = _conv_dpre(s["zx"], conv_w[j], ssd_conv_b[j], d_inner, d_inner + d_bc, dC, None, f"conv_dpre_c_{i}",
                              into=dpre)
            dzx, gs["conv_w"][j], gs["conv_b"][j] = run(_conv_bwd, s["zx"], dpre, conv_w[j], d_inner, dzx,
                                                        name=f"conv_bwd_{i}")
            (g_in,) = run(_matmul, s["x0b"], dzx, "tn", name=f"in_proj_dw_{i}", outs=[BF16], tn=1152)
            produced(("in", j), slab_cols(g_in))
            (dh,) = run(_matmul, dzx, W["in", j], "nt", name=f"in_proj_dx_{i}", outs=[F32], extras=[ds1], tk=1152,
                        epilogue=lambda acc, e: (acc + alpha * e,))
        else:
            dm, g_pw, gp["b"][j], gp["scale"][j] = _pool_bwd_a(
                s["x0"], W["pool", j], pool_bias[j], pool_sc[j], ds1, f"pool_bwd_a_{i}")
            produced(("pool", j), slab_pool(g_pw))
            dh = _pool_bwd_b(dm, ds1, alpha, f"pool_bwd_b_{i}")
    grad_x = dh.reshape(x.shape)

    g_conv_w = jnp.stack(gs["conv_w"]).reshape(n_ssd, CONV_WIDTH, 1, N_DEV, d_xbc // N_DEV)
    g_pool_b = jnp.stack(gp["b"]).reshape(n_pool, ng, N_DEV, gd // N_DEV)
    g_pool_s = jnp.stack(gp["scale"]).reshape(n_pool, N_DEV, D // N_DEV)
    s_small = jnp.stack([_pack([g_conv_w[:, :, :, k], g_pool_b[:, :, k], g_pool_s[:, k]])[0] for k in range(N_DEV)])
    repl_grads = [jnp.stack(gs["conv_b"]).reshape(ssd_conv_b.shape), jnp.stack(gs["dt_bias"]).reshape(ssd_dt_bias.shape),
                  jnp.stack(gs["A_log"]).reshape(ssd_A_log.shape), jnp.stack(gs["D"]).reshape(ssd_D.shape),
                  jnp.stack(gs["norm_w"]).reshape(ssd_norm_w.shape),
                  jnp.stack(gr["ln_mix_g"]).reshape(ln_mix_g.shape), jnp.stack(gr["ln_mix_b"]).reshape(ln_mix_b.shape),
                  jnp.stack(gr["ln_ffn_g"]).reshape(ln_ffn_g.shape), jnp.stack(gr["ln_ffn_b"]).reshape(ln_ffn_b.shape)]
    repl_pack, repl_meta = _pack(repl_grads)
    s_repl = jnp.broadcast_to(repl_pack[None], (N_DEV,) + repl_pack.shape)
    left = list(queue)
    del queue[:]
    got = _exchange([("slabs", s_small, 0, s_small.shape[1]), ("slabs", s_repl, 0, s_repl.shape[1])]
                    + [("slabs", G[key], r0, rows) for key, r0, rows, _ in left], "exchange_last")
    r_small, r_repl = got[0], got[1]
    for (key, r0, _, _), g in zip(left, got[2:]):
        R.setdefault(key, []).append((r0, g))

    upd = {}
    for nm, kind, count in (("ssd_in_proj", "in", n_ssd), ("ssd_out_proj", "out", n_ssd), ("pool_w", "pool", n_pool),
                            ("mlp_w1", "w1", depth), ("mlp_w2", "w2", depth)):
        upd[nm] = update(kind, count)
    sm = _adamw(r_small, small_pack, _pack([m_ssd_conv_w, m_pool_b, m_pool_scale])[0],
                _pack([v_ssd_conv_w, v_pool_b, v_pool_scale])[0], 0, None, "adamw_small_sharded")
    for idx, nm in enumerate(["ssd_conv_w", "pool_b", "pool_scale"]):
        upd[nm] = [_unpack(r, small_meta)[idx] for r in sm]
    repl_names = ["ssd_conv_b", "ssd_dt_bias", "ssd_A_log", "ssd_D", "ssd_norm_w",
                  "ln_mix_g", "ln_mix_b", "ln_ffn_g", "ln_ffn_b"]
    repl_w = [ssd_conv_b, ssd_dt_bias, ssd_A_log, ssd_D, ssd_norm_w, ln_mix_g, ln_mix_b, ln_ffn_g, ln_ffn_b]
    repl_m = [m_ssd_conv_b, m_ssd_dt_bias, m_ssd_A_log, m_ssd_D, m_ssd_norm_w, m_ln_mix_g, m_ln_mix_b, m_ln_ffn_g, m_ln_ffn_b]
    repl_v = [v_ssd_conv_b, v_ssd_dt_bias, v_ssd_A_log, v_ssd_D, v_ssd_norm_w, v_ln_mix_g, v_ln_mix_b, v_ln_ffn_g, v_ln_ffn_b]
    rp = _adamw(r_repl, _pack(repl_w)[0], _pack(repl_m)[0], _pack(repl_v)[0], 0, None, "adamw_replicated")
    for idx, nm in enumerate(repl_names):
        upd[nm] = [_unpack(r, repl_meta)[idx] for r in rp]

    order = ["ssd_in_proj", "ssd_conv_w", "ssd_conv_b", "ssd_dt_bias", "ssd_A_log", "ssd_D", "ssd_norm_w",
             "ssd_out_proj", "pool_w", "pool_b", "pool_scale", "mlp_w1", "mlp_w2",
             "ln_mix_g", "ln_mix_b", "ln_ffn_g", "ln_ffn_b"]
    return (loss, grad_x, *[upd[n][0] for n in order], *[upd[n][1] for n in order],
            *[upd[n][2] for n in order], *[upd[n][3] for n in order])
```
